```python
import math
import jax, jax.numpy as jnp
from jax import lax
import numpy as np

D_MODEL = 1024
BATCH = 16
SEQ = 256
DEPTH = 2
DEC_BATCH = 4
DEC_SEQ = 4096
PAST_LEN = 512

GRID_W = 64
N_HEADS = 8
HEAD_DIM = 64
V_DIM = 2 * HEAD_DIM
QK_WIDTH = N_HEADS * 2 * HEAD_DIM
ATTN_WIDTH = N_HEADS * V_DIM
ROPE_BASE = 10000.0
Q_BLOCK = 128
D_LRU = 1024
LRU_BLOCKS = 16
LRU_BW = D_LRU // LRU_BLOCKS
CONV_WIDTH = 4
LRU_C = 8.0
N_EXPERTS = 64
TOP_K = 8
N_GROUPS = 8
TOPK_GROUPS = 4
D_EXPERT = 256
ROUTE_SCALE = 2.5
MOE_BLOCK = 128
ALPHA = (2 * DEPTH) ** 0.25
BETA = (8 * DEPTH) ** -0.25
LN_EPS = 1e-5
D_IN = 2 * QK_WIDTH + ATTN_WIDTH + 2 * D_LRU + 2 * D_MODEL
SPLITS = (QK_WIDTH, 2 * QK_WIDTH, 2 * QK_WIDTH + ATTN_WIDTH,
          2 * QK_WIDTH + ATTN_WIDTH + D_LRU, 2 * QK_WIDTH + ATTN_WIDTH + 2 * D_LRU,
          2 * QK_WIDTH + ATTN_WIDTH + 2 * D_LRU + D_MODEL)
f32 = jnp.float32

kernel_name = "hybrid_diffattn_rglru_moe_diffusion_step"


def layer_norm(x, g, b):
    xf = x.astype(f32)
    mu = xf.mean(-1, keepdims=True)
    var = jnp.mean(jnp.square(xf - mu), -1, keepdims=True)
    return ((xf - mu) * lax.rsqrt(var + LN_EPS) * g.astype(f32) + b.astype(f32)).astype(x.dtype)


def modulation(cond, w_mod, b_mod):
    m = jax.nn.silu(cond) @ w_mod + b_mod
    return jnp.split(m[:, None, :], 6, axis=-1)


def axial_rope(L):
    rows = L // GRID_W
    row = jnp.repeat(jnp.arange(rows, dtype=f32), GRID_W)
    col = jnp.tile(jnp.arange(GRID_W, dtype=f32), rows)
    half = HEAD_DIM // 2
    inv = ROPE_BASE ** (-jnp.arange(0, half, 2, dtype=f32) / half)
    ang = jnp.stack([row[:, None] * inv, col[:, None] * inv], 0)
    return jnp.cos(ang), jnp.sin(ang)


def apply_rope(x, cos, sin):
    shp = x.shape
    xa = x.reshape(shp[:-1] + (2, 2, HEAD_DIM // 4)).astype(f32)
    c = cos.transpose(1, 0, 2)[None, :, None, None]
    s = sin.transpose(1, 0, 2)[None, :, None, None]
    x1, x2 = xa[..., 0, :], xa[..., 1, :]
    out = jnp.stack([x1 * c - x2 * s, x2 * c + x1 * s], axis=-2)
    return out.reshape(shp).astype(x.dtype)


def diff_attention(q, k, v, lam, lam_init, subln_w):
    B, Lq = q.shape[:2]
    nb = Lq // Q_BLOCK
    qb = jnp.moveaxis(q.reshape((B, nb, Q_BLOCK) + q.shape[2:]), 1, 0)
    scale = HEAD_DIM ** -0.5

    def block(qi):
        s = jnp.einsum('bqhcd,bkhcd->bchqk', qi, k, preferred_element_type=f32) * scale
        p = jax.nn.softmax(s, axis=-1)
        a = (p[:, 0] - lam * p[:, 1]).astype(v.dtype)
        return jnp.einsum('bhqk,bkhe->bqhe', a, v)

    o = lax.map(block, qb)
    of = jnp.moveaxis(o, 0, 1).reshape(B, Lq, N_HEADS, V_DIM).astype(f32)
    of = of * lax.rsqrt(jnp.mean(of * of, -1, keepdims=True) + LN_EPS) * subln_w.astype(f32)
    return (of * (1.0 - lam_init)).reshape(B, Lq, ATTN_WIDTH).astype(q.dtype)


def depthwise_conv(x, w, b):
    y = lax.conv_general_dilated(x, w[:, None, :].astype(x.dtype), window_strides=(1,),
                                 padding=[(1, 2)], dimension_numbers=('NWC', 'WIO', 'NWC'),
                                 feature_group_count=D_LRU)
    return y + b


def block_diag(x, w, b):
    B, L, _ = x.shape
    y = jnp.einsum('blnc,ncd->blnd', x.reshape(B, L, LRU_BLOCKS, LRU_BW), w)
    return y.reshape(B, L, D_LRU) + b


def linear_scan(a, bx, h0, reverse):
    idx = -1 if reverse else 0
    bx = bx.at[:, idx].add(a[:, idx] * h0)

    def comb(e1, e2):
        a1, b1 = e1
        a2, b2 = e2
        return a1 * a2, a2 * b1 + b2

    _, h = lax.associative_scan(comb, (a, bx), reverse=reverse, axis=1)
    return h


def rglru(xc, h0, w_r, b_r, w_i, b_i, lru_lam):
    xf = xc.astype(f32)
    outs, finals = [], []
    for d in range(2):
        r = jax.nn.sigmoid(block_diag(xc, w_r[d], b_r[d]).astype(f32))
        i = jax.nn.sigmoid(block_diag(xc, w_i[d], b_i[d]).astype(f32))
        log_a = -LRU_C * r * jax.nn.softplus(-lru_lam[d].astype(f32))
        a = jnp.exp(log_a)
        bx = jnp.sqrt(-jnp.expm1(2.0 * log_a)) * (i * xf)
        h = linear_scan(a, bx, h0[:, d].astype(f32), reverse=(d == 1))
        outs.append(h)
        finals.append(h[:, 0] if d == 1 else h[:, -1])
    return outs[0] + outs[1], jnp.stack(finals, 1)


def token_mixer(h, w_in, lam_p, lam_init, subln_w, conv_w, conv_b, w_r, b_r, w_i, b_i,
                lru_lam, w_br_attn, w_br_lru, w_o, rope, ctx_k, ctx_v, h0):
    B, L, _ = h.shape
    q, k, v, xr, yr, g_attn, g_lru = jnp.split(h @ w_in, SPLITS, axis=-1)
    q = q.reshape(B, L, N_HEADS, 2, HEAD_DIM)
    k = k.reshape(B, L, N_HEADS, 2, HEAD_DIM)
    v = v.reshape(B, L, N_HEADS, V_DIM)
    if rope is not None:
        q = apply_rope(q, *rope)
        k = apply_rope(k, *rope)
    k_own = k.reshape(B, L, N_HEADS, 2 * HEAD_DIM)
    if ctx_k is None:
        keys, vals = k_own, v
    else:
        keys = jnp.concatenate([ctx_k.astype(k_own.dtype), k_own], axis=1)
        vals = jnp.concatenate([ctx_v.astype(v.dtype), v], axis=1)
    lp = lam_p.astype(f32)
    lam = jnp.exp(jnp.sum(lp[0] * lp[1])) - jnp.exp(jnp.sum(lp[2] * lp[3])) + lam_init
    attn = diff_attention(q, keys.reshape(B, -1, N_HEADS, 2, HEAD_DIM), vals, lam, lam_init, subln_w)
    xc = depthwise_conv(xr, conv_w, conv_b)
    rec, h_final = rglru(xc, h0, w_r, b_r, w_i, b_i, lru_lam)
    rec = (rec * jax.nn.gelu(yr.astype(f32))).astype(h.dtype)
    merged = jax.nn.sigmoid(g_attn) * (attn @ w_br_attn) + jax.nn.sigmoid(g_lru) * (rec @ w_br_lru)
    return merged @ w_o, k_own, v, h_final.astype(h.dtype)


def swiglu(x, wg, wu, wd):
    return (jax.nn.silu(x @ wg) * (x @ wu)) @ wd


def moe(h, router_w, router_b, we_g, we_u, we_d, ws_g, ws_u, ws_d):
    B, L, D = h.shape
    x = h.reshape(-1, D)
    N = x.shape[0]
    s = jax.nn.sigmoid((x @ router_w).astype(f32))
    sb = s + router_b.astype(f32)
    grp_score = lax.top_k(sb.reshape(N, N_GROUPS, -1), 2)[0].sum(-1)
    _, gidx = lax.top_k(grp_score, TOPK_GROUPS)
    gmask = jnp.any(gidx[..., None] == jnp.arange(N_GROUPS), axis=1)
    sb = jnp.where(jnp.repeat(gmask, N_EXPERTS // N_GROUPS, axis=1), sb, -jnp.inf)
    _, eidx = lax.top_k(sb, TOP_K)
    g = jnp.take_along_axis(s, eidx, 1)
    g = g / g.sum(-1, keepdims=True) * ROUTE_SCALE
    flat_e = eidx.reshape(-1).astype(jnp.int32)
    order = jnp.argsort(flat_e)
    e_sorted = flat_e[order]
    counts = jnp.bincount(flat_e, length=N_EXPERTS).astype(jnp.int32)
    padded = (counts + MOE_BLOCK - 1) // MOE_BLOCK * MOE_BLOCK
    pad_end = jnp.cumsum(padded)
    pad_start = pad_end - padded
    start = jnp.cumsum(counts) - counts
    dest = pad_start[e_sorted] + jnp.arange(N * TOP_K, dtype=jnp.int32) - start[e_sorted]
    n_blocks = -(-(N * TOP_K + N_EXPERTS * (MOE_BLOCK - 1)) // MOE_BLOCK)
    n_rows = n_blocks * MOE_BLOCK
    row_tok = jnp.full((n_rows,), N, jnp.int32).at[dest].set((order // TOP_K).astype(jnp.int32))
    row_gate = jnp.zeros((n_rows,), f32).at[dest].set(g.reshape(-1)[order])
    blk_e = jnp.minimum(jnp.searchsorted(pad_end, jnp.arange(n_blocks, dtype=jnp.int32) * MOE_BLOCK,
                                         side='right'), N_EXPERTS - 1)
    x_pad = jnp.concatenate([x, jnp.zeros((1, D), x.dtype)], 0)
    xb = x_pad[row_tok].reshape(n_blocks, MOE_BLOCK, D)

    def expert_block(args):
        xi, e = args
        return swiglu(xi, we_g[e], we_u[e], we_d[e])

    yb = lax.map(expert_block, (xb, blk_e)).reshape(n_rows, D)
    routed = jnp.zeros((N + 1, D), f32).at[row_tok].add(yb.astype(f32) * row_gate[:, None])[:N]
    shared = swiglu(x, ws_g, ws_u, ws_d).astype(f32)
    return (shared + routed).astype(h.dtype).reshape(B, L, D)


def setup_inputs(seed: int = 0) -> dict:
    key = jax.random.key(seed)
    k = jax.random.split(key, 32)
    D = D_MODEL

    def nrm(i, shape, scale):
        return scale * jax.random.normal(k[i], shape, f32)

    u = jax.random.uniform(k[18], (DEPTH, 2, D_LRU), f32, 0.9, 0.999)
    p = u ** (1.0 / LRU_C)
    lru_lambda = jnp.log(p) - jnp.log1p(-p)
    return {
        "x_prompt": nrm(0, (BATCH, SEQ, D), 1.0),
        "x_sample": nrm(1, (DEC_BATCH, DEC_SEQ, D), 1.0),
        "c": nrm(2, (DEC_BATCH, D), 1.0),
        "cache_k": nrm(3, (DEC_BATCH, DEPTH, PAST_LEN, N_HEADS, 2 * HEAD_DIM), 1.0),
        "cache_v": nrm(4, (DEC_BATCH, DEPTH, PAST_LEN, N_HEADS, V_DIM), 1.0),
        "state_lru": nrm(5, (DEC_BATCH, DEPTH, 2, D_LRU), 0.5),
        "c_ctx": nrm(6, (D,), 1.0),
        "w_mod": nrm(7, (DEPTH, D, 6 * D), 0.3 * D ** -0.5),
        "b_mod": nrm(8, (DEPTH, 6 * D), 0.02),
        "w_in": nrm(9, (DEPTH, D, D_IN), D ** -0.5),
        "lam_param": nrm(10, (DEPTH, 4, HEAD_DIM), 0.1),
        "subln_w": 1.0 + nrm(11, (DEPTH, V_DIM), 0.02),
        "conv_w": nrm(12, (DEPTH, CONV_WIDTH, D_LRU), CONV_WIDTH ** -0.5),
        "conv_b": nrm(13, (DEPTH, D_LRU), 0.02),
        "gate_w_r": nrm(14, (DEPTH, 2, LRU_BLOCKS, LRU_BW, LRU_BW), LRU_BW ** -0.5),
        "gate_b_r": nrm(15, (DEPTH, 2, D_LRU), 0.02),
        "gate_w_i": nrm(16, (DEPTH, 2, LRU_BLOCKS, LRU_BW, LRU_BW), LRU_BW ** -0.5),
        "gate_b_i": nrm(17, (DEPTH, 2, D_LRU), 0.02),
        "lru_lambda": lru_lambda,
        "w_br_attn": nrm(19, (DEPTH, ATTN_WIDTH, D), ATTN_WIDTH ** -0.5),
        "w_br_lru": nrm(20, (DEPTH, D_LRU, D), D_LRU ** -0.5),
        "w_out": nrm(21, (DEPTH, D, D), BETA * D ** -0.5),
        "ln_g": 1.0 + nrm(22, (DEPTH, 2, D), 0.02),
        "ln_b": nrm(23, (DEPTH, 2, D), 0.02),
        "router_w": nrm(24, (DEPTH, D, N_EXPERTS), D ** -0.5),
        "router_b": nrm(25, (DEPTH, N_EXPERTS), 0.01),
        "exp_w_gate": nrm(26, (DEPTH, N_EXPERTS, D, D_EXPERT), D ** -0.5),
        "exp_w_up": nrm(27, (DEPTH, N_EXPERTS, D, D_EXPERT), D ** -0.5),
        "exp_w_down": nrm(28, (DEPTH, N_EXPERTS, D_EXPERT, D), BETA * D_EXPERT ** -0.5),
        "sh_w_gate": nrm(29, (DEPTH, D, D_EXPERT), D ** -0.5),
        "sh_w_up": nrm(30, (DEPTH, D, D_EXPERT), D ** -0.5),
        "sh_w_down": nrm(31, (DEPTH, D_EXPERT, D), BETA * D_EXPERT ** -0.5),
    }


def reference(x_prompt, x_sample, c, cache_k, cache_v, state_lru, c_ctx, w_mod, b_mod, w_in,
              lam_param, subln_w, conv_w, conv_b, gate_w_r, gate_b_r, gate_w_i, gate_b_i,
              lru_lambda, w_br_attn, w_br_lru, w_out, ln_g, ln_b, router_w, router_b,
              exp_w_gate, exp_w_up, exp_w_down, sh_w_gate, sh_w_up, sh_w_down):
    lat_rope = axial_rope(x_sample.shape[1])
    h0_ctx = jnp.zeros((x_prompt.shape[0], 2, D_LRU), x_prompt.dtype)
    xp, xs = x_prompt, x_sample
    new_k, new_v, new_s = [], [], []
    for l in range(DEPTH):
        lam_init = 0.8 - 0.6 * math.exp(-0.3 * l)
        mix_w = (w_in[l], lam_param[l], lam_init, subln_w[l], conv_w[l], conv_b[l],
                 gate_w_r[l], gate_b_r[l], gate_w_i[l], gate_b_i[l], lru_lambda[l],
                 w_br_attn[l], w_br_lru[l], w_out[l])
        moe_w = (router_w[l], router_b[l], exp_w_gate[l], exp_w_up[l], exp_w_down[l],
                 sh_w_gate[l], sh_w_up[l], sh_w_down[l])
        sh_a, sc_a, gt_a, sh_f, sc_f, gt_f = modulation(c_ctx[None], w_mod[l], b_mod[l])
        out, k_own, v_own, h_fin = token_mixer(xp * (1 + sc_a) + sh_a, *mix_w, None, None, None, h0_ctx)
        xp = layer_norm(ALPHA * xp + gt_a * out, ln_g[l, 0], ln_b[l, 0])
        xp = layer_norm(ALPHA * xp + gt_f * moe(xp * (1 + sc_f) + sh_f, *moe_w), ln_g[l, 1], ln_b[l, 1])
        new_k.append(k_own)
        new_v.append(v_own)
        new_s.append(h_fin)
        sh_a, sc_a, gt_a, sh_f, sc_f, gt_f = modulation(c, w_mod[l], b_mod[l])
        out, _, _, _ = token_mixer(xs * (1 + sc_a) + sh_a, *mix_w, lat_rope,
                                   cache_k[:, l], cache_v[:, l], state_lru[:, l])
        xs = layer_norm(ALPHA * xs + gt_a * out, ln_g[l, 0], ln_b[l, 0])
        xs = layer_norm(ALPHA * xs + gt_f * moe(xs * (1 + sc_f) + sh_f, *moe_w), ln_g[l, 1], ln_b[l, 1])
    return (xp, xs, jnp.stack(new_k, 1), jnp.stack(new_v, 1), jnp.stack(new_s, 1))
```

```python
import functools
import math

import jax
import jax.numpy as jnp
from jax import lax
from jax.experimental import pallas as pl
from jax.experimental.pallas import tpu as pltpu

f32, bf16, i32 = jnp.float32, jnp.bfloat16, jnp.int32

GRID_W = 64
N_HEADS = 8
HEAD_DIM = 64
V_DIM = 2 * HEAD_DIM
ROPE_BASE = 10000.0
LRU_BLOCKS = 16
CONV_WIDTH = 4
LRU_C = 8.0
TOP_K = 8
N_GROUPS = 8
TOPK_GROUPS = 4
ROUTE_SCALE = 2.5
LN_EPS = 1e-5
MOD_ROWS = 8
MOE_BLOCK = 256
DMA_WINDOW = 16
VMEM_LIMIT = 56 * 1024 * 1024
HIGHEST = lax.Precision.HIGHEST


def _cparams(n_axes):
    return pltpu.CompilerParams(dimension_semantics=("arbitrary",) * n_axes,
                                vmem_limit_bytes=VMEM_LIMIT)


def _tile(n, pref):
    t = min(n, pref)
    while n % t:
        t -= 8
    assert t > 0 and n % t == 0
    return t


def _layer_norm(z, g, b):
    mu = jnp.mean(z, -1, keepdims=True)
    zc = z - mu
    var = jnp.mean(zc * zc, -1, keepdims=True)
    return zc * lax.rsqrt(var + LN_EPS) * g + b


def _mod_kernel(c_ref, w_ref, b_ref, o_ref):
    c = c_ref[...]
    a = c * jax.nn.sigmoid(c)
    o_ref[...] = jnp.dot(a, w_ref[...], precision=HIGHEST, preferred_element_type=f32) + b_ref[...]


def _mod_table(cond, w_mod, b_mod):
    depth, d, d6 = w_mod.shape
    tn = _tile(d6, 1024)
    return pl.pallas_call(
        _mod_kernel,
        grid=(depth, d6 // tn),
        in_specs=[pl.BlockSpec((MOD_ROWS, d), lambda l, j: (0, 0)),
                  pl.BlockSpec((None, d, tn), lambda l, j: (l, 0, j)),
                  pl.BlockSpec((None, 1, tn), lambda l, j: (l, 0, j))],
        out_specs=pl.BlockSpec((None, MOD_ROWS, tn), lambda l, j: (l, 0, j)),
        out_shape=jax.ShapeDtypeStruct((depth, MOD_ROWS, d6), f32),
        compiler_params=_cparams(2),
        name="mod_table",
    )(cond, w_mod, b_mod.reshape(depth, 1, d6))


def _inproj_kernel(x_ref, sh_ref, sc_ref, w_ref, cos_ref, sin_ref, o_ref, hb_ref, *,
                   n_ctx_tiles, n_rope_cols, tn):
    i = pl.program_id(0)
    j = pl.program_id(1)

    @pl.when(j == 0)
    def _():
        hb_ref[...] = (x_ref[...] * (1.0 + sc_ref[0]) + sh_ref[0]).astype(bf16)

    acc = jnp.dot(hb_ref[...], w_ref[...], preferred_element_type=f32)
    rope = jnp.logical_and(i >= n_ctx_tiles, j < n_rope_cols)

    @pl.when(rope)
    def _():
        lane = lax.broadcasted_iota(i32, acc.shape, 1)
        first = (lane % (HEAD_DIM // 2)) < (HEAD_DIM // 4)
        partner = jnp.where(first, pltpu.roll(acc, tn - HEAD_DIM // 4, 1),
                            pltpu.roll(acc, HEAD_DIM // 4, 1))
        o_ref[...] = acc * cos_ref[...] + partner * sin_ref[...]

    @pl.when(jnp.logical_not(rope))
    def _():
        o_ref[...] = acc


def _inproj(x, mod4, w_in_b, cos_t, sin_t, l, n_ctx, dec_seq, qk_cols):
    n, d = x.shape
    d_in = w_in_b.shape[-1]
    tm = _tile(math.gcd(n_ctx, dec_seq), 1024)
    tn = cos_t.shape[1]
    n_ctx_tiles = n_ctx // tm
    tiles_per_seq = dec_seq // tm

    def seg(i):
        return jnp.where(i < n_ctx_tiles, 0, 1 + (i - n_ctx_tiles) // tiles_per_seq)

    def pos(i):
        return jnp.where(i < n_ctx_tiles, 0, (i - n_ctx_tiles) % tiles_per_seq)

    base = l * MOD_ROWS * 6
    return pl.pallas_call(
        functools.partial(_inproj_kernel, n_ctx_tiles=n_ctx_tiles, n_rope_cols=qk_cols // tn, tn=tn),
        grid=(n // tm, d_in // tn),
        in_specs=[pl.BlockSpec((tm, d), lambda i, j: (i, 0)),
                  pl.BlockSpec((1, 1, d), lambda i, j: (base + seg(i) * 6 + 0, 0, 0)),
                  pl.BlockSpec((1, 1, d), lambda i, j: (base + seg(i) * 6 + 1, 0, 0)),
                  pl.BlockSpec((None, d, tn), lambda i, j: (l, 0, j)),
                  pl.BlockSpec((tm, tn), lambda i, j: (pos(i), 0)),
                  pl.BlockSpec((tm, tn), lambda i, j: (pos(i), 0))],
        out_specs=pl.BlockSpec((tm, tn), lambda i, j: (i, j)),
        out_shape=jax.ShapeDtypeStruct((n, d_in), f32),
        scratch_shapes=[pltpu.VMEM((tm, d), bf16)],
        compiler_params=_cparams(2),
        name="in_proj",
    )(x, mod4, mod4, w_in_b, cos_t, sin_t)


def _attn_kernel(*refs, has_cache, lam_init, past):
    if has_cache:
        q_ref, k_ref, v_ref, ck_ref, cv_ref, lp_ref, sw_ref, o_ref, kt_ref, vb_ref = refs
    else:
        q_ref, k_ref, v_ref, lp_ref, sw_ref, o_ref, kt_ref, vb_ref = refs

    @pl.when(pl.program_id(2) == 0)
    def _():
        if has_cache:
            kt_ref[:, :past] = ck_ref[...].T.astype(bf16)
            kt_ref[:, past:] = k_ref[...].T.astype(bf16)
            vb_ref[:past, :] = cv_ref[...].astype(bf16)
            vb_ref[past:, :] = v_ref[...].astype(bf16)
        else:
            kt_ref[...] = k_ref[...].T.astype(bf16)
            vb_ref[...] = v_ref[...].astype(bf16)

    tq = q_ref.shape[0]
    q = q_ref[...] * (HEAD_DIM ** -0.5)
    lane = lax.broadcasted_iota(i32, q.shape, 1)
    q2 = jnp.concatenate([jnp.where(lane < HEAD_DIM, q, 0.0).astype(bf16),
                          jnp.where(lane >= HEAD_DIM, q, 0.0).astype(bf16)], axis=0)
    s = jnp.dot(q2, kt_ref[...], preferred_element_type=f32)
    e = jnp.exp(s - jnp.max(s, axis=-1, keepdims=True))
    p = e / jnp.sum(e, axis=-1, keepdims=True)
    lp = lp_ref[...]
    lam = (jnp.exp(jnp.sum(lp[0:1] * lp[1:2], keepdims=True))
           - jnp.exp(jnp.sum(lp[2:3] * lp[3:4], keepdims=True)) + lam_init)
    a = (p[:tq] - lam * p[tq:]).astype(bf16)
    o = jnp.dot(a, vb_ref[...], preferred_element_type=f32)
    o = o * lax.rsqrt(jnp.mean(o * o, -1, keepdims=True) + LN_EPS) * sw_ref[...]
    o_ref[...] = o * (1.0 - lam_init)


def _attention(y, lam_param, subln_w3, l, lam_init, row0, nseq, seq, cache_k=None, cache_v=None):
    has_cache = cache_k is not None
    past = cache_k.shape[2] if has_cache else 0
    lk = past + seq
    tq = _tile(seq, 256)
    nq = seq // tq
    assert row0 % seq == 0
    qb0, kb0 = row0 // tq, row0 // seq
    in_specs = [pl.BlockSpec((tq, V_DIM), lambda b, h, i: (qb0 + b * nq + i, h)),
                pl.BlockSpec((seq, V_DIM), lambda b, h, i: (kb0 + b, N_HEADS + h)),
                pl.BlockSpec((seq, V_DIM), lambda b, h, i: (kb0 + b, 2 * N_HEADS + h))]
    args = [y, y, y]
    if has_cache:
        in_specs += [pl.BlockSpec((None, None, past, V_DIM), lambda b, h, i: (b, l, 0, h)),
                     pl.BlockSpec((None, None, past, V_DIM), lambda b, h, i: (b, l, 0, h))]
        args += [cache_k, cache_v]
    in_specs += [pl.BlockSpec((None, 4, HEAD_DIM), lambda b, h, i: (l, 0, 0)),
                 pl.BlockSpec((None, 1, V_DIM), lambda b, h, i: (l, 0, 0))]
    args += [lam_param, subln_w3]
    return pl.pallas_call(
        functools.partial(_attn_kernel, has_cache=has_cache, lam_init=lam_init, past=past),
        grid=(nseq, N_HEADS, nq),
        in_specs=in_specs,
        out_specs=pl.BlockSpec((tq, V_DIM), lambda b, h, i: (b * nq + i, h)),
        out_shape=jax.ShapeDtypeStruct((nseq * seq, N_HEADS * V_DIM), f32),
        scratch_shapes=[pltpu.VMEM((V_DIM, lk), bf16), pltpu.VMEM((lk, V_DIM), bf16)],
        compiler_params=_cparams(3),
        name="diff_attn_lat" if has_cache else "diff_attn_ctx",
    )(*args)


def _lru_kernel(xr_ref, yr_ref, cw_ref, cb_ref, wg_ref, bg_ref, lam_ref, h0_ref,
                rec_ref, hfin_ref, xp_ref, a_ref, b_ref, *, chunk):
    seq, c = xr_ref.shape
    xp_ref[0:8, :] = jnp.zeros((8, c), f32)
    xp_ref[8:8 + seq, :] = xr_ref[...]
    xp_ref[8 + seq:16 + seq, :] = jnp.zeros((8, c), f32)
    lam = lam_ref[...]
    nlam = -lam
    softplus = jnp.maximum(nlam, 0.0) + jnp.log(1.0 + jnp.exp(-jnp.abs(nlam)))
    for ci in range(seq // chunk):
        t0 = ci * chunk
        xc = cb_ref[...] + sum(cw_ref[j:j + 1, :] * xp_ref[8 + t0 + j - 1:8 + t0 + j - 1 + chunk, :]
                               for j in range(CONV_WIDTH))
        xcb = xc.astype(bf16)
        for d in range(2):
            r = jax.nn.sigmoid(jnp.dot(xcb, wg_ref[2 * d], preferred_element_type=f32)
                               + bg_ref[2 * d:2 * d + 1, :])
            g = jax.nn.sigmoid(jnp.dot(xcb, wg_ref[2 * d + 1], preferred_element_type=f32)
                               + bg_ref[2 * d + 1:2 * d + 2, :])
            log_a = -LRU_C * r * softplus[d:d + 1, :]
            a = jnp.exp(log_a)
            a_ref[d, t0:t0 + chunk, :] = a
            b_ref[d, t0:t0 + chunk, :] = jnp.sqrt(-jnp.tanh(log_a) * (a * a + 1.0)) * (g * xc)

    rows = lax.broadcasted_iota(i32, (8, c), 0)
    nch = seq // 8

    def scan_body(i, carry):
        cf, cb = carry
        tf = pl.multiple_of(8 * i, 8)
        af = a_ref[0, pl.ds(tf, 8), :]
        bf = b_ref[0, pl.ds(tf, 8), :]
        for d in (1, 2, 4):
            keep = rows >= d
            bf = af * jnp.where(keep, pltpu.roll(bf, d, 0), 0.0) + bf
            af = af * jnp.where(keep, pltpu.roll(af, d, 0), 1.0)
        hf = af * cf + bf
        b_ref[0, pl.ds(tf, 8), :] = hf
        tb = pl.multiple_of(8 * (nch - 1 - i), 8)
        ab = a_ref[1, pl.ds(tb, 8), :]
        bb = b_ref[1, pl.ds(tb, 8), :]
        for d in (1, 2, 4):
            keep = rows < 8 - d
            bb = ab * jnp.where(keep, pltpu.roll(bb, 8 - d, 0), 0.0) + bb
            ab = ab * jnp.where(keep, pltpu.roll(ab, 8 - d, 0), 1.0)
        hb = ab * cb + bb
        b_ref[1, pl.ds(tb, 8), :] = hb
        return hf[7:8, :], hb[0:1, :]

    h0 = h0_ref[...]
    cf, cb = lax.fori_loop(0, nch, scan_body, (h0[0:1, :], h0[1:2, :]))
    hfin_ref[0:1, :] = cf
    hfin_ref[1:2, :] = cb
    for ci in range(seq // chunk):
        t0 = ci * chunk
        rec_ref[t0:t0 + chunk, :] = ((b_ref[0, t0:t0 + chunk, :] + b_ref[1, t0:t0 + chunk, :])
                                     * jax.nn.gelu(yr_ref[t0:t0 + chunk, :]))


def _lru(y, conv_w, conv_b3, wg, bg, lru_lambda, h0, l, row0, nseq, seq, xr_col, yr_col):
    d_lru = conv_w.shape[-1]
    cb = 256
    ncb = d_lru // cb
    assert row0 % seq == 0
    rb0 = row0 // seq
    h0_batched = h0.shape[0] > 1
    in_specs = [pl.BlockSpec((seq, cb), lambda b, c: (rb0 + b, xr_col // cb + c)),
                pl.BlockSpec((seq, cb), lambda b, c: (rb0 + b, yr_col // cb + c)),
                pl.BlockSpec((None, CONV_WIDTH, cb), lambda b, c: (l, 0, c)),
                pl.BlockSpec((None, 1, cb), lambda b, c: (l, 0, c)),
                pl.BlockSpec((None, None, 4, cb, cb), lambda b, c: (l, c, 0, 0, 0)),
                pl.BlockSpec((None, 4, cb), lambda b, c: (l, 0, c)),
                pl.BlockSpec((None, 2, cb), lambda b, c: (l, 0, c)),
                pl.BlockSpec((None, 2, cb), (lambda b, c: (b, 0, c)) if h0_batched else (lambda b, c: (0, 0, c)))]
    return pl.pallas_call(
        functools.partial(_lru_kernel, chunk=_tile(seq, 512)),
        grid=(nseq, ncb),
        in_specs=in_specs,
        out_specs=[pl.BlockSpec((seq, cb), lambda b, c: (b, c)),
                   pl.BlockSpec((None, 2, cb), lambda b, c: (b, 0, c))],
        out_shape=[jax.ShapeDtypeStruct((nseq * seq, d_lru), f32),
                   jax.ShapeDtypeStruct((nseq, 2, d_lru), f32)],
        scratch_shapes=[pltpu.VMEM((seq + 16, cb), f32), pltpu.VMEM((2, seq, cb), f32),
                        pltpu.VMEM((2, seq, cb), f32)],
        compiler_params=_cparams(2),
        name="rglru",
    )(y, y, conv_w, conv_b3, wg, bg, lru_lambda, h0)


def _merge_kernel(atc_ref, atl_ref, rcc_ref, rcl_ref, ga_ref, gl_ref, x_ref, gt_ref, shf_ref, scf_ref,
                  g_ref, b_ref, wa_ref, wl_ref, wo_ref, x1_ref, h2_ref, ab_ref, rb_ref, *, alpha, n_ctx_tiles):
    @pl.when(pl.program_id(0) < n_ctx_tiles)
    def _():
        ab_ref[...] = atc_ref[...].astype(bf16)
        rb_ref[...] = rcc_ref[...].astype(bf16)

    @pl.when(pl.program_id(0) >= n_ctx_tiles)
    def _():
        ab_ref[...] = atl_ref[...].astype(bf16)
        rb_ref[...] = rcl_ref[...].astype(bf16)

    pa = jnp.dot(ab_ref[...], wa_ref[...], preferred_element_type=f32)
    pr = jnp.dot(rb_ref[...], wl_ref[...], preferred_element_type=f32)
    merged = jax.nn.sigmoid(ga_ref[...]) * pa + jax.nn.sigmoid(gl_ref[...]) * pr
    out = jnp.dot(merged.astype(bf16), wo_ref[...], preferred_element_type=f32)
    x1 = _layer_norm(alpha * x_ref[...] + gt_ref[0] * out, g_ref[0], b_ref[0])
    x1_ref[...] = x1
    h2_ref[...] = x1 * (1.0 + scf_ref[0]) + shf_ref[0]


def _merge(attn_c, attn_l, rec_c, rec_l, y, x, mod4, ln_g3, ln_b3, wa, wl, wo, l, n_ctx, dec_seq, ga_col, gl_col,
           alpha):
    n, d = x.shape
    tm = _tile(math.gcd(n_ctx, dec_seq), 256)
    n_ctx_tiles = n_ctx // tm
    tiles_per_seq = dec_seq // tm

    def seg(i):
        return jnp.where(i < n_ctx_tiles, 0, 1 + (i - n_ctx_tiles) // tiles_per_seq)

    base = l * MOD_ROWS * 6
    row = lambda i: (i, 0)
    ctx_row = lambda i: (jnp.minimum(i, n_ctx_tiles - 1), 0)
    lat_row = lambda i: (jnp.maximum(i - n_ctx_tiles, 0), 0)
    mod = lambda k: pl.BlockSpec((1, 1, d), lambda i: (base + seg(i) * 6 + k, 0, 0))
    wspec = pl.BlockSpec((None, d, d), lambda i: (l, 0, 0))
    return pl.pallas_call(
        functools.partial(_merge_kernel, alpha=alpha, n_ctx_tiles=n_ctx_tiles),
        grid=(n // tm,),
        in_specs=[pl.BlockSpec((tm, d), ctx_row), pl.BlockSpec((tm, d), lat_row),
                  pl.BlockSpec((tm, d), ctx_row), pl.BlockSpec((tm, d), lat_row),
                  pl.BlockSpec((tm, d), lambda i: (i, ga_col // d)),
                  pl.BlockSpec((tm, d), lambda i: (i, gl_col // d)),
                  pl.BlockSpec((tm, d), row), mod(2), mod(3), mod(4),
                  pl.BlockSpec((1, 1, d), lambda i: (2 * l, 0, 0)),
                  pl.BlockSpec((1, 1, d), lambda i: (2 * l, 0, 0)),
                  wspec, wspec, wspec],
        out_specs=[pl.BlockSpec((tm, d), row), pl.BlockSpec((tm, d), row)],
        out_shape=[jax.ShapeDtypeStruct((n, d), f32), jax.ShapeDtypeStruct((n, d), f32)],
        scratch_shapes=[pltpu.VMEM((tm, d), bf16), pltpu.VMEM((tm, d), bf16)],
        compiler_params=_cparams(1),
        name="merge_outproj_ln",
    )(attn_c, attn_l, rec_c, rec_l, y, y, x, mod4, mod4, mod4, ln_g3, ln_b3, wa, wl, wo)


def _route_kernel(h_ref, rw_ref, rb_ref, eid_ref, pos_ref, gate_ref, cnt_ref, carry_ref):
    n_exp = rw_ref.shape[0]
    tm = h_ref.shape[0]
    gsz = n_exp // N_GROUPS

    @pl.when(pl.program_id(0) == 0)
    def _():
        carry_ref[...] = jnp.zeros_like(carry_ref)

    logits = lax.dot_general(rw_ref[...], h_ref[...], (((1,), (1,)), ((), ())),
                             precision=HIGHEST, preferred_element_type=f32)
    s = jax.nn.sigmoid(logits)
    sb = s + rb_ref[...]
    ninf = -jnp.inf
    gs_rows = []
    for g in range(N_GROUPS):
        blk = sb[g * gsz:(g + 1) * gsz]
        m1 = jnp.max(blk, axis=0, keepdims=True)
        cnt = jnp.sum((blk == m1).astype(f32), axis=0, keepdims=True)
        m2 = jnp.max(jnp.where(blk < m1, blk, ninf), axis=0, keepdims=True)
        gs_rows.append(m1 + jnp.where(cnt >= 2.0, m1, m2))
    gs = jnp.concatenate(gs_rows, axis=0)
    gi = lax.broadcasted_iota(i32, gs.shape, 0)
    grank = jnp.zeros(gs.shape, f32)
    for g2 in range(N_GROUPS):
        row = gs[g2:g2 + 1]
        grank += jnp.where((row > gs) | ((row == gs) & (g2 < gi)), 1.0, 0.0)
    gsel = grank < float(TOPK_GROUPS)
    sbm = jnp.concatenate([jnp.where(gsel[g:g + 1], sb[g * gsz:(g + 1) * gsz], ninf)
                           for g in range(N_GROUPS)], axis=0)
    ei = lax.broadcasted_iota(i32, sbm.shape, 0)
    erank = jnp.zeros(sbm.shape, f32)
    for e2 in range(n_exp):
        row = sbm[e2:e2 + 1]
        erank += jnp.where((row > sbm) | ((row == sbm) & (e2 < ei)), 1.0, 0.0)
    sel = erank < float(TOP_K)
    ssel = jnp.where(sel, s, 0.0)
    gate = ssel / jnp.sum(ssel, axis=0, keepdims=True) * ROUTE_SCALE
    selb = sel.astype(bf16)
    ti = lax.broadcasted_iota(i32, (tm, tm), 0)
    tj = lax.broadcasted_iota(i32, (tm, tm), 1)
    upper = (ti < tj).astype(bf16)
    posf = jnp.dot(selb, upper, preferred_element_type=f32) + carry_ref[:, 0:1]
    new_carry = carry_ref[...] + jnp.sum(sel.astype(f32), axis=1, keepdims=True)
    carry_ref[...] = new_carry
    cnt_ref[...] = new_carry.astype(i32)
    li = lax.broadcasted_iota(i32, (n_exp, n_exp), 0)
    lj = lax.broadcasted_iota(i32, (n_exp, n_exp), 1)
    lower = (lj < li).astype(bf16)
    slot = jnp.dot(lower, selb, preferred_element_type=f32)
    eif = ei.astype(f32)
    e_rows, p_rows, g_rows = [], [], []
    for k in range(TOP_K):
        mk = sel & (slot == float(k))
        e_rows.append(jnp.sum(jnp.where(mk, eif, 0.0), axis=0, keepdims=True))
        p_rows.append(jnp.sum(jnp.where(mk, posf, 0.0), axis=0, keepdims=True))
        g_rows.append(jnp.sum(jnp.where(mk, gate, 0.0), axis=0, keepdims=True))
    eid_ref[...] = jnp.concatenate(e_rows, axis=0).astype(i32)
    pos_ref[...] = jnp.concatenate(p_rows, axis=0).astype(i32)
    gate_ref[...] = jnp.concatenate(g_rows, axis=0)


def _route(h2, rwt, rb3, l):
    n, d = h2.shape
    n_exp = rwt.shape[1]
    tm = _tile(n, 512)
    lst = pl.BlockSpec((TOP_K, tm), lambda i: (0, i))
    return pl.pallas_call(
        _route_kernel,
        grid=(n // tm,),
        in_specs=[pl.BlockSpec((tm, d), lambda i: (i, 0)),
                  pl.BlockSpec((None, n_exp, d), lambda i: (l, 0, 0)),
                  pl.BlockSpec((None, n_exp, 1), lambda i: (l, 0, 0))],
        out_specs=[lst, lst, lst, pl.BlockSpec((n_exp, 128), lambda i: (0, 0))],
        out_shape=[jax.ShapeDtypeStruct((TOP_K, n), i32), jax.ShapeDtypeStruct((TOP_K, n), i32),
                   jax.ShapeDtypeStruct((TOP_K, n), f32), jax.ShapeDtypeStruct((n_exp, 128), i32)],
        scratch_shapes=[pltpu.VMEM((n_exp, 128), f32)],
        compiler_params=_cparams(1),
        name="moe_route",
    )(h2, rwt, rb3)


def _row_copy(src_ref, src_row, dst_ref, dst_row, sem):
    return pltpu.make_async_copy(src_ref.at[pl.ds(src_row, 1)], dst_ref.at[pl.ds(dst_row, 1)], sem)


def _dispatch_kernel(ps_ref, eid_ref, pos_ref, h_ref, xs_ref, sem, *, tm):
    base = pl.program_id(0) * tm

    def wait_token():
        for _ in range(TOP_K):
            _row_copy(h_ref, 0, xs_ref, 0, sem).wait()

    def body(t, _):
        for k in range(TOP_K):
            dst = ps_ref[eid_ref[k, t]] + pos_ref[k, t]
            _row_copy(h_ref, base + t, xs_ref, dst, sem).start()

        @pl.when(t >= DMA_WINDOW)
        def _():
            wait_token()
        return 0

    lax.fori_loop(0, tm, body, 0)

    def drain(t, _):
        wait_token()
        return 0

    lax.fori_loop(0, min(DMA_WINDOW, tm), drain, 0)


def _dispatch(h2, eid, pos, pad_start, n_rows):
    n, d = h2.shape
    tm = _tile(n, 512)
    smem = lambda: pl.BlockSpec((TOP_K, tm), lambda i, ps: (0, i), memory_space=pltpu.SMEM)
    return pl.pallas_call(
        functools.partial(_dispatch_kernel, tm=tm),
        grid_spec=pltpu.PrefetchScalarGridSpec(
            num_scalar_prefetch=1,
            grid=(n // tm,),
            in_specs=[smem(), smem(), pl.BlockSpec(memory_space=pl.ANY)],
            out_specs=pl.BlockSpec(memory_space=pl.ANY),
            scratch_shapes=[pltpu.SemaphoreType.DMA]),
        out_shape=jax.ShapeDtypeStruct((n_rows, d), f32),
        compiler_params=_cparams(1),
        name="moe_dispatch",
    )(pad_start, eid, pos, h2)


def _expert_kernel(be_ref, nu_ref, x_ref, wg_ref, wu_ref, wd_ref, o_ref):
    @pl.when(pl.program_id(0) < nu_ref[0])
    def _():
        x = x_ref[...].astype(bf16)
        g = jnp.dot(x, wg_ref[...].astype(bf16), preferred_element_type=f32)
        u = jnp.dot(x, wu_ref[...].astype(bf16), preferred_element_type=f32)
        h = (g * jax.nn.sigmoid(g)) * u
        o_ref[...] = jnp.dot(h.astype(bf16), wd_ref[...].astype(bf16), preferred_element_type=f32)


def _experts(xs, blk_e, n_used, we_g, we_u, we_d, l):
    n_rows, d = xs.shape
    f = we_g.shape[-1]
    nblk = n_rows // MOE_BLOCK
    blk = lambda b, be, nu: (jnp.minimum(b, nu[0] - 1), 0)
    wsel = lambda b, be, nu: (l, be[jnp.minimum(b, nu[0] - 1)], 0, 0)
    return pl.pallas_call(
        _expert_kernel,
        grid_spec=pltpu.PrefetchScalarGridSpec(
            num_scalar_prefetch=2,
            grid=(nblk,),
            in_specs=[pl.BlockSpec((MOE_BLOCK, d), blk),
                      pl.BlockSpec((None, None, d, f), wsel),
                      pl.BlockSpec((None, None, d, f), wsel),
                      pl.BlockSpec((None, None, f, d), wsel)],
            out_specs=pl.BlockSpec((MOE_BLOCK, d), blk)),
        out_shape=jax.ShapeDtypeStruct((n_rows, d), f32),
        compiler_params=_cparams(1),
        name="moe_experts",
    )(blk_e, n_used, xs, we_g, we_u, we_d)


def _combine_kernel(ps_ref, eid_ref, pos_ref, gate_ref, ys_ref, x1_ref, h2_ref, gt_ref, g_ref, b_ref,
                    sg_ref, su_ref, sd_ref, o_ref, buf_ref, sem, *, alpha):
    tm = x1_ref.shape[0]

    def issue(t, _):
        for k in range(TOP_K):
            src = ps_ref[eid_ref[k, t]] + pos_ref[k, t]
            _row_copy(ys_ref, src, buf_ref.at[k], t, sem).start()
        return 0

    lax.fori_loop(0, tm, issue, 0)
    hb = h2_ref[...].astype(bf16)
    g = jnp.dot(hb, sg_ref[...], preferred_element_type=f32)
    u = jnp.dot(hb, su_ref[...], preferred_element_type=f32)
    shared = jnp.dot(((g * jax.nn.sigmoid(g)) * u).astype(bf16), sd_ref[...], preferred_element_type=f32)
    gate_t = jnp.concatenate([gate_ref[...], jnp.zeros((128 - TOP_K, tm), f32)], axis=0).T

    def drain(t, _):
        for k in range(TOP_K):
            _row_copy(ys_ref, 0, buf_ref.at[k], 0, sem).wait()
        return 0

    lax.fori_loop(0, tm, drain, 0)
    routed = gate_t[:, 0:1] * buf_ref[0]
    for k in range(1, TOP_K):
        routed += gate_t[:, k:k + 1] * buf_ref[k]
    o_ref[...] = _layer_norm(alpha * x1_ref[...] + gt_ref[0] * (shared + routed), g_ref[0], b_ref[0])


def _combine(ys, eid, pos, gate, pad_start, x1, h2, mod4, ln_g3, ln_b3, sg, su, sd, l, n_ctx, dec_seq, alpha):
    n, d = x1.shape
    f = sg.shape[-1]
    tm = _tile(math.gcd(n_ctx, dec_seq), 128)
    assert tm % 128 == 0
    n_ctx_tiles = n_ctx // tm
    tiles_per_seq = dec_seq // tm

    def seg(i):
        return jnp.where(i < n_ctx_tiles, 0, 1 + (i - n_ctx_tiles) // tiles_per_seq)

    base = l * MOD_ROWS * 6
    smem = lambda: pl.BlockSpec((TOP_K, tm), lambda i, ps: (0, i), memory_space=pltpu.SMEM)
    row = lambda i, ps: (i, 0)
    return pl.pallas_call(
        functools.partial(_combine_kernel, alpha=alpha),
        grid_spec=pltpu.PrefetchScalarGridSpec(
            num_scalar_prefetch=1,
            grid=(n // tm,),
            in_specs=[smem(), smem(), pl.BlockSpec((TOP_K, tm), lambda i, ps: (0, i)),
                      pl.BlockSpec(memory_space=pl.ANY),
                      pl.BlockSpec((tm, d), row), pl.BlockSpec((tm, d), row),
                      pl.BlockSpec((1, 1, d), lambda i, ps: (base + seg(i) * 6 + 5, 0, 0)),
                      pl.BlockSpec((1, 1, d), lambda i, ps: (2 * l + 1, 0, 0)),
                      pl.BlockSpec((1, 1, d), lambda i, ps: (2 * l + 1, 0, 0)),
                      pl.BlockSpec((None, d, f), lambda i, ps: (l, 0, 0)),
                      pl.BlockSpec((None, d, f), lambda i, ps: (l, 0, 0)),
                      pl.BlockSpec((None, f, d), lambda i, ps: (l, 0, 0))],
            out_specs=pl.BlockSpec((tm, d), row),
            scratch_shapes=[pltpu.VMEM((TOP_K, tm, d), f32), pltpu.SemaphoreType.DMA]),
        out_shape=jax.ShapeDtypeStruct((n, d), f32),
        compiler_params=_cparams(1),
        name="moe_combine_ln",
    )(pad_start, eid, pos, gate, ys, x1, h2, mod4, ln_g3, ln_b3, sg, su, sd)


def _rope_tables(seq, width):
    rows = seq // GRID_W
    row = jnp.repeat(jnp.arange(rows, dtype=f32), GRID_W)
    col = jnp.tile(jnp.arange(GRID_W, dtype=f32), rows)
    half = HEAD_DIM // 2
    inv = ROPE_BASE ** (-jnp.arange(0, half, 2, dtype=f32) / half)
    lane = jnp.arange(width)
    d = lane % HEAD_DIM
    pos = jnp.where((d // half)[None, :] == 0, row[:, None], col[:, None])
    ang = pos * inv[d % (HEAD_DIM // 4)][None, :]
    sign = jnp.where((d % half) < HEAD_DIM // 4, -1.0, 1.0)[None, :]
    return jnp.cos(ang), jnp.sin(ang) * sign


def _block_diag_gates(gate_w_r, gate_w_i, cb):
    depth, _, nb, bw, _ = gate_w_r.shape
    per = cb // bw
    w = jnp.stack([gate_w_r[:, 0], gate_w_i[:, 0], gate_w_r[:, 1], gate_w_i[:, 1]], axis=1)
    w = w.reshape(depth, 4, nb // per, per, bw, bw)
    eye = jnp.eye(per, dtype=w.dtype)
    bd = jnp.einsum("lgcaio,ab->lcgaibo", w, eye)
    return bd.reshape(depth, nb // per, 4, cb, cb).astype(bf16)


def kernel(x_prompt, x_sample, c, cache_k, cache_v, state_lru, c_ctx, w_mod, b_mod, w_in, lam_param, subln_w,
           conv_w, conv_b, gate_w_r, gate_b_r, gate_w_i, gate_b_i, lru_lambda, w_br_attn, w_br_lru, w_out,
           ln_g, ln_b, router_w, router_b, exp_w_gate, exp_w_up, exp_w_down, sh_w_gate, sh_w_up, sh_w_down):
    batch, seq, d = x_prompt.shape
    dec_batch, dec_seq, _ = x_sample.shape
    depth = w_mod.shape[0]
    past = cache_k.shape[2]
    d_lru = conv_w.shape[-1]
    n_exp = router_w.shape[-1]
    qk = N_HEADS * 2 * HEAD_DIM
    attn_w = N_HEADS * V_DIM
    n_ctx, n_lat = batch * seq, dec_batch * dec_seq
    n = n_ctx + n_lat
    alpha = (2 * depth) ** 0.25
    assert 1 + dec_batch <= MOD_ROWS and n_ctx % dec_seq == 0

    cond = jnp.concatenate([c_ctx[None], c, jnp.zeros((MOD_ROWS - 1 - dec_batch, d), f32)], 0)
    w_in_b = w_in.astype(bf16)
    wa, wl, wo = w_br_attn.astype(bf16), w_br_lru.astype(bf16), w_out.astype(bf16)
    sg, su, sd = sh_w_gate.astype(bf16), sh_w_up.astype(bf16), sh_w_down.astype(bf16)
    rwt = jnp.swapaxes(router_w, 1, 2)
    rb3 = router_b.reshape(depth, n_exp, 1)
    wg = _block_diag_gates(gate_w_r, gate_w_i, 256)
    bg = jnp.stack([gate_b_r[:, 0], gate_b_i[:, 0], gate_b_r[:, 1], gate_b_i[:, 1]], axis=1)
    conv_b3 = conv_b.reshape(depth, 1, d_lru)
    subln_w3 = subln_w.reshape(depth, 1, V_DIM)
    ln_g3 = ln_g.reshape(depth * 2, 1, d)
    ln_b3 = ln_b.reshape(depth * 2, 1, d)
    ck = cache_k.reshape(dec_batch, depth, past, N_HEADS * 2 * HEAD_DIM)
    cv = cache_v.reshape(dec_batch, depth, past, attn_w)
    cos_t, sin_t = _rope_tables(dec_seq, 512)
    h0_ctx = jnp.zeros((1, 2, d_lru), f32)

    mod4 = _mod_table(cond, w_mod, b_mod).reshape(depth * MOD_ROWS * 6, 1, d)
    x = jnp.concatenate([x_prompt.reshape(n_ctx, d), x_sample.reshape(n_lat, d)], 0)
    n_rows = -(-(n * TOP_K + n_exp * (MOE_BLOCK - 1)) // MOE_BLOCK) * MOE_BLOCK
    xr_col, yr_col = 2 * qk + attn_w, 2 * qk + attn_w + d_lru
    ga_col, gl_col = yr_col + d_lru, yr_col + d_lru + d

    new_k, new_v, new_s = [], [], []
    for l in range(depth):
        lam_init = 0.8 - 0.6 * math.exp(-0.3 * l)
        y = _inproj(x, mod4, w_in_b, cos_t, sin_t, l, n_ctx, dec_seq, 2 * qk)
        new_k.append(y[:n_ctx, qk:2 * qk].reshape(batch, seq, N_HEADS, 2 * HEAD_DIM))
        new_v.append(y[:n_ctx, 2 * qk:2 * qk + attn_w].reshape(batch, seq, N_HEADS, V_DIM))
        attn_c = _attention(y, lam_param, subln_w3, l, lam_init, 0, batch, seq)
        attn_l = _attention(y, lam_param, subln_w3, l, lam_init, n_ctx, dec_batch, dec_seq, ck, cv)
        rec_c, hfin = _lru(y, conv_w, conv_b3, wg, bg, lru_lambda, h0_ctx, l, 0, batch, seq, xr_col, yr_col)
        rec_l, _ = _lru(y, conv_w, conv_b3, wg, bg, lru_lambda, state_lru[:, l], l, n_ctx, dec_batch, dec_seq,
                        xr_col, yr_col)
        new_s.append(hfin)
        x1, h2 = _merge(attn_c, attn_l, rec_c, rec_l, y, x, mod4, ln_g3, ln_b3, wa, wl, wo, l, n_ctx, dec_seq, ga_col, gl_col, alpha)
        eid, pos, gate, cnt = _route(h2, rwt, rb3, l)
        counts = cnt[:, 0]
        padded = (counts + MOE_BLOCK - 1) // MOE_BLOCK * MOE_BLOCK
        pad_end = jnp.cumsum(padded)
        pad_start = (pad_end - padded).astype(i32)
        blk_e = jnp.minimum(jnp.searchsorted(pad_end, jnp.arange(n_rows // MOE_BLOCK, dtype=i32) * MOE_BLOCK,
                                             side="right"), n_exp - 1).astype(i32)
        n_used = (pad_end[-1:] // MOE_BLOCK).astype(i32)
        xs = _dispatch(h2, eid, pos, pad_start, n_rows)
        ys = _experts(xs, blk_e, n_used, exp_w_gate, exp_w_up, exp_w_down, l)
        x = _combine(ys, eid, pos, gate, pad_start, x1, h2, mod4, ln_g3, ln_b3, sg, su, sd, l, n_ctx, dec_seq,
                     alpha)
    return (x[:n_ctx].reshape(batch, seq, d), x[n_ctx:].reshape(dec_batch, dec_seq, d),
            jnp.stack(new_k, 1), jnp.stack(new_v, 1), jnp.stack(new_s, 1))
```

```python
import functools
import math

import jax
import jax.numpy as jnp
from jax import lax
from jax.experimental import pallas as pl
from jax.experimental.pallas import tpu as pltpu

f32, bf16, i32 = jnp.float32, jnp.bfloat16, jnp.int32

GRID_W = 64
N_HEADS = 8
HEAD_DIM = 64
V_DIM = 2 * HEAD_DIM
ROPE_BASE = 10000.0
LRU_BLOCKS = 16
CONV_WIDTH = 4
LRU_C = 8.0
TOP_K = 8
N_GROUPS = 8
TOPK_GROUPS = 4
ROUTE_SCALE = 2.5
LN_EPS = 1e-5
MOD_ROWS = 8
MOE_BLOCK = 256
DMA_WINDOW = 64
VMEM_LIMIT = 56 * 1024 * 1024
HIGHEST = lax.Precision.HIGHEST


def _cparams(n_axes):
    return pltpu.CompilerParams(dimension_semantics=("arbitrary",) * n_axes,
                                vmem_limit_bytes=VMEM_LIMIT)


def _tile(n, pref):
    t = min(n, pref)
    while n % t:
        t -= 8
    assert t > 0 and n % t == 0
    return t


def _layer_norm(z, g, b):
    mu = jnp.mean(z, -1, keepdims=True)
    zc = z - mu
    var = jnp.mean(zc * zc, -1, keepdims=True)
    return zc * lax.rsqrt(var + LN_EPS) * g + b


def _mod_kernel(c_ref, w_ref, b_ref, o_ref):
    c = c_ref[...]
    a = c * jax.nn.sigmoid(c)
    o_ref[...] = jnp.dot(a, w_ref[...], precision=HIGHEST, preferred_element_type=f32) + b_ref[...]


def _mod_table(cond, w_mod, b_mod):
    depth, d, d6 = w_mod.shape
    tn = _tile(d6, 1024)
    return pl.pallas_call(
        _mod_kernel,
        grid=(depth, d6 // tn),
        in_specs=[pl.BlockSpec((MOD_ROWS, d), lambda l, j: (0, 0)),
                  pl.BlockSpec((None, d, tn), lambda l, j: (l, 0, j)),
                  pl.BlockSpec((None, 1, tn), lambda l, j: (l, 0, j))],
        out_specs=pl.BlockSpec((None, MOD_ROWS, tn), lambda l, j: (l, 0, j)),
        out_shape=jax.ShapeDtypeStruct((depth, MOD_ROWS, d6), f32),
        compiler_params=_cparams(2),
        name="mod_table",
    )(cond, w_mod, b_mod.reshape(depth, 1, d6))


def _inproj_kernel(x_ref, sh_ref, sc_ref, w_ref, cos_ref, sin_ref, o_ref, hb_ref, *,
                   n_ctx_tiles, n_rope_cols, tn):
    i = pl.program_id(0)
    j = pl.program_id(1)

    @pl.when(j == 0)
    def _():
        hb_ref[...] = (x_ref[...] * (1.0 + sc_ref[0]) + sh_ref[0]).astype(bf16)

    acc = jnp.dot(hb_ref[...], w_ref[...], preferred_element_type=f32)
    rope = jnp.logical_and(i >= n_ctx_tiles, j < n_rope_cols)

    @pl.when(rope)
    def _():
        lane = lax.broadcasted_iota(i32, acc.shape, 1)
        first = (lane % (HEAD_DIM // 2)) < (HEAD_DIM // 4)
        partner = jnp.where(first, pltpu.roll(acc, tn - HEAD_DIM // 4, 1),
                            pltpu.roll(acc, HEAD_DIM // 4, 1))
        o_ref[...] = acc * cos_ref[...] + partner * sin_ref[...]

    @pl.when(jnp.logical_not(rope))
    def _():
        o_ref[...] = acc


def _inproj(x, mod4, w_in_b, cos_t, sin_t, l, n_ctx, dec_seq, qk_cols):
    n, d = x.shape
    d_in = w_in_b.shape[-1]
    tm = _tile(math.gcd(n_ctx, dec_seq), 1024)
    tn = cos_t.shape[1]
    n_ctx_tiles = n_ctx // tm
    tiles_per_seq = dec_seq // tm

    def seg(i):
        return jnp.where(i < n_ctx_tiles, 0, 1 + (i - n_ctx_tiles) // tiles_per_seq)

    def pos(i):
        return jnp.where(i < n_ctx_tiles, 0, (i - n_ctx_tiles) % tiles_per_seq)

    base = l * MOD_ROWS * 6
    return pl.pallas_call(
        functools.partial(_inproj_kernel, n_ctx_tiles=n_ctx_tiles, n_rope_cols=qk_cols // tn, tn=tn),
        grid=(n // tm, d_in // tn),
        in_specs=[pl.BlockSpec((tm, d), lambda i, j: (i, 0)),
                  pl.BlockSpec((1, 1, d), lambda i, j: (base + seg(i) * 6 + 0, 0, 0)),
                  pl.BlockSpec((1, 1, d), lambda i, j: (base + seg(i) * 6 + 1, 0, 0)),
                  pl.BlockSpec((None, d, tn), lambda i, j: (l, 0, j)),
                  pl.BlockSpec((tm, tn), lambda i, j: (pos(i), 0)),
                  pl.BlockSpec((tm, tn), lambda i, j: (pos(i), 0))],
        out_specs=pl.BlockSpec((tm, tn), lambda i, j: (i, j)),
        out_shape=jax.ShapeDtypeStruct((n, d_in), f32),
        scratch_shapes=[pltpu.VMEM((tm, d), bf16)],
        compiler_params=_cparams(2),
        name="in_proj",
    )(x, mod4, mod4, w_in_b, cos_t, sin_t)


def _attn_kernel(*refs, has_cache, lam_init, past):
    if has_cache:
        q_ref, k_ref, v_ref, ck_ref, cv_ref, lp_ref, sw_ref, o_ref, kt_ref, vb_ref = refs
    else:
        q_ref, k_ref, v_ref, lp_ref, sw_ref, o_ref, kt_ref, vb_ref = refs

    @pl.when(pl.program_id(2) == 0)
    def _():
        if has_cache:
            kt_ref[:, :past] = ck_ref[...].T.astype(bf16)
            kt_ref[:, past:] = k_ref[...].T.astype(bf16)
            vb_ref[:past, :] = cv_ref[...].astype(bf16)
            vb_ref[past:, :] = v_ref[...].astype(bf16)
        else:
            kt_ref[...] = k_ref[...].T.astype(bf16)
            vb_ref[...] = v_ref[...].astype(bf16)

    tq = q_ref.shape[0]
    q = q_ref[...] * (HEAD_DIM ** -0.5)
    lane = lax.broadcasted_iota(i32, q.shape, 1)
    q2 = jnp.concatenate([jnp.where(lane < HEAD_DIM, q, 0.0).astype(bf16),
                          jnp.where(lane >= HEAD_DIM, q, 0.0).astype(bf16)], axis=0)
    s = jnp.dot(q2, kt_ref[...], preferred_element_type=f32)
    e = jnp.exp(s - jnp.max(s, axis=-1, keepdims=True))
    p = e / jnp.sum(e, axis=-1, keepdims=True)
    lp = lp_ref[...]
    lam = (jnp.exp(jnp.sum(lp[0:1] * lp[1:2], keepdims=True))
           - jnp.exp(jnp.sum(lp[2:3] * lp[3:4], keepdims=True)) + lam_init)
    a = (p[:tq] - lam * p[tq:]).astype(bf16)
    o = jnp.dot(a, vb_ref[...], preferred_element_type=f32)
    o = o * lax.rsqrt(jnp.mean(o * o, -1, keepdims=True) + LN_EPS) * sw_ref[...]
    o_ref[...] = o * (1.0 - lam_init)


def _attention(y, lam_param, subln_w3, l, lam_init, row0, nseq, seq, cache_k=None, cache_v=None):
    has_cache = cache_k is not None
    past = cache_k.shape[2] if has_cache else 0
    lk = past + seq
    tq = _tile(seq, 256)
    nq = seq // tq
    assert row0 % seq == 0
    qb0, kb0 = row0 // tq, row0 // seq
    in_specs = [pl.BlockSpec((tq, V_DIM), lambda b, h, i: (qb0 + b * nq + i, h)),
                pl.BlockSpec((seq, V_DIM), lambda b, h, i: (kb0 + b, N_HEADS + h)),
                pl.BlockSpec((seq, V_DIM), lambda b, h, i: (kb0 + b, 2 * N_HEADS + h))]
    args = [y, y, y]
    if has_cache:
        in_specs += [pl.BlockSpec((None, None, past, V_DIM), lambda b, h, i: (b, l, 0, h)),
                     pl.BlockSpec((None, None, past, V_DIM), lambda b, h, i: (b, l, 0, h))]
        args += [cache_k, cache_v]
    in_specs += [pl.BlockSpec((None, 4, HEAD_DIM), lambda b, h, i: (l, 0, 0)),
                 pl.BlockSpec((None, 1, V_DIM), lambda b, h, i: (l, 0, 0))]
    args += [lam_param, subln_w3]
    return pl.pallas_call(
        functools.partial(_attn_kernel, has_cache=has_cache, lam_init=lam_init, past=past),
        grid=(nseq, N_HEADS, nq),
        in_specs=in_specs,
        out_specs=pl.BlockSpec((tq, V_DIM), lambda b, h, i: (b * nq + i, h)),
        out_shape=jax.ShapeDtypeStruct((nseq * seq, N_HEADS * V_DIM), f32),
        scratch_shapes=[pltpu.VMEM((V_DIM, lk), bf16), pltpu.VMEM((lk, V_DIM), bf16)],
        compiler_params=_cparams(3),
        name="diff_attn_lat" if has_cache else "diff_attn_ctx",
    )(*args)


def _lru_kernel(xr_ref, yr_ref, cw_ref, cb_ref, wg_ref, bg_ref, lam_ref, h0_ref,
                rec_ref, hfin_ref, xp_ref, a_ref, b_ref, *, chunk):
    seq, c = xr_ref.shape
    xp_ref[0:8, :] = jnp.zeros((8, c), f32)
    xp_ref[8:8 + seq, :] = xr_ref[...]
    xp_ref[8 + seq:16 + seq, :] = jnp.zeros((8, c), f32)
    lam = lam_ref[...]
    nlam = -lam
    softplus = jnp.maximum(nlam, 0.0) + jnp.log(1.0 + jnp.exp(-jnp.abs(nlam)))
    for ci in range(seq // chunk):
        t0 = ci * chunk
        xc = cb_ref[...] + sum(cw_ref[j:j + 1, :] * xp_ref[8 + t0 + j - 1:8 + t0 + j - 1 + chunk, :]
                               for j in range(CONV_WIDTH))
        xcb = xc.astype(bf16)
        for d in range(2):
            r = jax.nn.sigmoid(jnp.dot(xcb, wg_ref[2 * d], preferred_element_type=f32)
                               + bg_ref[2 * d:2 * d + 1, :])
            g = jax.nn.sigmoid(jnp.dot(xcb, wg_ref[2 * d + 1], preferred_element_type=f32)
                               + bg_ref[2 * d + 1:2 * d + 2, :])
            log_a = -LRU_C * r * softplus[d:d + 1, :]
            a = jnp.exp(log_a)
            a_ref[d, t0:t0 + chunk, :] = a
            b_ref[d, t0:t0 + chunk, :] = jnp.sqrt(-jnp.tanh(log_a) * (a * a + 1.0)) * (g * xc)

    rows = lax.broadcasted_iota(i32, (8, c), 0)
    nch = seq // 8

    def scan_body(i, carry):
        cf, cb = carry
        tf = pl.multiple_of(8 * i, 8)
        af = a_ref[0, pl.ds(tf, 8), :]
        bf = b_ref[0, pl.ds(tf, 8), :]
        for d in (1, 2, 4):
            keep = rows >= d
            bf = af * jnp.where(keep, pltpu.roll(bf, d, 0), 0.0) + bf
            af = af * jnp.where(keep, pltpu.roll(af, d, 0), 1.0)
        hf = af * cf + bf
        b_ref[0, pl.ds(tf, 8), :] = hf
        tb = pl.multiple_of(8 * (nch - 1 - i), 8)
        ab = a_ref[1, pl.ds(tb, 8), :]
        bb = b_ref[1, pl.ds(tb, 8), :]
        for d in (1, 2, 4):
            keep = rows < 8 - d
            bb = ab * jnp.where(keep, pltpu.roll(bb, 8 - d, 0), 0.0) + bb
            ab = ab * jnp.where(keep, pltpu.roll(ab, 8 - d, 0), 1.0)
        hb = ab * cb + bb
        b_ref[1, pl.ds(tb, 8), :] = hb
        return hf[7:8, :], hb[0:1, :]

    h0 = h0_ref[...]
    cf, cb = lax.fori_loop(0, nch, scan_body, (h0[0:1, :], h0[1:2, :]))
    hfin_ref[0:1, :] = cf
    hfin_ref[1:2, :] = cb
    for ci in range(seq // chunk):
        t0 = ci * chunk
        rec_ref[t0:t0 + chunk, :] = ((b_ref[0, t0:t0 + chunk, :] + b_ref[1, t0:t0 + chunk, :])
                                     * jax.nn.gelu(yr_ref[t0:t0 + chunk, :]))


def _lru(y, conv_w, conv_b3, wg, bg, lru_lambda, h0, l, row0, nseq, seq, xr_col, yr_col):
    d_lru = conv_w.shape[-1]
    cb = 256
    ncb = d_lru // cb
    assert row0 % seq == 0
    rb0 = row0 // seq
    h0_batched = h0.shape[0] > 1
    in_specs = [pl.BlockSpec((seq, cb), lambda b, c: (rb0 + b, xr_col // cb + c)),
                pl.BlockSpec((seq, cb), lambda b, c: (rb0 + b, yr_col // cb + c)),
                pl.BlockSpec((None, CONV_WIDTH, cb), lambda b, c: (l, 0, c)),
                pl.BlockSpec((None, 1, cb), lambda b, c: (l, 0, c)),
                pl.BlockSpec((None, None, 4, cb, cb), lambda b, c: (l, c, 0, 0, 0)),
                pl.BlockSpec((None, 4, cb), lambda b, c: (l, 0, c)),
                pl.BlockSpec((None, 2, cb), lambda b, c: (l, 0, c)),
                pl.BlockSpec((None, 2, cb), (lambda b, c: (b, 0, c)) if h0_batched else (lambda b, c: (0, 0, c)))]
    return pl.pallas_call(
        functools.partial(_lru_kernel, chunk=_tile(seq, 512)),
        grid=(nseq, ncb),
        in_specs=in_specs,
        out_specs=[pl.BlockSpec((seq, cb), lambda b, c: (b, c)),
                   pl.BlockSpec((None, 2, cb), lambda b, c: (b, 0, c))],
        out_shape=[jax.ShapeDtypeStruct((nseq * seq, d_lru), f32),
                   jax.ShapeDtypeStruct((nseq, 2, d_lru), f32)],
        scratch_shapes=[pltpu.VMEM((seq + 16, cb), f32), pltpu.VMEM((2, seq, cb), f32),
                        pltpu.VMEM((2, seq, cb), f32)],
        compiler_params=_cparams(2),
        name="rglru",
    )(y, y, conv_w, conv_b3, wg, bg, lru_lambda, h0)


def _merge_kernel(atc_ref, atl_ref, rcc_ref, rcl_ref, ga_ref, gl_ref, x_ref, gt_ref, shf_ref, scf_ref,
                  g_ref, b_ref, wa_ref, wl_ref, wo_ref, x1_ref, h2_ref, ab_ref, rb_ref, *, alpha, n_ctx_tiles):
    @pl.when(pl.program_id(0) < n_ctx_tiles)
    def _():
        ab_ref[...] = atc_ref[...].astype(bf16)
        rb_ref[...] = rcc_ref[...].astype(bf16)

    @pl.when(pl.program_id(0) >= n_ctx_tiles)
    def _():
        ab_ref[...] = atl_ref[...].astype(bf16)
        rb_ref[...] = rcl_ref[...].astype(bf16)

    pa = jnp.dot(ab_ref[...], wa_ref[...], preferred_element_type=f32)
    pr = jnp.dot(rb_ref[...], wl_ref[...], preferred_element_type=f32)
    merged = jax.nn.sigmoid(ga_ref[...]) * pa + jax.nn.sigmoid(gl_ref[...]) * pr
    out = jnp.dot(merged.astype(bf16), wo_ref[...], preferred_element_type=f32)
    x1 = _layer_norm(alpha * x_ref[...] + gt_ref[0] * out, g_ref[0], b_ref[0])
    x1_ref[...] = x1
    h2_ref[...] = x1 * (1.0 + scf_ref[0]) + shf_ref[0]


def _merge(attn_c, attn_l, rec_c, rec_l, y, x, mod4, ln_g3, ln_b3, wa, wl, wo, l, n_ctx, dec_seq, ga_col, gl_col,
           alpha):
    n, d = x.shape
    tm = _tile(math.gcd(n_ctx, dec_seq), 256)
    n_ctx_tiles = n_ctx // tm
    tiles_per_seq = dec_seq // tm

    def seg(i):
        return jnp.where(i < n_ctx_tiles, 0, 1 + (i - n_ctx_tiles) // tiles_per_seq)

    base = l * MOD_ROWS * 6
    row = lambda i: (i, 0)
    ctx_row = lambda i: (jnp.minimum(i, n_ctx_tiles - 1), 0)
    lat_row = lambda i: (jnp.maximum(i - n_ctx_tiles, 0), 0)
    mod = lambda k: pl.BlockSpec((1, 1, d), lambda i: (base + seg(i) * 6 + k, 0, 0))
    wspec = pl.BlockSpec((None, d, d), lambda i: (l, 0, 0))
    return pl.pallas_call(
        functools.partial(_merge_kernel, alpha=alpha, n_ctx_tiles=n_ctx_tiles),
        grid=(n // tm,),
        in_specs=[pl.BlockSpec((tm, d), ctx_row), pl.BlockSpec((tm, d), lat_row),
                  pl.BlockSpec((tm, d), ctx_row), pl.BlockSpec((tm, d), lat_row),
                  pl.BlockSpec((tm, d), lambda i: (i, ga_col // d)),
                  pl.BlockSpec((tm, d), lambda i: (i, gl_col // d)),
                  pl.BlockSpec((tm, d), row), mod(2), mod(3), mod(4),
                  pl.BlockSpec((1, 1, d), lambda i: (2 * l, 0, 0)),
                  pl.BlockSpec((1, 1, d), lambda i: (2 * l, 0, 0)),
                  wspec, wspec, wspec],
        out_specs=[pl.BlockSpec((tm, d), row), pl.BlockSpec((tm, d), row)],
        out_shape=[jax.ShapeDtypeStruct((n, d), f32), jax.ShapeDtypeStruct((n, d), f32)],
        scratch_shapes=[pltpu.VMEM((tm, d), bf16), pltpu.VMEM((tm, d), bf16)],
        compiler_params=_cparams(1),
        name="merge_outproj_ln",
    )(attn_c, attn_l, rec_c, rec_l, y, y, x, mod4, mod4, mod4, ln_g3, ln_b3, wa, wl, wo)


def _route_kernel(h_ref, rw_ref, rb_ref, eid_ref, pos_ref, gate_ref, cnt_ref, carry_ref):
    n_exp = rw_ref.shape[0]
    tm = h_ref.shape[0]
    gsz = n_exp // N_GROUPS

    @pl.when(pl.program_id(0) == 0)
    def _():
        carry_ref[...] = jnp.zeros_like(carry_ref)

    logits = lax.dot_general(rw_ref[...], h_ref[...], (((1,), (1,)), ((), ())),
                             precision=HIGHEST, preferred_element_type=f32)
    s = jax.nn.sigmoid(logits)
    sb = s + rb_ref[...]
    ninf = -jnp.inf
    gs_rows = []
    for g in range(N_GROUPS):
        blk = sb[g * gsz:(g + 1) * gsz]
        m1 = jnp.max(blk, axis=0, keepdims=True)
        cnt = jnp.sum((blk == m1).astype(f32), axis=0, keepdims=True)
        m2 = jnp.max(jnp.where(blk < m1, blk, ninf), axis=0, keepdims=True)
        gs_rows.append(m1 + jnp.where(cnt >= 2.0, m1, m2))
    gs = jnp.concatenate(gs_rows, axis=0)
    gi = lax.broadcasted_iota(i32, gs.shape, 0)
    grank = jnp.zeros(gs.shape, f32)
    for g2 in range(N_GROUPS):
        row = gs[g2:g2 + 1]
        grank += jnp.where((row > gs) | ((row == gs) & (g2 < gi)), 1.0, 0.0)
    gsel = grank < float(TOPK_GROUPS)
    sbm = jnp.concatenate([jnp.where(gsel[g:g + 1], sb[g * gsz:(g + 1) * gsz], ninf)
                           for g in range(N_GROUPS)], axis=0)
    ei = lax.broadcasted_iota(i32, sbm.shape, 0)
    erank = jnp.zeros(sbm.shape, f32)
    for e2 in range(n_exp):
        row = sbm[e2:e2 + 1]
        erank += jnp.where((row > sbm) | ((row == sbm) & (e2 < ei)), 1.0, 0.0)
    sel = erank < float(TOP_K)
    ssel = jnp.where(sel, s, 0.0)
    gate = ssel / jnp.sum(ssel, axis=0, keepdims=True) * ROUTE_SCALE
    selb = sel.astype(bf16)
    ti = lax.broadcasted_iota(i32, (tm, tm), 0)
    tj = lax.broadcasted_iota(i32, (tm, tm), 1)
    upper = (ti < tj).astype(bf16)
    posf = jnp.dot(selb, upper, preferred_element_type=f32) + carry_ref[:, 0:1]
    new_carry = carry_ref[...] + jnp.sum(sel.astype(f32), axis=1, keepdims=True)
    carry_ref[...] = new_carry
    cnt_ref[...] = new_carry.astype(i32)
    li = lax.broadcasted_iota(i32, (n_exp, n_exp), 0)
    lj = lax.broadcasted_iota(i32, (n_exp, n_exp), 1)
    lower = (lj < li).astype(bf16)
    slot = jnp.dot(lower, selb, preferred_element_type=f32)
    eif = ei.astype(f32)
    e_rows, p_rows, g_rows = [], [], []
    for k in range(TOP_K):
        mk = sel & (slot == float(k))
        e_rows.append(jnp.sum(jnp.where(mk, eif, 0.0), axis=0, keepdims=True))
        p_rows.append(jnp.sum(jnp.where(mk, posf, 0.0), axis=0, keepdims=True))
        g_rows.append(jnp.sum(jnp.where(mk, gate, 0.0), axis=0, keepdims=True))
    eid_ref[...] = jnp.concatenate(e_rows, axis=0).astype(i32)
    pos_ref[...] = jnp.concatenate(p_rows, axis=0).astype(i32)
    gate_ref[...] = jnp.concatenate(g_rows, axis=0)


def _route(h2, rwt, rb3, l):
    n, d = h2.shape
    n_exp = rwt.shape[1]
    tm = _tile(n, 512)
    lst = pl.BlockSpec((TOP_K, tm), lambda i: (0, i))
    return pl.pallas_call(
        _route_kernel,
        grid=(n // tm,),
        in_specs=[pl.BlockSpec((tm, d), lambda i: (i, 0)),
                  pl.BlockSpec((None, n_exp, d), lambda i: (l, 0, 0)),
                  pl.BlockSpec((None, n_exp, 1), lambda i: (l, 0, 0))],
        out_specs=[lst, lst, lst, pl.BlockSpec((n_exp, 128), lambda i: (0, 0))],
        out_shape=[jax.ShapeDtypeStruct((TOP_K, n), i32), jax.ShapeDtypeStruct((TOP_K, n), i32),
                   jax.ShapeDtypeStruct((TOP_K, n), f32), jax.ShapeDtypeStruct((n_exp, 128), i32)],
        scratch_shapes=[pltpu.VMEM((n_exp, 128), f32)],
        compiler_params=_cparams(1),
        name="moe_route",
    )(h2, rwt, rb3)


def _row_copy(src_ref, src_row, dst_ref, dst_row, sem):
    return pltpu.make_async_copy(src_ref.at[pl.ds(src_row, 1)], dst_ref.at[pl.ds(dst_row, 1)], sem)


def _dispatch_kernel(ps_ref, eid_ref, pos_ref, h_ref, xs_ref, sem, *, tm):
    def wait_token():
        for _ in range(TOP_K):
            _row_copy(h_ref, 0, xs_ref, 0, sem).wait()

    def body(t, _):
        for k in range(TOP_K):
            dst = ps_ref[eid_ref[k, t]] + pos_ref[k, t]
            _row_copy(h_ref, t, xs_ref, dst, sem).start()

        @pl.when(t >= DMA_WINDOW)
        def _():
            wait_token()
        return 0

    lax.fori_loop(0, tm, body, 0)

    def drain(t, _):
        wait_token()
        return 0

    lax.fori_loop(0, min(DMA_WINDOW, tm), drain, 0)


def _dispatch(h2, eid, pos, pad_start, n_rows):
    n, d = h2.shape
    tm = _tile(n, 512)
    smem = lambda: pl.BlockSpec((TOP_K, tm), lambda i, ps: (0, i), memory_space=pltpu.SMEM)
    return pl.pallas_call(
        functools.partial(_dispatch_kernel, tm=tm),
        grid_spec=pltpu.PrefetchScalarGridSpec(
            num_scalar_prefetch=1,
            grid=(n // tm,),
            in_specs=[smem(), smem(), pl.BlockSpec((tm, d), lambda i, ps: (i, 0))],
            out_specs=pl.BlockSpec(memory_space=pl.ANY),
            scratch_shapes=[pltpu.SemaphoreType.DMA]),
        out_shape=jax.ShapeDtypeStruct((n_rows, d), f32),
        compiler_params=_cparams(1),
        name="moe_dispatch",
    )(pad_start, eid, pos, h2)


def _expert_kernel(be_ref, nu_ref, x_ref, wg_ref, wu_ref, wd_ref, o_ref):
    @pl.when(pl.program_id(0) < nu_ref[0])
    def _():
        x = x_ref[...].astype(bf16)
        g = jnp.dot(x, wg_ref[...].astype(bf16), preferred_element_type=f32)
        u = jnp.dot(x, wu_ref[...].astype(bf16), preferred_element_type=f32)
        h = (g * jax.nn.sigmoid(g)) * u
        o_ref[...] = jnp.dot(h.astype(bf16), wd_ref[...].astype(bf16), preferred_element_type=f32)


def _experts(xs, blk_e, n_used, we_g, we_u, we_d, l):
    n_rows, d = xs.shape
    f = we_g.shape[-1]
    nblk = n_rows // MOE_BLOCK
    blk = lambda b, be, nu: (jnp.minimum(b, nu[0] - 1), 0)
    wsel = lambda b, be, nu: (l, be[jnp.minimum(b, nu[0] - 1)], 0, 0)
    return pl.pallas_call(
        _expert_kernel,
        grid_spec=pltpu.PrefetchScalarGridSpec(
            num_scalar_prefetch=2,
            grid=(nblk,),
            in_specs=[pl.BlockSpec((MOE_BLOCK, d), blk),
                      pl.BlockSpec((None, None, d, f), wsel),
                      pl.BlockSpec((None, None, d, f), wsel),
                      pl.BlockSpec((None, None, f, d), wsel)],
            out_specs=pl.BlockSpec((MOE_BLOCK, d), blk)),
        out_shape=jax.ShapeDtypeStruct((n_rows, d), f32),
        compiler_params=_cparams(1),
        name="moe_experts",
    )(blk_e, n_used, xs, we_g, we_u, we_d)


def _combine_kernel(ps_ref, eid_ref, pos_ref, gate_ref, ys_ref, x1_ref, h2_ref, gt_ref, g_ref, b_ref,
                    sg_ref, su_ref, sd_ref, o_ref, buf_ref, sem, *, alpha):
    tm = x1_ref.shape[0]

    def issue(t, _):
        for k in range(TOP_K):
            src = ps_ref[eid_ref[k, t]] + pos_ref[k, t]
            _row_copy(ys_ref, src, buf_ref.at[k], t, sem).start()
        return 0

    lax.fori_loop(0, tm, issue, 0)
    hb = h2_ref[...].astype(bf16)
    g = jnp.dot(hb, sg_ref[...], preferred_element_type=f32)
    u = jnp.dot(hb, su_ref[...], preferred_element_type=f32)
    shared = jnp.dot(((g * jax.nn.sigmoid(g)) * u).astype(bf16), sd_ref[...], preferred_element_type=f32)
    gate_t = jnp.concatenate([gate_ref[...], jnp.zeros((128 - TOP_K, tm), f32)], axis=0).T

    def drain(t, _):
        for k in range(TOP_K):
            _row_copy(ys_ref, 0, buf_ref.at[k], 0, sem).wait()
        return 0

    lax.fori_loop(0, tm, drain, 0)
    routed = gate_t[:, 0:1] * buf_ref[0]
    for k in range(1, TOP_K):
        routed += gate_t[:, k:k + 1] * buf_ref[k]
    o_ref[...] = _layer_norm(alpha * x1_ref[...] + gt_ref[0] * (shared + routed), g_ref[0], b_ref[0])


def _combine(ys, eid, pos, gate, pad_start, x1, h2, mod4, ln_g3, ln_b3, sg, su, sd, l, n_ctx, dec_seq, alpha):
    n, d = x1.shape
    f = sg.shape[-1]
    tm = _tile(math.gcd(n_ctx, dec_seq), 128)
    assert tm % 128 == 0
    n_ctx_tiles = n_ctx // tm
    tiles_per_seq = dec_seq // tm

    def seg(i):
        return jnp.where(i < n_ctx_tiles, 0, 1 + (i - n_ctx_tiles) // tiles_per_seq)

    base = l * MOD_ROWS * 6
    smem = lambda: pl.BlockSpec((TOP_K, tm), lambda i, ps: (0, i), memory_space=pltpu.SMEM)
    row = lambda i, ps: (i, 0)
    return pl.pallas_call(
        functools.partial(_combine_kernel, alpha=alpha),
        grid_spec=pltpu.PrefetchScalarGridSpec(
            num_scalar_prefetch=1,
            grid=(n // tm,),
            in_specs=[smem(), smem(), pl.BlockSpec((TOP_K, tm), lambda i, ps: (0, i)),
                      pl.BlockSpec(memory_space=pl.ANY),
                      pl.BlockSpec((tm, d), row), pl.BlockSpec((tm, d), row),
                      pl.BlockSpec((1, 1, d), lambda i, ps: (base + seg(i) * 6 + 5, 0, 0)),
                      pl.BlockSpec((1, 1, d), lambda i, ps: (2 * l + 1, 0, 0)),
                      pl.BlockSpec((1, 1, d), lambda i, ps: (2 * l + 1, 0, 0)),
                      pl.BlockSpec((None, d, f), lambda i, ps: (l, 0, 0)),
                      pl.BlockSpec((None, d, f), lambda i, ps: (l, 0, 0)),
                      pl.BlockSpec((None, f, d), lambda i, ps: (l, 0, 0))],
            out_specs=pl.BlockSpec((tm, d), row),
            scratch_shapes=[pltpu.VMEM((TOP_K, tm, d), f32), pltpu.SemaphoreType.DMA]),
        out_shape=jax.ShapeDtypeStruct((n, d), f32),
        compiler_params=_cparams(1),
        name="moe_combine_ln",
    )(pad_start, eid, pos, gate, ys, x1, h2, mod4, ln_g3, ln_b3, sg, su, sd)


def _rope_tables(seq, width):
    rows = seq // GRID_W
    row = jnp.repeat(jnp.arange(rows, dtype=f32), GRID_W)
    col = jnp.tile(jnp.arange(GRID_W, dtype=f32), rows)
    half = HEAD_DIM // 2
    inv = ROPE_BASE ** (-jnp.arange(0, half, 2, dtype=f32) / half)
    lane = jnp.arange(width)
    d = lane % HEAD_DIM
    pos = jnp.where((d // half)[None, :] == 0, row[:, None], col[:, None])
    ang = pos * inv[d % (HEAD_DIM // 4)][None, :]
    sign = jnp.where((d % half) < HEAD_DIM // 4, -1.0, 1.0)[None, :]
    return jnp.cos(ang), jnp.sin(ang) * sign


def _block_diag_gates(gate_w_r, gate_w_i, cb):
    depth, _, nb, bw, _ = gate_w_r.shape
    per = cb // bw
    w = jnp.stack([gate_w_r[:, 0], gate_w_i[:, 0], gate_w_r[:, 1], gate_w_i[:, 1]], axis=1)
    w = w.reshape(depth, 4, nb // per, per, bw, bw)
    eye = jnp.eye(per, dtype=w.dtype)
    bd = jnp.einsum("lgcaio,ab->lcgaibo", w, eye)
    return bd.reshape(depth, nb // per, 4, cb, cb).astype(bf16)


def kernel(x_prompt, x_sample, c, cache_k, cache_v, state_lru, c_ctx, w_mod, b_mod, w_in, lam_param, subln_w,
           conv_w, conv_b, gate_w_r, gate_b_r, gate_w_i, gate_b_i, lru_lambda, w_br_attn, w_br_lru, w_out,
           ln_g, ln_b, router_w, router_b, exp_w_gate, exp_w_up, exp_w_down, sh_w_gate, sh_w_up, sh_w_down):
    batch, seq, d = x_prompt.shape
    dec_batch, dec_seq, _ = x_sample.shape
    depth = w_mod.shape[0]
    past = cache_k.shape[2]
    d_lru = conv_w.shape[-1]
    n_exp = router_w.shape[-1]
    qk = N_HEADS * 2 * HEAD_DIM
    attn_w = N_HEADS * V_DIM
    n_ctx, n_lat = batch * seq, dec_batch * dec_seq
    n = n_ctx + n_lat
    alpha = (2 * depth) ** 0.25
    assert 1 + dec_batch <= MOD_ROWS and n_ctx % dec_seq == 0

    cond = jnp.concatenate([c_ctx[None], c, jnp.zeros((MOD_ROWS - 1 - dec_batch, d), f32)], 0)
    w_in_b = w_in.astype(bf16)
    wa, wl, wo = w_br_attn.astype(bf16), w_br_lru.astype(bf16), w_out.astype(bf16)
    sg, su, sd = sh_w_gate.astype(bf16), sh_w_up.astype(bf16), sh_w_down.astype(bf16)
    rwt = jnp.swapaxes(router_w, 1, 2)
    rb3 = router_b.reshape(depth, n_exp, 1)
    wg = _block_diag_gates(gate_w_r, gate_w_i, 256)
    bg = jnp.stack([gate_b_r[:, 0], gate_b_i[:, 0], gate_b_r[:, 1], gate_b_i[:, 1]], axis=1)
    conv_b3 = conv_b.reshape(depth, 1, d_lru)
    subln_w3 = subln_w.reshape(depth, 1, V_DIM)
    ln_g3 = ln_g.reshape(depth * 2, 1, d)
    ln_b3 = ln_b.reshape(depth * 2, 1, d)
    ck = cache_k.reshape(dec_batch, depth, past, N_HEADS * 2 * HEAD_DIM)
    cv = cache_v.reshape(dec_batch, depth, past, attn_w)
    cos_t, sin_t = _rope_tables(dec_seq, 512)
    h0_ctx = jnp.zeros((1, 2, d_lru), f32)

    mod4 = _mod_table(cond, w_mod, b_mod).reshape(depth * MOD_ROWS * 6, 1, d)
    x = jnp.concatenate([x_prompt.reshape(n_ctx, d), x_sample.reshape(n_lat, d)], 0)
    n_rows = -(-(n * TOP_K + n_exp * (MOE_BLOCK - 1)) // MOE_BLOCK) * MOE_BLOCK
    xr_col, yr_col = 2 * qk + attn_w, 2 * qk + attn_w + d_lru
    ga_col, gl_col = yr_col + d_lru, yr_col + d_lru + d

    new_k, new_v, new_s = [], [], []
    for l in range(depth):
        lam_init = 0.8 - 0.6 * math.exp(-0.3 * l)
        y = _inproj(x, mod4, w_in_b, cos_t, sin_t, l, n_ctx, dec_seq, 2 * qk)
        new_k.append(y[:n_ctx, qk:2 * qk].reshape(batch, seq, N_HEADS, 2 * HEAD_DIM))
        new_v.append(y[:n_ctx, 2 * qk:2 * qk + attn_w].reshape(batch, seq, N_HEADS, V_DIM))
        attn_c = _attention(y, lam_param, subln_w3, l, lam_init, 0, batch, seq)
        attn_l = _attention(y, lam_param, subln_w3, l, lam_init, n_ctx, dec_batch, dec_seq, ck, cv)
        rec_c, hfin = _lru(y, conv_w, conv_b3, wg, bg, lru_lambda, h0_ctx, l, 0, batch, seq, xr_col, yr_col)
        rec_l, _ = _lru(y, conv_w, conv_b3, wg, bg, lru_lambda, state_lru[:, l], l, n_ctx, dec_batch, dec_seq,
                        xr_col, yr_col)
        new_s.append(hfin)
        x1, h2 = _merge(attn_c, attn_l, rec_c, rec_l, y, x, mod4, ln_g3, ln_b3, wa, wl, wo, l, n_ctx, dec_seq, ga_col, gl_col, alpha)
        eid, pos, gate, cnt = _route(h2, rwt, rb3, l)
        counts = cnt[:, 0]
        padded = (counts + MOE_BLOCK - 1) // MOE_BLOCK * MOE_BLOCK
        pad_end = jnp.cumsum(padded)
        pad_start = (pad_end - padded).astype(i32)
        blk_row0 = jnp.arange(n_rows // MOE_BLOCK, dtype=i32) * MOE_BLOCK
        blk_e = jnp.minimum(jnp.sum(pad_end[None, :] <= blk_row0[:, None], axis=1), n_exp - 1).astype(i32)
        n_used = (pad_end[-1:] // MOE_BLOCK).astype(i32)
        xs = _dispatch(h2, eid, pos, pad_start, n_rows)
        ys = _experts(xs, blk_e, n_used, exp_w_gate, exp_w_up, exp_w_down, l)
        x = _combine(ys, eid, pos, gate, pad_start, x1, h2, mod4, ln_g3, ln_b3, sg, su, sd, l, n_ctx, dec_seq,
                     alpha)
    return (x[:n_ctx].reshape(batch, seq, d), x[n_ctx:].reshape(dec_batch, dec_seq, d),
            jnp.stack(new_k, 1), jnp.stack(new_v, 1), jnp.stack(new_s, 1))
```

```python
import functools
import math

import jax
import jax.numpy as jnp
from jax import lax
from jax.experimental import pallas as pl
from jax.experimental.pallas import tpu as pltpu

f32, bf16, i32 = jnp.float32, jnp.bfloat16, jnp.int32

GRID_W = 64
N_HEADS = 8
HEAD_DIM = 64
V_DIM = 2 * HEAD_DIM
ROPE_BASE = 10000.0
LRU_BLOCKS = 16
CONV_WIDTH = 4
LRU_C = 8.0
TOP_K = 8
N_GROUPS = 8
TOPK_GROUPS = 4
ROUTE_SCALE = 2.5
LN_EPS = 1e-5
MOD_ROWS = 8
MOE_BLOCK = 512
ROUTE_TILE = 256
SEG_ALIGN = 16
VMEM_LIMIT = 56 * 1024 * 1024
HIGHEST = lax.Precision.HIGHEST


def _cparams(n_axes):
    return pltpu.CompilerParams(dimension_semantics=("arbitrary",) * n_axes,
                                vmem_limit_bytes=VMEM_LIMIT)


def _tile(n, pref):
    t = min(n, pref)
    while n % t:
        t -= 8
    assert t > 0 and n % t == 0
    return t


def _layer_norm(z, g, b):
    mu = jnp.mean(z, -1, keepdims=True)
    zc = z - mu
    var = jnp.mean(zc * zc, -1, keepdims=True)
    return zc * lax.rsqrt(var + LN_EPS) * g + b


def _mod_kernel(c_ref, w_ref, b_ref, o_ref):
    c = c_ref[...]
    a = c * jax.nn.sigmoid(c)
    o_ref[...] = jnp.dot(a, w_ref[...], precision=HIGHEST, preferred_element_type=f32) + b_ref[...]


def _mod_table(cond, w_mod, b_mod):
    depth, d, d6 = w_mod.shape
    tn = _tile(d6, 1024)
    return pl.pallas_call(
        _mod_kernel,
        grid=(depth, d6 // tn),
        in_specs=[pl.BlockSpec((MOD_ROWS, d), lambda l, j: (0, 0)),
                  pl.BlockSpec((None, d, tn), lambda l, j: (l, 0, j)),
                  pl.BlockSpec((None, 1, tn), lambda l, j: (l, 0, j))],
        out_specs=pl.BlockSpec((None, MOD_ROWS, tn), lambda l, j: (l, 0, j)),
        out_shape=jax.ShapeDtypeStruct((depth, MOD_ROWS, d6), f32),
        compiler_params=_cparams(2),
        name="mod_table",
    )(cond, w_mod, b_mod.reshape(depth, 1, d6))


def _inproj_kernel(x_ref, sh_ref, sc_ref, w_ref, cos_ref, sin_ref, o_ref, hb_ref, *,
                   n_ctx_tiles, n_rope_cols, tn):
    i = pl.program_id(0)
    j = pl.program_id(1)

    @pl.when(j == 0)
    def _():
        hb_ref[...] = (x_ref[...] * (1.0 + sc_ref[0]) + sh_ref[0]).astype(bf16)

    acc = jnp.dot(hb_ref[...], w_ref[...], preferred_element_type=f32)
    rope = jnp.logical_and(i >= n_ctx_tiles, j < n_rope_cols)

    @pl.when(rope)
    def _():
        lane = lax.broadcasted_iota(i32, acc.shape, 1)
        first = (lane % (HEAD_DIM // 2)) < (HEAD_DIM // 4)
        partner = jnp.where(first, pltpu.roll(acc, tn - HEAD_DIM // 4, 1),
                            pltpu.roll(acc, HEAD_DIM // 4, 1))
        o_ref[...] = acc * cos_ref[...] + partner * sin_ref[...]

    @pl.when(jnp.logical_not(rope))
    def _():
        o_ref[...] = acc


def _inproj(x, mod4, w_in_b, cos_t, sin_t, l, n_ctx, dec_seq, qk_cols):
    n, d = x.shape
    d_in = w_in_b.shape[-1]
    tm = _tile(math.gcd(n_ctx, dec_seq), 1024)
    tn = cos_t.shape[1]
    n_ctx_tiles = n_ctx // tm
    tiles_per_seq = dec_seq // tm

    def seg(i):
        return jnp.where(i < n_ctx_tiles, 0, 1 + (i - n_ctx_tiles) // tiles_per_seq)

    def pos(i):
        return jnp.where(i < n_ctx_tiles, 0, (i - n_ctx_tiles) % tiles_per_seq)

    base = l * MOD_ROWS * 6
    return pl.pallas_call(
        functools.partial(_inproj_kernel, n_ctx_tiles=n_ctx_tiles, n_rope_cols=qk_cols // tn, tn=tn),
        grid=(n // tm, d_in // tn),
        in_specs=[pl.BlockSpec((tm, d), lambda i, j: (i, 0)),
                  pl.BlockSpec((1, 1, d), lambda i, j: (base + seg(i) * 6 + 0, 0, 0)),
                  pl.BlockSpec((1, 1, d), lambda i, j: (base + seg(i) * 6 + 1, 0, 0)),
                  pl.BlockSpec((None, d, tn), lambda i, j: (l, 0, j)),
                  pl.BlockSpec((tm, tn), lambda i, j: (pos(i), 0)),
                  pl.BlockSpec((tm, tn), lambda i, j: (pos(i), 0))],
        out_specs=pl.BlockSpec((tm, tn), lambda i, j: (i, j)),
        out_shape=jax.ShapeDtypeStruct((n, d_in), f32),
        scratch_shapes=[pltpu.VMEM((tm, d), bf16)],
        compiler_params=_cparams(2),
        name="in_proj",
    )(x, mod4, mod4, w_in_b, cos_t, sin_t)


def _attn_kernel(*refs, has_cache, lam_init, past):
    if has_cache:
        q_ref, k_ref, v_ref, ck_ref, cv_ref, lp_ref, sw_ref, o_ref, kt_ref, vb_ref = refs
    else:
        q_ref, k_ref, v_ref, lp_ref, sw_ref, o_ref, kt_ref, vb_ref = refs

    @pl.when(pl.program_id(2) == 0)
    def _():
        if has_cache:
            kt_ref[:, :past] = ck_ref[...].T.astype(bf16)
            kt_ref[:, past:] = k_ref[...].T.astype(bf16)
            vb_ref[:past, :] = cv_ref[...].astype(bf16)
            vb_ref[past:, :] = v_ref[...].astype(bf16)
        else:
            kt_ref[...] = k_ref[...].T.astype(bf16)
            vb_ref[...] = v_ref[...].astype(bf16)

    tq = q_ref.shape[0]
    q = q_ref[...] * (HEAD_DIM ** -0.5)
    lane = lax.broadcasted_iota(i32, q.shape, 1)
    q2 = jnp.concatenate([jnp.where(lane < HEAD_DIM, q, 0.0).astype(bf16),
                          jnp.where(lane >= HEAD_DIM, q, 0.0).astype(bf16)], axis=0)
    s = jnp.dot(q2, kt_ref[...], preferred_element_type=f32)
    e = jnp.exp(s - jnp.max(s, axis=-1, keepdims=True))
    p = e / jnp.sum(e, axis=-1, keepdims=True)
    lp = lp_ref[...]
    lam = (jnp.exp(jnp.sum(lp[0:1] * lp[1:2], keepdims=True))
           - jnp.exp(jnp.sum(lp[2:3] * lp[3:4], keepdims=True)) + lam_init)
    a = (p[:tq] - lam * p[tq:]).astype(bf16)
    o = jnp.dot(a, vb_ref[...], preferred_element_type=f32)
    o = o * lax.rsqrt(jnp.mean(o * o, -1, keepdims=True) + LN_EPS) * sw_ref[...]
    o_ref[...] = o * (1.0 - lam_init)


def _attention(y, lam_param, subln_w3, l, lam_init, row0, nseq, seq, cache_k=None, cache_v=None):
    has_cache = cache_k is not None
    past = cache_k.shape[2] if has_cache else 0
    lk = past + seq
    tq = _tile(seq, 256)
    nq = seq // tq
    assert row0 % seq == 0
    qb0, kb0 = row0 // tq, row0 // seq
    in_specs = [pl.BlockSpec((tq, V_DIM), lambda b, h, i: (qb0 + b * nq + i, h)),
                pl.BlockSpec((seq, V_DIM), lambda b, h, i: (kb0 + b, N_HEADS + h)),
                pl.BlockSpec((seq, V_DIM), lambda b, h, i: (kb0 + b, 2 * N_HEADS + h))]
    args = [y, y, y]
    if has_cache:
        in_specs += [pl.BlockSpec((None, None, past, V_DIM), lambda b, h, i: (b, l, 0, h)),
                     pl.BlockSpec((None, None, past, V_DIM), lambda b, h, i: (b, l, 0, h))]
        args += [cache_k, cache_v]
    in_specs += [pl.BlockSpec((None, 4, HEAD_DIM), lambda b, h, i: (l, 0, 0)),
                 pl.BlockSpec((None, 1, V_DIM), lambda b, h, i: (l, 0, 0))]
    args += [lam_param, subln_w3]
    return pl.pallas_call(
        functools.partial(_attn_kernel, has_cache=has_cache, lam_init=lam_init, past=past),
        grid=(nseq, N_HEADS, nq),
        in_specs=in_specs,
        out_specs=pl.BlockSpec((tq, V_DIM), lambda b, h, i: (b * nq + i, h)),
        out_shape=jax.ShapeDtypeStruct((nseq * seq, N_HEADS * V_DIM), f32),
        scratch_shapes=[pltpu.VMEM((V_DIM, lk), bf16), pltpu.VMEM((lk, V_DIM), bf16)],
        compiler_params=_cparams(3),
        name="diff_attn_lat" if has_cache else "diff_attn_ctx",
    )(*args)


def _lru_kernel(xr_ref, yr_ref, cw_ref, cb_ref, wg_ref, bg_ref, lam_ref, h0_ref,
                rec_ref, hfin_ref, xp_ref, a_ref, b_ref, *, chunk):
    seq, c = xr_ref.shape
    xp_ref[0:8, :] = jnp.zeros((8, c), f32)
    xp_ref[8:8 + seq, :] = xr_ref[...]
    xp_ref[8 + seq:16 + seq, :] = jnp.zeros((8, c), f32)
    lam = lam_ref[...]
    nlam = -lam
    softplus = jnp.maximum(nlam, 0.0) + jnp.log(1.0 + jnp.exp(-jnp.abs(nlam)))
    for ci in range(seq // chunk):
        t0 = ci * chunk
        xc = cb_ref[...] + sum(cw_ref[j:j + 1, :] * xp_ref[8 + t0 + j - 1:8 + t0 + j - 1 + chunk, :]
                               for j in range(CONV_WIDTH))
        xcb = xc.astype(bf16)
        for d in range(2):
            r = jax.nn.sigmoid(jnp.dot(xcb, wg_ref[2 * d], preferred_element_type=f32)
                               + bg_ref[2 * d:2 * d + 1, :])
            g = jax.nn.sigmoid(jnp.dot(xcb, wg_ref[2 * d + 1], preferred_element_type=f32)
                               + bg_ref[2 * d + 1:2 * d + 2, :])
            log_a = -LRU_C * r * softplus[d:d + 1, :]
            a = jnp.exp(log_a)
            a_ref[d, t0:t0 + chunk, :] = a
            b_ref[d, t0:t0 + chunk, :] = jnp.sqrt(-jnp.tanh(log_a) * (a * a + 1.0)) * (g * xc)

    rows = lax.broadcasted_iota(i32, (8, c), 0)
    nch = seq // 8

    def scan_body(i, carry):
        cf, cb = carry
        tf = pl.multiple_of(8 * i, 8)
        af = a_ref[0, pl.ds(tf, 8), :]
        bf = b_ref[0, pl.ds(tf, 8), :]
        for d in (1, 2, 4):
            keep = rows >= d
            bf = af * jnp.where(keep, pltpu.roll(bf, d, 0), 0.0) + bf
            af = af * jnp.where(keep, pltpu.roll(af, d, 0), 1.0)
        hf = af * cf + bf
        b_ref[0, pl.ds(tf, 8), :] = hf
        tb = pl.multiple_of(8 * (nch - 1 - i), 8)
        ab = a_ref[1, pl.ds(tb, 8), :]
        bb = b_ref[1, pl.ds(tb, 8), :]
        for d in (1, 2, 4):
            keep = rows < 8 - d
            bb = ab * jnp.where(keep, pltpu.roll(bb, 8 - d, 0), 0.0) + bb
            ab = ab * jnp.where(keep, pltpu.roll(ab, 8 - d, 0), 1.0)
        hb = ab * cb + bb
        b_ref[1, pl.ds(tb, 8), :] = hb
        return hf[7:8, :], hb[0:1, :]

    h0 = h0_ref[...]
    cf, cb = lax.fori_loop(0, nch, scan_body, (h0[0:1, :], h0[1:2, :]))
    hfin_ref[0:1, :] = cf
    hfin_ref[1:2, :] = cb
    for ci in range(seq // chunk):
        t0 = ci * chunk
        rec_ref[t0:t0 + chunk, :] = ((b_ref[0, t0:t0 + chunk, :] + b_ref[1, t0:t0 + chunk, :])
                                     * jax.nn.gelu(yr_ref[t0:t0 + chunk, :]))


def _lru(y, conv_w, conv_b3, wg, bg, lru_lambda, h0, l, row0, nseq, seq, xr_col, yr_col):
    d_lru = conv_w.shape[-1]
    cb = 256
    ncb = d_lru // cb
    assert row0 % seq == 0
    rb0 = row0 // seq
    h0_batched = h0.shape[0] > 1
    in_specs = [pl.BlockSpec((seq, cb), lambda b, c: (rb0 + b, xr_col // cb + c)),
                pl.BlockSpec((seq, cb), lambda b, c: (rb0 + b, yr_col // cb + c)),
                pl.BlockSpec((None, CONV_WIDTH, cb), lambda b, c: (l, 0, c)),
                pl.BlockSpec((None, 1, cb), lambda b, c: (l, 0, c)),
                pl.BlockSpec((None, None, 4, cb, cb), lambda b, c: (l, c, 0, 0, 0)),
                pl.BlockSpec((None, 4, cb), lambda b, c: (l, 0, c)),
                pl.BlockSpec((None, 2, cb), lambda b, c: (l, 0, c)),
                pl.BlockSpec((None, 2, cb), (lambda b, c: (b, 0, c)) if h0_batched else (lambda b, c: (0, 0, c)))]
    return pl.pallas_call(
        functools.partial(_lru_kernel, chunk=_tile(seq, 512)),
        grid=(nseq, ncb),
        in_specs=in_specs,
        out_specs=[pl.BlockSpec((seq, cb), lambda b, c: (b, c)),
                   pl.BlockSpec((None, 2, cb), lambda b, c: (b, 0, c))],
        out_shape=[jax.ShapeDtypeStruct((nseq * seq, d_lru), f32),
                   jax.ShapeDtypeStruct((nseq, 2, d_lru), f32)],
        scratch_shapes=[pltpu.VMEM((seq + 16, cb), f32), pltpu.VMEM((2, seq, cb), f32),
                        pltpu.VMEM((2, seq, cb), f32)],
        compiler_params=_cparams(2),
        name="rglru",
    )(y, y, conv_w, conv_b3, wg, bg, lru_lambda, h0)


def _merge_kernel(atc_ref, atl_ref, rcc_ref, rcl_ref, ga_ref, gl_ref, x_ref, gt_ref, shf_ref, scf_ref,
                  g_ref, b_ref, wa_ref, wl_ref, wo_ref, x1_ref, h2_ref, ab_ref, rb_ref, *, alpha, n_ctx_tiles):
    @pl.when(pl.program_id(0) < n_ctx_tiles)
    def _():
        ab_ref[...] = atc_ref[...].astype(bf16)
        rb_ref[...] = rcc_ref[...].astype(bf16)

    @pl.when(pl.program_id(0) >= n_ctx_tiles)
    def _():
        ab_ref[...] = atl_ref[...].astype(bf16)
        rb_ref[...] = rcl_ref[...].astype(bf16)

    pa = jnp.dot(ab_ref[...], wa_ref[...], preferred_element_type=f32)
    pr = jnp.dot(rb_ref[...], wl_ref[...], preferred_element_type=f32)
    merged = jax.nn.sigmoid(ga_ref[...]) * pa + jax.nn.sigmoid(gl_ref[...]) * pr
    out = jnp.dot(merged.astype(bf16), wo_ref[...], preferred_element_type=f32)
    x1 = _layer_norm(alpha * x_ref[...] + gt_ref[0] * out, g_ref[0], b_ref[0])
    x1_ref[...] = x1
    h2_ref[...] = x1 * (1.0 + scf_ref[0]) + shf_ref[0]


def _merge(attn_c, attn_l, rec_c, rec_l, y, x, mod4, ln_g3, ln_b3, wa, wl, wo, l, n_ctx, dec_seq, ga_col, gl_col,
           alpha):
    n, d = x.shape
    tm = _tile(math.gcd(n_ctx, dec_seq), 256)
    n_ctx_tiles = n_ctx // tm
    tiles_per_seq = dec_seq // tm

    def seg(i):
        return jnp.where(i < n_ctx_tiles, 0, 1 + (i - n_ctx_tiles) // tiles_per_seq)

    base = l * MOD_ROWS * 6
    row = lambda i: (i, 0)
    ctx_row = lambda i: (jnp.minimum(i, n_ctx_tiles - 1), 0)
    lat_row = lambda i: (jnp.maximum(i - n_ctx_tiles, 0), 0)
    mod = lambda k: pl.BlockSpec((1, 1, d), lambda i: (base + seg(i) * 6 + k, 0, 0))
    wspec = pl.BlockSpec((None, d, d), lambda i: (l, 0, 0))
    return pl.pallas_call(
        functools.partial(_merge_kernel, alpha=alpha, n_ctx_tiles=n_ctx_tiles),
        grid=(n // tm,),
        in_specs=[pl.BlockSpec((tm, d), ctx_row), pl.BlockSpec((tm, d), lat_row),
                  pl.BlockSpec((tm, d), ctx_row), pl.BlockSpec((tm, d), lat_row),
                  pl.BlockSpec((tm, d), lambda i: (i, ga_col // d)),
                  pl.BlockSpec((tm, d), lambda i: (i, gl_col // d)),
                  pl.BlockSpec((tm, d), row), mod(2), mod(3), mod(4),
                  pl.BlockSpec((1, 1, d), lambda i: (2 * l, 0, 0)),
                  pl.BlockSpec((1, 1, d), lambda i: (2 * l, 0, 0)),
                  wspec, wspec, wspec],
        out_specs=[pl.BlockSpec((tm, d), row), pl.BlockSpec((tm, d), row)],
        out_shape=[jax.ShapeDtypeStruct((n, d), f32), jax.ShapeDtypeStruct((n, d), f32)],
        scratch_shapes=[pltpu.VMEM((tm, d), bf16), pltpu.VMEM((tm, d), bf16)],
        compiler_params=_cparams(1),
        name="merge_outproj_ln",
    )(attn_c, attn_l, rec_c, rec_l, y, y, x, mod4, mod4, mod4, ln_g3, ln_b3, wa, wl, wo)


def _route_kernel(h_ref, rw_ref, rb_ref, loc_ref, gate_ref, tab_ref, carry_ref):
    n_exp = rw_ref.shape[0]
    tm = h_ref.shape[0]
    gsz = n_exp // N_GROUPS

    @pl.when(pl.program_id(0) == 0)
    def _():
        carry_ref[...] = jnp.zeros_like(carry_ref)

    logits = lax.dot_general(rw_ref[...], h_ref[...], (((1,), (1,)), ((), ())),
                             precision=HIGHEST, preferred_element_type=f32)
    s = jax.nn.sigmoid(logits)
    sb = s + rb_ref[...]
    ninf = -jnp.inf
    gs_rows = []
    for g in range(N_GROUPS):
        blk = sb[g * gsz:(g + 1) * gsz]
        m1 = jnp.max(blk, axis=0, keepdims=True)
        cnt = jnp.sum((blk == m1).astype(f32), axis=0, keepdims=True)
        m2 = jnp.max(jnp.where(blk < m1, blk, ninf), axis=0, keepdims=True)
        gs_rows.append(m1 + jnp.where(cnt >= 2.0, m1, m2))
    gs = jnp.concatenate(gs_rows, axis=0)
    gi = lax.broadcasted_iota(i32, gs.shape, 0)
    grank = jnp.zeros(gs.shape, f32)
    for g2 in range(N_GROUPS):
        row = gs[g2:g2 + 1]
        grank += jnp.where((row > gs) | ((row == gs) & (g2 < gi)), 1.0, 0.0)
    gsel = grank < float(TOPK_GROUPS)
    sbm = jnp.concatenate([jnp.where(gsel[g:g + 1], sb[g * gsz:(g + 1) * gsz], ninf)
                           for g in range(N_GROUPS)], axis=0)
    ei = lax.broadcasted_iota(i32, sbm.shape, 0)
    erank = jnp.zeros(sbm.shape, f32)
    for e2 in range(n_exp):
        row = sbm[e2:e2 + 1]
        erank += jnp.where((row > sbm) | ((row == sbm) & (e2 < ei)), 1.0, 0.0)
    sel = erank < float(TOP_K)
    ssel = jnp.where(sel, s, 0.0)
    gate = ssel / jnp.sum(ssel, axis=0, keepdims=True) * ROUTE_SCALE
    selb = sel.astype(bf16)
    ti = lax.broadcasted_iota(i32, (tm, tm), 0)
    tj = lax.broadcasted_iota(i32, (tm, tm), 1)
    upper = (ti < tj).astype(bf16)
    pos_local = jnp.dot(selb, upper, preferred_element_type=f32)
    cnt = jnp.sum(sel.astype(f32), axis=1, keepdims=True)
    cpad = jnp.floor((cnt + (SEG_ALIGN - 1)) * (1.0 / SEG_ALIGN)) * SEG_ALIGN
    cpad_l = jnp.broadcast_to(cpad, (n_exp, 128))
    li = lax.broadcasted_iota(i32, (n_exp, n_exp), 0)
    lj = lax.broadcasted_iota(i32, (n_exp, n_exp), 1)
    lower = (lj < li).astype(bf16)
    run_start = jnp.dot(lower, cpad_l.astype(bf16), preferred_element_type=f32)
    loc = run_start[:, 0:1] + pos_local
    slot = jnp.dot(lower, selb, preferred_element_type=f32)
    l_rows, g_rows = [], []
    for k in range(TOP_K):
        mk = sel & (slot == float(k))
        l_rows.append(jnp.sum(jnp.where(mk, loc, 0.0), axis=0, keepdims=True))
        g_rows.append(jnp.sum(jnp.where(mk, gate, 0.0), axis=0, keepdims=True))
    loc_ref[...] = jnp.concatenate(l_rows, axis=0).astype(i32)
    gate_ref[...] = jnp.concatenate(g_rows, axis=0)
    lane = lax.broadcasted_iota(i32, (n_exp, 128), 1)
    tab_ref[...] = jnp.where(lane == 0, cpad_l, carry_ref[...]).astype(i32)
    carry_ref[...] = carry_ref[...] + cpad_l


def _route(h2, rwt, rb3, l):
    n, d = h2.shape
    n_exp = rwt.shape[1]
    tm = ROUTE_TILE
    assert n % tm == 0
    lst = pl.BlockSpec((TOP_K, tm), lambda i: (0, i))
    return pl.pallas_call(
        _route_kernel,
        grid=(n // tm,),
        in_specs=[pl.BlockSpec((tm, d), lambda i: (i, 0)),
                  pl.BlockSpec((None, n_exp, d), lambda i: (l, 0, 0)),
                  pl.BlockSpec((None, n_exp, 1), lambda i: (l, 0, 0))],
        out_specs=[lst, lst, pl.BlockSpec((None, n_exp, 128), lambda i: (i, 0, 0))],
        out_shape=[jax.ShapeDtypeStruct((TOP_K, n), i32), jax.ShapeDtypeStruct((TOP_K, n), f32),
                   jax.ShapeDtypeStruct((n // tm, n_exp, 128), i32)],
        scratch_shapes=[pltpu.VMEM((n_exp, 128), f32)],
        compiler_params=_cparams(1),
        name="moe_route",
    )(h2, rwt, rb3)


def _pow2_pieces(count, max_rows, fn):
    off = 0
    size = max_rows
    while size >= SEG_ALIGN:
        take = (count & size) != 0
        pl.when(take)(functools.partial(fn, off, size))
        off = off + jnp.where(take, size, 0)
        size //= 2


def _wait_rows(count, max_rows, src_ref, dst_ref, sem):
    def wait(_, size):
        pltpu.make_async_copy(src_ref.at[pl.ds(0, size)], dst_ref.at[pl.ds(0, size)], sem).wait()

    _pow2_pieces(count, max_rows, wait)


def _dispatch_kernel(cp_ref, ga_ref, tot_ref, ts_ref, tl_ref, loc_ref, h_ref, xs_ref, buf_ref, z_ref, sem, *,
                     n_exp, n_tiles, tile_rows, wait_max):
    t = pl.program_id(0)
    slot = t % 2
    tm = h_ref.shape[0]

    @pl.when(t >= 2)
    def _():
        _wait_rows(tot_ref[t - 2], wait_max, buf_ref.at[slot], xs_ref, sem.at[slot])

    loc = loc_ref[...]
    rid = lax.broadcasted_iota(i32, (tile_rows, tm), 0)
    perm = jnp.zeros((tile_rows, tm), f32)
    for k in range(TOP_K):
        perm = jnp.where(loc[k:k + 1, :] == rid, 1.0, perm)
    buf_ref[slot] = jnp.dot(perm.astype(bf16), h_ref[...].astype(bf16),
                            preferred_element_type=f32).astype(bf16)

    def issue(e, lo):
        c = cp_ref[t * n_exp + e]
        g = ga_ref[t * n_exp + e]

        def copy(off, size):
            src = pl.multiple_of(lo + off, SEG_ALIGN)
            dst = pl.multiple_of(g + off, SEG_ALIGN)
            pltpu.make_async_copy(buf_ref.at[slot, pl.ds(src, size)], xs_ref.at[pl.ds(dst, size)],
                                  sem.at[slot]).start()

        _pow2_pieces(c, tm, copy)
        return lo + c

    lax.fori_loop(0, n_exp, issue, 0)

    @pl.when(t == n_tiles - 1)
    def _():
        z_ref[...] = jnp.zeros_like(z_ref)

        def fill(e, _):
            def copy(off, size):
                dst = pl.multiple_of(ts_ref[e] + off, SEG_ALIGN)
                pltpu.make_async_copy(z_ref.at[pl.ds(0, size)], xs_ref.at[pl.ds(dst, size)], sem.at[2]).start()

            _pow2_pieces(tl_ref[e], MOE_BLOCK // 2, copy)
            return 0

        lax.fori_loop(0, n_exp, fill, 0)

        def drain(e, _):
            _wait_rows(tl_ref[e], MOE_BLOCK // 2, z_ref, xs_ref, sem.at[2])
            return 0

        lax.fori_loop(0, n_exp, drain, 0)
        if n_tiles > 1:
            _wait_rows(tot_ref[t - 1], wait_max, buf_ref.at[1 - slot], xs_ref, sem.at[1 - slot])
        _wait_rows(tot_ref[t], wait_max, buf_ref.at[slot], xs_ref, sem.at[slot])


def _tile_rows(tm, n_exp):
    return -(-(tm * TOP_K + n_exp * (SEG_ALIGN - 1)) // 128) * 128


def _wait_max(tile_rows):
    return 1 << (tile_rows.bit_length() - 1)


def _dispatch(h2, loc, cpad, gabs, tot, tail_start, tail_len, n_rows, n_exp):
    n, d = h2.shape
    tm = ROUTE_TILE
    n_tiles = n // tm
    tile_rows = _tile_rows(tm, n_exp)
    return pl.pallas_call(
        functools.partial(_dispatch_kernel, n_exp=n_exp, n_tiles=n_tiles, tile_rows=tile_rows,
                          wait_max=_wait_max(tile_rows)),
        grid_spec=pltpu.PrefetchScalarGridSpec(
            num_scalar_prefetch=5,
            grid=(n_tiles,),
            in_specs=[pl.BlockSpec((TOP_K, tm), lambda i, *_: (0, i)),
                      pl.BlockSpec((tm, d), lambda i, *_: (i, 0))],
            out_specs=pl.BlockSpec(memory_space=pl.ANY),
            scratch_shapes=[pltpu.VMEM((2, tile_rows, d), bf16), pltpu.VMEM((MOE_BLOCK // 2, d), bf16),
                            pltpu.SemaphoreType.DMA((3,))]),
        out_shape=jax.ShapeDtypeStruct((n_rows, d), bf16),
        compiler_params=_cparams(1),
        name="moe_dispatch",
    )(cpad, gabs, tot, tail_start, tail_len, loc, h2)


def _expert_kernel(be_ref, nu_ref, x_ref, wg_ref, wu_ref, wd_ref, o_ref):
    @pl.when(pl.program_id(0) < nu_ref[0])
    def _():
        x = x_ref[...]
        g = jnp.dot(x, wg_ref[...].astype(bf16), preferred_element_type=f32)
        u = jnp.dot(x, wu_ref[...].astype(bf16), preferred_element_type=f32)
        h = (g * jax.nn.sigmoid(g)) * u
        o_ref[...] = jnp.dot(h.astype(bf16), wd_ref[...].astype(bf16),
                             preferred_element_type=f32).astype(bf16)


def _experts(xs, blk_e, n_used, we_g, we_u, we_d, l):
    n_rows, d = xs.shape
    f = we_g.shape[-1]
    nblk = n_rows // MOE_BLOCK
    blk = lambda b, be, nu: (jnp.minimum(b, nu[0] - 1), 0)
    wsel = lambda b, be, nu: (l, be[jnp.minimum(b, nu[0] - 1)], 0, 0)
    return pl.pallas_call(
        _expert_kernel,
        grid_spec=pltpu.PrefetchScalarGridSpec(
            num_scalar_prefetch=2,
            grid=(nblk,),
            in_specs=[pl.BlockSpec((MOE_BLOCK, d), blk),
                      pl.BlockSpec((None, None, d, f), wsel),
                      pl.BlockSpec((None, None, d, f), wsel),
                      pl.BlockSpec((None, None, f, d), wsel)],
            out_specs=pl.BlockSpec((MOE_BLOCK, d), blk)),
        out_shape=jax.ShapeDtypeStruct((n_rows, d), bf16),
        compiler_params=_cparams(1),
        name="moe_experts",
    )(blk_e, n_used, xs, we_g, we_u, we_d)


def _combine_kernel(cp_ref, ga_ref, tot_ref, loc_ref, gate_ref, ys_ref, x1_ref, h2_ref, gt_ref, g_ref, b_ref,
                    sg_ref, su_ref, sd_ref, o_ref, buf_ref, sem, *, alpha, n_exp, n_tiles, tile_rows, wait_max):
    t = pl.program_id(0)
    slot = t % 2
    tm = x1_ref.shape[0]

    def fetch(tile, dst_slot):
        def issue(e, lo):
            c = cp_ref[tile * n_exp + e]
            g = ga_ref[tile * n_exp + e]

            def copy(off, size):
                src = pl.multiple_of(g + off, SEG_ALIGN)
                dst = pl.multiple_of(lo + off, SEG_ALIGN)
                pltpu.make_async_copy(ys_ref.at[pl.ds(src, size)], buf_ref.at[dst_slot, pl.ds(dst, size)],
                                      sem.at[dst_slot]).start()

            _pow2_pieces(c, tm, copy)
            return lo + c

        lax.fori_loop(0, n_exp, issue, 0)

    @pl.when(t == 0)
    def _():
        buf_ref[...] = jnp.zeros_like(buf_ref)
        fetch(0, 0)

    @pl.when(t + 1 < n_tiles)
    def _():
        fetch(t + 1, 1 - slot)

    hb = h2_ref[...].astype(bf16)
    g = jnp.dot(hb, sg_ref[...], preferred_element_type=f32)
    u = jnp.dot(hb, su_ref[...], preferred_element_type=f32)
    shared = jnp.dot(((g * jax.nn.sigmoid(g)) * u).astype(bf16), sd_ref[...], preferred_element_type=f32)
    fill = jnp.zeros((128 - TOP_K, tm), f32)
    gate_t = jnp.concatenate([gate_ref[...], fill], axis=0).T
    loc_t = jnp.concatenate([loc_ref[...].astype(f32), fill], axis=0).T
    rid = lax.broadcasted_iota(i32, (tm, tile_rows), 1).astype(f32)
    wgt = jnp.zeros((tm, tile_rows), f32)
    for k in range(TOP_K):
        wgt = jnp.where(loc_t[:, k:k + 1] == rid, gate_t[:, k:k + 1], wgt)
    _wait_rows(tot_ref[t], wait_max, ys_ref, buf_ref.at[slot], sem.at[slot])
    routed = jnp.dot(wgt.astype(bf16), buf_ref[slot], preferred_element_type=f32)
    o_ref[...] = _layer_norm(alpha * x1_ref[...] + gt_ref[0] * (shared + routed), g_ref[0], b_ref[0])


def _combine(ys, loc, gate, cpad, gabs, tot, x1, h2, mod4, ln_g3, ln_b3, sg, su, sd, l, n_ctx, dec_seq, alpha,
             n_exp):
    n, d = x1.shape
    f = sg.shape[-1]
    tm = ROUTE_TILE
    assert n_ctx % tm == 0 and dec_seq % tm == 0
    n_tiles = n // tm
    tile_rows = _tile_rows(tm, n_exp)
    n_ctx_tiles = n_ctx // tm
    tiles_per_seq = dec_seq // tm

    def seg(i):
        return jnp.where(i < n_ctx_tiles, 0, 1 + (i - n_ctx_tiles) // tiles_per_seq)

    base = l * MOD_ROWS * 6
    row = lambda i, *_: (i, 0)
    lst = lambda i, *_: (0, i)
    return pl.pallas_call(
        functools.partial(_combine_kernel, alpha=alpha, n_exp=n_exp, n_tiles=n_tiles, tile_rows=tile_rows,
                          wait_max=_wait_max(tile_rows)),
        grid_spec=pltpu.PrefetchScalarGridSpec(
            num_scalar_prefetch=3,
            grid=(n_tiles,),
            in_specs=[pl.BlockSpec((TOP_K, tm), lst), pl.BlockSpec((TOP_K, tm), lst),
                      pl.BlockSpec(memory_space=pl.ANY),
                      pl.BlockSpec((tm, d), row), pl.BlockSpec((tm, d), row),
                      pl.BlockSpec((1, 1, d), lambda i, *_: (base + seg(i) * 6 + 5, 0, 0)),
                      pl.BlockSpec((1, 1, d), lambda i, *_: (2 * l + 1, 0, 0)),
                      pl.BlockSpec((1, 1, d), lambda i, *_: (2 * l + 1, 0, 0)),
                      pl.BlockSpec((None, d, f), lambda i, *_: (l, 0, 0)),
                      pl.BlockSpec((None, d, f), lambda i, *_: (l, 0, 0)),
                      pl.BlockSpec((None, f, d), lambda i, *_: (l, 0, 0))],
            out_specs=pl.BlockSpec((tm, d), row),
            scratch_shapes=[pltpu.VMEM((2, tile_rows, d), bf16), pltpu.SemaphoreType.DMA((2,))]),
        out_shape=jax.ShapeDtypeStruct((n, d), f32),
        compiler_params=_cparams(1),
        name="moe_combine_ln",
    )(cpad, gabs, tot, loc, gate, ys, x1, h2, mod4, ln_g3, ln_b3, sg, su, sd)


def _rope_tables(seq, width):
    rows = seq // GRID_W
    row = jnp.repeat(jnp.arange(rows, dtype=f32), GRID_W)
    col = jnp.tile(jnp.arange(GRID_W, dtype=f32), rows)
    half = HEAD_DIM // 2
    inv = ROPE_BASE ** (-jnp.arange(0, half, 2, dtype=f32) / half)
    lane = jnp.arange(width)
    d = lane % HEAD_DIM
    pos = jnp.where((d // half)[None, :] == 0, row[:, None], col[:, None])
    ang = pos * inv[d % (HEAD_DIM // 4)][None, :]
    sign = jnp.where((d % half) < HEAD_DIM // 4, -1.0, 1.0)[None, :]
    return jnp.cos(ang), jnp.sin(ang) * sign


def _block_diag_gates(gate_w_r, gate_w_i, cb):
    depth, _, nb, bw, _ = gate_w_r.shape
    per = cb // bw
    w = jnp.stack([gate_w_r[:, 0], gate_w_i[:, 0], gate_w_r[:, 1], gate_w_i[:, 1]], axis=1)
    w = w.reshape(depth, 4, nb // per, per, bw, bw)
    eye = jnp.eye(per, dtype=w.dtype)
    bd = jnp.einsum("lgcaio,ab->lcgaibo", w, eye)
    return bd.reshape(depth, nb // per, 4, cb, cb).astype(bf16)


def kernel(x_prompt, x_sample, c, cache_k, cache_v, state_lru, c_ctx, w_mod, b_mod, w_in, lam_param, subln_w,
           conv_w, conv_b, gate_w_r, gate_b_r, gate_w_i, gate_b_i, lru_lambda, w_br_attn, w_br_lru, w_out,
           ln_g, ln_b, router_w, router_b, exp_w_gate, exp_w_up, exp_w_down, sh_w_gate, sh_w_up, sh_w_down):
    batch, seq, d = x_prompt.shape
    dec_batch, dec_seq, _ = x_sample.shape
    depth = w_mod.shape[0]
    past = cache_k.shape[2]
    d_lru = conv_w.shape[-1]
    n_exp = router_w.shape[-1]
    qk = N_HEADS * 2 * HEAD_DIM
    attn_w = N_HEADS * V_DIM
    n_ctx, n_lat = batch * seq, dec_batch * dec_seq
    n = n_ctx + n_lat
    alpha = (2 * depth) ** 0.25
    assert 1 + dec_batch <= MOD_ROWS and n_ctx % dec_seq == 0

    cond = jnp.concatenate([c_ctx[None], c, jnp.zeros((MOD_ROWS - 1 - dec_batch, d), f32)], 0)
    w_in_b = w_in.astype(bf16)
    wa, wl, wo = w_br_attn.astype(bf16), w_br_lru.astype(bf16), w_out.astype(bf16)
    sg, su, sd = sh_w_gate.astype(bf16), sh_w_up.astype(bf16), sh_w_down.astype(bf16)
    rwt = jnp.swapaxes(router_w, 1, 2)
    rb3 = router_b.reshape(depth, n_exp, 1)
    wg = _block_diag_gates(gate_w_r, gate_w_i, 256)
    bg = jnp.stack([gate_b_r[:, 0], gate_b_i[:, 0], gate_b_r[:, 1], gate_b_i[:, 1]], axis=1)
    conv_b3 = conv_b.reshape(depth, 1, d_lru)
    subln_w3 = subln_w.reshape(depth, 1, V_DIM)
    ln_g3 = ln_g.reshape(depth * 2, 1, d)
    ln_b3 = ln_b.reshape(depth * 2, 1, d)
    ck = cache_k.reshape(dec_batch, depth, past, N_HEADS * 2 * HEAD_DIM)
    cv = cache_v.reshape(dec_batch, depth, past, attn_w)
    cos_t, sin_t = _rope_tables(dec_seq, 512)
    h0_ctx = jnp.zeros((1, 2, d_lru), f32)

    mod4 = _mod_table(cond, w_mod, b_mod).reshape(depth * MOD_ROWS * 6, 1, d)
    x = jnp.concatenate([x_prompt.reshape(n_ctx, d), x_sample.reshape(n_lat, d)], 0)
    n_rows = n * TOP_K + (n // ROUTE_TILE) * n_exp * (SEG_ALIGN - 1) + n_exp * (MOE_BLOCK - 1)
    n_rows = -(-n_rows // MOE_BLOCK) * MOE_BLOCK
    xr_col, yr_col = 2 * qk + attn_w, 2 * qk + attn_w + d_lru
    ga_col, gl_col = yr_col + d_lru, yr_col + d_lru + d

    new_k, new_v, new_s = [], [], []
    for l in range(depth):
        lam_init = 0.8 - 0.6 * math.exp(-0.3 * l)
        y = _inproj(x, mod4, w_in_b, cos_t, sin_t, l, n_ctx, dec_seq, 2 * qk)
        new_k.append(y[:n_ctx, qk:2 * qk].reshape(batch, seq, N_HEADS, 2 * HEAD_DIM))
        new_v.append(y[:n_ctx, 2 * qk:2 * qk + attn_w].reshape(batch, seq, N_HEADS, V_DIM))
        attn_c = _attention(y, lam_param, subln_w3, l, lam_init, 0, batch, seq)
        attn_l = _attention(y, lam_param, subln_w3, l, lam_init, n_ctx, dec_batch, dec_seq, ck, cv)
        rec_c, hfin = _lru(y, conv_w, conv_b3, wg, bg, lru_lambda, h0_ctx, l, 0, batch, seq, xr_col, yr_col)
        rec_l, _ = _lru(y, conv_w, conv_b3, wg, bg, lru_lambda, state_lru[:, l], l, n_ctx, dec_batch, dec_seq,
                        xr_col, yr_col)
        new_s.append(hfin)
        x1, h2 = _merge(attn_c, attn_l, rec_c, rec_l, y, x, mod4, ln_g3, ln_b3, wa, wl, wo, l, n_ctx, dec_seq, ga_col, gl_col, alpha)
        loc, gate, tab = _route(h2, rwt, rb3, l)
        cpad, grel = tab[:, :, 0], tab[:, :, 1]
        used = grel[-1] + cpad[-1]
        padded = (used + MOE_BLOCK - 1) // MOE_BLOCK * MOE_BLOCK
        pad_end = jnp.cumsum(padded)
        pad_start = pad_end - padded
        gabs = (pad_start[None, :] + grel).reshape(-1).astype(i32)
        tot = jnp.sum(cpad, axis=1).astype(i32)
        blk_row0 = jnp.arange(n_rows // MOE_BLOCK, dtype=i32) * MOE_BLOCK
        blk_e = jnp.minimum(jnp.sum(pad_end[None, :] <= blk_row0[:, None], axis=1), n_exp - 1).astype(i32)
        n_used = (pad_end[-1:] // MOE_BLOCK).astype(i32)
        cpad = cpad.reshape(-1)
        xs = _dispatch(h2, loc, cpad, gabs, tot, (pad_start + used).astype(i32), (padded - used).astype(i32),
                       n_rows, n_exp)
        ys = _experts(xs, blk_e, n_used, exp_w_gate, exp_w_up, exp_w_down, l)
        x = _combine(ys, loc, gate, cpad, gabs, tot, x1, h2, mod4, ln_g3, ln_b3, sg, su, sd, l, n_ctx, dec_seq,
                     alpha, n_exp)
    return (x[:n_ctx].reshape(batch, seq, d), x[n_ctx:].reshape(dec_batch, dec_seq, d),
            jnp.stack(new_k, 1), jnp.stack(new_v, 1), jnp.stack(new_s, 1))
```

```python
import functools
import math

import jax
import jax.numpy as jnp
from jax import lax
from jax.experimental import pallas as pl
from jax.experimental.pallas import tpu as pltpu

f32, bf16, i32 = jnp.float32, jnp.bfloat16, jnp.int32

GRID_W = 64
N_HEADS = 8
HEAD_DIM = 64
V_DIM = 2 * HEAD_DIM
ROPE_BASE = 10000.0
LRU_BLOCKS = 16
CONV_WIDTH = 4
LRU_C = 8.0
TOP_K = 8
N_GROUPS = 8
TOPK_GROUPS = 4
ROUTE_SCALE = 2.5
LN_EPS = 1e-5
MOD_ROWS = 8
MOE_BLOCK = 512
ROUTE_TILE = 256
SEG_ALIGN = 16
SCAN_UNROLL = 4
VMEM_LIMIT = 56 * 1024 * 1024
HIGHEST = lax.Precision.HIGHEST


def _cparams(n_axes):
    return pltpu.CompilerParams(dimension_semantics=("arbitrary",) * n_axes,
                                vmem_limit_bytes=VMEM_LIMIT)


def _tile(n, pref):
    t = min(n, pref)
    while n % t:
        t -= 8
    assert t > 0 and n % t == 0
    return t


def _sigmoid(x):
    return 0.5 * jnp.tanh(0.5 * x) + 0.5


def _layer_norm(z, g, b):
    mu = jnp.mean(z, -1, keepdims=True)
    zc = z - mu
    var = jnp.mean(zc * zc, -1, keepdims=True)
    return zc * lax.rsqrt(var + LN_EPS) * g + b


def _mod_kernel(c_ref, w_ref, b_ref, o_ref):
    c = c_ref[...]
    a = c * jax.nn.sigmoid(c)
    o_ref[...] = jnp.dot(a, w_ref[...], precision=HIGHEST, preferred_element_type=f32) + b_ref[...]


def _mod_table(cond, w_mod, b_mod):
    depth, d, d6 = w_mod.shape
    tn = _tile(d6, 1024)
    return pl.pallas_call(
        _mod_kernel,
        grid=(depth, d6 // tn),
        in_specs=[pl.BlockSpec((MOD_ROWS, d), lambda l, j: (0, 0)),
                  pl.BlockSpec((None, d, tn), lambda l, j: (l, 0, j)),
                  pl.BlockSpec((None, 1, tn), lambda l, j: (l, 0, j))],
        out_specs=pl.BlockSpec((None, MOD_ROWS, tn), lambda l, j: (l, 0, j)),
        out_shape=jax.ShapeDtypeStruct((depth, MOD_ROWS, d6), f32),
        compiler_params=_cparams(2),
        name="mod_table",
    )(cond, w_mod, b_mod.reshape(depth, 1, d6))


def _inproj_kernel(x_ref, sh_ref, sc_ref, w_ref, cos_ref, sin_ref, o_ref, hb_ref, *,
                   n_ctx_tiles, n_rope_cols, tn):
    i = pl.program_id(0)
    j = pl.program_id(1)

    @pl.when(j == 0)
    def _():
        hb_ref[...] = (x_ref[...] * (1.0 + sc_ref[0]) + sh_ref[0]).astype(bf16)

    acc = jnp.dot(hb_ref[...], w_ref[...], preferred_element_type=f32)
    rope = jnp.logical_and(i >= n_ctx_tiles, j < n_rope_cols)

    @pl.when(rope)
    def _():
        lane = lax.broadcasted_iota(i32, acc.shape, 1)
        first = (lane % (HEAD_DIM // 2)) < (HEAD_DIM // 4)
        partner = jnp.where(first, pltpu.roll(acc, tn - HEAD_DIM // 4, 1),
                            pltpu.roll(acc, HEAD_DIM // 4, 1))
        o_ref[...] = acc * cos_ref[...] + partner * sin_ref[...]

    @pl.when(jnp.logical_not(rope))
    def _():
        o_ref[...] = acc


def _inproj(x, mod4, w_in_b, cos_t, sin_t, l, n_ctx, dec_seq, qk_cols):
    n, d = x.shape
    d_in = w_in_b.shape[-1]
    tm = _tile(math.gcd(n_ctx, dec_seq), 1024)
    tn = cos_t.shape[1]
    n_ctx_tiles = n_ctx // tm
    tiles_per_seq = dec_seq // tm

    def seg(i):
        return jnp.where(i < n_ctx_tiles, 0, 1 + (i - n_ctx_tiles) // tiles_per_seq)

    def pos(i):
        return jnp.where(i < n_ctx_tiles, 0, (i - n_ctx_tiles) % tiles_per_seq)

    base = l * MOD_ROWS * 6
    return pl.pallas_call(
        functools.partial(_inproj_kernel, n_ctx_tiles=n_ctx_tiles, n_rope_cols=qk_cols // tn, tn=tn),
        grid=(n // tm, d_in // tn),
        in_specs=[pl.BlockSpec((tm, d), lambda i, j: (i, 0)),
                  pl.BlockSpec((1, 1, d), lambda i, j: (base + seg(i) * 6 + 0, 0, 0)),
                  pl.BlockSpec((1, 1, d), lambda i, j: (base + seg(i) * 6 + 1, 0, 0)),
                  pl.BlockSpec((None, d, tn), lambda i, j: (l, 0, j)),
                  pl.BlockSpec((tm, tn), lambda i, j: (pos(i), 0)),
                  pl.BlockSpec((tm, tn), lambda i, j: (pos(i), 0))],
        out_specs=pl.BlockSpec((tm, tn), lambda i, j: (i, j)),
        out_shape=jax.ShapeDtypeStruct((n, d_in), f32),
        scratch_shapes=[pltpu.VMEM((tm, d), bf16)],
        compiler_params=_cparams(2),
        name="in_proj",
    )(x, mod4, mod4, w_in_b, cos_t, sin_t)


def _key_chunk(lk):
    for ck in (512, 384, 256, 128):
        if lk % ck == 0:
            return ck
    return lk


def _attn_kernel(*refs, has_cache, lam_init, past):
    if has_cache:
        q_ref, k_ref, v_ref, ck_ref, cv_ref, lp_ref, sw_ref, o_ref, kt_ref, vb_ref, s_ref = refs
    else:
        q_ref, k_ref, v_ref, lp_ref, sw_ref, o_ref, kt_ref, vb_ref, s_ref = refs

    @pl.when(pl.program_id(2) == 0)
    def _():
        if has_cache:
            kt_ref[:, :past] = ck_ref[...].T.astype(bf16)
            kt_ref[:, past:] = k_ref[...].T.astype(bf16)
            vb_ref[:past, :] = cv_ref[...].astype(bf16)
            vb_ref[past:, :] = v_ref[...].astype(bf16)
        else:
            kt_ref[...] = k_ref[...].T.astype(bf16)
            vb_ref[...] = v_ref[...].astype(bf16)

    tq = q_ref.shape[0]
    q = q_ref[...] * (HEAD_DIM ** -0.5 * math.log2(math.e))
    lane = lax.broadcasted_iota(i32, q.shape, 1)
    q2 = jnp.concatenate([jnp.where(lane < HEAD_DIM, q, 0.0).astype(bf16),
                          jnp.where(lane >= HEAD_DIM, q, 0.0).astype(bf16)], axis=0)
    lk = kt_ref.shape[1]
    ck = _key_chunk(lk)
    lanes = 128
    mx = None
    for c in range(lk // ck):
        s_c = jnp.dot(q2, kt_ref[:, c * ck:(c + 1) * ck], preferred_element_type=f32)
        s_ref[:, c * ck:(c + 1) * ck] = s_c
        for j in range(ck // lanes):
            t = s_c[:, j * lanes:(j + 1) * lanes]
            mx = t if mx is None else jnp.maximum(mx, t)
    mb = jnp.broadcast_to(jnp.max(mx, axis=-1, keepdims=True), (2 * tq, lanes))
    ls = jnp.zeros((2 * tq, lanes), f32)
    for j in range(lk // lanes):
        e_t = jnp.exp2(s_ref[:, j * lanes:(j + 1) * lanes] - mb)
        ls = ls + e_t
        s_ref[:, j * lanes:(j + 1) * lanes] = e_t
    inv = 1.0 / jnp.sum(ls, axis=-1, keepdims=True)
    lp = lp_ref[...]
    lam = (jnp.exp(jnp.sum(lp[0:1] * lp[1:2], keepdims=True))
           - jnp.exp(jnp.sum(lp[2:3] * lp[3:4], keepdims=True)) + lam_init)
    c0 = jnp.broadcast_to(inv[:tq], (tq, lanes))
    c1 = jnp.broadcast_to(lam * inv[tq:], (tq, lanes))
    o = jnp.zeros((tq, V_DIM), f32)
    for c in range(lk // ck):
        a_c = jnp.concatenate(
            [(s_ref[:tq, j * lanes:(j + 1) * lanes] * c0 - s_ref[tq:, j * lanes:(j + 1) * lanes] * c1).astype(bf16)
             for j in range(c * ck // lanes, (c + 1) * ck // lanes)], axis=1)
        o = o + jnp.dot(a_c, vb_ref[c * ck:(c + 1) * ck, :], preferred_element_type=f32)
    o = o * lax.rsqrt(jnp.mean(o * o, -1, keepdims=True) + LN_EPS) * sw_ref[...]
    o_ref[...] = o * (1.0 - lam_init)


def _attention(y, lam_param, subln_w3, l, lam_init, row0, nseq, seq, cache_k=None, cache_v=None):
    has_cache = cache_k is not None
    past = cache_k.shape[2] if has_cache else 0
    lk = past + seq
    tq = _tile(seq, 256)
    nq = seq // tq
    assert row0 % seq == 0
    qb0, kb0 = row0 // tq, row0 // seq
    in_specs = [pl.BlockSpec((tq, V_DIM), lambda b, h, i: (qb0 + b * nq + i, h)),
                pl.BlockSpec((seq, V_DIM), lambda b, h, i: (kb0 + b, N_HEADS + h)),
                pl.BlockSpec((seq, V_DIM), lambda b, h, i: (kb0 + b, 2 * N_HEADS + h))]
    args = [y, y, y]
    if has_cache:
        in_specs += [pl.BlockSpec((None, None, past, V_DIM), lambda b, h, i: (b, l, 0, h)),
                     pl.BlockSpec((None, None, past, V_DIM), lambda b, h, i: (b, l, 0, h))]
        args += [cache_k, cache_v]
    in_specs += [pl.BlockSpec((None, 4, HEAD_DIM), lambda b, h, i: (l, 0, 0)),
                 pl.BlockSpec((None, 1, V_DIM), lambda b, h, i: (l, 0, 0))]
    args += [lam_param, subln_w3]
    return pl.pallas_call(
        functools.partial(_attn_kernel, has_cache=has_cache, lam_init=lam_init, past=past),
        grid=(nseq, N_HEADS, nq),
        in_specs=in_specs,
        out_specs=pl.BlockSpec((tq, V_DIM), lambda b, h, i: (b * nq + i, h)),
        out_shape=jax.ShapeDtypeStruct((nseq * seq, N_HEADS * V_DIM), f32),
        scratch_shapes=[pltpu.VMEM((V_DIM, lk), bf16), pltpu.VMEM((lk, V_DIM), bf16),
                        pltpu.VMEM((2 * tq, lk), f32)],
        compiler_params=_cparams(3),
        name="diff_attn_lat" if has_cache else "diff_attn_ctx",
    )(*args)


def _lru_kernel(xr_ref, yr_ref, cw_ref, cb_ref, wg_ref, bg_ref, lam_ref, h0_ref,
                rec_ref, hfin_ref, xp_ref, a_ref, b_ref, *, chunk):
    seq, c = xr_ref.shape
    xp_ref[0:8, :] = jnp.zeros((8, c), f32)
    xp_ref[8:8 + seq, :] = xr_ref[...]
    xp_ref[8 + seq:16 + seq, :] = jnp.zeros((8, c), f32)
    lam = lam_ref[...]
    nlam = -lam
    softplus = jnp.maximum(nlam, 0.0) + jnp.log(1.0 + jnp.exp(-jnp.abs(nlam)))
    for ci in range(seq // chunk):
        t0 = ci * chunk
        xc = cb_ref[...] + sum(cw_ref[j:j + 1, :] * xp_ref[8 + t0 + j - 1:8 + t0 + j - 1 + chunk, :]
                               for j in range(CONV_WIDTH))
        xcb = xc.astype(bf16)
        for d in range(2):
            r = _sigmoid(jnp.dot(xcb, wg_ref[2 * d], preferred_element_type=f32) + bg_ref[2 * d:2 * d + 1, :])
            g = _sigmoid(jnp.dot(xcb, wg_ref[2 * d + 1], preferred_element_type=f32)
                         + bg_ref[2 * d + 1:2 * d + 2, :])
            log_a = -LRU_C * r * softplus[d:d + 1, :]
            a = jnp.exp(log_a)
            a_ref[d, t0:t0 + chunk, :] = a
            u = -jnp.tanh(log_a) * (a * a + 1.0)
            b_ref[d, t0:t0 + chunk, :] = jnp.where(u > 0.0, u * lax.rsqrt(u), 0.0) * (g * xc)

    rows = lax.broadcasted_iota(i32, (8, c), 0)
    nch = seq // 8

    def scan_body(i, carry):
        cf, cb = carry
        tf = pl.multiple_of(8 * i, 8)
        af = a_ref[0, pl.ds(tf, 8), :]
        bf = b_ref[0, pl.ds(tf, 8), :]
        for d in (1, 2, 4):
            keep = rows >= d
            bf = af * jnp.where(keep, pltpu.roll(bf, d, 0), 0.0) + bf
            af = af * jnp.where(keep, pltpu.roll(af, d, 0), 1.0)
        hf = af * cf + bf
        rec_ref[pl.ds(tf, 8), :] = hf
        tb = pl.multiple_of(8 * (nch - 1 - i), 8)
        ab = a_ref[1, pl.ds(tb, 8), :]
        bb = b_ref[1, pl.ds(tb, 8), :]
        for d in (1, 2, 4):
            keep = rows < 8 - d
            bb = ab * jnp.where(keep, pltpu.roll(bb, 8 - d, 0), 0.0) + bb
            ab = ab * jnp.where(keep, pltpu.roll(ab, 8 - d, 0), 1.0)
        hb = ab * cb + bb
        xp_ref[pl.ds(tb, 8), :] = hb
        return hf[7:8, :], hb[0:1, :]

    h0 = h0_ref[...]
    cf, cb = lax.fori_loop(0, nch, scan_body, (h0[0:1, :], h0[1:2, :]), unroll=SCAN_UNROLL)
    hfin_ref[0:1, :] = cf
    hfin_ref[1:2, :] = cb
    for ci in range(seq // chunk):
        t0 = ci * chunk
        rec_ref[t0:t0 + chunk, :] = ((rec_ref[t0:t0 + chunk, :] + xp_ref[t0:t0 + chunk, :])
                                     * jax.nn.gelu(yr_ref[t0:t0 + chunk, :]))


def _lru(y, conv_w, conv_b3, wg, bg, lru_lambda, h0, l, row0, nseq, seq, xr_col, yr_col):
    d_lru = conv_w.shape[-1]
    cb = 256
    ncb = d_lru // cb
    assert row0 % seq == 0
    rb0 = row0 // seq
    h0_batched = h0.shape[0] > 1
    in_specs = [pl.BlockSpec((seq, cb), lambda b, c: (rb0 + b, xr_col // cb + c)),
                pl.BlockSpec((seq, cb), lambda b, c: (rb0 + b, yr_col // cb + c)),
                pl.BlockSpec((None, CONV_WIDTH, cb), lambda b, c: (l, 0, c)),
                pl.BlockSpec((None, 1, cb), lambda b, c: (l, 0, c)),
                pl.BlockSpec((None, None, 4, cb, cb), lambda b, c: (l, c, 0, 0, 0)),
                pl.BlockSpec((None, 4, cb), lambda b, c: (l, 0, c)),
                pl.BlockSpec((None, 2, cb), lambda b, c: (l, 0, c)),
                pl.BlockSpec((None, 2, cb), (lambda b, c: (b, 0, c)) if h0_batched else (lambda b, c: (0, 0, c)))]
    return pl.pallas_call(
        functools.partial(_lru_kernel, chunk=_tile(seq, 512)),
        grid=(nseq, ncb),
        in_specs=in_specs,
        out_specs=[pl.BlockSpec((seq, cb), lambda b, c: (b, c)),
                   pl.BlockSpec((None, 2, cb), lambda b, c: (b, 0, c))],
        out_shape=[jax.ShapeDtypeStruct((nseq * seq, d_lru), f32),
                   jax.ShapeDtypeStruct((nseq, 2, d_lru), f32)],
        scratch_shapes=[pltpu.VMEM((seq + 16, cb), f32), pltpu.VMEM((2, seq, cb), f32),
                        pltpu.VMEM((2, seq, cb), f32)],
        compiler_params=_cparams(2),
        name="rglru",
    )(y, y, conv_w, conv_b3, wg, bg, lru_lambda, h0)


def _merge_kernel(atc_ref, atl_ref, rcc_ref, rcl_ref, ga_ref, gl_ref, x_ref, gt_ref, shf_ref, scf_ref,
                  g_ref, b_ref, wa_ref, wl_ref, wo_ref, x1_ref, h2_ref, ab_ref, rb_ref, *, alpha, n_ctx_tiles):
    @pl.when(pl.program_id(0) < n_ctx_tiles)
    def _():
        ab_ref[...] = atc_ref[...].astype(bf16)
        rb_ref[...] = rcc_ref[...].astype(bf16)

    @pl.when(pl.program_id(0) >= n_ctx_tiles)
    def _():
        ab_ref[...] = atl_ref[...].astype(bf16)
        rb_ref[...] = rcl_ref[...].astype(bf16)

    pa = jnp.dot(ab_ref[...], wa_ref[...], preferred_element_type=f32)
    pr = jnp.dot(rb_ref[...], wl_ref[...], preferred_element_type=f32)
    merged = jax.nn.sigmoid(ga_ref[...]) * pa + jax.nn.sigmoid(gl_ref[...]) * pr
    out = jnp.dot(merged.astype(bf16), wo_ref[...], preferred_element_type=f32)
    x1 = _layer_norm(alpha * x_ref[...] + gt_ref[0] * out, g_ref[0], b_ref[0])
    x1_ref[...] = x1
    h2_ref[...] = x1 * (1.0 + scf_ref[0]) + shf_ref[0]


def _merge(attn_c, attn_l, rec_c, rec_l, y, x, mod4, ln_g3, ln_b3, wa, wl, wo, l, n_ctx, dec_seq, ga_col, gl_col,
           alpha):
    n, d = x.shape
    tm = _tile(math.gcd(n_ctx, dec_seq), 256)
    n_ctx_tiles = n_ctx // tm
    tiles_per_seq = dec_seq // tm

    def seg(i):
        return jnp.where(i < n_ctx_tiles, 0, 1 + (i - n_ctx_tiles) // tiles_per_seq)

    base = l * MOD_ROWS * 6
    row = lambda i: (i, 0)
    ctx_row = lambda i: (jnp.minimum(i, n_ctx_tiles - 1), 0)
    lat_row = lambda i: (jnp.maximum(i - n_ctx_tiles, 0), 0)
    mod = lambda k: pl.BlockSpec((1, 1, d), lambda i: (base + seg(i) * 6 + k, 0, 0))
    wspec = pl.BlockSpec((None, d, d), lambda i: (l, 0, 0))
    return pl.pallas_call(
        functools.partial(_merge_kernel, alpha=alpha, n_ctx_tiles=n_ctx_tiles),
        grid=(n // tm,),
        in_specs=[pl.BlockSpec((tm, d), ctx_row), pl.BlockSpec((tm, d), lat_row),
                  pl.BlockSpec((tm, d), ctx_row), pl.BlockSpec((tm, d), lat_row),
                  pl.BlockSpec((tm, d), lambda i: (i, ga_col // d)),
                  pl.BlockSpec((tm, d), lambda i: (i, gl_col // d)),
                  pl.BlockSpec((tm, d), row), mod(2), mod(3), mod(4),
                  pl.BlockSpec((1, 1, d), lambda i: (2 * l, 0, 0)),
                  pl.BlockSpec((1, 1, d), lambda i: (2 * l, 0, 0)),
                  wspec, wspec, wspec],
        out_specs=[pl.BlockSpec((tm, d), row), pl.BlockSpec((tm, d), row)],
        out_shape=[jax.ShapeDtypeStruct((n, d), f32), jax.ShapeDtypeStruct((n, d), f32)],
        scratch_shapes=[pltpu.VMEM((tm, d), bf16), pltpu.VMEM((tm, d), bf16)],
        compiler_params=_cparams(1),
        name="merge_outproj_ln",
    )(attn_c, attn_l, rec_c, rec_l, y, y, x, mod4, mod4, mod4, ln_g3, ln_b3, wa, wl, wo)


def _route_kernel(h_ref, rw_ref, rb_ref, loc_ref, gate_ref, tab_ref, carry_ref):
    n_exp = rw_ref.shape[0]
    tm = h_ref.shape[0]
    gsz = n_exp // N_GROUPS

    @pl.when(pl.program_id(0) == 0)
    def _():
        carry_ref[...] = jnp.zeros_like(carry_ref)

    logits = lax.dot_general(rw_ref[...], h_ref[...], (((1,), (1,)), ((), ())),
                             precision=HIGHEST, preferred_element_type=f32)
    s = jax.nn.sigmoid(logits)
    sb = s + rb_ref[...]
    ninf = -jnp.inf
    gs_rows = []
    for g in range(N_GROUPS):
        blk = sb[g * gsz:(g + 1) * gsz]
        m1 = jnp.max(blk, axis=0, keepdims=True)
        cnt = jnp.sum((blk == m1).astype(f32), axis=0, keepdims=True)
        m2 = jnp.max(jnp.where(blk < m1, blk, ninf), axis=0, keepdims=True)
        gs_rows.append(m1 + jnp.where(cnt >= 2.0, m1, m2))
    gs = jnp.concatenate(gs_rows, axis=0)
    gi = lax.broadcasted_iota(i32, gs.shape, 0)
    grank = jnp.zeros(gs.shape, f32)
    for g2 in range(N_GROUPS):
        row = gs[g2:g2 + 1]
        grank += jnp.where((row > gs) | ((row == gs) & (g2 < gi)), 1.0, 0.0)
    gsel = grank < float(TOPK_GROUPS)
    sbm = jnp.concatenate([jnp.where(gsel[g:g + 1], sb[g * gsz:(g + 1) * gsz], ninf)
                           for g in range(N_GROUPS)], axis=0)
    ei = lax.broadcasted_iota(i32, sbm.shape, 0)
    erank = jnp.zeros(sbm.shape, f32)
    for e2 in range(n_exp):
        row = sbm[e2:e2 + 1]
        erank += jnp.where((row > sbm) | ((row == sbm) & (e2 < ei)), 1.0, 0.0)
    sel = erank < float(TOP_K)
    ssel = jnp.where(sel, s, 0.0)
    gate = ssel / jnp.sum(ssel, axis=0, keepdims=True) * ROUTE_SCALE
    selb = sel.astype(bf16)
    ti = lax.broadcasted_iota(i32, (tm, tm), 0)
    tj = lax.broadcasted_iota(i32, (tm, tm), 1)
    upper = (ti < tj).astype(bf16)
    pos_local = jnp.dot(selb, upper, preferred_element_type=f32)
    cnt = jnp.sum(sel.astype(f32), axis=1, keepdims=True)
    cpad = jnp.floor((cnt + (SEG_ALIGN - 1)) * (1.0 / SEG_ALIGN)) * SEG_ALIGN
    cpad_l = jnp.broadcast_to(cpad, (n_exp, 128))
    li = lax.broadcasted_iota(i32, (n_exp, n_exp), 0)
    lj = lax.broadcasted_iota(i32, (n_exp, n_exp), 1)
    lower = (lj < li).astype(bf16)
    run_start = jnp.dot(lower, cpad_l.astype(bf16), preferred_element_type=f32)
    loc = run_start[:, 0:1] + pos_local
    slot = jnp.dot(lower, selb, preferred_element_type=f32)
    l_rows, g_rows = [], []
    for k in range(TOP_K):
        mk = sel & (slot == float(k))
        l_rows.append(jnp.sum(jnp.where(mk, loc, 0.0), axis=0, keepdims=True))
        g_rows.append(jnp.sum(jnp.where(mk, gate, 0.0), axis=0, keepdims=True))
    loc_ref[...] = jnp.concatenate(l_rows, axis=0).astype(i32)
    gate_ref[...] = jnp.concatenate(g_rows, axis=0)
    lane = lax.broadcasted_iota(i32, (n_exp, 128), 1)
    tab_ref[...] = jnp.where(lane == 0, cpad_l, carry_ref[...]).astype(i32)
    carry_ref[...] = carry_ref[...] + cpad_l


def _route(h2, rwt, rb3, l):
    n, d = h2.shape
    n_exp = rwt.shape[1]
    tm = ROUTE_TILE
    assert n % tm == 0
    lst = pl.BlockSpec((TOP_K, tm), lambda i: (0, i))
    return pl.pallas_call(
        _route_kernel,
        grid=(n // tm,),
        in_specs=[pl.BlockSpec((tm, d), lambda i: (i, 0)),
                  pl.BlockSpec((None, n_exp, d), lambda i: (l, 0, 0)),
                  pl.BlockSpec((None, n_exp, 1), lambda i: (l, 0, 0))],
        out_specs=[lst, lst, pl.BlockSpec((None, n_exp, 128), lambda i: (i, 0, 0))],
        out_shape=[jax.ShapeDtypeStruct((TOP_K, n), i32), jax.ShapeDtypeStruct((TOP_K, n), f32),
                   jax.ShapeDtypeStruct((n // tm, n_exp, 128), i32)],
        scratch_shapes=[pltpu.VMEM((n_exp, 128), f32)],
        compiler_params=_cparams(1),
        name="moe_route",
    )(h2, rwt, rb3)


def _pow2_pieces(count, max_rows, fn):
    off = 0
    size = max_rows
    while size >= SEG_ALIGN:
        take = (count & size) != 0
        pl.when(take)(functools.partial(fn, off, size))
        off = off + jnp.where(take, size, 0)
        size //= 2


def _wait_rows(count, max_rows, src_ref, dst_ref, sem):
    def wait(_, size):
        pltpu.make_async_copy(src_ref.at[pl.ds(0, size)], dst_ref.at[pl.ds(0, size)], sem).wait()

    _pow2_pieces(count, max_rows, wait)


def _dispatch_kernel(cp_ref, ga_ref, tot_ref, ts_ref, tl_ref, loc_ref, h_ref, xs_ref, buf_ref, z_ref, sem, *,
                     n_exp, n_tiles, tile_rows, wait_max):
    t = pl.program_id(0)
    slot = t % 2
    tm = h_ref.shape[0]

    @pl.when(t >= 2)
    def _():
        _wait_rows(tot_ref[t - 2], wait_max, buf_ref.at[slot], xs_ref, sem.at[slot])

    loc = loc_ref[...]
    rid = lax.broadcasted_iota(i32, (tile_rows, tm), 0)
    perm = jnp.zeros((tile_rows, tm), f32)
    for k in range(TOP_K):
        perm = jnp.where(loc[k:k + 1, :] == rid, 1.0, perm)
    buf_ref[slot] = jnp.dot(perm.astype(bf16), h_ref[...].astype(bf16),
                            preferred_element_type=f32).astype(bf16)

    def issue(e, lo):
        c = cp_ref[t * n_exp + e]
        g = ga_ref[t * n_exp + e]

        def copy(off, size):
            src = pl.multiple_of(lo + off, SEG_ALIGN)
            dst = pl.multiple_of(g + off, SEG_ALIGN)
            pltpu.make_async_copy(buf_ref.at[slot, pl.ds(src, size)], xs_ref.at[pl.ds(dst, size)],
                                  sem.at[slot]).start()

        _pow2_pieces(c, tm, copy)
        return lo + c

    lax.fori_loop(0, n_exp, issue, 0)

    @pl.when(t == n_tiles - 1)
    def _():
        z_ref[...] = jnp.zeros_like(z_ref)

        def fill(e, _):
            def copy(off, size):
                dst = pl.multiple_of(ts_ref[e] + off, SEG_ALIGN)
                pltpu.make_async_copy(z_ref.at[pl.ds(0, size)], xs_ref.at[pl.ds(dst, size)], sem.at[2]).start()

            _pow2_pieces(tl_ref[e], MOE_BLOCK // 2, copy)
            return 0

        lax.fori_loop(0, n_exp, fill, 0)

        def drain(e, _):
            _wait_rows(tl_ref[e], MOE_BLOCK // 2, z_ref, xs_ref, sem.at[2])
            return 0

        lax.fori_loop(0, n_exp, drain, 0)
        if n_tiles > 1:
            _wait_rows(tot_ref[t - 1], wait_max, buf_ref.at[1 - slot], xs_ref, sem.at[1 - slot])
        _wait_rows(tot_ref[t], wait_max, buf_ref.at[slot], xs_ref, sem.at[slot])


def _tile_rows(tm, n_exp):
    return -(-(tm * TOP_K + n_exp * (SEG_ALIGN - 1)) // 128) * 128


def _wait_max(tile_rows):
    return 1 << (tile_rows.bit_length() - 1)


def _dispatch(h2, loc, cpad, gabs, tot, tail_start, tail_len, n_rows, n_exp):
    n, d = h2.shape
    tm = ROUTE_TILE
    n_tiles = n // tm
    tile_rows = _tile_rows(tm, n_exp)
    return pl.pallas_call(
        functools.partial(_dispatch_kernel, n_exp=n_exp, n_tiles=n_tiles, tile_rows=tile_rows,
                          wait_max=_wait_max(tile_rows)),
        grid_spec=pltpu.PrefetchScalarGridSpec(
            num_scalar_prefetch=5,
            grid=(n_tiles,),
            in_specs=[pl.BlockSpec((TOP_K, tm), lambda i, *_: (0, i)),
                      pl.BlockSpec((tm, d), lambda i, *_: (i, 0))],
            out_specs=pl.BlockSpec(memory_space=pl.ANY),
            scratch_shapes=[pltpu.VMEM((2, tile_rows, d), bf16), pltpu.VMEM((MOE_BLOCK // 2, d), bf16),
                            pltpu.SemaphoreType.DMA((3,))]),
        out_shape=jax.ShapeDtypeStruct((n_rows, d), bf16),
        compiler_params=_cparams(1),
        name="moe_dispatch",
    )(cpad, gabs, tot, tail_start, tail_len, loc, h2)


def _expert_kernel(be_ref, nu_ref, x_ref, wg_ref, wu_ref, wd_ref, o_ref):
    @pl.when(pl.program_id(0) < nu_ref[0])
    def _():
        x = x_ref[...]
        g = jnp.dot(x, wg_ref[...].astype(bf16), preferred_element_type=f32)
        u = jnp.dot(x, wu_ref[...].astype(bf16), preferred_element_type=f32)
        h = (g * jax.nn.sigmoid(g)) * u
        o_ref[...] = jnp.dot(h.astype(bf16), wd_ref[...].astype(bf16),
                             preferred_element_type=f32).astype(bf16)


def _experts(xs, blk_e, n_used, we_g, we_u, we_d, l):
    n_rows, d = xs.shape
    f = we_g.shape[-1]
    nblk = n_rows // MOE_BLOCK
    blk = lambda b, be, nu: (jnp.minimum(b, nu[0] - 1), 0)
    wsel = lambda b, be, nu: (l, be[jnp.minimum(b, nu[0] - 1)], 0, 0)
    return pl.pallas_call(
        _expert_kernel,
        grid_spec=pltpu.PrefetchScalarGridSpec(
            num_scalar_prefetch=2,
            grid=(nblk,),
            in_specs=[pl.BlockSpec((MOE_BLOCK, d), blk),
                      pl.BlockSpec((None, None, d, f), wsel),
                      pl.BlockSpec((None, None, d, f), wsel),
                      pl.BlockSpec((None, None, f, d), wsel)],
            out_specs=pl.BlockSpec((MOE_BLOCK, d), blk)),
        out_shape=jax.ShapeDtypeStruct((n_rows, d), bf16),
        compiler_params=_cparams(1),
        name="moe_experts",
    )(blk_e, n_used, xs, we_g, we_u, we_d)


def _combine_kernel(cp_ref, ga_ref, tot_ref, loc_ref, gate_ref, ys_ref, x1_ref, h2_ref, gt_ref, g_ref, b_ref,
                    sg_ref, su_ref, sd_ref, o_ref, buf_ref, sem, *, alpha, n_exp, n_tiles, tile_rows, wait_max):
    t = pl.program_id(0)
    slot = t % 2
    tm = x1_ref.shape[0]

    def fetch(tile, dst_slot):
        def issue(e, lo):
            c = cp_ref[tile * n_exp + e]
            g = ga_ref[tile * n_exp + e]

            def copy(off, size):
                src = pl.multiple_of(g + off, SEG_ALIGN)
                dst = pl.multiple_of(lo + off, SEG_ALIGN)
                pltpu.make_async_copy(ys_ref.at[pl.ds(src, size)], buf_ref.at[dst_slot, pl.ds(dst, size)],
                                      sem.at[dst_slot]).start()

            _pow2_pieces(c, tm, copy)
            return lo + c

        lax.fori_loop(0, n_exp, issue, 0)

    @pl.when(t == 0)
    def _():
        buf_ref[...] = jnp.zeros_like(buf_ref)
        fetch(0, 0)

    @pl.when(t + 1 < n_tiles)
    def _():
        fetch(t + 1, 1 - slot)

    hb = h2_ref[...].astype(bf16)
    g = jnp.dot(hb, sg_ref[...], preferred_element_type=f32)
    u = jnp.dot(hb, su_ref[...], preferred_element_type=f32)
    shared = jnp.dot(((g * jax.nn.sigmoid(g)) * u).astype(bf16), sd_ref[...], preferred_element_type=f32)
    fill = jnp.zeros((128 - TOP_K, tm), f32)
    gate_t = jnp.concatenate([gate_ref[...], fill], axis=0).T
    loc_t = jnp.concatenate([loc_ref[...].astype(f32), fill], axis=0).T
    rid = lax.broadcasted_iota(i32, (tm, tile_rows), 1).astype(f32)
    wgt = jnp.zeros((tm, tile_rows), f32)
    for k in range(TOP_K):
        wgt = jnp.where(loc_t[:, k:k + 1] == rid, gate_t[:, k:k + 1], wgt)
    _wait_rows(tot_ref[t], wait_max, ys_ref, buf_ref.at[slot], sem.at[slot])
    routed = jnp.dot(wgt.astype(bf16), buf_ref[slot], preferred_element_type=f32)
    o_ref[...] = _layer_norm(alpha * x1_ref[...] + gt_ref[0] * (shared + routed), g_ref[0], b_ref[0])


def _combine(ys, loc, gate, cpad, gabs, tot, x1, h2, mod4, ln_g3, ln_b3, sg, su, sd, l, n_ctx, dec_seq, alpha,
             n_exp):
    n, d = x1.shape
    f = sg.shape[-1]
    tm = ROUTE_TILE
    assert n_ctx % tm == 0 and dec_seq % tm == 0
    n_tiles = n // tm
    tile_rows = _tile_rows(tm, n_exp)
    n_ctx_tiles = n_ctx // tm
    tiles_per_seq = dec_seq // tm

    def seg(i):
        return jnp.where(i < n_ctx_tiles, 0, 1 + (i - n_ctx_tiles) // tiles_per_seq)

    base = l * MOD_ROWS * 6
    row = lambda i, *_: (i, 0)
    lst = lambda i, *_: (0, i)
    return pl.pallas_call(
        functools.partial(_combine_kernel, alpha=alpha, n_exp=n_exp, n_tiles=n_tiles, tile_rows=tile_rows,
                          wait_max=_wait_max(tile_rows)),
        grid_spec=pltpu.PrefetchScalarGridSpec(
            num_scalar_prefetch=3,
            grid=(n_tiles,),
            in_specs=[pl.BlockSpec((TOP_K, tm), lst), pl.BlockSpec((TOP_K, tm), lst),
                      pl.BlockSpec(memory_space=pl.ANY),
                      pl.BlockSpec((tm, d), row), pl.BlockSpec((tm, d), row),
                      pl.BlockSpec((1, 1, d), lambda i, *_: (base + seg(i) * 6 + 5, 0, 0)),
                      pl.BlockSpec((1, 1, d), lambda i, *_: (2 * l + 1, 0, 0)),
                      pl.BlockSpec((1, 1, d), lambda i, *_: (2 * l + 1, 0, 0)),
                      pl.BlockSpec((None, d, f), lambda i, *_: (l, 0, 0)),
                      pl.BlockSpec((None, d, f), lambda i, *_: (l, 0, 0)),
                      pl.BlockSpec((None, f, d), lambda i, *_: (l, 0, 0))],
            out_specs=pl.BlockSpec((tm, d), row),
            scratch_shapes=[pltpu.VMEM((2, tile_rows, d), bf16), pltpu.SemaphoreType.DMA((2,))]),
        out_shape=jax.ShapeDtypeStruct((n, d), f32),
        compiler_params=_cparams(1),
        name="moe_combine_ln",
    )(cpad, gabs, tot, loc, gate, ys, x1, h2, mod4, ln_g3, ln_b3, sg, su, sd)


def _rope_tables(seq, width):
    rows = seq // GRID_W
    row = jnp.repeat(jnp.arange(rows, dtype=f32), GRID_W)
    col = jnp.tile(jnp.arange(GRID_W, dtype=f32), rows)
    half = HEAD_DIM // 2
    inv = ROPE_BASE ** (-jnp.arange(0, half, 2, dtype=f32) / half)
    lane = jnp.arange(width)
    d = lane % HEAD_DIM
    pos = jnp.where((d // half)[None, :] == 0, row[:, None], col[:, None])
    ang = pos * inv[d % (HEAD_DIM // 4)][None, :]
    sign = jnp.where((d % half) < HEAD_DIM // 4, -1.0, 1.0)[None, :]
    return jnp.cos(ang), jnp.sin(ang) * sign


def _block_diag_gates(gate_w_r, gate_w_i, cb):
    depth, _, nb, bw, _ = gate_w_r.shape
    per = cb // bw
    w = jnp.stack([gate_w_r[:, 0], gate_w_i[:, 0], gate_w_r[:, 1], gate_w_i[:, 1]], axis=1)
    w = w.reshape(depth, 4, nb // per, per, bw, bw)
    eye = jnp.eye(per, dtype=w.dtype)
    bd = jnp.einsum("lgcaio,ab->lcgaibo", w, eye)
    return bd.reshape(depth, nb // per, 4, cb, cb).astype(bf16)


def kernel(x_prompt, x_sample, c, cache_k, cache_v, state_lru, c_ctx, w_mod, b_mod, w_in, lam_param, subln_w,
           conv_w, conv_b, gate_w_r, gate_b_r, gate_w_i, gate_b_i, lru_lambda, w_br_attn, w_br_lru, w_out,
           ln_g, ln_b, router_w, router_b, exp_w_gate, exp_w_up, exp_w_down, sh_w_gate, sh_w_up, sh_w_down):
    batch, seq, d = x_prompt.shape
    dec_batch, dec_seq, _ = x_sample.shape
    depth = w_mod.shape[0]
    past = cache_k.shape[2]
    d_lru = conv_w.shape[-1]
    n_exp = router_w.shape[-1]
    qk = N_HEADS * 2 * HEAD_DIM
    attn_w = N_HEADS * V_DIM
    n_ctx, n_lat = batch * seq, dec_batch * dec_seq
    n = n_ctx + n_lat
    alpha = (2 * depth) ** 0.25
    assert 1 + dec_batch <= MOD_ROWS and n_ctx % dec_seq == 0

    cond = jnp.concatenate([c_ctx[None], c, jnp.zeros((MOD_ROWS - 1 - dec_batch, d), f32)], 0)
    w_in_b = w_in.astype(bf16)
    wa, wl, wo = w_br_attn.astype(bf16), w_br_lru.astype(bf16), w_out.astype(bf16)
    sg, su, sd = sh_w_gate.astype(bf16), sh_w_up.astype(bf16), sh_w_down.astype(bf16)
    rwt = jnp.swapaxes(router_w, 1, 2)
    rb3 = router_b.reshape(depth, n_exp, 1)
    wg = _block_diag_gates(gate_w_r, gate_w_i, 256)
    bg = jnp.stack([gate_b_r[:, 0], gate_b_i[:, 0], gate_b_r[:, 1], gate_b_i[:, 1]], axis=1)
    conv_b3 = conv_b.reshape(depth, 1, d_lru)
    subln_w3 = subln_w.reshape(depth, 1, V_DIM)
    ln_g3 = ln_g.reshape(depth * 2, 1, d)
    ln_b3 = ln_b.reshape(depth * 2, 1, d)
    ck = cache_k.reshape(dec_batch, depth, past, N_HEADS * 2 * HEAD_DIM)
    cv = cache_v.reshape(dec_batch, depth, past, attn_w)
    cos_t, sin_t = _rope_tables(dec_seq, 512)
    h0_ctx = jnp.zeros((1, 2, d_lru), f32)

    mod4 = _mod_table(cond, w_mod, b_mod).reshape(depth * MOD_ROWS * 6, 1, d)
    x = jnp.concatenate([x_prompt.reshape(n_ctx, d), x_sample.reshape(n_lat, d)], 0)
    n_rows = n * TOP_K + (n // ROUTE_TILE) * n_exp * (SEG_ALIGN - 1) + n_exp * (MOE_BLOCK - 1)
    n_rows = -(-n_rows // MOE_BLOCK) * MOE_BLOCK
    xr_col, yr_col = 2 * qk + attn_w, 2 * qk + attn_w + d_lru
    ga_col, gl_col = yr_col + d_lru, yr_col + d_lru + d

    new_k, new_v, new_s = [], [], []
    for l in range(depth):
        lam_init = 0.8 - 0.6 * math.exp(-0.3 * l)
        y = _inproj(x, mod4, w_in_b, cos_t, sin_t, l, n_ctx, dec_seq, 2 * qk)
        new_k.append(y[:n_ctx, qk:2 * qk].reshape(batch, seq, N_HEADS, 2 * HEAD_DIM))
        new_v.append(y[:n_ctx, 2 * qk:2 * qk + attn_w].reshape(batch, seq, N_HEADS, V_DIM))
        attn_c = _attention(y, lam_param, subln_w3, l, lam_init, 0, batch, seq)
        attn_l = _attention(y, lam_param, subln_w3, l, lam_init, n_ctx, dec_batch, dec_seq, ck, cv)
        rec_c, hfin = _lru(y, conv_w, conv_b3, wg, bg, lru_lambda, h0_ctx, l, 0, batch, seq, xr_col, yr_col)
        rec_l, _ = _lru(y, conv_w, conv_b3, wg, bg, lru_lambda, state_lru[:, l], l, n_ctx, dec_batch, dec_seq,
                        xr_col, yr_col)
        new_s.append(hfin)
        x1, h2 = _merge(attn_c, attn_l, rec_c, rec_l, y, x, mod4, ln_g3, ln_b3, wa, wl, wo, l, n_ctx, dec_seq, ga_col, gl_col, alpha)
        loc, gate, tab = _route(h2, rwt, rb3, l)
        cpad, grel = tab[:, :, 0], tab[:, :, 1]
        used = grel[-1] + cpad[-1]
        padded = (used + MOE_BLOCK - 1) // MOE_BLOCK * MOE_BLOCK
        pad_end = jnp.cumsum(padded)
        pad_start = pad_end - padded
        gabs = (pad_start[None, :] + grel).reshape(-1).astype(i32)
        tot = jnp.sum(cpad, axis=1).astype(i32)
        blk_row0 = jnp.arange(n_rows // MOE_BLOCK, dtype=i32) * MOE_BLOCK
        blk_e = jnp.minimum(jnp.sum(pad_end[None, :] <= blk_row0[:, None], axis=1), n_exp - 1).astype(i32)
        n_used = (pad_end[-1:] // MOE_BLOCK).astype(i32)
        cpad = cpad.reshape(-1)
        xs = _dispatch(h2, loc, cpad, gabs, tot, (pad_start + used).astype(i32), (padded - used).astype(i32),
                       n_rows, n_exp)
        ys = _experts(xs, blk_e, n_used, exp_w_gate, exp_w_up, exp_w_down, l)
        x = _combine(ys, loc, gate, cpad, gabs, tot, x1, h2, mod4, ln_g3, ln_b3, sg, su, sd, l, n_ctx, dec_seq,
                     alpha, n_exp)
    return (x[:n_ctx].reshape(batch, seq, d), x[n_ctx:].reshape(dec_batch, dec_seq, d),
            jnp.stack(new_k, 1), jnp.stack(new_v, 1), jnp.stack(new_s, 1))
```

```python
import functools
import math

import jax
import jax.numpy as jnp
from jax import lax
from jax.experimental import pallas as pl
from jax.experimental.pallas import tpu as pltpu

f32, bf16, i32 = jnp.float32, jnp.bfloat16, jnp.int32

GRID_W = 64
N_HEADS = 8
HEAD_DIM = 64
V_DIM = 2 * HEAD_DIM
ROPE_BASE = 10000.0
LRU_BLOCKS = 16
CONV_WIDTH = 4
LRU_C = 8.0
TOP_K = 8
N_GROUPS = 8
TOPK_GROUPS = 4
ROUTE_SCALE = 2.5
LN_EPS = 1e-5
MOD_ROWS = 8
MOE_BLOCK = 1024
ROUTE_TILE = 256
SEG_ALIGN = 16
SCAN_UNROLL = 4
VMEM_LIMIT = 56 * 1024 * 1024
HIGHEST = lax.Precision.HIGHEST


def _cparams(n_axes):
    return pltpu.CompilerParams(dimension_semantics=("arbitrary",) * n_axes,
                                vmem_limit_bytes=VMEM_LIMIT)


def _tile(n, pref):
    t = min(n, pref)
    while n % t:
        t -= 8
    assert t > 0 and n % t == 0
    return t


def _sigmoid(x):
    return 0.5 * jnp.tanh(0.5 * x) + 0.5


def _layer_norm(z, g, b):
    mu = jnp.mean(z, -1, keepdims=True)
    zc = z - mu
    var = jnp.mean(zc * zc, -1, keepdims=True)
    return zc * lax.rsqrt(var + LN_EPS) * g + b


def _mod_kernel(c_ref, w_ref, b_ref, o_ref):
    c = c_ref[...]
    a = c * jax.nn.sigmoid(c)
    o_ref[...] = jnp.dot(a, w_ref[...], precision=HIGHEST, preferred_element_type=f32) + b_ref[...]


def _mod_table(cond, w_mod, b_mod):
    depth, d, d6 = w_mod.shape
    tn = _tile(d6, 1024)
    return pl.pallas_call(
        _mod_kernel,
        grid=(depth, d6 // tn),
        in_specs=[pl.BlockSpec((MOD_ROWS, d), lambda l, j: (0, 0)),
                  pl.BlockSpec((None, d, tn), lambda l, j: (l, 0, j)),
                  pl.BlockSpec((None, 1, tn), lambda l, j: (l, 0, j))],
        out_specs=pl.BlockSpec((None, MOD_ROWS, tn), lambda l, j: (l, 0, j)),
        out_shape=jax.ShapeDtypeStruct((depth, MOD_ROWS, d6), f32),
        compiler_params=_cparams(2),
        name="mod_table",
    )(cond, w_mod, b_mod.reshape(depth, 1, d6))


def _inproj_kernel(x_ref, sh_ref, sc_ref, w_ref, cos_ref, sin_ref, o_ref, hb_ref, *,
                   n_ctx_tiles, n_rope_cols, tn):
    i = pl.program_id(0)
    j = pl.program_id(1)

    @pl.when(j == 0)
    def _():
        hb_ref[...] = (x_ref[...] * (1.0 + sc_ref[0]) + sh_ref[0]).astype(bf16)

    acc = jnp.dot(hb_ref[...], w_ref[...], preferred_element_type=f32)
    rope = jnp.logical_and(i >= n_ctx_tiles, j < n_rope_cols)

    @pl.when(rope)
    def _():
        lane = lax.broadcasted_iota(i32, acc.shape, 1)
        first = (lane % (HEAD_DIM // 2)) < (HEAD_DIM // 4)
        partner = jnp.where(first, pltpu.roll(acc, tn - HEAD_DIM // 4, 1),
                            pltpu.roll(acc, HEAD_DIM // 4, 1))
        o_ref[...] = acc * cos_ref[...] + partner * sin_ref[...]

    @pl.when(jnp.logical_not(rope))
    def _():
        o_ref[...] = acc


def _inproj(x, mod4, w_in_b, cos_t, sin_t, l, n_ctx, dec_seq, qk_cols):
    n, d = x.shape
    d_in = w_in_b.shape[-1]
    tm = _tile(math.gcd(n_ctx, dec_seq), 1024)
    tn = cos_t.shape[1]
    n_ctx_tiles = n_ctx // tm
    tiles_per_seq = dec_seq // tm

    def seg(i):
        return jnp.where(i < n_ctx_tiles, 0, 1 + (i - n_ctx_tiles) // tiles_per_seq)

    def pos(i):
        return jnp.where(i < n_ctx_tiles, 0, (i - n_ctx_tiles) % tiles_per_seq)

    base = l * MOD_ROWS * 6
    return pl.pallas_call(
        functools.partial(_inproj_kernel, n_ctx_tiles=n_ctx_tiles, n_rope_cols=qk_cols // tn, tn=tn),
        grid=(n // tm, d_in // tn),
        in_specs=[pl.BlockSpec((tm, d), lambda i, j: (i, 0)),
                  pl.BlockSpec((1, 1, d), lambda i, j: (base + seg(i) * 6 + 0, 0, 0)),
                  pl.BlockSpec((1, 1, d), lambda i, j: (base + seg(i) * 6 + 1, 0, 0)),
                  pl.BlockSpec((None, d, tn), lambda i, j: (l, 0, j)),
                  pl.BlockSpec((tm, tn), lambda i, j: (pos(i), 0)),
                  pl.BlockSpec((tm, tn), lambda i, j: (pos(i), 0))],
        out_specs=pl.BlockSpec((tm, tn), lambda i, j: (i, j)),
        out_shape=jax.ShapeDtypeStruct((n, d_in), f32),
        scratch_shapes=[pltpu.VMEM((tm, d), bf16)],
        compiler_params=_cparams(2),
        name="in_proj",
    )(x, mod4, mod4, w_in_b, cos_t, sin_t)


def _key_chunk(lk):
    for ck in (512, 384, 256, 128):
        if lk % ck == 0:
            return ck
    return lk


def _attn_kernel(*refs, has_cache, lam_init, past, heads):
    if has_cache:
        q_ref, k_ref, v_ref, ck_ref, cv_ref, lp_ref, sw_ref, o_ref, kt_ref, vb_ref, s_ref = refs
    else:
        q_ref, k_ref, v_ref, lp_ref, sw_ref, o_ref, kt_ref, vb_ref, s_ref = refs

    @pl.when(pl.program_id(2) == 0)
    def _():
        for h in range(heads):
            cols = slice(h * V_DIM, (h + 1) * V_DIM)
            if has_cache:
                kt_ref[h, :, :past] = ck_ref[:, cols].T.astype(bf16)
                vb_ref[h, :past, :] = cv_ref[:, cols].astype(bf16)
            kt_ref[h, :, past:] = k_ref[:, cols].T.astype(bf16)
            vb_ref[h, past:, :] = v_ref[:, cols].astype(bf16)

    lp = lp_ref[...]
    lam = (jnp.exp(jnp.sum(lp[0:1] * lp[1:2], keepdims=True))
           - jnp.exp(jnp.sum(lp[2:3] * lp[3:4], keepdims=True)) + lam_init)
    for h in range(heads):
        cols = slice(h * V_DIM, (h + 1) * V_DIM)
        o = _attn_head(q_ref[:, cols], kt_ref.at[h], vb_ref.at[h], s_ref, lam)
        o = o * lax.rsqrt(jnp.mean(o * o, -1, keepdims=True) + LN_EPS) * sw_ref[...]
        o_ref[:, cols] = o * (1.0 - lam_init)


def _attn_head(q, kt_ref, vb_ref, s_ref, lam):
    tq = q.shape[0]
    q = q * (HEAD_DIM ** -0.5 * math.log2(math.e))
    lane = lax.broadcasted_iota(i32, q.shape, 1)
    q2 = jnp.concatenate([jnp.where(lane < HEAD_DIM, q, 0.0).astype(bf16),
                          jnp.where(lane >= HEAD_DIM, q, 0.0).astype(bf16)], axis=0)
    lk = kt_ref.shape[1]
    ck = _key_chunk(lk)
    lanes = 128
    mx = None
    for c in range(lk // ck):
        s_c = jnp.dot(q2, kt_ref[:, c * ck:(c + 1) * ck], preferred_element_type=f32)
        s_ref[:, c * ck:(c + 1) * ck] = s_c
        for j in range(ck // lanes):
            t = s_c[:, j * lanes:(j + 1) * lanes]
            mx = t if mx is None else jnp.maximum(mx, t)
    mb = jnp.broadcast_to(jnp.max(mx, axis=-1, keepdims=True), (2 * tq, lanes))
    ls = jnp.zeros((2 * tq, lanes), f32)
    for j in range(lk // lanes):
        e_t = jnp.exp2(s_ref[:, j * lanes:(j + 1) * lanes] - mb)
        ls = ls + e_t
        s_ref[:, j * lanes:(j + 1) * lanes] = e_t
    inv = 1.0 / jnp.sum(ls, axis=-1, keepdims=True)
    c0 = jnp.broadcast_to(inv[:tq], (tq, lanes))
    c1 = jnp.broadcast_to(lam * inv[tq:], (tq, lanes))
    o = jnp.zeros((tq, V_DIM), f32)
    for c in range(lk // ck):
        a_c = jnp.concatenate(
            [(s_ref[:tq, j * lanes:(j + 1) * lanes] * c0 - s_ref[tq:, j * lanes:(j + 1) * lanes] * c1).astype(bf16)
             for j in range(c * ck // lanes, (c + 1) * ck // lanes)], axis=1)
        o = o + jnp.dot(a_c, vb_ref[c * ck:(c + 1) * ck, :], preferred_element_type=f32)
    return o


def _attention(y, lam_param, subln_w3, l, lam_init, row0, nseq, seq, cache_k=None, cache_v=None):
    has_cache = cache_k is not None
    past = cache_k.shape[2] if has_cache else 0
    lk = past + seq
    tq = _tile(seq, 256)
    nq = seq // tq
    heads = N_HEADS if lk <= 512 else 1
    hg = N_HEADS // heads
    w = heads * V_DIM
    assert row0 % seq == 0
    qb0, kb0 = row0 // tq, row0 // seq
    in_specs = [pl.BlockSpec((tq, w), lambda b, h, i: (qb0 + b * nq + i, h)),
                pl.BlockSpec((seq, w), lambda b, h, i: (kb0 + b, hg + h)),
                pl.BlockSpec((seq, w), lambda b, h, i: (kb0 + b, 2 * hg + h))]
    args = [y, y, y]
    if has_cache:
        in_specs += [pl.BlockSpec((None, None, past, w), lambda b, h, i: (b, l, 0, h)),
                     pl.BlockSpec((None, None, past, w), lambda b, h, i: (b, l, 0, h))]
        args += [cache_k, cache_v]
    in_specs += [pl.BlockSpec((None, 4, HEAD_DIM), lambda b, h, i: (l, 0, 0)),
                 pl.BlockSpec((None, 1, V_DIM), lambda b, h, i: (l, 0, 0))]
    args += [lam_param, subln_w3]
    return pl.pallas_call(
        functools.partial(_attn_kernel, has_cache=has_cache, lam_init=lam_init, past=past, heads=heads),
        grid=(nseq, hg, nq),
        in_specs=in_specs,
        out_specs=pl.BlockSpec((tq, w), lambda b, h, i: (b * nq + i, h)),
        out_shape=jax.ShapeDtypeStruct((nseq * seq, N_HEADS * V_DIM), f32),
        scratch_shapes=[pltpu.VMEM((heads, V_DIM, lk), bf16), pltpu.VMEM((heads, lk, V_DIM), bf16),
                        pltpu.VMEM((2 * tq, lk), f32)],
        compiler_params=_cparams(3),
        name="diff_attn_lat" if has_cache else "diff_attn_ctx",
    )(*args)


def _lru_kernel(xr_ref, yr_ref, cw_ref, cb_ref, wg_ref, bg_ref, lam_ref, h0_ref,
                rec_ref, hfin_ref, xp_ref, a_ref, b_ref, *, chunk):
    seq, c = xr_ref.shape
    xp_ref[0:8, :] = jnp.zeros((8, c), f32)
    xp_ref[8:8 + seq, :] = xr_ref[...]
    xp_ref[8 + seq:16 + seq, :] = jnp.zeros((8, c), f32)
    lam = lam_ref[...]
    nlam = -lam
    softplus = jnp.maximum(nlam, 0.0) + jnp.log(1.0 + jnp.exp(-jnp.abs(nlam)))
    for ci in range(seq // chunk):
        t0 = ci * chunk
        xc = cb_ref[...] + sum(cw_ref[j:j + 1, :] * xp_ref[8 + t0 + j - 1:8 + t0 + j - 1 + chunk, :]
                               for j in range(CONV_WIDTH))
        xcb = xc.astype(bf16)
        for d in range(2):
            r = _sigmoid(jnp.dot(xcb, wg_ref[2 * d], preferred_element_type=f32) + bg_ref[2 * d:2 * d + 1, :])
            g = _sigmoid(jnp.dot(xcb, wg_ref[2 * d + 1], preferred_element_type=f32)
                         + bg_ref[2 * d + 1:2 * d + 2, :])
            log_a = -LRU_C * r * softplus[d:d + 1, :]
            a = jnp.exp(log_a)
            a_ref[d, t0:t0 + chunk, :] = a
            u = -jnp.tanh(log_a) * (a * a + 1.0)
            b_ref[d, t0:t0 + chunk, :] = jnp.where(u > 0.0, u * lax.rsqrt(u), 0.0) * (g * xc)

    rows = lax.broadcasted_iota(i32, (8, c), 0)
    nch = seq // 8

    def scan_body(i, carry):
        cf, cb = carry
        tf = pl.multiple_of(8 * i, 8)
        af = a_ref[0, pl.ds(tf, 8), :]
        bf = b_ref[0, pl.ds(tf, 8), :]
        for d in (1, 2, 4):
            keep = rows >= d
            bf = af * jnp.where(keep, pltpu.roll(bf, d, 0), 0.0) + bf
            af = af * jnp.where(keep, pltpu.roll(af, d, 0), 1.0)
        hf = af * cf + bf
        rec_ref[pl.ds(tf, 8), :] = hf
        tb = pl.multiple_of(8 * (nch - 1 - i), 8)
        ab = a_ref[1, pl.ds(tb, 8), :]
        bb = b_ref[1, pl.ds(tb, 8), :]
        for d in (1, 2, 4):
            keep = rows < 8 - d
            bb = ab * jnp.where(keep, pltpu.roll(bb, 8 - d, 0), 0.0) + bb
            ab = ab * jnp.where(keep, pltpu.roll(ab, 8 - d, 0), 1.0)
        hb = ab * cb + bb
        xp_ref[pl.ds(tb, 8), :] = hb
        return hf[7:8, :], hb[0:1, :]

    h0 = h0_ref[...]
    cf, cb = lax.fori_loop(0, nch, scan_body, (h0[0:1, :], h0[1:2, :]), unroll=SCAN_UNROLL)
    hfin_ref[0:1, :] = cf
    hfin_ref[1:2, :] = cb
    for ci in range(seq // chunk):
        t0 = ci * chunk
        rec_ref[t0:t0 + chunk, :] = ((rec_ref[t0:t0 + chunk, :] + xp_ref[t0:t0 + chunk, :])
                                     * jax.nn.gelu(yr_ref[t0:t0 + chunk, :]))


def _lru(y, conv_w, conv_b3, wg, bg, lru_lambda, h0, l, row0, nseq, seq, xr_col, yr_col):
    d_lru = conv_w.shape[-1]
    cb = 256
    ncb = d_lru // cb
    assert row0 % seq == 0
    rb0 = row0 // seq
    h0_batched = h0.shape[0] > 1
    in_specs = [pl.BlockSpec((seq, cb), lambda b, c: (rb0 + b, xr_col // cb + c)),
                pl.BlockSpec((seq, cb), lambda b, c: (rb0 + b, yr_col // cb + c)),
                pl.BlockSpec((None, CONV_WIDTH, cb), lambda b, c: (l, 0, c)),
                pl.BlockSpec((None, 1, cb), lambda b, c: (l, 0, c)),
                pl.BlockSpec((None, None, 4, cb, cb), lambda b, c: (l, c, 0, 0, 0)),
                pl.BlockSpec((None, 4, cb), lambda b, c: (l, 0, c)),
                pl.BlockSpec((None, 2, cb), lambda b, c: (l, 0, c)),
                pl.BlockSpec((None, 2, cb), (lambda b, c: (b, 0, c)) if h0_batched else (lambda b, c: (0, 0, c)))]
    return pl.pallas_call(
        functools.partial(_lru_kernel, chunk=_tile(seq, 512)),
        grid=(nseq, ncb),
        in_specs=in_specs,
        out_specs=[pl.BlockSpec((seq, cb), lambda b, c: (b, c)),
                   pl.BlockSpec((None, 2, cb), lambda b, c: (b, 0, c))],
        out_shape=[jax.ShapeDtypeStruct((nseq * seq, d_lru), f32),
                   jax.ShapeDtypeStruct((nseq, 2, d_lru), f32)],
        scratch_shapes=[pltpu.VMEM((seq + 16, cb), f32), pltpu.VMEM((2, seq, cb), f32),
                        pltpu.VMEM((2, seq, cb), f32)],
        compiler_params=_cparams(2),
        name="rglru",
    )(y, y, conv_w, conv_b3, wg, bg, lru_lambda, h0)


def _merge_kernel(atc_ref, atl_ref, rcc_ref, rcl_ref, ga_ref, gl_ref, x_ref, gt_ref, shf_ref, scf_ref,
                  g_ref, b_ref, wa_ref, wl_ref, wo_ref, x1_ref, h2_ref, ab_ref, rb_ref, *, alpha, n_ctx_tiles):
    @pl.when(pl.program_id(0) < n_ctx_tiles)
    def _():
        ab_ref[...] = atc_ref[...].astype(bf16)
        rb_ref[...] = rcc_ref[...].astype(bf16)

    @pl.when(pl.program_id(0) >= n_ctx_tiles)
    def _():
        ab_ref[...] = atl_ref[...].astype(bf16)
        rb_ref[...] = rcl_ref[...].astype(bf16)

    pa = jnp.dot(ab_ref[...], wa_ref[...], preferred_element_type=f32)
    pr = jnp.dot(rb_ref[...], wl_ref[...], preferred_element_type=f32)
    merged = jax.nn.sigmoid(ga_ref[...]) * pa + jax.nn.sigmoid(gl_ref[...]) * pr
    out = jnp.dot(merged.astype(bf16), wo_ref[...], preferred_element_type=f32)
    x1 = _layer_norm(alpha * x_ref[...] + gt_ref[0] * out, g_ref[0], b_ref[0])
    x1_ref[...] = x1
    h2_ref[...] = x1 * (1.0 + scf_ref[0]) + shf_ref[0]


def _merge(attn_c, attn_l, rec_c, rec_l, y, x, mod4, ln_g3, ln_b3, wa, wl, wo, l, n_ctx, dec_seq, ga_col, gl_col,
           alpha):
    n, d = x.shape
    tm = _tile(math.gcd(n_ctx, dec_seq), 256)
    n_ctx_tiles = n_ctx // tm
    tiles_per_seq = dec_seq // tm

    def seg(i):
        return jnp.where(i < n_ctx_tiles, 0, 1 + (i - n_ctx_tiles) // tiles_per_seq)

    base = l * MOD_ROWS * 6
    row = lambda i: (i, 0)
    ctx_row = lambda i: (jnp.minimum(i, n_ctx_tiles - 1), 0)
    lat_row = lambda i: (jnp.maximum(i - n_ctx_tiles, 0), 0)
    mod = lambda k: pl.BlockSpec((1, 1, d), lambda i: (base + seg(i) * 6 + k, 0, 0))
    wspec = pl.BlockSpec((None, d, d), lambda i: (l, 0, 0))
    return pl.pallas_call(
        functools.partial(_merge_kernel, alpha=alpha, n_ctx_tiles=n_ctx_tiles),
        grid=(n // tm,),
        in_specs=[pl.BlockSpec((tm, d), ctx_row), pl.BlockSpec((tm, d), lat_row),
                  pl.BlockSpec((tm, d), ctx_row), pl.BlockSpec((tm, d), lat_row),
                  pl.BlockSpec((tm, d), lambda i: (i, ga_col // d)),
                  pl.BlockSpec((tm, d), lambda i: (i, gl_col // d)),
                  pl.BlockSpec((tm, d), row), mod(2), mod(3), mod(4),
                  pl.BlockSpec((1, 1, d), lambda i: (2 * l, 0, 0)),
                  pl.BlockSpec((1, 1, d), lambda i: (2 * l, 0, 0)),
                  wspec, wspec, wspec],
        out_specs=[pl.BlockSpec((tm, d), row), pl.BlockSpec((tm, d), row)],
        out_shape=[jax.ShapeDtypeStruct((n, d), f32), jax.ShapeDtypeStruct((n, d), f32)],
        scratch_shapes=[pltpu.VMEM((tm, d), bf16), pltpu.VMEM((tm, d), bf16)],
        compiler_params=_cparams(1),
        name="merge_outproj_ln",
    )(attn_c, attn_l, rec_c, rec_l, y, y, x, mod4, mod4, mod4, ln_g3, ln_b3, wa, wl, wo)


def _route_kernel(h_ref, rw_ref, rb_ref, loc_ref, gate_ref, tab_ref, carry_ref):
    n_exp = rw_ref.shape[0]
    tm = h_ref.shape[0]
    gsz = n_exp // N_GROUPS

    @pl.when(pl.program_id(0) == 0)
    def _():
        carry_ref[...] = jnp.zeros_like(carry_ref)

    logits = lax.dot_general(rw_ref[...], h_ref[...], (((1,), (1,)), ((), ())),
                             precision=HIGHEST, preferred_element_type=f32)
    s = jax.nn.sigmoid(logits)
    sb = s + rb_ref[...]
    ninf = -jnp.inf
    gs_rows = []
    for g in range(N_GROUPS):
        blk = sb[g * gsz:(g + 1) * gsz]
        m1 = jnp.max(blk, axis=0, keepdims=True)
        cnt = jnp.sum((blk == m1).astype(f32), axis=0, keepdims=True)
        m2 = jnp.max(jnp.where(blk < m1, blk, ninf), axis=0, keepdims=True)
        gs_rows.append(m1 + jnp.where(cnt >= 2.0, m1, m2))
    gs = jnp.concatenate(gs_rows, axis=0)
    gi = lax.broadcasted_iota(i32, gs.shape, 0)
    grank = jnp.zeros(gs.shape, f32)
    for g2 in range(N_GROUPS):
        row = gs[g2:g2 + 1]
        grank += jnp.where((row > gs) | ((row == gs) & (g2 < gi)), 1.0, 0.0)
    gsel = grank < float(TOPK_GROUPS)
    sbm = jnp.concatenate([jnp.where(gsel[g:g + 1], sb[g * gsz:(g + 1) * gsz], ninf)
                           for g in range(N_GROUPS)], axis=0)
    ei = lax.broadcasted_iota(i32, sbm.shape, 0)
    erank = jnp.zeros(sbm.shape, f32)
    for e2 in range(n_exp):
        row = sbm[e2:e2 + 1]
        erank += jnp.where((row > sbm) | ((row == sbm) & (e2 < ei)), 1.0, 0.0)
    sel = erank < float(TOP_K)
    ssel = jnp.where(sel, s, 0.0)
    gate = ssel / jnp.sum(ssel, axis=0, keepdims=True) * ROUTE_SCALE
    selb = sel.astype(bf16)
    ti = lax.broadcasted_iota(i32, (tm, tm), 0)
    tj = lax.broadcasted_iota(i32, (tm, tm), 1)
    upper = (ti < tj).astype(bf16)
    pos_local = jnp.dot(selb, upper, preferred_element_type=f32)
    cnt = jnp.sum(sel.astype(f32), axis=1, keepdims=True)
    cpad = jnp.floor((cnt + (SEG_ALIGN - 1)) * (1.0 / SEG_ALIGN)) * SEG_ALIGN
    cpad_l = jnp.broadcast_to(cpad, (n_exp, 128))
    li = lax.broadcasted_iota(i32, (n_exp, n_exp), 0)
    lj = lax.broadcasted_iota(i32, (n_exp, n_exp), 1)
    lower = (lj < li).astype(bf16)
    run_start = jnp.dot(lower, cpad_l.astype(bf16), preferred_element_type=f32)
    loc = run_start[:, 0:1] + pos_local
    slot = jnp.dot(lower, selb, preferred_element_type=f32)
    l_rows, g_rows = [], []
    for k in range(TOP_K):
        mk = sel & (slot == float(k))
        l_rows.append(jnp.sum(jnp.where(mk, loc, 0.0), axis=0, keepdims=True))
        g_rows.append(jnp.sum(jnp.where(mk, gate, 0.0), axis=0, keepdims=True))
    loc_ref[...] = jnp.concatenate(l_rows, axis=0).astype(i32)
    gate_ref[...] = jnp.concatenate(g_rows, axis=0)
    lane = lax.broadcasted_iota(i32, (n_exp, 128), 1)
    tab_ref[...] = jnp.where(lane == 0, cpad_l, carry_ref[...]).astype(i32)
    carry_ref[...] = carry_ref[...] + cpad_l


def _route(h2, rwt, rb3, l):
    n, d = h2.shape
    n_exp = rwt.shape[1]
    tm = ROUTE_TILE
    assert n % tm == 0
    lst = pl.BlockSpec((TOP_K, tm), lambda i: (0, i))
    return pl.pallas_call(
        _route_kernel,
        grid=(n // tm,),
        in_specs=[pl.BlockSpec((tm, d), lambda i: (i, 0)),
                  pl.BlockSpec((None, n_exp, d), lambda i: (l, 0, 0)),
                  pl.BlockSpec((None, n_exp, 1), lambda i: (l, 0, 0))],
        out_specs=[lst, lst, pl.BlockSpec((None, n_exp, 128), lambda i: (i, 0, 0))],
        out_shape=[jax.ShapeDtypeStruct((TOP_K, n), i32), jax.ShapeDtypeStruct((TOP_K, n), f32),
                   jax.ShapeDtypeStruct((n // tm, n_exp, 128), i32)],
        scratch_shapes=[pltpu.VMEM((n_exp, 128), f32)],
        compiler_params=_cparams(1),
        name="moe_route",
    )(h2, rwt, rb3)


def _pow2_pieces(count, max_rows, fn):
    size = max_rows
    while size >= SEG_ALIGN:
        off = count & (-2 * size)
        pl.when((count & size) != 0)(functools.partial(fn, off, size))
        size //= 2


def _wait_rows(count, max_rows, src_ref, dst_ref, sem):
    def wait(_, size):
        pltpu.make_async_copy(src_ref.at[pl.ds(0, size)], dst_ref.at[pl.ds(0, size)], sem).wait()

    _pow2_pieces(count, max_rows, wait)


def _dispatch_kernel(cp_ref, ga_ref, tot_ref, ts_ref, tl_ref, loc_ref, h_ref, xs_ref, buf_ref, z_ref, sem, *,
                     n_exp, n_tiles, tile_rows, wait_max):
    t = pl.program_id(0)
    slot = t % 2
    tm = h_ref.shape[0]

    @pl.when(t >= 2)
    def _():
        _wait_rows(tot_ref[t - 2], wait_max, buf_ref.at[slot], xs_ref, sem.at[slot])

    loc = loc_ref[...]
    rid = lax.broadcasted_iota(i32, (tile_rows, tm), 0)
    perm = jnp.zeros((tile_rows, tm), f32)
    for k in range(TOP_K):
        perm = jnp.where(loc[k:k + 1, :] == rid, 1.0, perm)
    buf_ref[slot] = jnp.dot(perm.astype(bf16), h_ref[...].astype(bf16),
                            preferred_element_type=f32).astype(bf16)

    def issue(e, lo):
        c = cp_ref[t * n_exp + e]
        g = ga_ref[t * n_exp + e]

        def copy(off, size):
            src = pl.multiple_of(lo + off, SEG_ALIGN)
            dst = pl.multiple_of(g + off, SEG_ALIGN)
            pltpu.make_async_copy(buf_ref.at[slot, pl.ds(src, size)], xs_ref.at[pl.ds(dst, size)],
                                  sem.at[slot]).start()

        _pow2_pieces(c, tm, copy)
        return lo + c

    lax.fori_loop(0, n_exp, issue, 0)

    @pl.when(t == n_tiles - 1)
    def _():
        z_ref[...] = jnp.zeros_like(z_ref)

        def fill(e, _):
            def copy(off, size):
                dst = pl.multiple_of(ts_ref[e] + off, SEG_ALIGN)
                pltpu.make_async_copy(z_ref.at[pl.ds(0, size)], xs_ref.at[pl.ds(dst, size)], sem.at[2]).start()

            _pow2_pieces(tl_ref[e], MOE_BLOCK // 2, copy)
            return 0

        lax.fori_loop(0, n_exp, fill, 0)

        def drain(e, _):
            _wait_rows(tl_ref[e], MOE_BLOCK // 2, z_ref, xs_ref, sem.at[2])
            return 0

        lax.fori_loop(0, n_exp, drain, 0)
        if n_tiles > 1:
            _wait_rows(tot_ref[t - 1], wait_max, buf_ref.at[1 - slot], xs_ref, sem.at[1 - slot])
        _wait_rows(tot_ref[t], wait_max, buf_ref.at[slot], xs_ref, sem.at[slot])


def _tile_rows(tm, n_exp):
    return -(-(tm * TOP_K + n_exp * (SEG_ALIGN - 1)) // 128) * 128


def _wait_max(tile_rows):
    return 1 << (tile_rows.bit_length() - 1)


def _dispatch(h2, loc, cpad, gabs, tot, tail_start, tail_len, n_rows, n_exp):
    n, d = h2.shape
    tm = ROUTE_TILE
    n_tiles = n // tm
    tile_rows = _tile_rows(tm, n_exp)
    return pl.pallas_call(
        functools.partial(_dispatch_kernel, n_exp=n_exp, n_tiles=n_tiles, tile_rows=tile_rows,
                          wait_max=_wait_max(tile_rows)),
        grid_spec=pltpu.PrefetchScalarGridSpec(
            num_scalar_prefetch=5,
            grid=(n_tiles,),
            in_specs=[pl.BlockSpec((TOP_K, tm), lambda i, *_: (0, i)),
                      pl.BlockSpec((tm, d), lambda i, *_: (i, 0))],
            out_specs=pl.BlockSpec(memory_space=pl.ANY),
            scratch_shapes=[pltpu.VMEM((2, tile_rows, d), bf16), pltpu.VMEM((MOE_BLOCK // 2, d), bf16),
                            pltpu.SemaphoreType.DMA((3,))]),
        out_shape=jax.ShapeDtypeStruct((n_rows, d), bf16),
        compiler_params=_cparams(1),
        name="moe_dispatch",
    )(cpad, gabs, tot, tail_start, tail_len, loc, h2)


def _expert_kernel(be_ref, nu_ref, x_ref, wg_ref, wu_ref, wd_ref, o_ref):
    @pl.when(pl.program_id(0) < nu_ref[0])
    def _():
        x = x_ref[...]
        g = jnp.dot(x, wg_ref[...].astype(bf16), preferred_element_type=f32)
        u = jnp.dot(x, wu_ref[...].astype(bf16), preferred_element_type=f32)
        h = (g * jax.nn.sigmoid(g)) * u
        o_ref[...] = jnp.dot(h.astype(bf16), wd_ref[...].astype(bf16),
                             preferred_element_type=f32).astype(bf16)


def _experts(xs, blk_e, n_used, we_g, we_u, we_d, l):
    n_rows, d = xs.shape
    f = we_g.shape[-1]
    nblk = n_rows // MOE_BLOCK
    blk = lambda b, be, nu: (jnp.minimum(b, nu[0] - 1), 0)
    wsel = lambda b, be, nu: (l, be[jnp.minimum(b, nu[0] - 1)], 0, 0)
    return pl.pallas_call(
        _expert_kernel,
        grid_spec=pltpu.PrefetchScalarGridSpec(
            num_scalar_prefetch=2,
            grid=(nblk,),
            in_specs=[pl.BlockSpec((MOE_BLOCK, d), blk),
                      pl.BlockSpec((None, None, d, f), wsel),
                      pl.BlockSpec((None, None, d, f), wsel),
                      pl.BlockSpec((None, None, f, d), wsel)],
            out_specs=pl.BlockSpec((MOE_BLOCK, d), blk)),
        out_shape=jax.ShapeDtypeStruct((n_rows, d), bf16),
        compiler_params=_cparams(1),
        name="moe_experts",
    )(blk_e, n_used, xs, we_g, we_u, we_d)


def _combine_kernel(cp_ref, ga_ref, tot_ref, loc_ref, gate_ref, ys_ref, x1_ref, h2_ref, gt_ref, g_ref, b_ref,
                    sg_ref, su_ref, sd_ref, o_ref, buf_ref, sem, *, alpha, n_exp, n_tiles, tile_rows, wait_max):
    t = pl.program_id(0)
    slot = t % 2
    tm = x1_ref.shape[0]

    def fetch(tile, dst_slot):
        def issue(e, lo):
            c = cp_ref[tile * n_exp + e]
            g = ga_ref[tile * n_exp + e]

            def copy(off, size):
                src = pl.multiple_of(g + off, SEG_ALIGN)
                dst = pl.multiple_of(lo + off, SEG_ALIGN)
                pltpu.make_async_copy(ys_ref.at[pl.ds(src, size)], buf_ref.at[dst_slot, pl.ds(dst, size)],
                                      sem.at[dst_slot]).start()

            _pow2_pieces(c, tm, copy)
            return lo + c

        lax.fori_loop(0, n_exp, issue, 0)

    @pl.when(t == 0)
    def _():
        buf_ref[...] = jnp.zeros_like(buf_ref)
        fetch(0, 0)

    @pl.when(t + 1 < n_tiles)
    def _():
        fetch(t + 1, 1 - slot)

    hb = h2_ref[...].astype(bf16)
    g = jnp.dot(hb, sg_ref[...], preferred_element_type=f32)
    u = jnp.dot(hb, su_ref[...], preferred_element_type=f32)
    shared = jnp.dot(((g * jax.nn.sigmoid(g)) * u).astype(bf16), sd_ref[...], preferred_element_type=f32)
    fill = jnp.zeros((128 - TOP_K, tm), f32)
    gate_t = jnp.concatenate([gate_ref[...], fill], axis=0).T
    loc_t = jnp.concatenate([loc_ref[...].astype(f32), fill], axis=0).T
    rid = lax.broadcasted_iota(i32, (tm, tile_rows), 1).astype(f32)
    wgt = jnp.zeros((tm, tile_rows), f32)
    for k in range(TOP_K):
        wgt = jnp.where(loc_t[:, k:k + 1] == rid, gate_t[:, k:k + 1], wgt)
    _wait_rows(tot_ref[t], wait_max, ys_ref, buf_ref.at[slot], sem.at[slot])
    routed = jnp.dot(wgt.astype(bf16), buf_ref[slot], preferred_element_type=f32)
    o_ref[...] = _layer_norm(alpha * x1_ref[...] + gt_ref[0] * (shared + routed), g_ref[0], b_ref[0])


def _combine(ys, loc, gate, cpad, gabs, tot, x1, h2, mod4, ln_g3, ln_b3, sg, su, sd, l, n_ctx, dec_seq, alpha,
             n_exp):
    n, d = x1.shape
    f = sg.shape[-1]
    tm = ROUTE_TILE
    assert n_ctx % tm == 0 and dec_seq % tm == 0
    n_tiles = n // tm
    tile_rows = _tile_rows(tm, n_exp)
    n_ctx_tiles = n_ctx // tm
    tiles_per_seq = dec_seq // tm

    def seg(i):
        return jnp.where(i < n_ctx_tiles, 0, 1 + (i - n_ctx_tiles) // tiles_per_seq)

    base = l * MOD_ROWS * 6
    row = lambda i, *_: (i, 0)
    lst = lambda i, *_: (0, i)
    return pl.pallas_call(
        functools.partial(_combine_kernel, alpha=alpha, n_exp=n_exp, n_tiles=n_tiles, tile_rows=tile_rows,
                          wait_max=_wait_max(tile_rows)),
        grid_spec=pltpu.PrefetchScalarGridSpec(
            num_scalar_prefetch=3,
            grid=(n_tiles,),
            in_specs=[pl.BlockSpec((TOP_K, tm), lst), pl.BlockSpec((TOP_K, tm), lst),
                      pl.BlockSpec(memory_space=pl.ANY),
                      pl.BlockSpec((tm, d), row), pl.BlockSpec((tm, d), row),
                      pl.BlockSpec((1, 1, d), lambda i, *_: (base + seg(i) * 6 + 5, 0, 0)),
                      pl.BlockSpec((1, 1, d), lambda i, *_: (2 * l + 1, 0, 0)),
                      pl.BlockSpec((1, 1, d), lambda i, *_: (2 * l + 1, 0, 0)),
                      pl.BlockSpec((None, d, f), lambda i, *_: (l, 0, 0)),
                      pl.BlockSpec((None, d, f), lambda i, *_: (l, 0, 0)),
                      pl.BlockSpec((None, f, d), lambda i, *_: (l, 0, 0))],
            out_specs=pl.BlockSpec((tm, d), row),
            scratch_shapes=[pltpu.VMEM((2, tile_rows, d), bf16), pltpu.SemaphoreType.DMA((2,))]),
        out_shape=jax.ShapeDtypeStruct((n, d), f32),
        compiler_params=_cparams(1),
        name="moe_combine_ln",
    )(cpad, gabs, tot, loc, gate, ys, x1, h2, mod4, ln_g3, ln_b3, sg, su, sd)


def _rope_tables(seq, width):
    rows = seq // GRID_W
    row = jnp.repeat(jnp.arange(rows, dtype=f32), GRID_W)
    col = jnp.tile(jnp.arange(GRID_W, dtype=f32), rows)
    half = HEAD_DIM // 2
    inv = ROPE_BASE ** (-jnp.arange(0, half, 2, dtype=f32) / half)
    lane = jnp.arange(width)
    d = lane % HEAD_DIM
    pos = jnp.where((d // half)[None, :] == 0, row[:, None], col[:, None])
    ang = pos * inv[d % (HEAD_DIM // 4)][None, :]
    sign = jnp.where((d % half) < HEAD_DIM // 4, -1.0, 1.0)[None, :]
    return jnp.cos(ang), jnp.sin(ang) * sign


def _block_diag_gates(gate_w_r, gate_w_i, cb):
    depth, _, nb, bw, _ = gate_w_r.shape
    per = cb // bw
    w = jnp.stack([gate_w_r[:, 0], gate_w_i[:, 0], gate_w_r[:, 1], gate_w_i[:, 1]], axis=1)
    w = w.reshape(depth, 4, nb // per, per, bw, bw)
    eye = jnp.eye(per, dtype=w.dtype)
    bd = jnp.einsum("lgcaio,ab->lcgaibo", w, eye)
    return bd.reshape(depth, nb // per, 4, cb, cb).astype(bf16)


def kernel(x_prompt, x_sample, c, cache_k, cache_v, state_lru, c_ctx, w_mod, b_mod, w_in, lam_param, subln_w,
           conv_w, conv_b, gate_w_r, gate_b_r, gate_w_i, gate_b_i, lru_lambda, w_br_attn, w_br_lru, w_out,
           ln_g, ln_b, router_w, router_b, exp_w_gate, exp_w_up, exp_w_down, sh_w_gate, sh_w_up, sh_w_down):
    batch, seq, d = x_prompt.shape
    dec_batch, dec_seq, _ = x_sample.shape
    depth = w_mod.shape[0]
    past = cache_k.shape[2]
    d_lru = conv_w.shape[-1]
    n_exp = router_w.shape[-1]
    qk = N_HEADS * 2 * HEAD_DIM
    attn_w = N_HEADS * V_DIM
    n_ctx, n_lat = batch * seq, dec_batch * dec_seq
    n = n_ctx + n_lat
    alpha = (2 * depth) ** 0.25
    assert 1 + dec_batch <= MOD_ROWS and n_ctx % dec_seq == 0

    cond = jnp.concatenate([c_ctx[None], c, jnp.zeros((MOD_ROWS - 1 - dec_batch, d), f32)], 0)
    w_in_b = w_in.astype(bf16)
    wa, wl, wo = w_br_attn.astype(bf16), w_br_lru.astype(bf16), w_out.astype(bf16)
    sg, su, sd = sh_w_gate.astype(bf16), sh_w_up.astype(bf16), sh_w_down.astype(bf16)
    rwt = jnp.swapaxes(router_w, 1, 2)
    rb3 = router_b.reshape(depth, n_exp, 1)
    wg = _block_diag_gates(gate_w_r, gate_w_i, 256)
    bg = jnp.stack([gate_b_r[:, 0], gate_b_i[:, 0], gate_b_r[:, 1], gate_b_i[:, 1]], axis=1)
    conv_b3 = conv_b.reshape(depth, 1, d_lru)
    subln_w3 = subln_w.reshape(depth, 1, V_DIM)
    ln_g3 = ln_g.reshape(depth * 2, 1, d)
    ln_b3 = ln_b.reshape(depth * 2, 1, d)
    ck = cache_k.reshape(dec_batch, depth, past, N_HEADS * 2 * HEAD_DIM)
    cv = cache_v.reshape(dec_batch, depth, past, attn_w)
    cos_t, sin_t = _rope_tables(dec_seq, 512)
    h0_ctx = jnp.zeros((1, 2, d_lru), f32)

    mod4 = _mod_table(cond, w_mod, b_mod).reshape(depth * MOD_ROWS * 6, 1, d)
    x = jnp.concatenate([x_prompt.reshape(n_ctx, d), x_sample.reshape(n_lat, d)], 0)
    n_rows = n * TOP_K + (n // ROUTE_TILE) * n_exp * (SEG_ALIGN - 1) + n_exp * (MOE_BLOCK - 1)
    n_rows = -(-n_rows // MOE_BLOCK) * MOE_BLOCK
    xr_col, yr_col = 2 * qk + attn_w, 2 * qk + attn_w + d_lru
    ga_col, gl_col = yr_col + d_lru, yr_col + d_lru + d

    new_k, new_v, new_s = [], [], []
    for l in range(depth):
        lam_init = 0.8 - 0.6 * math.exp(-0.3 * l)
        y = _inproj(x, mod4, w_in_b, cos_t, sin_t, l, n_ctx, dec_seq, 2 * qk)
        new_k.append(y[:n_ctx, qk:2 * qk].reshape(batch, seq, N_HEADS, 2 * HEAD_DIM))
        new_v.append(y[:n_ctx, 2 * qk:2 * qk + attn_w].reshape(batch, seq, N_HEADS, V_DIM))
        attn_c = _attention(y, lam_param, subln_w3, l, lam_init, 0, batch, seq)
        attn_l = _attention(y, lam_param, subln_w3, l, lam_init, n_ctx, dec_batch, dec_seq, ck, cv)
        rec_c, hfin = _lru(y, conv_w, conv_b3, wg, bg, lru_lambda, h0_ctx, l, 0, batch, seq, xr_col, yr_col)
        rec_l, _ = _lru(y, conv_w, conv_b3, wg, bg, lru_lambda, state_lru[:, l], l, n_ctx, dec_batch, dec_seq,
                        xr_col, yr_col)
        new_s.append(hfin)
        x1, h2 = _merge(attn_c, attn_l, rec_c, rec_l, y, x, mod4, ln_g3, ln_b3, wa, wl, wo, l, n_ctx, dec_seq, ga_col, gl_col, alpha)
        loc, gate, tab = _route(h2, rwt, rb3, l)
        cpad, grel = tab[:, :, 0], tab[:, :, 1]
        used = grel[-1] + cpad[-1]
        padded = (used + MOE_BLOCK - 1) // MOE_BLOCK * MOE_BLOCK
        pad_end = jnp.cumsum(padded)
        pad_start = pad_end - padded
        gabs = (pad_start[None, :] + grel).reshape(-1).astype(i32)
        tot = jnp.sum(cpad, axis=1).astype(i32)
        blk_row0 = jnp.arange(n_rows // MOE_BLOCK, dtype=i32) * MOE_BLOCK
        blk_e = jnp.minimum(jnp.sum(pad_end[None, :] <= blk_row0[:, None], axis=1), n_exp - 1).astype(i32)
        n_used = (pad_end[-1:] // MOE_BLOCK).astype(i32)
        cpad = cpad.reshape(-1)
        xs = _dispatch(h2, loc, cpad, gabs, tot, (pad_start + used).astype(i32), (padded - used).astype(i32),
                       n_rows, n_exp)
        ys = _experts(xs, blk_e, n_used, exp_w_gate, exp_w_up, exp_w_down, l)
        x = _combine(ys, loc, gate, cpad, gabs, tot, x1, h2, mod4, ln_g3, ln_b3, sg, su, sd, l, n_ctx, dec_seq,
                     alpha, n_exp)
    return (x[:n_ctx].reshape(batch, seq, d), x[n_ctx:].reshape(dec_batch, dec_seq, d),
            jnp.stack(new_k, 1), jnp.stack(new_v, 1), jnp.stack(new_s, 1))
```

```python
import functools
import math

import jax
import jax.numpy as jnp
from jax import lax
from jax.experimental import pallas as pl
from jax.experimental.pallas import tpu as pltpu

f32, bf16, i32 = jnp.float32, jnp.bfloat16, jnp.int32

GRID_W = 64
N_HEADS = 8
HEAD_DIM = 64
V_DIM = 2 * HEAD_DIM
ROPE_BASE = 10000.0
LRU_BLOCKS = 16
CONV_WIDTH = 4
LRU_C = 8.0
TOP_K = 8
N_GROUPS = 8
TOPK_GROUPS = 4
ROUTE_SCALE = 2.5
LN_EPS = 1e-5
MOD_ROWS = 8
MOE_BLOCK = 1024
ROUTE_TILE = 256
SEG_ALIGN = 16
TAB_LEN, TAB_BASE, TAB_START = 0, 1, 2
SCAN_UNROLL = 4
VMEM_LIMIT = 56 * 1024 * 1024
HIGHEST = lax.Precision.HIGHEST


def _cparams(n_axes):
    return pltpu.CompilerParams(dimension_semantics=("arbitrary",) * n_axes,
                                vmem_limit_bytes=VMEM_LIMIT)


def _tile(n, pref):
    t = min(n, pref)
    while n % t:
        t -= 8
    assert t > 0 and n % t == 0
    return t


def _sigmoid(x):
    return 0.5 * jnp.tanh(0.5 * x) + 0.5


def _layer_norm(z, g, b):
    mu = jnp.mean(z, -1, keepdims=True)
    zc = z - mu
    var = jnp.mean(zc * zc, -1, keepdims=True)
    return zc * lax.rsqrt(var + LN_EPS) * g + b


def _mod_kernel(c_ref, w_ref, b_ref, o_ref):
    c = c_ref[...]
    a = c * jax.nn.sigmoid(c)
    o_ref[...] = jnp.dot(a, w_ref[...], precision=HIGHEST, preferred_element_type=f32) + b_ref[...]


def _mod_table(cond, w_mod, b_mod):
    depth, d, d6 = w_mod.shape
    tn = _tile(d6, 1024)
    return pl.pallas_call(
        _mod_kernel,
        grid=(depth, d6 // tn),
        in_specs=[pl.BlockSpec((MOD_ROWS, d), lambda l, j: (0, 0)),
                  pl.BlockSpec((None, d, tn), lambda l, j: (l, 0, j)),
                  pl.BlockSpec((None, 1, tn), lambda l, j: (l, 0, j))],
        out_specs=pl.BlockSpec((None, MOD_ROWS, tn), lambda l, j: (l, 0, j)),
        out_shape=jax.ShapeDtypeStruct((depth, MOD_ROWS, d6), f32),
        compiler_params=_cparams(2),
        name="mod_table",
    )(cond, w_mod, b_mod.reshape(depth, 1, d6))


def _inproj_kernel(x_ref, sh_ref, sc_ref, w_ref, cos_ref, sin_ref, o_ref, hb_ref, *,
                   n_ctx_tiles, n_rope_cols, tn):
    i = pl.program_id(0)
    j = pl.program_id(1)

    @pl.when(j == 0)
    def _():
        hb_ref[...] = (x_ref[...] * (1.0 + sc_ref[0]) + sh_ref[0]).astype(bf16)

    acc = jnp.dot(hb_ref[...], w_ref[...], preferred_element_type=f32)
    rope = jnp.logical_and(i >= n_ctx_tiles, j < n_rope_cols)

    @pl.when(rope)
    def _():
        cos, sin = cos_ref[...], sin_ref[...]
        lane = lax.broadcasted_iota(i32, cos.shape, 1)
        first = (lane % (HEAD_DIM // 2)) < (HEAD_DIM // 4)
        for c in range(tn // V_DIM):
            a = acc[:, c * V_DIM:(c + 1) * V_DIM]
            partner = jnp.where(first, pltpu.roll(a, V_DIM - HEAD_DIM // 4, 1), pltpu.roll(a, HEAD_DIM // 4, 1))
            o_ref[:, c * V_DIM:(c + 1) * V_DIM] = a * cos + partner * sin

    @pl.when(jnp.logical_not(rope))
    def _():
        o_ref[...] = acc


def _inproj(x, mod4, w_in_b, cos_t, sin_t, l, n_ctx, dec_seq, qk_cols):
    n, d = x.shape
    d_in = w_in_b.shape[-1]
    tm = _tile(math.gcd(n_ctx, dec_seq), 1024)
    tn = _tile(math.gcd(d_in, qk_cols), 1024)
    assert tn % V_DIM == 0
    n_ctx_tiles = n_ctx // tm
    tiles_per_seq = dec_seq // tm

    def seg(i):
        return jnp.where(i < n_ctx_tiles, 0, 1 + (i - n_ctx_tiles) // tiles_per_seq)

    def pos(i):
        return jnp.where(i < n_ctx_tiles, 0, (i - n_ctx_tiles) % tiles_per_seq)

    base = l * MOD_ROWS * 6
    return pl.pallas_call(
        functools.partial(_inproj_kernel, n_ctx_tiles=n_ctx_tiles, n_rope_cols=qk_cols // tn, tn=tn),
        grid=(n // tm, d_in // tn),
        in_specs=[pl.BlockSpec((tm, d), lambda i, j: (i, 0)),
                  pl.BlockSpec((1, 1, d), lambda i, j: (base + seg(i) * 6 + 0, 0, 0)),
                  pl.BlockSpec((1, 1, d), lambda i, j: (base + seg(i) * 6 + 1, 0, 0)),
                  pl.BlockSpec((None, d, tn), lambda i, j: (l, 0, j)),
                  pl.BlockSpec((tm, V_DIM), lambda i, j: (pos(i), 0)),
                  pl.BlockSpec((tm, V_DIM), lambda i, j: (pos(i), 0))],
        out_specs=pl.BlockSpec((tm, tn), lambda i, j: (i, j)),
        out_shape=jax.ShapeDtypeStruct((n, d_in), f32),
        scratch_shapes=[pltpu.VMEM((tm, d), bf16)],
        compiler_params=_cparams(2),
        name="in_proj",
    )(x, mod4, mod4, w_in_b, cos_t, sin_t)


def _key_chunk(lk):
    for ck in (512, 384, 256, 128):
        if lk % ck == 0:
            return ck
    return lk


def _attn_kernel(*refs, has_cache, lam_init, past, heads):
    if has_cache:
        q_ref, k_ref, v_ref, ck_ref, cv_ref, lp_ref, sw_ref, o_ref, kt_ref, vb_ref, s_ref = refs
    else:
        q_ref, k_ref, v_ref, lp_ref, sw_ref, o_ref, kt_ref, vb_ref, s_ref = refs

    @pl.when(pl.program_id(2) == 0)
    def _():
        for h in range(heads):
            cols = slice(h * V_DIM, (h + 1) * V_DIM)
            if has_cache:
                kt_ref[h, :, :past] = ck_ref[:, cols].T.astype(bf16)
                vb_ref[h, :past, :] = cv_ref[:, cols].astype(bf16)
            kt_ref[h, :, past:] = k_ref[:, cols].T.astype(bf16)
            vb_ref[h, past:, :] = v_ref[:, cols].astype(bf16)

    lp = lp_ref[...]
    lam = (jnp.exp(jnp.sum(lp[0:1] * lp[1:2], keepdims=True))
           - jnp.exp(jnp.sum(lp[2:3] * lp[3:4], keepdims=True)) + lam_init)
    for h in range(heads):
        cols = slice(h * V_DIM, (h + 1) * V_DIM)
        o = _attn_head(q_ref[:, cols], kt_ref.at[h], vb_ref.at[h], s_ref, lam)
        o = o * lax.rsqrt(jnp.mean(o * o, -1, keepdims=True) + LN_EPS) * sw_ref[...]
        o_ref[:, cols] = o * (1.0 - lam_init)


def _attn_head(q, kt_ref, vb_ref, s_ref, lam):
    tq = q.shape[0]
    q = q * (HEAD_DIM ** -0.5 * math.log2(math.e))
    lane = lax.broadcasted_iota(i32, q.shape, 1)
    q2 = jnp.concatenate([jnp.where(lane < HEAD_DIM, q, 0.0).astype(bf16),
                          jnp.where(lane >= HEAD_DIM, q, 0.0).astype(bf16)], axis=0)
    lk = kt_ref.shape[1]
    ck = _key_chunk(lk)
    lanes = 128
    mx = None
    for c in range(lk // ck):
        s_c = jnp.dot(q2, kt_ref[:, c * ck:(c + 1) * ck], preferred_element_type=f32)
        s_ref[:, c * ck:(c + 1) * ck] = s_c
        for j in range(ck // lanes):
            t = s_c[:, j * lanes:(j + 1) * lanes]
            mx = t if mx is None else jnp.maximum(mx, t)
    mb = jnp.broadcast_to(jnp.max(mx, axis=-1, keepdims=True), (2 * tq, lanes))
    ls = jnp.zeros((2 * tq, lanes), f32)
    for j in range(lk // lanes):
        e_t = jnp.exp2(s_ref[:, j * lanes:(j + 1) * lanes] - mb)
        ls = ls + e_t
        s_ref[:, j * lanes:(j + 1) * lanes] = e_t
    inv = 1.0 / jnp.sum(ls, axis=-1, keepdims=True)
    c0 = jnp.broadcast_to(inv[:tq], (tq, lanes))
    c1 = jnp.broadcast_to(lam * inv[tq:], (tq, lanes))
    o = jnp.zeros((tq, V_DIM), f32)
    for c in range(lk // ck):
        a_c = jnp.concatenate(
            [(s_ref[:tq, j * lanes:(j + 1) * lanes] * c0 - s_ref[tq:, j * lanes:(j + 1) * lanes] * c1).astype(bf16)
             for j in range(c * ck // lanes, (c + 1) * ck // lanes)], axis=1)
        o = o + jnp.dot(a_c, vb_ref[c * ck:(c + 1) * ck, :], preferred_element_type=f32)
    return o


def _attention(y, lam_param, subln_w3, l, lam_init, row0, nseq, seq, cache_k=None, cache_v=None):
    has_cache = cache_k is not None
    past = cache_k.shape[2] if has_cache else 0
    lk = past + seq
    tq = _tile(seq, 256)
    nq = seq // tq
    heads = N_HEADS if lk <= 512 else 1
    hg = N_HEADS // heads
    w = heads * V_DIM
    assert row0 % seq == 0
    qb0, kb0 = row0 // tq, row0 // seq
    in_specs = [pl.BlockSpec((tq, w), lambda b, h, i: (qb0 + b * nq + i, h)),
                pl.BlockSpec((seq, w), lambda b, h, i: (kb0 + b, hg + h)),
                pl.BlockSpec((seq, w), lambda b, h, i: (kb0 + b, 2 * hg + h))]
    args = [y, y, y]
    if has_cache:
        in_specs += [pl.BlockSpec((None, None, past, w), lambda b, h, i: (b, l, 0, h)),
                     pl.BlockSpec((None, None, past, w), lambda b, h, i: (b, l, 0, h))]
        args += [cache_k, cache_v]
    in_specs += [pl.BlockSpec((None, 4, HEAD_DIM), lambda b, h, i: (l, 0, 0)),
                 pl.BlockSpec((None, 1, V_DIM), lambda b, h, i: (l, 0, 0))]
    args += [lam_param, subln_w3]
    return pl.pallas_call(
        functools.partial(_attn_kernel, has_cache=has_cache, lam_init=lam_init, past=past, heads=heads),
        grid=(nseq, hg, nq),
        in_specs=in_specs,
        out_specs=pl.BlockSpec((tq, w), lambda b, h, i: (b * nq + i, h)),
        out_shape=jax.ShapeDtypeStruct((nseq * seq, N_HEADS * V_DIM), f32),
        scratch_shapes=[pltpu.VMEM((heads, V_DIM, lk), bf16), pltpu.VMEM((heads, lk, V_DIM), bf16),
                        pltpu.VMEM((2 * tq, lk), f32)],
        compiler_params=_cparams(3),
        name="diff_attn_lat" if has_cache else "diff_attn_ctx",
    )(*args)


def _lru_kernel(xr_ref, yr_ref, cw_ref, cb_ref, wg_ref, bg_ref, lam_ref, h0_ref,
                rec_ref, hfin_ref, xp_ref, a_ref, b_ref, *, chunk):
    seq, c = xr_ref.shape
    xp_ref[0:8, :] = jnp.zeros((8, c), f32)
    xp_ref[8:8 + seq, :] = xr_ref[...]
    xp_ref[8 + seq:16 + seq, :] = jnp.zeros((8, c), f32)
    lam = lam_ref[...]
    nlam = -lam
    softplus = jnp.maximum(nlam, 0.0) + jnp.log(1.0 + jnp.exp(-jnp.abs(nlam)))
    for ci in range(seq // chunk):
        t0 = ci * chunk
        xc = cb_ref[...] + sum(cw_ref[j:j + 1, :] * xp_ref[8 + t0 + j - 1:8 + t0 + j - 1 + chunk, :]
                               for j in range(CONV_WIDTH))
        xcb = xc.astype(bf16)
        for d in range(2):
            r = _sigmoid(jnp.dot(xcb, wg_ref[2 * d], preferred_element_type=f32) + bg_ref[2 * d:2 * d + 1, :])
            g = _sigmoid(jnp.dot(xcb, wg_ref[2 * d + 1], preferred_element_type=f32)
                         + bg_ref[2 * d + 1:2 * d + 2, :])
            log_a = -LRU_C * r * softplus[d:d + 1, :]
            a = jnp.exp(log_a)
            a_ref[d, t0:t0 + chunk, :] = a
            u = -jnp.tanh(log_a) * (a * a + 1.0)
            b_ref[d, t0:t0 + chunk, :] = jnp.where(u > 0.0, u * lax.rsqrt(u), 0.0) * (g * xc)

    rows = lax.broadcasted_iota(i32, (8, c), 0)
    nch = seq // 8

    def scan_body(i, carry):
        cf, cb = carry
        tf = pl.multiple_of(8 * i, 8)
        af = a_ref[0, pl.ds(tf, 8), :]
        bf = b_ref[0, pl.ds(tf, 8), :]
        for d in (1, 2, 4):
            keep = rows >= d
            bf = af * jnp.where(keep, pltpu.roll(bf, d, 0), 0.0) + bf
            af = af * jnp.where(keep, pltpu.roll(af, d, 0), 1.0)
        hf = af * cf + bf
        rec_ref[pl.ds(tf, 8), :] = hf
        tb = pl.multiple_of(8 * (nch - 1 - i), 8)
        ab = a_ref[1, pl.ds(tb, 8), :]
        bb = b_ref[1, pl.ds(tb, 8), :]
        for d in (1, 2, 4):
            keep = rows < 8 - d
            bb = ab * jnp.where(keep, pltpu.roll(bb, 8 - d, 0), 0.0) + bb
            ab = ab * jnp.where(keep, pltpu.roll(ab, 8 - d, 0), 1.0)
        hb = ab * cb + bb
        xp_ref[pl.ds(tb, 8), :] = hb
        return hf[7:8, :], hb[0:1, :]

    h0 = h0_ref[...]
    cf, cb = lax.fori_loop(0, nch, scan_body, (h0[0:1, :], h0[1:2, :]), unroll=SCAN_UNROLL)
    hfin_ref[0:1, :] = cf
    hfin_ref[1:2, :] = cb
    for ci in range(seq // chunk):
        t0 = ci * chunk
        rec_ref[t0:t0 + chunk, :] = ((rec_ref[t0:t0 + chunk, :] + xp_ref[t0:t0 + chunk, :])
                                     * jax.nn.gelu(yr_ref[t0:t0 + chunk, :]))


def _lru(y, conv_w, conv_b3, wg, bg, lru_lambda, h0, l, row0, nseq, seq, xr_col, yr_col):
    d_lru = conv_w.shape[-1]
    cb = 256
    ncb = d_lru // cb
    assert row0 % seq == 0
    rb0 = row0 // seq
    h0_batched = h0.shape[0] > 1
    in_specs = [pl.BlockSpec((seq, cb), lambda b, c: (rb0 + b, xr_col // cb + c)),
                pl.BlockSpec((seq, cb), lambda b, c: (rb0 + b, yr_col // cb + c)),
                pl.BlockSpec((None, CONV_WIDTH, cb), lambda b, c: (l, 0, c)),
                pl.BlockSpec((None, 1, cb), lambda b, c: (l, 0, c)),
                pl.BlockSpec((None, None, 4, cb, cb), lambda b, c: (l, c, 0, 0, 0)),
                pl.BlockSpec((None, 4, cb), lambda b, c: (l, 0, c)),
                pl.BlockSpec((None, 2, cb), lambda b, c: (l, 0, c)),
                pl.BlockSpec((None, 2, cb), (lambda b, c: (b, 0, c)) if h0_batched else (lambda b, c: (0, 0, c)))]
    return pl.pallas_call(
        functools.partial(_lru_kernel, chunk=_tile(seq, 512)),
        grid=(nseq, ncb),
        in_specs=in_specs,
        out_specs=[pl.BlockSpec((seq, cb), lambda b, c: (b, c)),
                   pl.BlockSpec((None, 2, cb), lambda b, c: (b, 0, c))],
        out_shape=[jax.ShapeDtypeStruct((nseq * seq, d_lru), f32),
                   jax.ShapeDtypeStruct((nseq, 2, d_lru), f32)],
        scratch_shapes=[pltpu.VMEM((seq + 16, cb), f32), pltpu.VMEM((2, seq, cb), f32),
                        pltpu.VMEM((2, seq, cb), f32)],
        compiler_params=_cparams(2),
        name="rglru",
    )(y, y, conv_w, conv_b3, wg, bg, lru_lambda, h0)


def _merge_kernel(atc_ref, atl_ref, rcc_ref, rcl_ref, ga_ref, gl_ref, x_ref, gt_ref, shf_ref, scf_ref,
                  g_ref, b_ref, wa_ref, wl_ref, wo_ref, x1_ref, h2_ref, ab_ref, rb_ref, *, alpha, n_ctx_tiles):
    @pl.when(pl.program_id(0) < n_ctx_tiles)
    def _():
        ab_ref[...] = atc_ref[...].astype(bf16)
        rb_ref[...] = rcc_ref[...].astype(bf16)

    @pl.when(pl.program_id(0) >= n_ctx_tiles)
    def _():
        ab_ref[...] = atl_ref[...].astype(bf16)
        rb_ref[...] = rcl_ref[...].astype(bf16)

    pa = jnp.dot(ab_ref[...], wa_ref[...], preferred_element_type=f32)
    pr = jnp.dot(rb_ref[...], wl_ref[...], preferred_element_type=f32)
    merged = jax.nn.sigmoid(ga_ref[...]) * pa + jax.nn.sigmoid(gl_ref[...]) * pr
    out = jnp.dot(merged.astype(bf16), wo_ref[...], preferred_element_type=f32)
    x1 = _layer_norm(alpha * x_ref[...] + gt_ref[0] * out, g_ref[0], b_ref[0])
    x1_ref[...] = x1
    h2_ref[...] = x1 * (1.0 + scf_ref[0]) + shf_ref[0]


def _merge(attn_c, attn_l, rec_c, rec_l, y, x, mod4, ln_g3, ln_b3, wa, wl, wo, l, n_ctx, dec_seq, ga_col, gl_col,
           alpha):
    n, d = x.shape
    tm = _tile(math.gcd(n_ctx, dec_seq), 256)
    n_ctx_tiles = n_ctx // tm
    tiles_per_seq = dec_seq // tm

    def seg(i):
        return jnp.where(i < n_ctx_tiles, 0, 1 + (i - n_ctx_tiles) // tiles_per_seq)

    base = l * MOD_ROWS * 6
    row = lambda i: (i, 0)
    ctx_row = lambda i: (jnp.minimum(i, n_ctx_tiles - 1), 0)
    lat_row = lambda i: (jnp.maximum(i - n_ctx_tiles, 0), 0)
    mod = lambda k: pl.BlockSpec((1, 1, d), lambda i: (base + seg(i) * 6 + k, 0, 0))
    wspec = pl.BlockSpec((None, d, d), lambda i: (l, 0, 0))
    return pl.pallas_call(
        functools.partial(_merge_kernel, alpha=alpha, n_ctx_tiles=n_ctx_tiles),
        grid=(n // tm,),
        in_specs=[pl.BlockSpec((tm, d), ctx_row), pl.BlockSpec((tm, d), lat_row),
                  pl.BlockSpec((tm, d), ctx_row), pl.BlockSpec((tm, d), lat_row),
                  pl.BlockSpec((tm, d), lambda i: (i, ga_col // d)),
                  pl.BlockSpec((tm, d), lambda i: (i, gl_col // d)),
                  pl.BlockSpec((tm, d), row), mod(2), mod(3), mod(4),
                  pl.BlockSpec((1, 1, d), lambda i: (2 * l, 0, 0)),
                  pl.BlockSpec((1, 1, d), lambda i: (2 * l, 0, 0)),
                  wspec, wspec, wspec],
        out_specs=[pl.BlockSpec((tm, d), row), pl.BlockSpec((tm, d), row)],
        out_shape=[jax.ShapeDtypeStruct((n, d), f32), jax.ShapeDtypeStruct((n, d), f32)],
        scratch_shapes=[pltpu.VMEM((tm, d), bf16), pltpu.VMEM((tm, d), bf16)],
        compiler_params=_cparams(1),
        name="merge_outproj_ln",
    )(attn_c, attn_l, rec_c, rec_l, y, y, x, mod4, mod4, mod4, ln_g3, ln_b3, wa, wl, wo)


def _route_kernel(h_ref, rw_ref, rb_ref, pos_ref, gate_ref, tab_ref, carry_ref):
    n_exp = rw_ref.shape[0]
    tm = h_ref.shape[0]
    gsz = n_exp // N_GROUPS

    @pl.when(pl.program_id(0) == 0)
    def _():
        carry_ref[...] = jnp.zeros_like(carry_ref)

    logits = lax.dot_general(rw_ref[...], h_ref[...], (((1,), (1,)), ((), ())),
                             precision=HIGHEST, preferred_element_type=f32)
    s = jax.nn.sigmoid(logits)
    sb = s + rb_ref[...]
    ninf = -jnp.inf
    gs_rows = []
    for g in range(N_GROUPS):
        blk = sb[g * gsz:(g + 1) * gsz]
        m1 = jnp.max(blk, axis=0, keepdims=True)
        cnt = jnp.sum((blk == m1).astype(f32), axis=0, keepdims=True)
        m2 = jnp.max(jnp.where(blk < m1, blk, ninf), axis=0, keepdims=True)
        gs_rows.append(m1 + jnp.where(cnt >= 2.0, m1, m2))
    gs = jnp.concatenate(gs_rows, axis=0)
    gi = lax.broadcasted_iota(i32, gs.shape, 0)
    grank = jnp.zeros(gs.shape, f32)
    for g2 in range(N_GROUPS):
        row = gs[g2:g2 + 1]
        grank += jnp.where((row > gs) | ((row == gs) & (g2 < gi)), 1.0, 0.0)
    gsel = grank < float(TOPK_GROUPS)
    sbm = jnp.concatenate([jnp.where(gsel[g:g + 1], sb[g * gsz:(g + 1) * gsz], ninf)
                           for g in range(N_GROUPS)], axis=0)
    ei = lax.broadcasted_iota(i32, sbm.shape, 0)
    erank = jnp.zeros(sbm.shape, f32)
    for e2 in range(n_exp):
        row = sbm[e2:e2 + 1]
        erank += jnp.where((row > sbm) | ((row == sbm) & (e2 < ei)), 1.0, 0.0)
    sel = erank < float(TOP_K)
    ssel = jnp.where(sel, s, 0.0)
    gate = ssel / jnp.sum(ssel, axis=0, keepdims=True) * ROUTE_SCALE
    selb = sel.astype(bf16)
    ti = lax.broadcasted_iota(i32, (tm, tm), 0)
    tj = lax.broadcasted_iota(i32, (tm, tm), 1)
    upper = (ti < tj).astype(bf16)
    pos_local = jnp.dot(selb, upper, preferred_element_type=f32)
    pos_ref[...] = jnp.where(sel, pos_local, -1.0)
    gate_ref[...] = gate
    cnt = jnp.sum(sel.astype(f32), axis=1, keepdims=True)
    cpad = jnp.floor((cnt + (SEG_ALIGN - 1)) * (1.0 / SEG_ALIGN)) * SEG_ALIGN
    cpad_l = jnp.broadcast_to(cpad, (n_exp, 128))
    li = lax.broadcasted_iota(i32, (n_exp, n_exp), 0)
    lj = lax.broadcasted_iota(i32, (n_exp, n_exp), 1)
    lower = (lj < li).astype(bf16)
    run_start = jnp.dot(lower, cpad_l.astype(bf16), preferred_element_type=f32)
    lane = lax.broadcasted_iota(i32, (n_exp, 128), 1)
    tab = jnp.where(lane == TAB_LEN, cpad_l, jnp.where(lane == TAB_BASE, carry_ref[...], run_start))
    tab_ref[...] = tab.astype(i32)
    carry_ref[...] = carry_ref[...] + cpad_l


def _route(h2, rwt, rb3, l):
    n, d = h2.shape
    n_exp = rwt.shape[1]
    tm = ROUTE_TILE
    assert n % tm == 0
    lst = pl.BlockSpec((n_exp, tm), lambda i: (0, i))
    return pl.pallas_call(
        _route_kernel,
        grid=(n // tm,),
        in_specs=[pl.BlockSpec((tm, d), lambda i: (i, 0)),
                  pl.BlockSpec((None, n_exp, d), lambda i: (l, 0, 0)),
                  pl.BlockSpec((None, n_exp, 1), lambda i: (l, 0, 0))],
        out_specs=[lst, lst, pl.BlockSpec((None, n_exp, 128), lambda i: (i, 0, 0))],
        out_shape=[jax.ShapeDtypeStruct((n_exp, n), f32), jax.ShapeDtypeStruct((n_exp, n), f32),
                   jax.ShapeDtypeStruct((n // tm, n_exp, 128), i32)],
        scratch_shapes=[pltpu.VMEM((n_exp, 128), f32)],
        compiler_params=_cparams(1),
        name="moe_route",
    )(h2, rwt, rb3)


def _pow2_pieces(count, max_rows, fn):
    size = max_rows
    while size >= SEG_ALIGN:
        off = count & (-2 * size)
        pl.when((count & size) != 0)(functools.partial(fn, off, size))
        size //= 2


def _wait_rows(count, max_rows, src_ref, dst_ref, sem):
    def wait(_, size):
        pltpu.make_async_copy(src_ref.at[pl.ds(0, size)], dst_ref.at[pl.ds(0, size)], sem).wait()

    _pow2_pieces(count, max_rows, wait)


def _pad_experts(x):
    return jnp.concatenate([x, jnp.zeros((128 - x.shape[0], x.shape[1]), x.dtype)], axis=0)


def _dispatch_kernel(cp_ref, ga_ref, tot_ref, ts_ref, tl_ref, pos_ref, tab_ref, h_ref, xs_ref, buf_ref, z_ref,
                     sem, *, n_exp, n_tiles, tile_rows, wait_max):
    t = pl.program_id(0)
    slot = t % 2
    tm = h_ref.shape[0]

    @pl.when(t >= 2)
    def _():
        _wait_rows(tot_ref[t - 2], wait_max, buf_ref.at[slot], xs_ref, sem.at[slot])

    tab = _pad_experts(tab_ref[...].astype(f32))
    tab_t = tab.T
    start_row, len_row = tab_t[TAB_START:TAB_START + 1, :], tab_t[TAB_LEN:TAB_LEN + 1, :]
    rid = lax.broadcasted_iota(i32, (tile_rows, 128), 0).astype(f32)
    in_run = ((rid >= start_row) & (rid < start_row + len_row)).astype(bf16)
    pos_rows = jnp.dot(in_run, _pad_experts(pos_ref[...]).astype(bf16), preferred_element_type=f32)
    start16 = (tab[:, TAB_START:TAB_START + 1] * (1.0 / SEG_ALIGN)).astype(bf16)
    run_start = jnp.dot(in_run, jnp.broadcast_to(start16, (128, 128)), preferred_element_type=f32) * SEG_ALIGN
    ordinal = rid - run_start
    ordinal = jnp.concatenate([ordinal] * (tm // 128), axis=1)
    perm = jnp.where(pos_rows == ordinal, 1.0, 0.0).astype(bf16)
    buf_ref[slot] = jnp.dot(perm, h_ref[...].astype(bf16), preferred_element_type=f32).astype(bf16)

    def issue(e, lo):
        c = cp_ref[t * n_exp + e]
        g = ga_ref[t * n_exp + e]

        def copy(off, size):
            src = pl.multiple_of(lo + off, SEG_ALIGN)
            dst = pl.multiple_of(g + off, SEG_ALIGN)
            pltpu.make_async_copy(buf_ref.at[slot, pl.ds(src, size)], xs_ref.at[pl.ds(dst, size)],
                                  sem.at[slot]).start()

        _pow2_pieces(c, tm, copy)
        return lo + c

    lax.fori_loop(0, n_exp, issue, 0)

    @pl.when(t == n_tiles - 1)
    def _():
        z_ref[...] = jnp.zeros_like(z_ref)

        def fill(e, _):
            def copy(off, size):
                dst = pl.multiple_of(ts_ref[e] + off, SEG_ALIGN)
                pltpu.make_async_copy(z_ref.at[pl.ds(0, size)], xs_ref.at[pl.ds(dst, size)], sem.at[2]).start()

            _pow2_pieces(tl_ref[e], MOE_BLOCK // 2, copy)
            return 0

        lax.fori_loop(0, n_exp, fill, 0)

        def drain(e, _):
            _wait_rows(tl_ref[e], MOE_BLOCK // 2, z_ref, xs_ref, sem.at[2])
            return 0

        lax.fori_loop(0, n_exp, drain, 0)
        if n_tiles > 1:
            _wait_rows(tot_ref[t - 1], wait_max, buf_ref.at[1 - slot], xs_ref, sem.at[1 - slot])
        _wait_rows(tot_ref[t], wait_max, buf_ref.at[slot], xs_ref, sem.at[slot])


def _tile_rows(tm, n_exp):
    return -(-(tm * TOP_K + n_exp * (SEG_ALIGN - 1)) // 128) * 128


def _wait_max(tile_rows):
    return 1 << (tile_rows.bit_length() - 1)


def _dispatch(h2, pos, tab, cpad, gabs, tot, tail_start, tail_len, n_rows, n_exp):
    n, d = h2.shape
    tm = ROUTE_TILE
    n_tiles = n // tm
    tile_rows = _tile_rows(tm, n_exp)
    assert n_exp <= 128 and tm % 128 == 0
    return pl.pallas_call(
        functools.partial(_dispatch_kernel, n_exp=n_exp, n_tiles=n_tiles, tile_rows=tile_rows,
                          wait_max=_wait_max(tile_rows)),
        grid_spec=pltpu.PrefetchScalarGridSpec(
            num_scalar_prefetch=5,
            grid=(n_tiles,),
            in_specs=[pl.BlockSpec((n_exp, tm), lambda i, *_: (0, i)),
                      pl.BlockSpec((None, n_exp, 128), lambda i, *_: (i, 0, 0)),
                      pl.BlockSpec((tm, d), lambda i, *_: (i, 0))],
            out_specs=pl.BlockSpec(memory_space=pl.ANY),
            scratch_shapes=[pltpu.VMEM((2, tile_rows, d), bf16), pltpu.VMEM((MOE_BLOCK // 2, d), bf16),
                            pltpu.SemaphoreType.DMA((3,))]),
        out_shape=jax.ShapeDtypeStruct((n_rows, d), bf16),
        compiler_params=_cparams(1),
        name="moe_dispatch",
    )(cpad, gabs, tot, tail_start, tail_len, pos, tab, h2)


def _expert_kernel(be_ref, nu_ref, x_ref, wg_ref, wu_ref, wd_ref, o_ref):
    @pl.when(pl.program_id(0) < nu_ref[0])
    def _():
        x = x_ref[...]
        g = jnp.dot(x, wg_ref[...].astype(bf16), preferred_element_type=f32)
        u = jnp.dot(x, wu_ref[...].astype(bf16), preferred_element_type=f32)
        h = (g * jax.nn.sigmoid(g)) * u
        o_ref[...] = jnp.dot(h.astype(bf16), wd_ref[...].astype(bf16),
                             preferred_element_type=f32).astype(bf16)


def _experts(xs, blk_e, n_used, we_g, we_u, we_d, l):
    n_rows, d = xs.shape
    f = we_g.shape[-1]
    nblk = n_rows // MOE_BLOCK
    blk = lambda b, be, nu: (jnp.minimum(b, nu[0] - 1), 0)
    wsel = lambda b, be, nu: (l, be[jnp.minimum(b, nu[0] - 1)], 0, 0)
    return pl.pallas_call(
        _expert_kernel,
        grid_spec=pltpu.PrefetchScalarGridSpec(
            num_scalar_prefetch=2,
            grid=(nblk,),
            in_specs=[pl.BlockSpec((MOE_BLOCK, d), blk),
                      pl.BlockSpec((None, None, d, f), wsel),
                      pl.BlockSpec((None, None, d, f), wsel),
                      pl.BlockSpec((None, None, f, d), wsel)],
            out_specs=pl.BlockSpec((MOE_BLOCK, d), blk)),
        out_shape=jax.ShapeDtypeStruct((n_rows, d), bf16),
        compiler_params=_cparams(1),
        name="moe_experts",
    )(blk_e, n_used, xs, we_g, we_u, we_d)


def _combine_kernel(cp_ref, ga_ref, tot_ref, pos_ref, gate_ref, tab_ref, ys_ref, x1_ref, h2_ref, gt_ref, g_ref, b_ref,
                    sg_ref, su_ref, sd_ref, o_ref, buf_ref, sem, *, alpha, n_exp, n_tiles, tile_rows, wait_max):
    t = pl.program_id(0)
    slot = t % 2
    tm = x1_ref.shape[0]

    def fetch(tile, dst_slot):
        def issue(e, lo):
            c = cp_ref[tile * n_exp + e]
            g = ga_ref[tile * n_exp + e]

            def copy(off, size):
                src = pl.multiple_of(g + off, SEG_ALIGN)
                dst = pl.multiple_of(lo + off, SEG_ALIGN)
                pltpu.make_async_copy(ys_ref.at[pl.ds(src, size)], buf_ref.at[dst_slot, pl.ds(dst, size)],
                                      sem.at[dst_slot]).start()

            _pow2_pieces(c, tm, copy)
            return lo + c

        lax.fori_loop(0, n_exp, issue, 0)

    @pl.when(t == 0)
    def _():
        buf_ref[...] = jnp.zeros_like(buf_ref)
        fetch(0, 0)

    @pl.when(t + 1 < n_tiles)
    def _():
        fetch(t + 1, 1 - slot)

    hb = h2_ref[...].astype(bf16)
    g = jnp.dot(hb, sg_ref[...], preferred_element_type=f32)
    u = jnp.dot(hb, su_ref[...], preferred_element_type=f32)
    shared = jnp.dot(((g * jax.nn.sigmoid(g)) * u).astype(bf16), sd_ref[...], preferred_element_type=f32)
    tab = _pad_experts(tab_ref[...].astype(f32))
    start_col, len_col = tab[:, TAB_START:TAB_START + 1], tab[:, TAB_LEN:TAB_LEN + 1]
    rid = lax.broadcasted_iota(i32, (128, tile_rows), 1).astype(f32)
    in_run = (rid >= start_col) & (rid < start_col + len_col)
    ordinal = rid[0:1, :] - jnp.sum(jnp.where(in_run, start_col, 0.0), axis=0, keepdims=True)
    in_run = in_run.astype(bf16)
    pos_t = _pad_experts(pos_ref[...]).T.astype(bf16)
    gate_t = _pad_experts(gate_ref[...]).T.astype(bf16)
    pos_rows = jnp.dot(pos_t, in_run, preferred_element_type=f32)
    gate_rows = jnp.dot(gate_t, in_run, preferred_element_type=f32)
    wgt = jnp.where(pos_rows == ordinal, gate_rows, 0.0).astype(bf16)
    _wait_rows(tot_ref[t], wait_max, ys_ref, buf_ref.at[slot], sem.at[slot])
    routed = jnp.dot(wgt, buf_ref[slot], preferred_element_type=f32)
    o_ref[...] = _layer_norm(alpha * x1_ref[...] + gt_ref[0] * (shared + routed), g_ref[0], b_ref[0])


def _combine(ys, pos, gate, tab, cpad, gabs, tot, x1, h2, mod4, ln_g3, ln_b3, sg, su, sd, l, n_ctx, dec_seq, alpha,
             n_exp):
    n, d = x1.shape
    f = sg.shape[-1]
    tm = ROUTE_TILE
    assert n_ctx % tm == 0 and dec_seq % tm == 0
    n_tiles = n // tm
    tile_rows = _tile_rows(tm, n_exp)
    n_ctx_tiles = n_ctx // tm
    tiles_per_seq = dec_seq // tm

    def seg(i):
        return jnp.where(i < n_ctx_tiles, 0, 1 + (i - n_ctx_tiles) // tiles_per_seq)

    base = l * MOD_ROWS * 6
    row = lambda i, *_: (i, 0)
    lst = lambda i, *_: (0, i)
    return pl.pallas_call(
        functools.partial(_combine_kernel, alpha=alpha, n_exp=n_exp, n_tiles=n_tiles, tile_rows=tile_rows,
                          wait_max=_wait_max(tile_rows)),
        grid_spec=pltpu.PrefetchScalarGridSpec(
            num_scalar_prefetch=3,
            grid=(n_tiles,),
            in_specs=[pl.BlockSpec((n_exp, tm), lst), pl.BlockSpec((n_exp, tm), lst),
                      pl.BlockSpec((None, n_exp, 128), lambda i, *_: (i, 0, 0)),
                      pl.BlockSpec(memory_space=pl.ANY),
                      pl.BlockSpec((tm, d), row), pl.BlockSpec((tm, d), row),
                      pl.BlockSpec((1, 1, d), lambda i, *_: (base + seg(i) * 6 + 5, 0, 0)),
                      pl.BlockSpec((1, 1, d), lambda i, *_: (2 * l + 1, 0, 0)),
                      pl.BlockSpec((1, 1, d), lambda i, *_: (2 * l + 1, 0, 0)),
                      pl.BlockSpec((None, d, f), lambda i, *_: (l, 0, 0)),
                      pl.BlockSpec((None, d, f), lambda i, *_: (l, 0, 0)),
                      pl.BlockSpec((None, f, d), lambda i, *_: (l, 0, 0))],
            out_specs=pl.BlockSpec((tm, d), row),
            scratch_shapes=[pltpu.VMEM((2, tile_rows, d), bf16), pltpu.SemaphoreType.DMA((2,))]),
        out_shape=jax.ShapeDtypeStruct((n, d), f32),
        compiler_params=_cparams(1),
        name="moe_combine_ln",
    )(cpad, gabs, tot, pos, gate, tab, ys, x1, h2, mod4, ln_g3, ln_b3, sg, su, sd)


def _rope_tables(seq, width):
    rows = seq // GRID_W
    row = jnp.repeat(jnp.arange(rows, dtype=f32), GRID_W)
    col = jnp.tile(jnp.arange(GRID_W, dtype=f32), rows)
    half = HEAD_DIM // 2
    inv = ROPE_BASE ** (-jnp.arange(0, half, 2, dtype=f32) / half)
    lane = jnp.arange(width)
    d = lane % HEAD_DIM
    pos = jnp.where((d // half)[None, :] == 0, row[:, None], col[:, None])
    ang = pos * inv[d % (HEAD_DIM // 4)][None, :]
    sign = jnp.where((d % half) < HEAD_DIM // 4, -1.0, 1.0)[None, :]
    return jnp.cos(ang), jnp.sin(ang) * sign


def _block_diag_gates(gate_w_r, gate_w_i, cb):
    depth, _, nb, bw, _ = gate_w_r.shape
    per = cb // bw
    w = jnp.stack([gate_w_r[:, 0], gate_w_i[:, 0], gate_w_r[:, 1], gate_w_i[:, 1]], axis=1)
    w = w.reshape(depth, 4, nb // per, per, bw, bw)
    eye = jnp.eye(per, dtype=w.dtype)
    bd = jnp.einsum("lgcaio,ab->lcgaibo", w, eye)
    return bd.reshape(depth, nb // per, 4, cb, cb).astype(bf16)


def kernel(x_prompt, x_sample, c, cache_k, cache_v, state_lru, c_ctx, w_mod, b_mod, w_in, lam_param, subln_w,
           conv_w, conv_b, gate_w_r, gate_b_r, gate_w_i, gate_b_i, lru_lambda, w_br_attn, w_br_lru, w_out,
           ln_g, ln_b, router_w, router_b, exp_w_gate, exp_w_up, exp_w_down, sh_w_gate, sh_w_up, sh_w_down):
    batch, seq, d = x_prompt.shape
    dec_batch, dec_seq, _ = x_sample.shape
    depth = w_mod.shape[0]
    past = cache_k.shape[2]
    d_lru = conv_w.shape[-1]
    n_exp = router_w.shape[-1]
    qk = N_HEADS * 2 * HEAD_DIM
    attn_w = N_HEADS * V_DIM
    n_ctx, n_lat = batch * seq, dec_batch * dec_seq
    n = n_ctx + n_lat
    alpha = (2 * depth) ** 0.25
    assert 1 + dec_batch <= MOD_ROWS and n_ctx % dec_seq == 0

    cond = jnp.concatenate([c_ctx[None], c, jnp.zeros((MOD_ROWS - 1 - dec_batch, d), f32)], 0)
    w_in_b = w_in.astype(bf16)
    wa, wl, wo = w_br_attn.astype(bf16), w_br_lru.astype(bf16), w_out.astype(bf16)
    sg, su, sd = sh_w_gate.astype(bf16), sh_w_up.astype(bf16), sh_w_down.astype(bf16)
    rwt = jnp.swapaxes(router_w, 1, 2)
    rb3 = router_b.reshape(depth, n_exp, 1)
    wg = _block_diag_gates(gate_w_r, gate_w_i, 256)
    bg = jnp.stack([gate_b_r[:, 0], gate_b_i[:, 0], gate_b_r[:, 1], gate_b_i[:, 1]], axis=1)
    conv_b3 = conv_b.reshape(depth, 1, d_lru)
    subln_w3 = subln_w.reshape(depth, 1, V_DIM)
    ln_g3 = ln_g.reshape(depth * 2, 1, d)
    ln_b3 = ln_b.reshape(depth * 2, 1, d)
    ck = cache_k.reshape(dec_batch, depth, past, N_HEADS * 2 * HEAD_DIM)
    cv = cache_v.reshape(dec_batch, depth, past, attn_w)
    cos_t, sin_t = _rope_tables(dec_seq, V_DIM)
    h0_ctx = jnp.zeros((1, 2, d_lru), f32)

    mod4 = _mod_table(cond, w_mod, b_mod).reshape(depth * MOD_ROWS * 6, 1, d)
    x = jnp.concatenate([x_prompt.reshape(n_ctx, d), x_sample.reshape(n_lat, d)], 0)
    n_rows = n * TOP_K + (n // ROUTE_TILE) * n_exp * (SEG_ALIGN - 1) + n_exp * (MOE_BLOCK - 1)
    n_rows = -(-n_rows // MOE_BLOCK) * MOE_BLOCK
    xr_col, yr_col = 2 * qk + attn_w, 2 * qk + attn_w + d_lru
    ga_col, gl_col = yr_col + d_lru, yr_col + d_lru + d

    new_k, new_v, new_s = [], [], []
    for l in range(depth):
        lam_init = 0.8 - 0.6 * math.exp(-0.3 * l)
        y = _inproj(x, mod4, w_in_b, cos_t, sin_t, l, n_ctx, dec_seq, 2 * qk)
        new_k.append(y[:n_ctx, qk:2 * qk].reshape(batch, seq, N_HEADS, 2 * HEAD_DIM))
        new_v.append(y[:n_ctx, 2 * qk:2 * qk + attn_w].reshape(batch, seq, N_HEADS, V_DIM))
        attn_c = _attention(y, lam_param, subln_w3, l, lam_init, 0, batch, seq)
        attn_l = _attention(y, lam_param, subln_w3, l, lam_init, n_ctx, dec_batch, dec_seq, ck, cv)
        rec_c, hfin = _lru(y, conv_w, conv_b3, wg, bg, lru_lambda, h0_ctx, l, 0, batch, seq, xr_col, yr_col)
        rec_l, _ = _lru(y, conv_w, conv_b3, wg, bg, lru_lambda, state_lru[:, l], l, n_ctx, dec_batch, dec_seq,
                        xr_col, yr_col)
        new_s.append(hfin)
        x1, h2 = _merge(attn_c, attn_l, rec_c, rec_l, y, x, mod4, ln_g3, ln_b3, wa, wl, wo, l, n_ctx, dec_seq, ga_col, gl_col, alpha)
        pos, gate, tab = _route(h2, rwt, rb3, l)
        cpad, grel = tab[:, :, TAB_LEN], tab[:, :, TAB_BASE]
        used = grel[-1] + cpad[-1]
        padded = (used + MOE_BLOCK - 1) // MOE_BLOCK * MOE_BLOCK
        pad_end = jnp.cumsum(padded)
        pad_start = pad_end - padded
        gabs = (pad_start[None, :] + grel).reshape(-1).astype(i32)
        tot = jnp.sum(cpad, axis=1).astype(i32)
        blk_row0 = jnp.arange(n_rows // MOE_BLOCK, dtype=i32) * MOE_BLOCK
        blk_e = jnp.minimum(jnp.sum(pad_end[None, :] <= blk_row0[:, None], axis=1), n_exp - 1).astype(i32)
        n_used = (pad_end[-1:] // MOE_BLOCK).astype(i32)
        cpad = cpad.reshape(-1)
        xs = _dispatch(h2, pos, tab, cpad, gabs, tot, (pad_start + used).astype(i32),
                       (padded - used).astype(i32), n_rows, n_exp)
        ys = _experts(xs, blk_e, n_used, exp_w_gate, exp_w_up, exp_w_down, l)
        x = _combine(ys, pos, gate, tab, cpad, gabs, tot, x1, h2, mod4, ln_g3, ln_b3, sg, su, sd, l, n_ctx, dec_seq,
                     alpha, n_exp)
    return (x[:n_ctx].reshape(batch, seq, d), x[n_ctx:].reshape(dec_batch, dec_seq, d),
            jnp.stack(new_k, 1), jnp.stack(new_v, 1), jnp.stack(new_s, 1))
```

```python
import functools
import math

import jax
import jax.numpy as jnp
from jax import lax
from jax.experimental import pallas as pl
from jax.experimental.pallas import tpu as pltpu

f32, bf16, i32 = jnp.float32, jnp.bfloat16, jnp.int32

GRID_W = 64
N_HEADS = 8
HEAD_DIM = 64
V_DIM = 2 * HEAD_DIM
ROPE_BASE = 10000.0
LRU_BLOCKS = 16
CONV_WIDTH = 4
LRU_C = 8.0
TOP_K = 8
N_GROUPS = 8
TOPK_GROUPS = 4
ROUTE_SCALE = 2.5
LN_EPS = 1e-5
MOD_ROWS = 8
MOE_BLOCK = 1024
ROUTE_TILE = 256
SEG_ALIGN = 16
TAB_LEN, TAB_BASE, TAB_START = 0, 1, 2
SCAN_UNROLL = 4
VMEM_LIMIT = 56 * 1024 * 1024
HIGHEST = lax.Precision.HIGHEST


def _cparams(n_axes):
    return pltpu.CompilerParams(dimension_semantics=("arbitrary",) * n_axes,
                                vmem_limit_bytes=VMEM_LIMIT)


def _tile(n, pref):
    t = min(n, pref)
    while n % t:
        t -= 8
    assert t > 0 and n % t == 0
    return t


def _sigmoid(x):
    return 0.5 * jnp.tanh(0.5 * x) + 0.5


def _layer_norm(z, g, b):
    mu = jnp.mean(z, -1, keepdims=True)
    zc = z - mu
    var = jnp.mean(zc * zc, -1, keepdims=True)
    return zc * lax.rsqrt(var + LN_EPS) * g + b


def _mod_kernel(c_ref, w_ref, b_ref, o_ref):
    c = c_ref[...]
    a = c * jax.nn.sigmoid(c)
    o_ref[...] = jnp.dot(a, w_ref[...], precision=HIGHEST, preferred_element_type=f32) + b_ref[...]


def _mod_table(cond, w_mod, b_mod):
    depth, d, d6 = w_mod.shape
    tn = _tile(d6, 1024)
    return pl.pallas_call(
        _mod_kernel,
        grid=(depth, d6 // tn),
        in_specs=[pl.BlockSpec((MOD_ROWS, d), lambda l, j: (0, 0)),
                  pl.BlockSpec((None, d, tn), lambda l, j: (l, 0, j)),
                  pl.BlockSpec((None, 1, tn), lambda l, j: (l, 0, j))],
        out_specs=pl.BlockSpec((None, MOD_ROWS, tn), lambda l, j: (l, 0, j)),
        out_shape=jax.ShapeDtypeStruct((depth, MOD_ROWS, d6), f32),
        compiler_params=_cparams(2),
        name="mod_table",
    )(cond, w_mod, b_mod.reshape(depth, 1, d6))


def _inproj_kernel(x_ref, sh_ref, sc_ref, w_ref, cos_ref, sin_ref, o_ref, hb_ref, *,
                   n_ctx_tiles, n_rope_cols, tn):
    i = pl.program_id(0)
    j = pl.program_id(1)

    @pl.when(j == 0)
    def _():
        hb_ref[...] = (x_ref[...] * (1.0 + sc_ref[0]) + sh_ref[0]).astype(bf16)

    acc = jnp.dot(hb_ref[...], w_ref[...], preferred_element_type=f32)
    rope = jnp.logical_and(i >= n_ctx_tiles, j < n_rope_cols)

    @pl.when(rope)
    def _():
        cos, sin = cos_ref[...], sin_ref[...]
        lane = lax.broadcasted_iota(i32, cos.shape, 1)
        first = (lane % (HEAD_DIM // 2)) < (HEAD_DIM // 4)
        for c in range(tn // V_DIM):
            a = acc[:, c * V_DIM:(c + 1) * V_DIM]
            partner = jnp.where(first, pltpu.roll(a, V_DIM - HEAD_DIM // 4, 1), pltpu.roll(a, HEAD_DIM // 4, 1))
            o_ref[:, c * V_DIM:(c + 1) * V_DIM] = a * cos + partner * sin

    @pl.when(jnp.logical_not(rope))
    def _():
        o_ref[...] = acc


def _inproj(x, mod4, w_in_b, cos_t, sin_t, l, n_ctx, dec_seq, qk_cols):
    n, d = x.shape
    d_in = w_in_b.shape[-1]
    tm = _tile(math.gcd(n_ctx, dec_seq), 1024)
    tn = _tile(math.gcd(d_in, qk_cols), 1024)
    assert tn % V_DIM == 0
    n_ctx_tiles = n_ctx // tm
    tiles_per_seq = dec_seq // tm

    def seg(i):
        return jnp.where(i < n_ctx_tiles, 0, 1 + (i - n_ctx_tiles) // tiles_per_seq)

    def pos(i):
        return jnp.where(i < n_ctx_tiles, 0, (i - n_ctx_tiles) % tiles_per_seq)

    base = l * MOD_ROWS * 6
    return pl.pallas_call(
        functools.partial(_inproj_kernel, n_ctx_tiles=n_ctx_tiles, n_rope_cols=qk_cols // tn, tn=tn),
        grid=(n // tm, d_in // tn),
        in_specs=[pl.BlockSpec((tm, d), lambda i, j: (i, 0)),
                  pl.BlockSpec((1, 1, d), lambda i, j: (base + seg(i) * 6 + 0, 0, 0)),
                  pl.BlockSpec((1, 1, d), lambda i, j: (base + seg(i) * 6 + 1, 0, 0)),
                  pl.BlockSpec((None, d, tn), lambda i, j: (l, 0, j)),
                  pl.BlockSpec((tm, V_DIM), lambda i, j: (pos(i), 0)),
                  pl.BlockSpec((tm, V_DIM), lambda i, j: (pos(i), 0))],
        out_specs=pl.BlockSpec((tm, tn), lambda i, j: (i, j)),
        out_shape=jax.ShapeDtypeStruct((n, d_in), f32),
        scratch_shapes=[pltpu.VMEM((tm, d), bf16)],
        compiler_params=_cparams(2),
        name="in_proj",
    )(x, mod4, mod4, w_in_b, cos_t, sin_t)


def _key_chunk(lk):
    for ck in (512, 384, 256, 128):
        if lk % ck == 0:
            return ck
    return lk


def _attn_kernel(*refs, has_cache, lam_init, past, heads):
    if has_cache:
        q_ref, k_ref, v_ref, ck_ref, cv_ref, lp_ref, sw_ref, o_ref, kt_ref, vb_ref, s_ref = refs
    else:
        q_ref, k_ref, v_ref, lp_ref, sw_ref, o_ref, kt_ref, vb_ref, s_ref = refs

    @pl.when(pl.program_id(2) == 0)
    def _():
        for h in range(heads):
            cols = slice(h * V_DIM, (h + 1) * V_DIM)
            if has_cache:
                kt_ref[h, :, :past] = ck_ref[:, cols].T.astype(bf16)
                vb_ref[h, :past, :] = cv_ref[:, cols].astype(bf16)
            kt_ref[h, :, past:] = k_ref[:, cols].T.astype(bf16)
            vb_ref[h, past:, :] = v_ref[:, cols].astype(bf16)

    lp = lp_ref[...]
    lam = (jnp.exp(jnp.sum(lp[0:1] * lp[1:2], keepdims=True))
           - jnp.exp(jnp.sum(lp[2:3] * lp[3:4], keepdims=True)) + lam_init)
    for h in range(heads):
        cols = slice(h * V_DIM, (h + 1) * V_DIM)
        o = _attn_head(q_ref[:, cols], kt_ref.at[h], vb_ref.at[h], s_ref, lam)
        o = o * lax.rsqrt(jnp.mean(o * o, -1, keepdims=True) + LN_EPS) * sw_ref[...]
        o_ref[:, cols] = o * (1.0 - lam_init)


def _attn_head(q, kt_ref, vb_ref, s_ref, lam):
    tq = q.shape[0]
    q = q * (HEAD_DIM ** -0.5 * math.log2(math.e))
    lane = lax.broadcasted_iota(i32, q.shape, 1)
    q2 = jnp.concatenate([jnp.where(lane < HEAD_DIM, q, 0.0).astype(bf16),
                          jnp.where(lane >= HEAD_DIM, q, 0.0).astype(bf16)], axis=0)
    lk = kt_ref.shape[1]
    ck = _key_chunk(lk)
    lanes = 128
    mx = None
    for c in range(lk // ck):
        s_c = jnp.dot(q2, kt_ref[:, c * ck:(c + 1) * ck], preferred_element_type=f32)
        s_ref[:, c * ck:(c + 1) * ck] = s_c
        for j in range(ck // lanes):
            t = s_c[:, j * lanes:(j + 1) * lanes]
            mx = t if mx is None else jnp.maximum(mx, t)
    mb = jnp.broadcast_to(jnp.max(mx, axis=-1, keepdims=True), (2 * tq, lanes))
    ls = jnp.zeros((2 * tq, lanes), f32)
    for j in range(lk // lanes):
        e_t = jnp.exp2(s_ref[:, j * lanes:(j + 1) * lanes] - mb)
        ls = ls + e_t
        s_ref[:, j * lanes:(j + 1) * lanes] = e_t
    inv = 1.0 / jnp.sum(ls, axis=-1, keepdims=True)
    c0 = jnp.broadcast_to(inv[:tq], (tq, lanes))
    c1 = jnp.broadcast_to(lam * inv[tq:], (tq, lanes))
    o = jnp.zeros((tq, V_DIM), f32)
    for c in range(lk // ck):
        a_c = jnp.concatenate(
            [(s_ref[:tq, j * lanes:(j + 1) * lanes] * c0 - s_ref[tq:, j * lanes:(j + 1) * lanes] * c1).astype(bf16)
             for j in range(c * ck // lanes, (c + 1) * ck // lanes)], axis=1)
        o = o + jnp.dot(a_c, vb_ref[c * ck:(c + 1) * ck, :], preferred_element_type=f32)
    return o


def _attention(y, lam_param, subln_w3, l, lam_init, row0, nseq, seq, cache_k=None, cache_v=None):
    has_cache = cache_k is not None
    past = cache_k.shape[2] if has_cache else 0
    lk = past + seq
    tq = _tile(seq, 256)
    nq = seq // tq
    heads = N_HEADS if lk <= 512 else 1
    hg = N_HEADS // heads
    w = heads * V_DIM
    assert row0 % seq == 0
    qb0, kb0 = row0 // tq, row0 // seq
    in_specs = [pl.BlockSpec((tq, w), lambda b, h, i: (qb0 + b * nq + i, h)),
                pl.BlockSpec((seq, w), lambda b, h, i: (kb0 + b, hg + h)),
                pl.BlockSpec((seq, w), lambda b, h, i: (kb0 + b, 2 * hg + h))]
    args = [y, y, y]
    if has_cache:
        in_specs += [pl.BlockSpec((None, None, past, w), lambda b, h, i: (b, l, 0, h)),
                     pl.BlockSpec((None, None, past, w), lambda b, h, i: (b, l, 0, h))]
        args += [cache_k, cache_v]
    in_specs += [pl.BlockSpec((None, 4, HEAD_DIM), lambda b, h, i: (l, 0, 0)),
                 pl.BlockSpec((None, 1, V_DIM), lambda b, h, i: (l, 0, 0))]
    args += [lam_param, subln_w3]
    return pl.pallas_call(
        functools.partial(_attn_kernel, has_cache=has_cache, lam_init=lam_init, past=past, heads=heads),
        grid=(nseq, hg, nq),
        in_specs=in_specs,
        out_specs=pl.BlockSpec((tq, w), lambda b, h, i: (b * nq + i, h)),
        out_shape=jax.ShapeDtypeStruct((nseq * seq, N_HEADS * V_DIM), f32),
        scratch_shapes=[pltpu.VMEM((heads, V_DIM, lk), bf16), pltpu.VMEM((heads, lk, V_DIM), bf16),
                        pltpu.VMEM((2 * tq, lk), f32)],
        compiler_params=_cparams(3),
        name="diff_attn_lat" if has_cache else "diff_attn_ctx",
    )(*args)


def _lru_kernel(xr_ref, yr_ref, cw_ref, cb_ref, wg_ref, bg_ref, lam_ref, h0_ref,
                rec_ref, hfin_ref, xp_ref, a_ref, b_ref, *, chunk):
    seq, c = xr_ref.shape
    xp_ref[0:8, :] = jnp.zeros((8, c), f32)
    xp_ref[8:8 + seq, :] = xr_ref[...]
    xp_ref[8 + seq:16 + seq, :] = jnp.zeros((8, c), f32)
    lam = lam_ref[...]
    nlam = -lam
    softplus = jnp.maximum(nlam, 0.0) + jnp.log(1.0 + jnp.exp(-jnp.abs(nlam)))
    for ci in range(seq // chunk):
        t0 = ci * chunk
        xc = cb_ref[...] + sum(cw_ref[j:j + 1, :] * xp_ref[8 + t0 + j - 1:8 + t0 + j - 1 + chunk, :]
                               for j in range(CONV_WIDTH))
        xcb = xc.astype(bf16)
        for d in range(2):
            r = _sigmoid(jnp.dot(xcb, wg_ref[2 * d], preferred_element_type=f32) + bg_ref[2 * d:2 * d + 1, :])
            g = _sigmoid(jnp.dot(xcb, wg_ref[2 * d + 1], preferred_element_type=f32)
                         + bg_ref[2 * d + 1:2 * d + 2, :])
            log_a = -LRU_C * r * softplus[d:d + 1, :]
            a = jnp.exp(log_a)
            a_ref[d, t0:t0 + chunk, :] = a
            u = -jnp.tanh(log_a) * (a * a + 1.0)
            b_ref[d, t0:t0 + chunk, :] = jnp.where(u > 0.0, u * lax.rsqrt(u), 0.0) * (g * xc)

    rows = lax.broadcasted_iota(i32, (8, c), 0)
    nch = seq // 8

    def scan_body(i, carry):
        cf, cb = carry
        tf = pl.multiple_of(8 * i, 8)
        af = a_ref[0, pl.ds(tf, 8), :]
        bf = b_ref[0, pl.ds(tf, 8), :]
        for d in (1, 2, 4):
            keep = rows >= d
            bf = af * jnp.where(keep, pltpu.roll(bf, d, 0), 0.0) + bf
            af = af * jnp.where(keep, pltpu.roll(af, d, 0), 1.0)
        hf = af * cf + bf
        rec_ref[pl.ds(tf, 8), :] = hf
        tb = pl.multiple_of(8 * (nch - 1 - i), 8)
        ab = a_ref[1, pl.ds(tb, 8), :]
        bb = b_ref[1, pl.ds(tb, 8), :]
        for d in (1, 2, 4):
            keep = rows < 8 - d
            bb = ab * jnp.where(keep, pltpu.roll(bb, 8 - d, 0), 0.0) + bb
            ab = ab * jnp.where(keep, pltpu.roll(ab, 8 - d, 0), 1.0)
        hb = ab * cb + bb
        xp_ref[pl.ds(tb, 8), :] = hb
        return hf[7:8, :], hb[0:1, :]

    h0 = h0_ref[...]
    cf, cb = lax.fori_loop(0, nch, scan_body, (h0[0:1, :], h0[1:2, :]), unroll=SCAN_UNROLL)
    hfin_ref[0:1, :] = cf
    hfin_ref[1:2, :] = cb
    for ci in range(seq // chunk):
        t0 = ci * chunk
        rec_ref[t0:t0 + chunk, :] = ((rec_ref[t0:t0 + chunk, :] + xp_ref[t0:t0 + chunk, :])
                                     * jax.nn.gelu(yr_ref[t0:t0 + chunk, :]))


def _lru(y, conv_w, conv_b3, wg, bg, lru_lambda, h0, l, row0, nseq, seq, xr_col, yr_col):
    d_lru = conv_w.shape[-1]
    cb = 256
    ncb = d_lru // cb
    assert row0 % seq == 0
    rb0 = row0 // seq
    h0_batched = h0.shape[0] > 1
    in_specs = [pl.BlockSpec((seq, cb), lambda b, c: (rb0 + b, xr_col // cb + c)),
                pl.BlockSpec((seq, cb), lambda b, c: (rb0 + b, yr_col // cb + c)),
                pl.BlockSpec((None, CONV_WIDTH, cb), lambda b, c: (l, 0, c)),
                pl.BlockSpec((None, 1, cb), lambda b, c: (l, 0, c)),
                pl.BlockSpec((None, None, 4, cb, cb), lambda b, c: (l, c, 0, 0, 0)),
                pl.BlockSpec((None, 4, cb), lambda b, c: (l, 0, c)),
                pl.BlockSpec((None, 2, cb), lambda b, c: (l, 0, c)),
                pl.BlockSpec((None, 2, cb), (lambda b, c: (b, 0, c)) if h0_batched else (lambda b, c: (0, 0, c)))]
    return pl.pallas_call(
        functools.partial(_lru_kernel, chunk=_tile(seq, 512)),
        grid=(nseq, ncb),
        in_specs=in_specs,
        out_specs=[pl.BlockSpec((seq, cb), lambda b, c: (b, c)),
                   pl.BlockSpec((None, 2, cb), lambda b, c: (b, 0, c))],
        out_shape=[jax.ShapeDtypeStruct((nseq * seq, d_lru), f32),
                   jax.ShapeDtypeStruct((nseq, 2, d_lru), f32)],
        scratch_shapes=[pltpu.VMEM((seq + 16, cb), f32), pltpu.VMEM((2, seq, cb), f32),
                        pltpu.VMEM((2, seq, cb), f32)],
        compiler_params=_cparams(2),
        name="rglru",
    )(y, y, conv_w, conv_b3, wg, bg, lru_lambda, h0)


def _merge_kernel(atc_ref, atl_ref, rcc_ref, rcl_ref, ga_ref, gl_ref, x_ref, gt_ref, shf_ref, scf_ref,
                  g_ref, b_ref, wa_ref, wl_ref, wo_ref, x1_ref, h2_ref, ab_ref, rb_ref, *, alpha, n_ctx_tiles):
    @pl.when(pl.program_id(0) < n_ctx_tiles)
    def _():
        ab_ref[...] = atc_ref[...].astype(bf16)
        rb_ref[...] = rcc_ref[...].astype(bf16)

    @pl.when(pl.program_id(0) >= n_ctx_tiles)
    def _():
        ab_ref[...] = atl_ref[...].astype(bf16)
        rb_ref[...] = rcl_ref[...].astype(bf16)

    pa = jnp.dot(ab_ref[...], wa_ref[...], preferred_element_type=f32)
    pr = jnp.dot(rb_ref[...], wl_ref[...], preferred_element_type=f32)
    merged = jax.nn.sigmoid(ga_ref[...]) * pa + jax.nn.sigmoid(gl_ref[...]) * pr
    out = jnp.dot(merged.astype(bf16), wo_ref[...], preferred_element_type=f32)
    x1 = _layer_norm(alpha * x_ref[...] + gt_ref[0] * out, g_ref[0], b_ref[0])
    x1_ref[...] = x1
    h2_ref[...] = x1 * (1.0 + scf_ref[0]) + shf_ref[0]


def _merge(attn_c, attn_l, rec_c, rec_l, y, x, mod4, ln_g3, ln_b3, wa, wl, wo, l, n_ctx, dec_seq, ga_col, gl_col,
           alpha):
    n, d = x.shape
    tm = _tile(math.gcd(n_ctx, dec_seq), 512)
    n_ctx_tiles = n_ctx // tm
    tiles_per_seq = dec_seq // tm

    def seg(i):
        return jnp.where(i < n_ctx_tiles, 0, 1 + (i - n_ctx_tiles) // tiles_per_seq)

    base = l * MOD_ROWS * 6
    row = lambda i: (i, 0)
    ctx_row = lambda i: (jnp.minimum(i, n_ctx_tiles - 1), 0)
    lat_row = lambda i: (jnp.maximum(i - n_ctx_tiles, 0), 0)
    mod = lambda k: pl.BlockSpec((1, 1, d), lambda i: (base + seg(i) * 6 + k, 0, 0))
    wspec = pl.BlockSpec((None, d, d), lambda i: (l, 0, 0))
    return pl.pallas_call(
        functools.partial(_merge_kernel, alpha=alpha, n_ctx_tiles=n_ctx_tiles),
        grid=(n // tm,),
        in_specs=[pl.BlockSpec((tm, d), ctx_row), pl.BlockSpec((tm, d), lat_row),
                  pl.BlockSpec((tm, d), ctx_row), pl.BlockSpec((tm, d), lat_row),
                  pl.BlockSpec((tm, d), lambda i: (i, ga_col // d)),
                  pl.BlockSpec((tm, d), lambda i: (i, gl_col // d)),
                  pl.BlockSpec((tm, d), row), mod(2), mod(3), mod(4),
                  pl.BlockSpec((1, 1, d), lambda i: (2 * l, 0, 0)),
                  pl.BlockSpec((1, 1, d), lambda i: (2 * l, 0, 0)),
                  wspec, wspec, wspec],
        out_specs=[pl.BlockSpec((tm, d), row), pl.BlockSpec((tm, d), row)],
        out_shape=[jax.ShapeDtypeStruct((n, d), f32), jax.ShapeDtypeStruct((n, d), f32)],
        scratch_shapes=[pltpu.VMEM((tm, d), bf16), pltpu.VMEM((tm, d), bf16)],
        compiler_params=_cparams(1),
        name="merge_outproj_ln",
    )(attn_c, attn_l, rec_c, rec_l, y, y, x, mod4, mod4, mod4, ln_g3, ln_b3, wa, wl, wo)


def _route_kernel(h_ref, rw_ref, rb_ref, pos_ref, gate_ref, tab_ref, carry_ref):
    n_exp = rw_ref.shape[0]
    tm = h_ref.shape[0]
    gsz = n_exp // N_GROUPS

    @pl.when(pl.program_id(0) == 0)
    def _():
        carry_ref[...] = jnp.zeros_like(carry_ref)

    logits = lax.dot_general(rw_ref[...], h_ref[...], (((1,), (1,)), ((), ())),
                             precision=HIGHEST, preferred_element_type=f32)
    s = jax.nn.sigmoid(logits)
    sb = s + rb_ref[...]
    ninf = -jnp.inf
    gs_rows = []
    for g in range(N_GROUPS):
        blk = sb[g * gsz:(g + 1) * gsz]
        m1 = jnp.max(blk, axis=0, keepdims=True)
        cnt = jnp.sum((blk == m1).astype(f32), axis=0, keepdims=True)
        m2 = jnp.max(jnp.where(blk < m1, blk, ninf), axis=0, keepdims=True)
        gs_rows.append(m1 + jnp.where(cnt >= 2.0, m1, m2))
    gs = jnp.concatenate(gs_rows, axis=0)
    gi = lax.broadcasted_iota(i32, gs.shape, 0)
    grank = jnp.zeros(gs.shape, f32)
    for g2 in range(N_GROUPS):
        row = gs[g2:g2 + 1]
        grank += jnp.where((row > gs) | ((row == gs) & (g2 < gi)), 1.0, 0.0)
    gsel = grank < float(TOPK_GROUPS)
    sbm = jnp.concatenate([jnp.where(gsel[g:g + 1], sb[g * gsz:(g + 1) * gsz], ninf)
                           for g in range(N_GROUPS)], axis=0)
    ei = lax.broadcasted_iota(i32, sbm.shape, 0)
    erank = jnp.zeros(sbm.shape, f32)
    for e2 in range(n_exp):
        row = sbm[e2:e2 + 1]
        erank += jnp.where((row > sbm) | ((row == sbm) & (e2 < ei)), 1.0, 0.0)
    sel = erank < float(TOP_K)
    ssel = jnp.where(sel, s, 0.0)
    gate = ssel / jnp.sum(ssel, axis=0, keepdims=True) * ROUTE_SCALE
    selb = sel.astype(bf16)
    ti = lax.broadcasted_iota(i32, (tm, tm), 0)
    tj = lax.broadcasted_iota(i32, (tm, tm), 1)
    upper = (ti < tj).astype(bf16)
    pos_local = jnp.dot(selb, upper, preferred_element_type=f32)
    pos_ref[...] = jnp.where(sel, pos_local, -1.0)
    gate_ref[...] = gate
    cnt = jnp.sum(sel.astype(f32), axis=1, keepdims=True)
    cpad = jnp.floor((cnt + (SEG_ALIGN - 1)) * (1.0 / SEG_ALIGN)) * SEG_ALIGN
    cpad_l = jnp.broadcast_to(cpad, (n_exp, 128))
    li = lax.broadcasted_iota(i32, (n_exp, n_exp), 0)
    lj = lax.broadcasted_iota(i32, (n_exp, n_exp), 1)
    lower = (lj < li).astype(bf16)
    run_start = jnp.dot(lower, cpad_l.astype(bf16), preferred_element_type=f32)
    lane = lax.broadcasted_iota(i32, (n_exp, 128), 1)
    tab = jnp.where(lane == TAB_LEN, cpad_l, jnp.where(lane == TAB_BASE, carry_ref[...], run_start))
    tab_ref[...] = tab.astype(i32)
    carry_ref[...] = carry_ref[...] + cpad_l


def _route(h2, rwt, rb3, l):
    n, d = h2.shape
    n_exp = rwt.shape[1]
    tm = ROUTE_TILE
    assert n % tm == 0
    lst = pl.BlockSpec((n_exp, tm), lambda i: (0, i))
    return pl.pallas_call(
        _route_kernel,
        grid=(n // tm,),
        in_specs=[pl.BlockSpec((tm, d), lambda i: (i, 0)),
                  pl.BlockSpec((None, n_exp, d), lambda i: (l, 0, 0)),
                  pl.BlockSpec((None, n_exp, 1), lambda i: (l, 0, 0))],
        out_specs=[lst, lst, pl.BlockSpec((None, n_exp, 128), lambda i: (i, 0, 0))],
        out_shape=[jax.ShapeDtypeStruct((n_exp, n), f32), jax.ShapeDtypeStruct((n_exp, n), f32),
                   jax.ShapeDtypeStruct((n // tm, n_exp, 128), i32)],
        scratch_shapes=[pltpu.VMEM((n_exp, 128), f32)],
        compiler_params=_cparams(1),
        name="moe_route",
    )(h2, rwt, rb3)


def _pow2_pieces(count, max_rows, fn):
    size = max_rows
    while size >= SEG_ALIGN:
        off = count & (-2 * size)
        pl.when((count & size) != 0)(functools.partial(fn, off, size))
        size //= 2


def _wait_rows(count, max_rows, src_ref, dst_ref, sem):
    def wait(_, size):
        pltpu.make_async_copy(src_ref.at[pl.ds(0, size)], dst_ref.at[pl.ds(0, size)], sem).wait()

    _pow2_pieces(count, max_rows, wait)


def _pad_experts(x):
    return jnp.concatenate([x, jnp.zeros((128 - x.shape[0], x.shape[1]), x.dtype)], axis=0)


def _dispatch_kernel(pd_ref, tot_ref, ts_ref, tl_ref, pos_ref, tab_ref, h_ref, xs_ref, buf_ref, z_ref,
                     sem, *, n_exp, n_tiles, tile_rows, wait_max):
    t = pl.program_id(0)
    slot = t % 2
    tm = h_ref.shape[0]
    pieces = tile_rows // SEG_ALIGN

    @pl.when(t >= 2)
    def _():
        _wait_rows(tot_ref[t - 2], wait_max, buf_ref.at[slot], xs_ref, sem.at[slot])

    tab = _pad_experts(tab_ref[...].astype(f32))
    tab_t = tab.T
    start_row, len_row = tab_t[TAB_START:TAB_START + 1, :], tab_t[TAB_LEN:TAB_LEN + 1, :]
    rid = lax.broadcasted_iota(i32, (tile_rows, 128), 0).astype(f32)
    in_run = ((rid >= start_row) & (rid < start_row + len_row)).astype(bf16)
    pos_rows = jnp.dot(in_run, _pad_experts(pos_ref[...]).astype(bf16), preferred_element_type=f32)
    start16 = (tab[:, TAB_START:TAB_START + 1] * (1.0 / SEG_ALIGN)).astype(bf16)
    run_start = jnp.dot(in_run, jnp.broadcast_to(start16, (128, 128)), preferred_element_type=f32) * SEG_ALIGN
    ordinal = rid - run_start
    ordinal = jnp.concatenate([ordinal] * (tm // 128), axis=1)
    perm = jnp.where(pos_rows == ordinal, 1.0, 0.0).astype(bf16)
    buf_ref[slot] = jnp.dot(perm, h_ref[...].astype(bf16), preferred_element_type=f32).astype(bf16)

    def issue(p, _):
        src = pl.multiple_of(p * SEG_ALIGN, SEG_ALIGN)
        dst = pl.multiple_of(pd_ref[t * pieces + p], SEG_ALIGN)
        pltpu.make_async_copy(buf_ref.at[slot, pl.ds(src, SEG_ALIGN)], xs_ref.at[pl.ds(dst, SEG_ALIGN)],
                              sem.at[slot]).start()
        return 0

    lax.fori_loop(0, tot_ref[t] // SEG_ALIGN, issue, 0)

    @pl.when(t == n_tiles - 1)
    def _():
        z_ref[...] = jnp.zeros_like(z_ref)

        def fill(e, _):
            def copy(off, size):
                dst = pl.multiple_of(ts_ref[e] + off, SEG_ALIGN)
                pltpu.make_async_copy(z_ref.at[pl.ds(0, size)], xs_ref.at[pl.ds(dst, size)], sem.at[2]).start()

            _pow2_pieces(tl_ref[e], MOE_BLOCK // 2, copy)
            return 0

        lax.fori_loop(0, n_exp, fill, 0)

        def drain(e, _):
            _wait_rows(tl_ref[e], MOE_BLOCK // 2, z_ref, xs_ref, sem.at[2])
            return 0

        lax.fori_loop(0, n_exp, drain, 0)
        if n_tiles > 1:
            _wait_rows(tot_ref[t - 1], wait_max, buf_ref.at[1 - slot], xs_ref, sem.at[1 - slot])
        _wait_rows(tot_ref[t], wait_max, buf_ref.at[slot], xs_ref, sem.at[slot])


def _tile_rows(tm, n_exp):
    return -(-(tm * TOP_K + n_exp * (SEG_ALIGN - 1)) // 128) * 128


def _wait_max(tile_rows):
    return 1 << (tile_rows.bit_length() - 1)


def _dispatch(h2, pos, tab, piece_dst, tot, tail_start, tail_len, n_rows, n_exp):
    n, d = h2.shape
    tm = ROUTE_TILE
    n_tiles = n // tm
    tile_rows = _tile_rows(tm, n_exp)
    assert n_exp <= 128 and tm % 128 == 0
    return pl.pallas_call(
        functools.partial(_dispatch_kernel, n_exp=n_exp, n_tiles=n_tiles, tile_rows=tile_rows,
                          wait_max=_wait_max(tile_rows)),
        grid_spec=pltpu.PrefetchScalarGridSpec(
            num_scalar_prefetch=4,
            grid=(n_tiles,),
            in_specs=[pl.BlockSpec((n_exp, tm), lambda i, *_: (0, i)),
                      pl.BlockSpec((None, n_exp, 128), lambda i, *_: (i, 0, 0)),
                      pl.BlockSpec((tm, d), lambda i, *_: (i, 0))],
            out_specs=pl.BlockSpec(memory_space=pl.ANY),
            scratch_shapes=[pltpu.VMEM((2, tile_rows, d), bf16), pltpu.VMEM((MOE_BLOCK // 2, d), bf16),
                            pltpu.SemaphoreType.DMA((3,))]),
        out_shape=jax.ShapeDtypeStruct((n_rows, d), bf16),
        compiler_params=_cparams(1),
        name="moe_dispatch",
    )(piece_dst, tot, tail_start, tail_len, pos, tab, h2)


def _expert_kernel(be_ref, nu_ref, x_ref, wg_ref, wu_ref, wd_ref, o_ref):
    @pl.when(pl.program_id(0) < nu_ref[0])
    def _():
        x = x_ref[...]
        g = jnp.dot(x, wg_ref[...].astype(bf16), preferred_element_type=f32)
        u = jnp.dot(x, wu_ref[...].astype(bf16), preferred_element_type=f32)
        h = (g * jax.nn.sigmoid(g)) * u
        o_ref[...] = jnp.dot(h.astype(bf16), wd_ref[...].astype(bf16),
                             preferred_element_type=f32).astype(bf16)


def _experts(xs, blk_e, n_used, we_g, we_u, we_d, l):
    n_rows, d = xs.shape
    f = we_g.shape[-1]
    nblk = n_rows // MOE_BLOCK
    blk = lambda b, be, nu: (jnp.minimum(b, nu[0] - 1), 0)
    wsel = lambda b, be, nu: (l, be[jnp.minimum(b, nu[0] - 1)], 0, 0)
    return pl.pallas_call(
        _expert_kernel,
        grid_spec=pltpu.PrefetchScalarGridSpec(
            num_scalar_prefetch=2,
            grid=(nblk,),
            in_specs=[pl.BlockSpec((MOE_BLOCK, d), blk),
                      pl.BlockSpec((None, None, d, f), wsel),
                      pl.BlockSpec((None, None, d, f), wsel),
                      pl.BlockSpec((None, None, f, d), wsel)],
            out_specs=pl.BlockSpec((MOE_BLOCK, d), blk)),
        out_shape=jax.ShapeDtypeStruct((n_rows, d), bf16),
        compiler_params=_cparams(1),
        name="moe_experts",
    )(blk_e, n_used, xs, we_g, we_u, we_d)


def _combine_kernel(pd_ref, tot_ref, pos_ref, gate_ref, tab_ref, ys_ref, x1_ref, h2_ref, gt_ref, g_ref, b_ref,
                    sg_ref, su_ref, sd_ref, o_ref, buf_ref, sem, *, alpha, n_exp, n_tiles, tile_rows, wait_max):
    t = pl.program_id(0)
    slot = t % 2
    tm = x1_ref.shape[0]
    pieces = tile_rows // SEG_ALIGN

    def fetch(tile, dst_slot):
        def issue(p, _):
            src = pl.multiple_of(pd_ref[tile * pieces + p], SEG_ALIGN)
            dst = pl.multiple_of(p * SEG_ALIGN, SEG_ALIGN)
            pltpu.make_async_copy(ys_ref.at[pl.ds(src, SEG_ALIGN)], buf_ref.at[dst_slot, pl.ds(dst, SEG_ALIGN)],
                                  sem.at[dst_slot]).start()
            return 0

        lax.fori_loop(0, tot_ref[tile] // SEG_ALIGN, issue, 0)

    @pl.when(t == 0)
    def _():
        buf_ref[...] = jnp.zeros_like(buf_ref)
        fetch(0, 0)

    @pl.when(t + 1 < n_tiles)
    def _():
        fetch(t + 1, 1 - slot)

    hb = h2_ref[...].astype(bf16)
    g = jnp.dot(hb, sg_ref[...], preferred_element_type=f32)
    u = jnp.dot(hb, su_ref[...], preferred_element_type=f32)
    shared = jnp.dot(((g * jax.nn.sigmoid(g)) * u).astype(bf16), sd_ref[...], preferred_element_type=f32)
    tab = _pad_experts(tab_ref[...].astype(f32))
    start_col, len_col = tab[:, TAB_START:TAB_START + 1], tab[:, TAB_LEN:TAB_LEN + 1]
    rid = lax.broadcasted_iota(i32, (128, tile_rows), 1).astype(f32)
    in_run = (rid >= start_col) & (rid < start_col + len_col)
    ordinal = rid[0:1, :] - jnp.sum(jnp.where(in_run, start_col, 0.0), axis=0, keepdims=True)
    in_run = in_run.astype(bf16)
    pos_t = _pad_experts(pos_ref[...]).T.astype(bf16)
    gate_t = _pad_experts(gate_ref[...]).T.astype(bf16)
    pos_rows = jnp.dot(pos_t, in_run, preferred_element_type=f32)
    gate_rows = jnp.dot(gate_t, in_run, preferred_element_type=f32)
    wgt = jnp.where(pos_rows == ordinal, gate_rows, 0.0).astype(bf16)
    _wait_rows(tot_ref[t], wait_max, ys_ref, buf_ref.at[slot], sem.at[slot])
    routed = jnp.dot(wgt, buf_ref[slot], preferred_element_type=f32)
    o_ref[...] = _layer_norm(alpha * x1_ref[...] + gt_ref[0] * (shared + routed), g_ref[0], b_ref[0])


def _combine(ys, pos, gate, tab, piece_dst, tot, x1, h2, mod4, ln_g3, ln_b3, sg, su, sd, l, n_ctx, dec_seq, alpha,
             n_exp):
    n, d = x1.shape
    f = sg.shape[-1]
    tm = ROUTE_TILE
    assert n_ctx % tm == 0 and dec_seq % tm == 0
    n_tiles = n // tm
    tile_rows = _tile_rows(tm, n_exp)
    n_ctx_tiles = n_ctx // tm
    tiles_per_seq = dec_seq // tm

    def seg(i):
        return jnp.where(i < n_ctx_tiles, 0, 1 + (i - n_ctx_tiles) // tiles_per_seq)

    base = l * MOD_ROWS * 6
    row = lambda i, *_: (i, 0)
    lst = lambda i, *_: (0, i)
    return pl.pallas_call(
        functools.partial(_combine_kernel, alpha=alpha, n_exp=n_exp, n_tiles=n_tiles, tile_rows=tile_rows,
                          wait_max=_wait_max(tile_rows)),
        grid_spec=pltpu.PrefetchScalarGridSpec(
            num_scalar_prefetch=2,
            grid=(n_tiles,),
            in_specs=[pl.BlockSpec((n_exp, tm), lst), pl.BlockSpec((n_exp, tm), lst),
                      pl.BlockSpec((None, n_exp, 128), lambda i, *_: (i, 0, 0)),
                      pl.BlockSpec(memory_space=pl.ANY),
                      pl.BlockSpec((tm, d), row), pl.BlockSpec((tm, d), row),
                      pl.BlockSpec((1, 1, d), lambda i, *_: (base + seg(i) * 6 + 5, 0, 0)),
                      pl.BlockSpec((1, 1, d), lambda i, *_: (2 * l + 1, 0, 0)),
                      pl.BlockSpec((1, 1, d), lambda i, *_: (2 * l + 1, 0, 0)),
                      pl.BlockSpec((None, d, f), lambda i, *_: (l, 0, 0)),
                      pl.BlockSpec((None, d, f), lambda i, *_: (l, 0, 0)),
                      pl.BlockSpec((None, f, d), lambda i, *_: (l, 0, 0))],
            out_specs=pl.BlockSpec((tm, d), row),
            scratch_shapes=[pltpu.VMEM((2, tile_rows, d), bf16), pltpu.SemaphoreType.DMA((2,))]),
        out_shape=jax.ShapeDtypeStruct((n, d), f32),
        compiler_params=_cparams(1),
        name="moe_combine_ln",
    )(piece_dst, tot, pos, gate, tab, ys, x1, h2, mod4, ln_g3, ln_b3, sg, su, sd)


def _rope_tables(seq, width):
    rows = seq // GRID_W
    row = jnp.repeat(jnp.arange(rows, dtype=f32), GRID_W)
    col = jnp.tile(jnp.arange(GRID_W, dtype=f32), rows)
    half = HEAD_DIM // 2
    inv = ROPE_BASE ** (-jnp.arange(0, half, 2, dtype=f32) / half)
    lane = jnp.arange(width)
    d = lane % HEAD_DIM
    pos = jnp.where((d // half)[None, :] == 0, row[:, None], col[:, None])
    ang = pos * inv[d % (HEAD_DIM // 4)][None, :]
    sign = jnp.where((d % half) < HEAD_DIM // 4, -1.0, 1.0)[None, :]
    return jnp.cos(ang), jnp.sin(ang) * sign


def _block_diag_gates(gate_w_r, gate_w_i, cb):
    depth, _, nb, bw, _ = gate_w_r.shape
    per = cb // bw
    w = jnp.stack([gate_w_r[:, 0], gate_w_i[:, 0], gate_w_r[:, 1], gate_w_i[:, 1]], axis=1)
    w = w.reshape(depth, 4, nb // per, per, bw, bw)
    eye = jnp.eye(per, dtype=w.dtype)
    bd = jnp.einsum("lgcaio,ab->lcgaibo", w, eye)
    return bd.reshape(depth, nb // per, 4, cb, cb).astype(bf16)


def kernel(x_prompt, x_sample, c, cache_k, cache_v, state_lru, c_ctx, w_mod, b_mod, w_in, lam_param, subln_w,
           conv_w, conv_b, gate_w_r, gate_b_r, gate_w_i, gate_b_i, lru_lambda, w_br_attn, w_br_lru, w_out,
           ln_g, ln_b, router_w, router_b, exp_w_gate, exp_w_up, exp_w_down, sh_w_gate, sh_w_up, sh_w_down):
    batch, seq, d = x_prompt.shape
    dec_batch, dec_seq, _ = x_sample.shape
    depth = w_mod.shape[0]
    past = cache_k.shape[2]
    d_lru = conv_w.shape[-1]
    n_exp = router_w.shape[-1]
    qk = N_HEADS * 2 * HEAD_DIM
    attn_w = N_HEADS * V_DIM
    n_ctx, n_lat = batch * seq, dec_batch * dec_seq
    n = n_ctx + n_lat
    alpha = (2 * depth) ** 0.25
    assert 1 + dec_batch <= MOD_ROWS and n_ctx % dec_seq == 0

    cond = jnp.concatenate([c_ctx[None], c, jnp.zeros((MOD_ROWS - 1 - dec_batch, d), f32)], 0)
    w_in_b = w_in.astype(bf16)
    wa, wl, wo = w_br_attn.astype(bf16), w_br_lru.astype(bf16), w_out.astype(bf16)
    sg, su, sd = sh_w_gate.astype(bf16), sh_w_up.astype(bf16), sh_w_down.astype(bf16)
    rwt = jnp.swapaxes(router_w, 1, 2)
    rb3 = router_b.reshape(depth, n_exp, 1)
    wg = _block_diag_gates(gate_w_r, gate_w_i, 256)
    bg = jnp.stack([gate_b_r[:, 0], gate_b_i[:, 0], gate_b_r[:, 1], gate_b_i[:, 1]], axis=1)
    conv_b3 = conv_b.reshape(depth, 1, d_lru)
    subln_w3 = subln_w.reshape(depth, 1, V_DIM)
    ln_g3 = ln_g.reshape(depth * 2, 1, d)
    ln_b3 = ln_b.reshape(depth * 2, 1, d)
    ck = cache_k.reshape(dec_batch, depth, past, N_HEADS * 2 * HEAD_DIM)
    cv = cache_v.reshape(dec_batch, depth, past, attn_w)
    cos_t, sin_t = _rope_tables(dec_seq, V_DIM)
    h0_ctx = jnp.zeros((1, 2, d_lru), f32)

    mod4 = _mod_table(cond, w_mod, b_mod).reshape(depth * MOD_ROWS * 6, 1, d)
    x = jnp.concatenate([x_prompt.reshape(n_ctx, d), x_sample.reshape(n_lat, d)], 0)
    n_rows = n * TOP_K + (n // ROUTE_TILE) * n_exp * (SEG_ALIGN - 1) + n_exp * (MOE_BLOCK - 1)
    n_rows = -(-n_rows // MOE_BLOCK) * MOE_BLOCK
    xr_col, yr_col = 2 * qk + attn_w, 2 * qk + attn_w + d_lru
    ga_col, gl_col = yr_col + d_lru, yr_col + d_lru + d

    new_k, new_v, new_s = [], [], []
    for l in range(depth):
        lam_init = 0.8 - 0.6 * math.exp(-0.3 * l)
        y = _inproj(x, mod4, w_in_b, cos_t, sin_t, l, n_ctx, dec_seq, 2 * qk)
        new_k.append(y[:n_ctx, qk:2 * qk].reshape(batch, seq, N_HEADS, 2 * HEAD_DIM))
        new_v.append(y[:n_ctx, 2 * qk:2 * qk + attn_w].reshape(batch, seq, N_HEADS, V_DIM))
        attn_c = _attention(y, lam_param, subln_w3, l, lam_init, 0, batch, seq)
        attn_l = _attention(y, lam_param, subln_w3, l, lam_init, n_ctx, dec_batch, dec_seq, ck, cv)
        rec_c, hfin = _lru(y, conv_w, conv_b3, wg, bg, lru_lambda, h0_ctx, l, 0, batch, seq, xr_col, yr_col)
        rec_l, _ = _lru(y, conv_w, conv_b3, wg, bg, lru_lambda, state_lru[:, l], l, n_ctx, dec_batch, dec_seq,
                        xr_col, yr_col)
        new_s.append(hfin)
        x1, h2 = _merge(attn_c, attn_l, rec_c, rec_l, y, x, mod4, ln_g3, ln_b3, wa, wl, wo, l, n_ctx, dec_seq, ga_col, gl_col, alpha)
        pos, gate, tab = _route(h2, rwt, rb3, l)
        cpad, grel, start = tab[:, :, TAB_LEN], tab[:, :, TAB_BASE], tab[:, :, TAB_START]
        used = grel[-1] + cpad[-1]
        padded = (used + MOE_BLOCK - 1) // MOE_BLOCK * MOE_BLOCK
        pad_end = jnp.cumsum(padded)
        pad_start = pad_end - padded
        tot = jnp.sum(cpad, axis=1).astype(i32)
        blk_row0 = jnp.arange(n_rows // MOE_BLOCK, dtype=i32) * MOE_BLOCK
        blk_e = jnp.minimum(jnp.sum(pad_end[None, :] <= blk_row0[:, None], axis=1), n_exp - 1).astype(i32)
        n_used = (pad_end[-1:] // MOE_BLOCK).astype(i32)
        piece0 = jnp.arange(_tile_rows(ROUTE_TILE, n_exp) // SEG_ALIGN, dtype=i32)[None, :, None] * SEG_ALIGN
        in_run = (piece0 >= start[:, None, :]) & (piece0 < (start + cpad)[:, None, :])
        shift = (pad_start[None, :] + grel - start)[:, None, :]
        piece_dst = (piece0[:, :, 0] + jnp.sum(jnp.where(in_run, shift, 0), axis=-1)).reshape(-1).astype(i32)
        xs = _dispatch(h2, pos, tab, piece_dst, tot, (pad_start + used).astype(i32),
                       (padded - used).astype(i32), n_rows, n_exp)
        ys = _experts(xs, blk_e, n_used, exp_w_gate, exp_w_up, exp_w_down, l)
        x = _combine(ys, pos, gate, tab, piece_dst, tot, x1, h2, mod4, ln_g3, ln_b3, sg, su, sd, l, n_ctx, dec_seq,
                     alpha, n_exp)
    return (x[:n_ctx].reshape(batch, seq, d), x[n_ctx:].reshape(dec_batch, dec_seq, d),
            jnp.stack(new_k, 1), jnp.stack(new_v, 1), jnp.stack(new_s, 1))
```

```python
import functools
import math

import jax
import jax.numpy as jnp
from jax import lax
from jax.experimental import pallas as pl
from jax.experimental.pallas import tpu as pltpu

f32, bf16, i32 = jnp.float32, jnp.bfloat16, jnp.int32

GRID_W = 64
N_HEADS = 8
HEAD_DIM = 64
V_DIM = 2 * HEAD_DIM
ROPE_BASE = 10000.0
LRU_BLOCKS = 16
CONV_WIDTH = 4
LRU_C = 8.0
TOP_K = 8
N_GROUPS = 8
TOPK_GROUPS = 4
ROUTE_SCALE = 2.5
LN_EPS = 1e-5
MOD_ROWS = 8
MOE_BLOCK = 1024
ROUTE_TILE = 256
SEG_ALIGN = 16
ROPE_CHUNK = 256
TAB_LEN, TAB_BASE, TAB_START = 0, 1, 2
SCAN_UNROLL = 4
VMEM_LIMIT = 56 * 1024 * 1024
HIGHEST = lax.Precision.HIGHEST


def _cparams(n_axes):
    return pltpu.CompilerParams(dimension_semantics=("arbitrary",) * n_axes,
                                vmem_limit_bytes=VMEM_LIMIT)


def _tile(n, pref):
    t = min(n, pref)
    while n % t:
        t -= 8
    assert t > 0 and n % t == 0
    return t


def _sigmoid(x):
    return 0.5 * jnp.tanh(0.5 * x) + 0.5


def _layer_norm(z, g, b):
    mu = jnp.mean(z, -1, keepdims=True)
    zc = z - mu
    var = jnp.mean(zc * zc, -1, keepdims=True)
    return zc * lax.rsqrt(var + LN_EPS) * g + b


def _mod_kernel(c_ref, w_ref, b_ref, o_ref):
    c = c_ref[...]
    a = c * jax.nn.sigmoid(c)
    o_ref[...] = jnp.dot(a, w_ref[...], precision=HIGHEST, preferred_element_type=f32) + b_ref[...]


def _mod_table(cond, w_mod, b_mod):
    depth, d, d6 = w_mod.shape
    tn = _tile(d6, 1024)
    return pl.pallas_call(
        _mod_kernel,
        grid=(depth, d6 // tn),
        in_specs=[pl.BlockSpec((MOD_ROWS, d), lambda l, j: (0, 0)),
                  pl.BlockSpec((None, d, tn), lambda l, j: (l, 0, j)),
                  pl.BlockSpec((None, 1, tn), lambda l, j: (l, 0, j))],
        out_specs=pl.BlockSpec((None, MOD_ROWS, tn), lambda l, j: (l, 0, j)),
        out_shape=jax.ShapeDtypeStruct((depth, MOD_ROWS, d6), f32),
        compiler_params=_cparams(2),
        name="mod_table",
    )(cond, w_mod, b_mod.reshape(depth, 1, d6))


def _inproj_kernel(x_ref, sh_ref, sc_ref, w_ref, cos_ref, sin_ref, o_ref, hb_ref, *,
                   n_ctx_tiles, n_rope_cols, tn):
    i = pl.program_id(0)
    j = pl.program_id(1)

    @pl.when(j == 0)
    def _():
        hb_ref[...] = (x_ref[...] * (1.0 + sc_ref[0]) + sh_ref[0]).astype(bf16)

    rope = jnp.logical_and(i >= n_ctx_tiles, j < n_rope_cols)

    @pl.when(rope)
    def _():
        cos, sin = cos_ref[...], sin_ref[...]
        lane = lax.broadcasted_iota(i32, cos.shape, 1)
        first = (lane % (HEAD_DIM // 2)) < (HEAD_DIM // 4)
        for c in range(tn // ROPE_CHUNK):
            acc = jnp.dot(hb_ref[...], w_ref[:, c * ROPE_CHUNK:(c + 1) * ROPE_CHUNK], preferred_element_type=f32)
            for h in range(ROPE_CHUNK // V_DIM):
                a = acc[:, h * V_DIM:(h + 1) * V_DIM]
                partner = jnp.where(first, pltpu.roll(a, V_DIM - HEAD_DIM // 4, 1), pltpu.roll(a, HEAD_DIM // 4, 1))
                col = c * ROPE_CHUNK + h * V_DIM
                o_ref[:, col:col + V_DIM] = a * cos + partner * sin

    @pl.when(jnp.logical_not(rope))
    def _():
        o_ref[...] = jnp.dot(hb_ref[...], w_ref[...], preferred_element_type=f32)


def _inproj(x, mod4, w_in_b, cos_t, sin_t, l, n_ctx, dec_seq, qk_cols):
    n, d = x.shape
    d_in = w_in_b.shape[-1]
    tm = _tile(math.gcd(n_ctx, dec_seq), 1024)
    tn = _tile(math.gcd(d_in, qk_cols), 1024)
    assert tn % V_DIM == 0
    n_ctx_tiles = n_ctx // tm
    tiles_per_seq = dec_seq // tm

    def seg(i):
        return jnp.where(i < n_ctx_tiles, 0, 1 + (i - n_ctx_tiles) // tiles_per_seq)

    def pos(i):
        return jnp.where(i < n_ctx_tiles, 0, (i - n_ctx_tiles) % tiles_per_seq)

    base = l * MOD_ROWS * 6
    return pl.pallas_call(
        functools.partial(_inproj_kernel, n_ctx_tiles=n_ctx_tiles, n_rope_cols=qk_cols // tn, tn=tn),
        grid=(n // tm, d_in // tn),
        in_specs=[pl.BlockSpec((tm, d), lambda i, j: (i, 0)),
                  pl.BlockSpec((1, 1, d), lambda i, j: (base + seg(i) * 6 + 0, 0, 0)),
                  pl.BlockSpec((1, 1, d), lambda i, j: (base + seg(i) * 6 + 1, 0, 0)),
                  pl.BlockSpec((None, d, tn), lambda i, j: (l, 0, j)),
                  pl.BlockSpec((tm, V_DIM), lambda i, j: (pos(i), 0)),
                  pl.BlockSpec((tm, V_DIM), lambda i, j: (pos(i), 0))],
        out_specs=pl.BlockSpec((tm, tn), lambda i, j: (i, j)),
        out_shape=jax.ShapeDtypeStruct((n, d_in), f32),
        scratch_shapes=[pltpu.VMEM((tm, d), bf16)],
        compiler_params=_cparams(2),
        name="in_proj",
    )(x, mod4, mod4, w_in_b, cos_t, sin_t)


def _key_chunk(lk):
    for ck in (512, 384, 256, 128):
        if lk % ck == 0:
            return ck
    return lk


def _attn_kernel(*refs, has_cache, lam_init, past, heads):
    if has_cache:
        q_ref, k_ref, v_ref, ck_ref, cv_ref, lp_ref, sw_ref, o_ref, kt_ref, vb_ref, s_ref = refs
    else:
        q_ref, k_ref, v_ref, lp_ref, sw_ref, o_ref, kt_ref, vb_ref, s_ref = refs

    @pl.when(pl.program_id(2) == 0)
    def _():
        for h in range(heads):
            cols = slice(h * V_DIM, (h + 1) * V_DIM)
            if has_cache:
                kt_ref[h, :, :past] = ck_ref[:, cols].T.astype(bf16)
                vb_ref[h, :past, :] = cv_ref[:, cols].astype(bf16)
            kt_ref[h, :, past:] = k_ref[:, cols].T.astype(bf16)
            vb_ref[h, past:, :] = v_ref[:, cols].astype(bf16)

    lp = lp_ref[...]
    lam = (jnp.exp(jnp.sum(lp[0:1] * lp[1:2], keepdims=True))
           - jnp.exp(jnp.sum(lp[2:3] * lp[3:4], keepdims=True)) + lam_init)
    for h in range(heads):
        cols = slice(h * V_DIM, (h + 1) * V_DIM)
        o = _attn_head(q_ref[:, cols], kt_ref.at[h], vb_ref.at[h], s_ref, lam)
        o = o * lax.rsqrt(jnp.mean(o * o, -1, keepdims=True) + LN_EPS) * sw_ref[...]
        o_ref[:, cols] = o * (1.0 - lam_init)


def _attn_head(q, kt_ref, vb_ref, s_ref, lam):
    tq = q.shape[0]
    q = q * (HEAD_DIM ** -0.5 * math.log2(math.e))
    lane = lax.broadcasted_iota(i32, q.shape, 1)
    q2 = jnp.concatenate([jnp.where(lane < HEAD_DIM, q, 0.0).astype(bf16),
                          jnp.where(lane >= HEAD_DIM, q, 0.0).astype(bf16)], axis=0)
    lk = kt_ref.shape[1]
    ck = _key_chunk(lk)
    lanes = 128
    mx = None
    for c in range(lk // ck):
        s_c = jnp.dot(q2, kt_ref[:, c * ck:(c + 1) * ck], preferred_element_type=f32)
        s_ref[:, c * ck:(c + 1) * ck] = s_c
        for j in range(ck // lanes):
            t = s_c[:, j * lanes:(j + 1) * lanes]
            mx = t if mx is None else jnp.maximum(mx, t)
    mb = jnp.broadcast_to(jnp.max(mx, axis=-1, keepdims=True), (2 * tq, lanes))
    ls = jnp.zeros((2 * tq, lanes), f32)
    for j in range(lk // lanes):
        e_t = jnp.exp2(s_ref[:, j * lanes:(j + 1) * lanes] - mb)
        ls = ls + e_t
        s_ref[:, j * lanes:(j + 1) * lanes] = e_t
    inv = 1.0 / jnp.sum(ls, axis=-1, keepdims=True)
    c0 = jnp.broadcast_to(inv[:tq], (tq, lanes))
    c1 = jnp.broadcast_to(lam * inv[tq:], (tq, lanes))
    o = jnp.zeros((tq, V_DIM), f32)
    for c in range(lk // ck):
        a_c = jnp.concatenate(
            [(s_ref[:tq, j * lanes:(j + 1) * lanes] * c0 - s_ref[tq:, j * lanes:(j + 1) * lanes] * c1).astype(bf16)
             for j in range(c * ck // lanes, (c + 1) * ck // lanes)], axis=1)
        o = o + jnp.dot(a_c, vb_ref[c * ck:(c + 1) * ck, :], preferred_element_type=f32)
    return o


def _attention(y, lam_param, subln_w3, l, lam_init, row0, nseq, seq, cache_k=None, cache_v=None):
    has_cache = cache_k is not None
    past = cache_k.shape[2] if has_cache else 0
    lk = past + seq
    tq = _tile(seq, 256)
    nq = seq // tq
    heads = N_HEADS if lk <= 512 else 1
    hg = N_HEADS // heads
    w = heads * V_DIM
    assert row0 % seq == 0
    qb0, kb0 = row0 // tq, row0 // seq
    in_specs = [pl.BlockSpec((tq, w), lambda b, h, i: (qb0 + b * nq + i, h)),
                pl.BlockSpec((seq, w), lambda b, h, i: (kb0 + b, hg + h)),
                pl.BlockSpec((seq, w), lambda b, h, i: (kb0 + b, 2 * hg + h))]
    args = [y, y, y]
    if has_cache:
        in_specs += [pl.BlockSpec((None, None, past, w), lambda b, h, i: (b, l, 0, h)),
                     pl.BlockSpec((None, None, past, w), lambda b, h, i: (b, l, 0, h))]
        args += [cache_k, cache_v]
    in_specs += [pl.BlockSpec((None, 4, HEAD_DIM), lambda b, h, i: (l, 0, 0)),
                 pl.BlockSpec((None, 1, V_DIM), lambda b, h, i: (l, 0, 0))]
    args += [lam_param, subln_w3]
    return pl.pallas_call(
        functools.partial(_attn_kernel, has_cache=has_cache, lam_init=lam_init, past=past, heads=heads),
        grid=(nseq, hg, nq),
        in_specs=in_specs,
        out_specs=pl.BlockSpec((tq, w), lambda b, h, i: (b * nq + i, h)),
        out_shape=jax.ShapeDtypeStruct((nseq * seq, N_HEADS * V_DIM), f32),
        scratch_shapes=[pltpu.VMEM((heads, V_DIM, lk), bf16), pltpu.VMEM((heads, lk, V_DIM), bf16),
                        pltpu.VMEM((2 * tq, lk), f32)],
        compiler_params=_cparams(3),
        name="diff_attn_lat" if has_cache else "diff_attn_ctx",
    )(*args)


def _lru_kernel(xr_ref, yr_ref, cw_ref, cb_ref, wg_ref, bg_ref, lam_ref, h0_ref,
                rec_ref, hfin_ref, xp_ref, a_ref, b_ref, *, chunk):
    seq, c = xr_ref.shape
    xp_ref[0:8, :] = jnp.zeros((8, c), f32)
    xp_ref[8:8 + seq, :] = xr_ref[...]
    xp_ref[8 + seq:16 + seq, :] = jnp.zeros((8, c), f32)
    lam = lam_ref[...]
    nlam = -lam
    softplus = jnp.maximum(nlam, 0.0) + jnp.log(1.0 + jnp.exp(-jnp.abs(nlam)))
    for ci in range(seq // chunk):
        t0 = ci * chunk
        xc = cb_ref[...] + sum(cw_ref[j:j + 1, :] * xp_ref[8 + t0 + j - 1:8 + t0 + j - 1 + chunk, :]
                               for j in range(CONV_WIDTH))
        xcb = xc.astype(bf16)
        for d in range(2):
            r = _sigmoid(jnp.dot(xcb, wg_ref[2 * d], preferred_element_type=f32) + bg_ref[2 * d:2 * d + 1, :])
            g = _sigmoid(jnp.dot(xcb, wg_ref[2 * d + 1], preferred_element_type=f32)
                         + bg_ref[2 * d + 1:2 * d + 2, :])
            log_a = -LRU_C * r * softplus[d:d + 1, :]
            a = jnp.exp(log_a)
            a_ref[d, t0:t0 + chunk, :] = a
            u = -jnp.tanh(log_a) * (a * a + 1.0)
            b_ref[d, t0:t0 + chunk, :] = jnp.where(u > 0.0, u * lax.rsqrt(u), 0.0) * (g * xc)

    rows = lax.broadcasted_iota(i32, (8, c), 0)
    nch = seq // 8

    def scan_body(i, carry):
        cf, cb = carry
        tf = pl.multiple_of(8 * i, 8)
        af = a_ref[0, pl.ds(tf, 8), :]
        bf = b_ref[0, pl.ds(tf, 8), :]
        for d in (1, 2, 4):
            keep = rows >= d
            bf = af * jnp.where(keep, pltpu.roll(bf, d, 0), 0.0) + bf
            af = af * jnp.where(keep, pltpu.roll(af, d, 0), 1.0)
        hf = af * cf + bf
        rec_ref[pl.ds(tf, 8), :] = hf
        tb = pl.multiple_of(8 * (nch - 1 - i), 8)
        ab = a_ref[1, pl.ds(tb, 8), :]
        bb = b_ref[1, pl.ds(tb, 8), :]
        for d in (1, 2, 4):
            keep = rows < 8 - d
            bb = ab * jnp.where(keep, pltpu.roll(bb, 8 - d, 0), 0.0) + bb
            ab = ab * jnp.where(keep, pltpu.roll(ab, 8 - d, 0), 1.0)
        hb = ab * cb + bb
        xp_ref[pl.ds(tb, 8), :] = hb
        return hf[7:8, :], hb[0:1, :]

    h0 = h0_ref[...]
    cf, cb = lax.fori_loop(0, nch, scan_body, (h0[0:1, :], h0[1:2, :]), unroll=SCAN_UNROLL)
    hfin_ref[0:1, :] = cf
    hfin_ref[1:2, :] = cb
    for ci in range(seq // chunk):
        t0 = ci * chunk
        rec_ref[t0:t0 + chunk, :] = ((rec_ref[t0:t0 + chunk, :] + xp_ref[t0:t0 + chunk, :])
                                     * jax.nn.gelu(yr_ref[t0:t0 + chunk, :]))


def _lru(y, conv_w, conv_b3, wg, bg, lru_lambda, h0, l, row0, nseq, seq, xr_col, yr_col):
    d_lru = conv_w.shape[-1]
    cb = 256
    ncb = d_lru // cb
    assert row0 % seq == 0
    rb0 = row0 // seq
    h0_batched = h0.shape[0] > 1
    in_specs = [pl.BlockSpec((seq, cb), lambda b, c: (rb0 + b, xr_col // cb + c)),
                pl.BlockSpec((seq, cb), lambda b, c: (rb0 + b, yr_col // cb + c)),
                pl.BlockSpec((None, CONV_WIDTH, cb), lambda b, c: (l, 0, c)),
                pl.BlockSpec((None, 1, cb), lambda b, c: (l, 0, c)),
                pl.BlockSpec((None, None, 4, cb, cb), lambda b, c: (l, c, 0, 0, 0)),
                pl.BlockSpec((None, 4, cb), lambda b, c: (l, 0, c)),
                pl.BlockSpec((None, 2, cb), lambda b, c: (l, 0, c)),
                pl.BlockSpec((None, 2, cb), (lambda b, c: (b, 0, c)) if h0_batched else (lambda b, c: (0, 0, c)))]
    return pl.pallas_call(
        functools.partial(_lru_kernel, chunk=_tile(seq, 512)),
        grid=(nseq, ncb),
        in_specs=in_specs,
        out_specs=[pl.BlockSpec((seq, cb), lambda b, c: (b, c)),
                   pl.BlockSpec((None, 2, cb), lambda b, c: (b, 0, c))],
        out_shape=[jax.ShapeDtypeStruct((nseq * seq, d_lru), f32),
                   jax.ShapeDtypeStruct((nseq, 2, d_lru), f32)],
        scratch_shapes=[pltpu.VMEM((seq + 16, cb), f32), pltpu.VMEM((2, seq, cb), f32),
                        pltpu.VMEM((2, seq, cb), f32)],
        compiler_params=_cparams(2),
        name="rglru",
    )(y, y, conv_w, conv_b3, wg, bg, lru_lambda, h0)


def _merge_kernel(atc_ref, atl_ref, rcc_ref, rcl_ref, ga_ref, gl_ref, x_ref, gt_ref, shf_ref, scf_ref,
                  g_ref, b_ref, wa_ref, wl_ref, wo_ref, x1_ref, h2_ref, ab_ref, rb_ref, *, alpha, n_ctx_tiles):
    @pl.when(pl.program_id(0) < n_ctx_tiles)
    def _():
        ab_ref[...] = atc_ref[...].astype(bf16)
        rb_ref[...] = rcc_ref[...].astype(bf16)

    @pl.when(pl.program_id(0) >= n_ctx_tiles)
    def _():
        ab_ref[...] = atl_ref[...].astype(bf16)
        rb_ref[...] = rcl_ref[...].astype(bf16)

    pa = jnp.dot(ab_ref[...], wa_ref[...], preferred_element_type=f32)
    pr = jnp.dot(rb_ref[...], wl_ref[...], preferred_element_type=f32)
    merged = jax.nn.sigmoid(ga_ref[...]) * pa + jax.nn.sigmoid(gl_ref[...]) * pr
    out = jnp.dot(merged.astype(bf16), wo_ref[...], preferred_element_type=f32)
    x1 = _layer_norm(alpha * x_ref[...] + gt_ref[0] * out, g_ref[0], b_ref[0])
    x1_ref[...] = x1
    h2_ref[...] = x1 * (1.0 + scf_ref[0]) + shf_ref[0]


def _merge(attn_c, attn_l, rec_c, rec_l, y, x, mod4, ln_g3, ln_b3, wa, wl, wo, l, n_ctx, dec_seq, ga_col, gl_col,
           alpha):
    n, d = x.shape
    tm = _tile(math.gcd(n_ctx, dec_seq), 512)
    n_ctx_tiles = n_ctx // tm
    tiles_per_seq = dec_seq // tm

    def seg(i):
        return jnp.where(i < n_ctx_tiles, 0, 1 + (i - n_ctx_tiles) // tiles_per_seq)

    base = l * MOD_ROWS * 6
    row = lambda i: (i, 0)
    ctx_row = lambda i: (jnp.minimum(i, n_ctx_tiles - 1), 0)
    lat_row = lambda i: (jnp.maximum(i - n_ctx_tiles, 0), 0)
    mod = lambda k: pl.BlockSpec((1, 1, d), lambda i: (base + seg(i) * 6 + k, 0, 0))
    wspec = pl.BlockSpec((None, d, d), lambda i: (l, 0, 0))
    return pl.pallas_call(
        functools.partial(_merge_kernel, alpha=alpha, n_ctx_tiles=n_ctx_tiles),
        grid=(n // tm,),
        in_specs=[pl.BlockSpec((tm, d), ctx_row), pl.BlockSpec((tm, d), lat_row),
                  pl.BlockSpec((tm, d), ctx_row), pl.BlockSpec((tm, d), lat_row),
                  pl.BlockSpec((tm, d), lambda i: (i, ga_col // d)),
                  pl.BlockSpec((tm, d), lambda i: (i, gl_col // d)),
                  pl.BlockSpec((tm, d), row), mod(2), mod(3), mod(4),
                  pl.BlockSpec((1, 1, d), lambda i: (2 * l, 0, 0)),
                  pl.BlockSpec((1, 1, d), lambda i: (2 * l, 0, 0)),
                  wspec, wspec, wspec],
        out_specs=[pl.BlockSpec((tm, d), row), pl.BlockSpec((tm, d), row)],
        out_shape=[jax.ShapeDtypeStruct((n, d), f32), jax.ShapeDtypeStruct((n, d), f32)],
        scratch_shapes=[pltpu.VMEM((tm, d), bf16), pltpu.VMEM((tm, d), bf16)],
        compiler_params=_cparams(1),
        name="merge_outproj_ln",
    )(attn_c, attn_l, rec_c, rec_l, y, y, x, mod4, mod4, mod4, ln_g3, ln_b3, wa, wl, wo)


def _route_kernel(h_ref, rw_ref, rb_ref, pos_ref, gate_ref, tab_ref, carry_ref):
    n_exp = rw_ref.shape[0]
    tm = h_ref.shape[0]
    gsz = n_exp // N_GROUPS

    @pl.when(pl.program_id(0) == 0)
    def _():
        carry_ref[...] = jnp.zeros_like(carry_ref)

    logits = lax.dot_general(rw_ref[...], h_ref[...], (((1,), (1,)), ((), ())),
                             precision=HIGHEST, preferred_element_type=f32)
    s = jax.nn.sigmoid(logits)
    sb = s + rb_ref[...]
    ninf = -jnp.inf
    gs_rows = []
    for g in range(N_GROUPS):
        blk = sb[g * gsz:(g + 1) * gsz]
        m1 = jnp.max(blk, axis=0, keepdims=True)
        cnt = jnp.sum((blk == m1).astype(f32), axis=0, keepdims=True)
        m2 = jnp.max(jnp.where(blk < m1, blk, ninf), axis=0, keepdims=True)
        gs_rows.append(m1 + jnp.where(cnt >= 2.0, m1, m2))
    gs = jnp.concatenate(gs_rows, axis=0)
    gi = lax.broadcasted_iota(i32, gs.shape, 0)
    grank = jnp.zeros(gs.shape, f32)
    for g2 in range(N_GROUPS):
        row = gs[g2:g2 + 1]
        grank += jnp.where((row > gs) | ((row == gs) & (g2 < gi)), 1.0, 0.0)
    gsel = grank < float(TOPK_GROUPS)
    sbm = jnp.concatenate([jnp.where(gsel[g:g + 1], sb[g * gsz:(g + 1) * gsz], ninf)
                           for g in range(N_GROUPS)], axis=0)
    ei = lax.broadcasted_iota(i32, sbm.shape, 0)
    erank = jnp.zeros(sbm.shape, f32)
    for e2 in range(n_exp):
        row = sbm[e2:e2 + 1]
        erank += jnp.where((row > sbm) | ((row == sbm) & (e2 < ei)), 1.0, 0.0)
    sel = erank < float(TOP_K)
    ssel = jnp.where(sel, s, 0.0)
    gate = ssel / jnp.sum(ssel, axis=0, keepdims=True) * ROUTE_SCALE
    selb = sel.astype(bf16)
    ti = lax.broadcasted_iota(i32, (tm, tm), 0)
    tj = lax.broadcasted_iota(i32, (tm, tm), 1)
    upper = (ti < tj).astype(bf16)
    pos_local = jnp.dot(selb, upper, preferred_element_type=f32)
    pos_ref[...] = jnp.where(sel, pos_local, -1.0)
    gate_ref[...] = gate
    cnt = jnp.sum(sel.astype(f32), axis=1, keepdims=True)
    cpad = jnp.floor((cnt + (SEG_ALIGN - 1)) * (1.0 / SEG_ALIGN)) * SEG_ALIGN
    cpad_l = jnp.broadcast_to(cpad, (n_exp, 128))
    li = lax.broadcasted_iota(i32, (n_exp, n_exp), 0)
    lj = lax.broadcasted_iota(i32, (n_exp, n_exp), 1)
    lower = (lj < li).astype(bf16)
    run_start = jnp.dot(lower, cpad_l.astype(bf16), preferred_element_type=f32)
    lane = lax.broadcasted_iota(i32, (n_exp, 128), 1)
    tab = jnp.where(lane == TAB_LEN, cpad_l, jnp.where(lane == TAB_BASE, carry_ref[...], run_start))
    tab_ref[...] = tab.astype(i32)
    carry_ref[...] = carry_ref[...] + cpad_l


def _route(h2, rwt, rb3, l):
    n, d = h2.shape
    n_exp = rwt.shape[1]
    tm = ROUTE_TILE
    assert n % tm == 0
    lst = pl.BlockSpec((n_exp, tm), lambda i: (0, i))
    return pl.pallas_call(
        _route_kernel,
        grid=(n // tm,),
        in_specs=[pl.BlockSpec((tm, d), lambda i: (i, 0)),
                  pl.BlockSpec((None, n_exp, d), lambda i: (l, 0, 0)),
                  pl.BlockSpec((None, n_exp, 1), lambda i: (l, 0, 0))],
        out_specs=[lst, lst, pl.BlockSpec((None, n_exp, 128), lambda i: (i, 0, 0))],
        out_shape=[jax.ShapeDtypeStruct((n_exp, n), f32), jax.ShapeDtypeStruct((n_exp, n), f32),
                   jax.ShapeDtypeStruct((n // tm, n_exp, 128), i32)],
        scratch_shapes=[pltpu.VMEM((n_exp, 128), f32)],
        compiler_params=_cparams(1),
        name="moe_route",
    )(h2, rwt, rb3)


def _pow2_pieces(count, max_rows, fn):
    size = max_rows
    while size >= SEG_ALIGN:
        off = count & (-2 * size)
        pl.when((count & size) != 0)(functools.partial(fn, off, size))
        size //= 2


def _wait_rows(count, max_rows, src_ref, dst_ref, sem):
    def wait(_, size):
        pltpu.make_async_copy(src_ref.at[pl.ds(0, size)], dst_ref.at[pl.ds(0, size)], sem).wait()

    _pow2_pieces(count, max_rows, wait)


def _pad_experts(x):
    return jnp.concatenate([x, jnp.zeros((128 - x.shape[0], x.shape[1]), x.dtype)], axis=0)


def _dispatch_kernel(pd_ref, tot_ref, ts_ref, tl_ref, pos_ref, tab_ref, h_ref, xs_ref, buf_ref, z_ref,
                     sem, *, n_exp, n_tiles, tile_rows, wait_max):
    t = pl.program_id(0)
    slot = t % 2
    tm = h_ref.shape[0]
    pieces = tile_rows // SEG_ALIGN

    @pl.when(t >= 2)
    def _():
        _wait_rows(tot_ref[t - 2], wait_max, buf_ref.at[slot], xs_ref, sem.at[slot])

    tab = _pad_experts(tab_ref[...].astype(f32))
    tab_t = tab.T
    start_row, len_row = tab_t[TAB_START:TAB_START + 1, :], tab_t[TAB_LEN:TAB_LEN + 1, :]
    rid = lax.broadcasted_iota(i32, (tile_rows, 128), 0).astype(f32)
    in_run = ((rid >= start_row) & (rid < start_row + len_row)).astype(bf16)
    pos_rows = jnp.dot(in_run, _pad_experts(pos_ref[...]).astype(bf16), preferred_element_type=f32)
    start16 = (tab[:, TAB_START:TAB_START + 1] * (1.0 / SEG_ALIGN)).astype(bf16)
    run_start = jnp.dot(in_run, jnp.broadcast_to(start16, (128, 128)), preferred_element_type=f32) * SEG_ALIGN
    ordinal = rid - run_start
    ordinal = jnp.concatenate([ordinal] * (tm // 128), axis=1)
    perm = jnp.where(pos_rows == ordinal, 1.0, 0.0).astype(bf16)
    buf_ref[slot] = jnp.dot(perm, h_ref[...].astype(bf16), preferred_element_type=f32).astype(bf16)

    def issue(p, _):
        src = pl.multiple_of(p * SEG_ALIGN, SEG_ALIGN)
        dst = pl.multiple_of(pd_ref[t * pieces + p], SEG_ALIGN)
        pltpu.make_async_copy(buf_ref.at[slot, pl.ds(src, SEG_ALIGN)], xs_ref.at[pl.ds(dst, SEG_ALIGN)],
                              sem.at[slot]).start()
        return 0

    lax.fori_loop(0, tot_ref[t] // SEG_ALIGN, issue, 0)

    @pl.when(t == n_tiles - 1)
    def _():
        z_ref[...] = jnp.zeros_like(z_ref)

        def fill(e, _):
            def copy(off, size):
                dst = pl.multiple_of(ts_ref[e] + off, SEG_ALIGN)
                pltpu.make_async_copy(z_ref.at[pl.ds(0, size)], xs_ref.at[pl.ds(dst, size)], sem.at[2]).start()

            _pow2_pieces(tl_ref[e], MOE_BLOCK // 2, copy)
            return 0

        lax.fori_loop(0, n_exp, fill, 0)

        def drain(e, _):
            _wait_rows(tl_ref[e], MOE_BLOCK // 2, z_ref, xs_ref, sem.at[2])
            return 0

        lax.fori_loop(0, n_exp, drain, 0)
        if n_tiles > 1:
            _wait_rows(tot_ref[t - 1], wait_max, buf_ref.at[1 - slot], xs_ref, sem.at[1 - slot])
        _wait_rows(tot_ref[t], wait_max, buf_ref.at[slot], xs_ref, sem.at[slot])


def _tile_rows(tm, n_exp):
    return -(-(tm * TOP_K + n_exp * (SEG_ALIGN - 1)) // 128) * 128


def _wait_max(tile_rows):
    return 1 << (tile_rows.bit_length() - 1)


def _dispatch(h2, pos, tab, piece_dst, tot, tail_start, tail_len, n_rows, n_exp):
    n, d = h2.shape
    tm = ROUTE_TILE
    n_tiles = n // tm
    tile_rows = _tile_rows(tm, n_exp)
    assert n_exp <= 128 and tm % 128 == 0
    return pl.pallas_call(
        functools.partial(_dispatch_kernel, n_exp=n_exp, n_tiles=n_tiles, tile_rows=tile_rows,
                          wait_max=_wait_max(tile_rows)),
        grid_spec=pltpu.PrefetchScalarGridSpec(
            num_scalar_prefetch=4,
            grid=(n_tiles,),
            in_specs=[pl.BlockSpec((n_exp, tm), lambda i, *_: (0, i)),
                      pl.BlockSpec((None, n_exp, 128), lambda i, *_: (i, 0, 0)),
                      pl.BlockSpec((tm, d), lambda i, *_: (i, 0))],
            out_specs=pl.BlockSpec(memory_space=pl.ANY),
            scratch_shapes=[pltpu.VMEM((2, tile_rows, d), bf16), pltpu.VMEM((MOE_BLOCK // 2, d), bf16),
                            pltpu.SemaphoreType.DMA((3,))]),
        out_shape=jax.ShapeDtypeStruct((n_rows, d), bf16),
        compiler_params=_cparams(1),
        name="moe_dispatch",
    )(piece_dst, tot, tail_start, tail_len, pos, tab, h2)


def _expert_kernel(be_ref, nu_ref, x_ref, wg_ref, wu_ref, wd_ref, o_ref):
    @pl.when(pl.program_id(0) < nu_ref[0])
    def _():
        x = x_ref[...]
        g = jnp.dot(x, wg_ref[...].astype(bf16), preferred_element_type=f32)
        u = jnp.dot(x, wu_ref[...].astype(bf16), preferred_element_type=f32)
        h = (g * jax.nn.sigmoid(g)) * u
        o_ref[...] = jnp.dot(h.astype(bf16), wd_ref[...].astype(bf16),
                             preferred_element_type=f32).astype(bf16)


def _experts(xs, blk_e, n_used, we_g, we_u, we_d, l):
    n_rows, d = xs.shape
    f = we_g.shape[-1]
    nblk = n_rows // MOE_BLOCK
    blk = lambda b, be, nu: (jnp.minimum(b, nu[0] - 1), 0)
    wsel = lambda b, be, nu: (l, be[jnp.minimum(b, nu[0] - 1)], 0, 0)
    return pl.pallas_call(
        _expert_kernel,
        grid_spec=pltpu.PrefetchScalarGridSpec(
            num_scalar_prefetch=2,
            grid=(nblk,),
            in_specs=[pl.BlockSpec((MOE_BLOCK, d), blk),
                      pl.BlockSpec((None, None, d, f), wsel),
                      pl.BlockSpec((None, None, d, f), wsel),
                      pl.BlockSpec((None, None, f, d), wsel)],
            out_specs=pl.BlockSpec((MOE_BLOCK, d), blk)),
        out_shape=jax.ShapeDtypeStruct((n_rows, d), bf16),
        compiler_params=_cparams(1),
        name="moe_experts",
    )(blk_e, n_used, xs, we_g, we_u, we_d)


def _combine_kernel(pd_ref, tot_ref, pos_ref, gate_ref, tab_ref, ys_ref, x1_ref, h2_ref, gt_ref, g_ref, b_ref,
                    sg_ref, su_ref, sd_ref, o_ref, buf_ref, sem, *, alpha, n_exp, n_tiles, tile_rows, wait_max):
    t = pl.program_id(0)
    slot = t % 2
    tm = x1_ref.shape[0]
    pieces = tile_rows // SEG_ALIGN

    def fetch(tile, dst_slot):
        def issue(p, _):
            src = pl.multiple_of(pd_ref[tile * pieces + p], SEG_ALIGN)
            dst = pl.multiple_of(p * SEG_ALIGN, SEG_ALIGN)
            pltpu.make_async_copy(ys_ref.at[pl.ds(src, SEG_ALIGN)], buf_ref.at[dst_slot, pl.ds(dst, SEG_ALIGN)],
                                  sem.at[dst_slot]).start()
            return 0

        lax.fori_loop(0, tot_ref[tile] // SEG_ALIGN, issue, 0)

    @pl.when(t == 0)
    def _():
        buf_ref[...] = jnp.zeros_like(buf_ref)
        fetch(0, 0)

    @pl.when(t + 1 < n_tiles)
    def _():
        fetch(t + 1, 1 - slot)

    hb = h2_ref[...].astype(bf16)
    g = jnp.dot(hb, sg_ref[...], preferred_element_type=f32)
    u = jnp.dot(hb, su_ref[...], preferred_element_type=f32)
    shared = jnp.dot(((g * jax.nn.sigmoid(g)) * u).astype(bf16), sd_ref[...], preferred_element_type=f32)
    tab = _pad_experts(tab_ref[...].astype(f32))
    start_col, len_col = tab[:, TAB_START:TAB_START + 1], tab[:, TAB_LEN:TAB_LEN + 1]
    rid = lax.broadcasted_iota(i32, (128, tile_rows), 1).astype(f32)
    in_run = (rid >= start_col) & (rid < start_col + len_col)
    ordinal = rid[0:1, :] - jnp.sum(jnp.where(in_run, start_col, 0.0), axis=0, keepdims=True)
    in_run = in_run.astype(bf16)
    pos_t = _pad_experts(pos_ref[...]).T.astype(bf16)
    gate_t = _pad_experts(gate_ref[...]).T.astype(bf16)
    pos_rows = jnp.dot(pos_t, in_run, preferred_element_type=f32)
    gate_rows = jnp.dot(gate_t, in_run, preferred_element_type=f32)
    wgt = jnp.where(pos_rows == ordinal, gate_rows, 0.0).astype(bf16)
    _wait_rows(tot_ref[t], wait_max, ys_ref, buf_ref.at[slot], sem.at[slot])
    routed = jnp.dot(wgt, buf_ref[slot], preferred_element_type=f32)
    o_ref[...] = _layer_norm(alpha * x1_ref[...] + gt_ref[0] * (shared + routed), g_ref[0], b_ref[0])


def _combine(ys, pos, gate, tab, piece_dst, tot, x1, h2, mod4, ln_g3, ln_b3, sg, su, sd, l, n_ctx, dec_seq, alpha,
             n_exp):
    n, d = x1.shape
    f = sg.shape[-1]
    tm = ROUTE_TILE
    assert n_ctx % tm == 0 and dec_seq % tm == 0
    n_tiles = n // tm
    tile_rows = _tile_rows(tm, n_exp)
    n_ctx_tiles = n_ctx // tm
    tiles_per_seq = dec_seq // tm

    def seg(i):
        return jnp.where(i < n_ctx_tiles, 0, 1 + (i - n_ctx_tiles) // tiles_per_seq)

    base = l * MOD_ROWS * 6
    row = lambda i, *_: (i, 0)
    lst = lambda i, *_: (0, i)
    return pl.pallas_call(
        functools.partial(_combine_kernel, alpha=alpha, n_exp=n_exp, n_tiles=n_tiles, tile_rows=tile_rows,
                          wait_max=_wait_max(tile_rows)),
        grid_spec=pltpu.PrefetchScalarGridSpec(
            num_scalar_prefetch=2,
            grid=(n_tiles,),
            in_specs=[pl.BlockSpec((n_exp, tm), lst), pl.BlockSpec((n_exp, tm), lst),
                      pl.BlockSpec((None, n_exp, 128), lambda i, *_: (i, 0, 0)),
                      pl.BlockSpec(memory_space=pl.ANY),
                      pl.BlockSpec((tm, d), row), pl.BlockSpec((tm, d), row),
                      pl.BlockSpec((1, 1, d), lambda i, *_: (base + seg(i) * 6 + 5, 0, 0)),
                      pl.BlockSpec((1, 1, d), lambda i, *_: (2 * l + 1, 0, 0)),
                      pl.BlockSpec((1, 1, d), lambda i, *_: (2 * l + 1, 0, 0)),
                      pl.BlockSpec((None, d, f), lambda i, *_: (l, 0, 0)),
                      pl.BlockSpec((None, d, f), lambda i, *_: (l, 0, 0)),
                      pl.BlockSpec((None, f, d), lambda i, *_: (l, 0, 0))],
            out_specs=pl.BlockSpec((tm, d), row),
            scratch_shapes=[pltpu.VMEM((2, tile_rows, d), bf16), pltpu.SemaphoreType.DMA((2,))]),
        out_shape=jax.ShapeDtypeStruct((n, d), f32),
        compiler_params=_cparams(1),
        name="moe_combine_ln",
    )(piece_dst, tot, pos, gate, tab, ys, x1, h2, mod4, ln_g3, ln_b3, sg, su, sd)


def _rope_tables(seq, width):
    rows = seq // GRID_W
    row = jnp.repeat(jnp.arange(rows, dtype=f32), GRID_W)
    col = jnp.tile(jnp.arange(GRID_W, dtype=f32), rows)
    half = HEAD_DIM // 2
    inv = ROPE_BASE ** (-jnp.arange(0, half, 2, dtype=f32) / half)
    lane = jnp.arange(width)
    d = lane % HEAD_DIM
    pos = jnp.where((d // half)[None, :] == 0, row[:, None], col[:, None])
    ang = pos * inv[d % (HEAD_DIM // 4)][None, :]
    sign = jnp.where((d % half) < HEAD_DIM // 4, -1.0, 1.0)[None, :]
    return jnp.cos(ang), jnp.sin(ang) * sign


def _block_diag_gates(gate_w_r, gate_w_i, cb):
    depth, _, nb, bw, _ = gate_w_r.shape
    per = cb // bw
    w = jnp.stack([gate_w_r[:, 0], gate_w_i[:, 0], gate_w_r[:, 1], gate_w_i[:, 1]], axis=1)
    w = w.reshape(depth, 4, nb // per, per, bw, bw)
    eye = jnp.eye(per, dtype=w.dtype)
    bd = jnp.einsum("lgcaio,ab->lcgaibo", w, eye)
    return bd.reshape(depth, nb // per, 4, cb, cb).astype(bf16)


def kernel(x_prompt, x_sample, c, cache_k, cache_v, state_lru, c_ctx, w_mod, b_mod, w_in, lam_param, subln_w,
           conv_w, conv_b, gate_w_r, gate_b_r, gate_w_i, gate_b_i, lru_lambda, w_br_attn, w_br_lru, w_out,
           ln_g, ln_b, router_w, router_b, exp_w_gate, exp_w_up, exp_w_down, sh_w_gate, sh_w_up, sh_w_down):
    batch, seq, d = x_prompt.shape
    dec_batch, dec_seq, _ = x_sample.shape
    depth = w_mod.shape[0]
    past = cache_k.shape[2]
    d_lru = conv_w.shape[-1]
    n_exp = router_w.shape[-1]
    qk = N_HEADS * 2 * HEAD_DIM
    attn_w = N_HEADS * V_DIM
    n_ctx, n_lat = batch * seq, dec_batch * dec_seq
    n = n_ctx + n_lat
    alpha = (2 * depth) ** 0.25
    assert 1 + dec_batch <= MOD_ROWS and n_ctx % dec_seq == 0

    cond = jnp.concatenate([c_ctx[None], c, jnp.zeros((MOD_ROWS - 1 - dec_batch, d), f32)], 0)
    w_in_b = w_in.astype(bf16)
    wa, wl, wo = w_br_attn.astype(bf16), w_br_lru.astype(bf16), w_out.astype(bf16)
    sg, su, sd = sh_w_gate.astype(bf16), sh_w_up.astype(bf16), sh_w_down.astype(bf16)
    rwt = jnp.swapaxes(router_w, 1, 2)
    rb3 = router_b.reshape(depth, n_exp, 1)
    wg = _block_diag_gates(gate_w_r, gate_w_i, 256)
    bg = jnp.stack([gate_b_r[:, 0], gate_b_i[:, 0], gate_b_r[:, 1], gate_b_i[:, 1]], axis=1)
    conv_b3 = conv_b.reshape(depth, 1, d_lru)
    subln_w3 = subln_w.reshape(depth, 1, V_DIM)
    ln_g3 = ln_g.reshape(depth * 2, 1, d)
    ln_b3 = ln_b.reshape(depth * 2, 1, d)
    ck = cache_k.reshape(dec_batch, depth, past, N_HEADS * 2 * HEAD_DIM)
    cv = cache_v.reshape(dec_batch, depth, past, attn_w)
    cos_t, sin_t = _rope_tables(dec_seq, V_DIM)
    h0_ctx = jnp.zeros((1, 2, d_lru), f32)

    mod4 = _mod_table(cond, w_mod, b_mod).reshape(depth * MOD_ROWS * 6, 1, d)
    x = jnp.concatenate([x_prompt.reshape(n_ctx, d), x_sample.reshape(n_lat, d)], 0)
    n_rows = n * TOP_K + (n // ROUTE_TILE) * n_exp * (SEG_ALIGN - 1) + n_exp * (MOE_BLOCK - 1)
    n_rows = -(-n_rows // MOE_BLOCK) * MOE_BLOCK
    xr_col, yr_col = 2 * qk + attn_w, 2 * qk + attn_w + d_lru
    ga_col, gl_col = yr_col + d_lru, yr_col + d_lru + d

    new_k, new_v, new_s = [], [], []
    for l in range(depth):
        lam_init = 0.8 - 0.6 * math.exp(-0.3 * l)
        y = _inproj(x, mod4, w_in_b, cos_t, sin_t, l, n_ctx, dec_seq, 2 * qk)
        new_k.append(y[:n_ctx, qk:2 * qk].reshape(batch, seq, N_HEADS, 2 * HEAD_DIM))
        new_v.append(y[:n_ctx, 2 * qk:2 * qk + attn_w].reshape(batch, seq, N_HEADS, V_DIM))
        attn_c = _attention(y, lam_param, subln_w3, l, lam_init, 0, batch, seq)
        attn_l = _attention(y, lam_param, subln_w3, l, lam_init, n_ctx, dec_batch, dec_seq, ck, cv)
        rec_c, hfin = _lru(y, conv_w, conv_b3, wg, bg, lru_lambda, h0_ctx, l, 0, batch, seq, xr_col, yr_col)
        rec_l, _ = _lru(y, conv_w, conv_b3, wg, bg, lru_lambda, state_lru[:, l], l, n_ctx, dec_batch, dec_seq,
                        xr_col, yr_col)
        new_s.append(hfin)
        x1, h2 = _merge(attn_c, attn_l, rec_c, rec_l, y, x, mod4, ln_g3, ln_b3, wa, wl, wo, l, n_ctx, dec_seq, ga_col, gl_col, alpha)
        pos, gate, tab = _route(h2, rwt, rb3, l)
        cpad, grel, start = tab[:, :, TAB_LEN], tab[:, :, TAB_BASE], tab[:, :, TAB_START]
        used = grel[-1] + cpad[-1]
        padded = (used + MOE_BLOCK - 1) // MOE_BLOCK * MOE_BLOCK
        pad_end = jnp.cumsum(padded)
        pad_start = pad_end - padded
        tot = jnp.sum(cpad, axis=1).astype(i32)
        blk_row0 = jnp.arange(n_rows // MOE_BLOCK, dtype=i32) * MOE_BLOCK
        blk_e = jnp.minimum(jnp.sum(pad_end[None, :] <= blk_row0[:, None], axis=1), n_exp - 1).astype(i32)
        n_used = (pad_end[-1:] // MOE_BLOCK).astype(i32)
        piece0 = jnp.arange(_tile_rows(ROUTE_TILE, n_exp) // SEG_ALIGN, dtype=i32)[None, :, None] * SEG_ALIGN
        in_run = (piece0 >= start[:, None, :]) & (piece0 < (start + cpad)[:, None, :])
        shift = (pad_start[None, :] + grel - start)[:, None, :]
        piece_dst = (piece0[:, :, 0] + jnp.sum(jnp.where(in_run, shift, 0), axis=-1)).reshape(-1).astype(i32)
        xs = _dispatch(h2, pos, tab, piece_dst, tot, (pad_start + used).astype(i32),
                       (padded - used).astype(i32), n_rows, n_exp)
        ys = _experts(xs, blk_e, n_used, exp_w_gate, exp_w_up, exp_w_down, l)
        x = _combine(ys, pos, gate, tab, piece_dst, tot, x1, h2, mod4, ln_g3, ln_b3, sg, su, sd, l, n_ctx, dec_seq,
                     alpha, n_exp)
    return (x[:n_ctx].reshape(batch, seq, d), x[n_ctx:].reshape(dec_batch, dec_seq, d),
            jnp.stack(new_k, 1), jnp.stack(new_v, 1), jnp.stack(new_s, 1))
```

```python
import functools
import math

import jax
import jax.numpy as jnp
from jax import lax
from jax.experimental import pallas as pl
from jax.experimental.pallas import tpu as pltpu

f32, bf16, i32 = jnp.float32, jnp.bfloat16, jnp.int32

GRID_W = 64
N_HEADS = 8
HEAD_DIM = 64
V_DIM = 2 * HEAD_DIM
ROPE_BASE = 10000.0
LRU_BLOCKS = 16
CONV_WIDTH = 4
LRU_C = 8.0
TOP_K = 8
N_GROUPS = 8
TOPK_GROUPS = 4
ROUTE_SCALE = 2.5
LN_EPS = 1e-5
MOD_ROWS = 8
MOE_BLOCK = 1024
ROUTE_TILE = 256
SEG_ALIGN = 16
ROPE_CHUNK = 256
TAB_LEN, TAB_BASE, TAB_START = 0, 1, 2
SCAN_UNROLL = 4
VMEM_LIMIT = 56 * 1024 * 1024
HIGHEST = lax.Precision.HIGHEST


def _cparams(n_axes):
    return pltpu.CompilerParams(dimension_semantics=("arbitrary",) * n_axes,
                                vmem_limit_bytes=VMEM_LIMIT)


def _tile(n, pref):
    t = min(n, pref)
    while n % t:
        t -= 8
    assert t > 0 and n % t == 0
    return t


def _sigmoid(x):
    return 0.5 * jnp.tanh(0.5 * x) + 0.5


def _layer_norm(z, g, b):
    mu = jnp.mean(z, -1, keepdims=True)
    zc = z - mu
    var = jnp.mean(zc * zc, -1, keepdims=True)
    return zc * lax.rsqrt(var + LN_EPS) * g + b


def _mod_kernel(c_ref, w_ref, b_ref, o_ref):
    c = c_ref[...]
    a = c * jax.nn.sigmoid(c)
    o_ref[...] = jnp.dot(a, w_ref[...], precision=HIGHEST, preferred_element_type=f32) + b_ref[...]


def _mod_table(cond, w_mod, b_mod):
    depth, d, d6 = w_mod.shape
    tn = _tile(d6, 1024)
    return pl.pallas_call(
        _mod_kernel,
        grid=(depth, d6 // tn),
        in_specs=[pl.BlockSpec((MOD_ROWS, d), lambda l, j: (0, 0)),
                  pl.BlockSpec((None, d, tn), lambda l, j: (l, 0, j)),
                  pl.BlockSpec((None, 1, tn), lambda l, j: (l, 0, j))],
        out_specs=pl.BlockSpec((None, MOD_ROWS, tn), lambda l, j: (l, 0, j)),
        out_shape=jax.ShapeDtypeStruct((depth, MOD_ROWS, d6), f32),
        compiler_params=_cparams(2),
        name="mod_table",
    )(cond, w_mod, b_mod.reshape(depth, 1, d6))


def _inproj_kernel(x_ref, sh_ref, sc_ref, w_ref, cos_ref, sin_ref, o_ref, hb_ref, *,
                   n_ctx_tiles, n_rope_cols, tn):
    i = pl.program_id(0)
    j = pl.program_id(1)

    @pl.when(j == 0)
    def _():
        hb_ref[...] = (x_ref[...] * (1.0 + sc_ref[0]) + sh_ref[0]).astype(bf16)

    rope = jnp.logical_and(i >= n_ctx_tiles, j < n_rope_cols)

    @pl.when(rope)
    def _():
        cos, sin = cos_ref[...], sin_ref[...]
        lane = lax.broadcasted_iota(i32, cos.shape, 1)
        first = (lane % (HEAD_DIM // 2)) < (HEAD_DIM // 4)
        for c in range(tn // ROPE_CHUNK):
            acc = jnp.dot(hb_ref[...], w_ref[:, c * ROPE_CHUNK:(c + 1) * ROPE_CHUNK], preferred_element_type=f32)
            for h in range(ROPE_CHUNK // V_DIM):
                a = acc[:, h * V_DIM:(h + 1) * V_DIM]
                partner = jnp.where(first, pltpu.roll(a, V_DIM - HEAD_DIM // 4, 1), pltpu.roll(a, HEAD_DIM // 4, 1))
                col = c * ROPE_CHUNK + h * V_DIM
                o_ref[:, col:col + V_DIM] = a * cos + partner * sin

    @pl.when(jnp.logical_not(rope))
    def _():
        o_ref[...] = jnp.dot(hb_ref[...], w_ref[...], preferred_element_type=f32)


def _inproj(x, mod4, w_in_b, cos_t, sin_t, l, n_ctx, dec_seq, qk_cols):
    n, d = x.shape
    d_in = w_in_b.shape[-1]
    tm = _tile(math.gcd(n_ctx, dec_seq), 1024)
    tn = _tile(math.gcd(d_in, qk_cols), 1024)
    assert tn % V_DIM == 0
    n_ctx_tiles = n_ctx // tm
    tiles_per_seq = dec_seq // tm

    def seg(i):
        return jnp.where(i < n_ctx_tiles, 0, 1 + (i - n_ctx_tiles) // tiles_per_seq)

    def pos(i):
        return jnp.where(i < n_ctx_tiles, 0, (i - n_ctx_tiles) % tiles_per_seq)

    base = l * MOD_ROWS * 6
    return pl.pallas_call(
        functools.partial(_inproj_kernel, n_ctx_tiles=n_ctx_tiles, n_rope_cols=qk_cols // tn, tn=tn),
        grid=(n // tm, d_in // tn),
        in_specs=[pl.BlockSpec((tm, d), lambda i, j: (i, 0)),
                  pl.BlockSpec((1, 1, d), lambda i, j: (base + seg(i) * 6 + 0, 0, 0)),
                  pl.BlockSpec((1, 1, d), lambda i, j: (base + seg(i) * 6 + 1, 0, 0)),
                  pl.BlockSpec((None, d, tn), lambda i, j: (l, 0, j)),
                  pl.BlockSpec((tm, V_DIM), lambda i, j: (pos(i), 0)),
                  pl.BlockSpec((tm, V_DIM), lambda i, j: (pos(i), 0))],
        out_specs=pl.BlockSpec((tm, tn), lambda i, j: (i, j)),
        out_shape=jax.ShapeDtypeStruct((n, d_in), f32),
        scratch_shapes=[pltpu.VMEM((tm, d), bf16)],
        compiler_params=_cparams(2),
        name="in_proj",
    )(x, mod4, mod4, w_in_b, cos_t, sin_t)


def _key_chunk(lk):
    for ck in (512, 384, 256, 128):
        if lk % ck == 0:
            return ck
    return lk


def _attn_kernel(*refs, has_cache, lam_init, past, heads):
    if has_cache:
        q_ref, k_ref, v_ref, ck_ref, cv_ref, lp_ref, sw_ref, o_ref, kt_ref, vb_ref = refs
    else:
        q_ref, k_ref, v_ref, lp_ref, sw_ref, o_ref, kt_ref, vb_ref = refs

    @pl.when(pl.program_id(2) == 0)
    def _():
        for h in range(heads):
            cols = slice(h * V_DIM, (h + 1) * V_DIM)
            if has_cache:
                kt_ref[h, :, :past] = ck_ref[:, cols].T.astype(bf16)
                vb_ref[h, :past, :V_DIM] = cv_ref[:, cols].astype(bf16)
            kt_ref[h, :, past:] = k_ref[:, cols].T.astype(bf16)
            vb_ref[h, past:, :V_DIM] = v_ref[:, cols].astype(bf16)
            vb_ref[h, :, V_DIM:] = jnp.ones((vb_ref.shape[1], V_DIM), bf16)

    lp = lp_ref[...]
    lam = (jnp.exp(jnp.sum(lp[0:1] * lp[1:2], keepdims=True))
           - jnp.exp(jnp.sum(lp[2:3] * lp[3:4], keepdims=True)) + lam_init)
    for h in range(heads):
        cols = slice(h * V_DIM, (h + 1) * V_DIM)
        o = _attn_head(q_ref[:, cols], kt_ref.at[h], vb_ref.at[h], lam)
        o = o * lax.rsqrt(jnp.mean(o * o, -1, keepdims=True) + LN_EPS) * sw_ref[...]
        o_ref[:, cols] = o * (1.0 - lam_init)


def _attn_head(q, kt_ref, vb_ref, lam):
    tq = q.shape[0]
    q = q * (HEAD_DIM ** -0.5 * math.log2(math.e))
    lane = lax.broadcasted_iota(i32, q.shape, 1)
    q2 = jnp.concatenate([jnp.where(lane < HEAD_DIM, q, 0.0).astype(bf16),
                          jnp.where(lane >= HEAD_DIM, q, 0.0).astype(bf16)], axis=0)
    lk = kt_ref.shape[1]
    ck = _key_chunk(lk)
    acc = jnp.zeros((2 * tq, 2 * V_DIM), f32)
    m_run = jnp.full((2 * tq, 1), -jnp.inf, f32)
    for c in range(lk // ck):
        s_c = jnp.dot(q2, kt_ref[:, c * ck:(c + 1) * ck], preferred_element_type=f32)
        m_new = jnp.maximum(m_run, jnp.max(s_c, axis=-1, keepdims=True))
        e_c = jnp.exp2(s_c - m_new).astype(bf16)
        acc = acc * jnp.exp2(m_run - m_new) + jnp.dot(e_c, vb_ref[c * ck:(c + 1) * ck, :],
                                                        preferred_element_type=f32)
        m_run = m_new
    o2 = acc[:, :V_DIM] / acc[:, V_DIM:]
    return o2[:tq] - lam * o2[tq:]


def _attention(y, lam_param, subln_w3, l, lam_init, row0, nseq, seq, cache_k=None, cache_v=None):
    has_cache = cache_k is not None
    past = cache_k.shape[2] if has_cache else 0
    lk = past + seq
    tq = _tile(seq, 512)
    nq = seq // tq
    heads = N_HEADS if lk <= 512 else 1
    hg = N_HEADS // heads
    w = heads * V_DIM
    assert row0 % seq == 0
    qb0, kb0 = row0 // tq, row0 // seq
    in_specs = [pl.BlockSpec((tq, w), lambda b, h, i: (qb0 + b * nq + i, h)),
                pl.BlockSpec((seq, w), lambda b, h, i: (kb0 + b, hg + h)),
                pl.BlockSpec((seq, w), lambda b, h, i: (kb0 + b, 2 * hg + h))]
    args = [y, y, y]
    if has_cache:
        in_specs += [pl.BlockSpec((None, None, past, w), lambda b, h, i: (b, l, 0, h)),
                     pl.BlockSpec((None, None, past, w), lambda b, h, i: (b, l, 0, h))]
        args += [cache_k, cache_v]
    in_specs += [pl.BlockSpec((None, 4, HEAD_DIM), lambda b, h, i: (l, 0, 0)),
                 pl.BlockSpec((None, 1, V_DIM), lambda b, h, i: (l, 0, 0))]
    args += [lam_param, subln_w3]
    return pl.pallas_call(
        functools.partial(_attn_kernel, has_cache=has_cache, lam_init=lam_init, past=past, heads=heads),
        grid=(nseq, hg, nq),
        in_specs=in_specs,
        out_specs=pl.BlockSpec((tq, w), lambda b, h, i: (b * nq + i, h)),
        out_shape=jax.ShapeDtypeStruct((nseq * seq, N_HEADS * V_DIM), f32),
        scratch_shapes=[pltpu.VMEM((heads, V_DIM, lk), bf16), pltpu.VMEM((heads, lk, 2 * V_DIM), bf16)],
        compiler_params=_cparams(3),
        name="diff_attn_lat" if has_cache else "diff_attn_ctx",
    )(*args)


def _lru_kernel(xr_ref, yr_ref, cw_ref, cb_ref, wg_ref, bg_ref, lam_ref, h0_ref,
                rec_ref, hfin_ref, xp_ref, a_ref, b_ref, *, chunk):
    seq, c = xr_ref.shape
    xp_ref[0:8, :] = jnp.zeros((8, c), f32)
    xp_ref[8:8 + seq, :] = xr_ref[...]
    xp_ref[8 + seq:16 + seq, :] = jnp.zeros((8, c), f32)
    lam = lam_ref[...]
    nlam = -lam
    softplus = jnp.maximum(nlam, 0.0) + jnp.log(1.0 + jnp.exp(-jnp.abs(nlam)))
    for ci in range(seq // chunk):
        t0 = ci * chunk
        xc = cb_ref[...] + sum(cw_ref[j:j + 1, :] * xp_ref[8 + t0 + j - 1:8 + t0 + j - 1 + chunk, :]
                               for j in range(CONV_WIDTH))
        xcb = xc.astype(bf16)
        for d in range(2):
            r = _sigmoid(jnp.dot(xcb, wg_ref[2 * d], preferred_element_type=f32) + bg_ref[2 * d:2 * d + 1, :])
            g = _sigmoid(jnp.dot(xcb, wg_ref[2 * d + 1], preferred_element_type=f32)
                         + bg_ref[2 * d + 1:2 * d + 2, :])
            log_a = -LRU_C * r * softplus[d:d + 1, :]
            a = jnp.exp(log_a)
            a_ref[d, t0:t0 + chunk, :] = a
            u = -jnp.tanh(log_a) * (a * a + 1.0)
            b_ref[d, t0:t0 + chunk, :] = jnp.where(u > 0.0, u * lax.rsqrt(u), 0.0) * (g * xc)

    rows = lax.broadcasted_iota(i32, (8, c), 0)
    nch = seq // 8

    def scan_body(i, carry):
        cf, cb = carry
        tf = pl.multiple_of(8 * i, 8)
        af = a_ref[0, pl.ds(tf, 8), :]
        bf = b_ref[0, pl.ds(tf, 8), :]
        for d in (1, 2, 4):
            keep = rows >= d
            bf = af * jnp.where(keep, pltpu.roll(bf, d, 0), 0.0) + bf
            af = af * jnp.where(keep, pltpu.roll(af, d, 0), 1.0)
        hf = af * cf + bf
        rec_ref[pl.ds(tf, 8), :] = hf
        tb = pl.multiple_of(8 * (nch - 1 - i), 8)
        ab = a_ref[1, pl.ds(tb, 8), :]
        bb = b_ref[1, pl.ds(tb, 8), :]
        for d in (1, 2, 4):
            keep = rows < 8 - d
            bb = ab * jnp.where(keep, pltpu.roll(bb, 8 - d, 0), 0.0) + bb
            ab = ab * jnp.where(keep, pltpu.roll(ab, 8 - d, 0), 1.0)
        hb = ab * cb + bb
        xp_ref[pl.ds(tb, 8), :] = hb
        return hf[7:8, :], hb[0:1, :]

    h0 = h0_ref[...]
    cf, cb = lax.fori_loop(0, nch, scan_body, (h0[0:1, :], h0[1:2, :]), unroll=SCAN_UNROLL)
    hfin_ref[0:1, :] = cf
    hfin_ref[1:2, :] = cb
    for ci in range(seq // chunk):
        t0 = ci * chunk
        rec_ref[t0:t0 + chunk, :] = ((rec_ref[t0:t0 + chunk, :] + xp_ref[t0:t0 + chunk, :])
                                     * jax.nn.gelu(yr_ref[t0:t0 + chunk, :]))


def _lru(y, conv_w, conv_b3, wg, bg, lru_lambda, h0, l, row0, nseq, seq, xr_col, yr_col):
    d_lru = conv_w.shape[-1]
    cb = 256
    ncb = d_lru // cb
    assert row0 % seq == 0
    rb0 = row0 // seq
    h0_batched = h0.shape[0] > 1
    in_specs = [pl.BlockSpec((seq, cb), lambda b, c: (rb0 + b, xr_col // cb + c)),
                pl.BlockSpec((seq, cb), lambda b, c: (rb0 + b, yr_col // cb + c)),
                pl.BlockSpec((None, CONV_WIDTH, cb), lambda b, c: (l, 0, c)),
                pl.BlockSpec((None, 1, cb), lambda b, c: (l, 0, c)),
                pl.BlockSpec((None, None, 4, cb, cb), lambda b, c: (l, c, 0, 0, 0)),
                pl.BlockSpec((None, 4, cb), lambda b, c: (l, 0, c)),
                pl.BlockSpec((None, 2, cb), lambda b, c: (l, 0, c)),
                pl.BlockSpec((None, 2, cb), (lambda b, c: (b, 0, c)) if h0_batched else (lambda b, c: (0, 0, c)))]
    return pl.pallas_call(
        functools.partial(_lru_kernel, chunk=_tile(seq, 512)),
        grid=(nseq, ncb),
        in_specs=in_specs,
        out_specs=[pl.BlockSpec((seq, cb), lambda b, c: (b, c)),
                   pl.BlockSpec((None, 2, cb), lambda b, c: (b, 0, c))],
        out_shape=[jax.ShapeDtypeStruct((nseq * seq, d_lru), f32),
                   jax.ShapeDtypeStruct((nseq, 2, d_lru), f32)],
        scratch_shapes=[pltpu.VMEM((seq + 16, cb), f32), pltpu.VMEM((2, seq, cb), f32),
                        pltpu.VMEM((2, seq, cb), f32)],
        compiler_params=_cparams(2),
        name="rglru",
    )(y, y, conv_w, conv_b3, wg, bg, lru_lambda, h0)


def _merge_kernel(atc_ref, atl_ref, rcc_ref, rcl_ref, ga_ref, gl_ref, x_ref, gt_ref, shf_ref, scf_ref,
                  g_ref, b_ref, wa_ref, wl_ref, wo_ref, x1_ref, h2_ref, ab_ref, rb_ref, *, alpha, n_ctx_tiles):
    @pl.when(pl.program_id(0) < n_ctx_tiles)
    def _():
        ab_ref[...] = atc_ref[...].astype(bf16)
        rb_ref[...] = rcc_ref[...].astype(bf16)

    @pl.when(pl.program_id(0) >= n_ctx_tiles)
    def _():
        ab_ref[...] = atl_ref[...].astype(bf16)
        rb_ref[...] = rcl_ref[...].astype(bf16)

    pa = jnp.dot(ab_ref[...], wa_ref[...], preferred_element_type=f32)
    pr = jnp.dot(rb_ref[...], wl_ref[...], preferred_element_type=f32)
    merged = jax.nn.sigmoid(ga_ref[...]) * pa + jax.nn.sigmoid(gl_ref[...]) * pr
    out = jnp.dot(merged.astype(bf16), wo_ref[...], preferred_element_type=f32)
    x1 = _layer_norm(alpha * x_ref[...] + gt_ref[0] * out, g_ref[0], b_ref[0])
    x1_ref[...] = x1
    h2_ref[...] = x1 * (1.0 + scf_ref[0]) + shf_ref[0]


def _merge(attn_c, attn_l, rec_c, rec_l, y, x, mod4, ln_g3, ln_b3, wa, wl, wo, l, n_ctx, dec_seq, ga_col, gl_col,
           alpha):
    n, d = x.shape
    tm = _tile(math.gcd(n_ctx, dec_seq), 512)
    n_ctx_tiles = n_ctx // tm
    tiles_per_seq = dec_seq // tm

    def seg(i):
        return jnp.where(i < n_ctx_tiles, 0, 1 + (i - n_ctx_tiles) // tiles_per_seq)

    base = l * MOD_ROWS * 6
    row = lambda i: (i, 0)
    ctx_row = lambda i: (jnp.minimum(i, n_ctx_tiles - 1), 0)
    lat_row = lambda i: (jnp.maximum(i - n_ctx_tiles, 0), 0)
    mod = lambda k: pl.BlockSpec((1, 1, d), lambda i: (base + seg(i) * 6 + k, 0, 0))
    wspec = pl.BlockSpec((None, d, d), lambda i: (l, 0, 0))
    return pl.pallas_call(
        functools.partial(_merge_kernel, alpha=alpha, n_ctx_tiles=n_ctx_tiles),
        grid=(n // tm,),
        in_specs=[pl.BlockSpec((tm, d), ctx_row), pl.BlockSpec((tm, d), lat_row),
                  pl.BlockSpec((tm, d), ctx_row), pl.BlockSpec((tm, d), lat_row),
                  pl.BlockSpec((tm, d), lambda i: (i, ga_col // d)),
                  pl.BlockSpec((tm, d), lambda i: (i, gl_col // d)),
                  pl.BlockSpec((tm, d), row), mod(2), mod(3), mod(4),
                  pl.BlockSpec((1, 1, d), lambda i: (2 * l, 0, 0)),
                  pl.BlockSpec((1, 1, d), lambda i: (2 * l, 0, 0)),
                  wspec, wspec, wspec],
        out_specs=[pl.BlockSpec((tm, d), row), pl.BlockSpec((tm, d), row)],
        out_shape=[jax.ShapeDtypeStruct((n, d), f32), jax.ShapeDtypeStruct((n, d), f32)],
        scratch_shapes=[pltpu.VMEM((tm, d), bf16), pltpu.VMEM((tm, d), bf16)],
        compiler_params=_cparams(1),
        name="merge_outproj_ln",
    )(attn_c, attn_l, rec_c, rec_l, y, y, x, mod4, mod4, mod4, ln_g3, ln_b3, wa, wl, wo)


def _route_kernel(h_ref, rw_ref, rb_ref, pos_ref, gate_ref, tab_ref, carry_ref):
    n_exp = rw_ref.shape[0]
    tm = h_ref.shape[0]
    gsz = n_exp // N_GROUPS

    @pl.when(pl.program_id(0) == 0)
    def _():
        carry_ref[...] = jnp.zeros_like(carry_ref)

    logits = lax.dot_general(rw_ref[...], h_ref[...], (((1,), (1,)), ((), ())),
                             precision=HIGHEST, preferred_element_type=f32)
    s = jax.nn.sigmoid(logits)
    sb = s + rb_ref[...]
    ninf = -jnp.inf
    gs_rows = []
    for g in range(N_GROUPS):
        blk = sb[g * gsz:(g + 1) * gsz]
        m1 = jnp.max(blk, axis=0, keepdims=True)
        cnt = jnp.sum((blk == m1).astype(f32), axis=0, keepdims=True)
        m2 = jnp.max(jnp.where(blk < m1, blk, ninf), axis=0, keepdims=True)
        gs_rows.append(m1 + jnp.where(cnt >= 2.0, m1, m2))
    gs = jnp.concatenate(gs_rows, axis=0)
    gi = lax.broadcasted_iota(i32, gs.shape, 0)
    grank = jnp.zeros(gs.shape, f32)
    for g2 in range(N_GROUPS):
        row = gs[g2:g2 + 1]
        grank += jnp.where((row > gs) | ((row == gs) & (g2 < gi)), 1.0, 0.0)
    gsel = grank < float(TOPK_GROUPS)
    sbm = jnp.concatenate([jnp.where(gsel[g:g + 1], sb[g * gsz:(g + 1) * gsz], ninf)
                           for g in range(N_GROUPS)], axis=0)
    ei = lax.broadcasted_iota(i32, sbm.shape, 0)
    erank = jnp.zeros(sbm.shape, f32)
    for e2 in range(n_exp):
        row = sbm[e2:e2 + 1]
        erank += jnp.where((row > sbm) | ((row == sbm) & (e2 < ei)), 1.0, 0.0)
    sel = erank < float(TOP_K)
    ssel = jnp.where(sel, s, 0.0)
    gate = ssel / jnp.sum(ssel, axis=0, keepdims=True) * ROUTE_SCALE
    selb = sel.astype(bf16)
    ti = lax.broadcasted_iota(i32, (tm, tm), 0)
    tj = lax.broadcasted_iota(i32, (tm, tm), 1)
    upper = (ti < tj).astype(bf16)
    pos_local = jnp.dot(selb, upper, preferred_element_type=f32)
    pos_ref[...] = jnp.where(sel, pos_local, -1.0)
    gate_ref[...] = gate
    cnt = jnp.sum(sel.astype(f32), axis=1, keepdims=True)
    cpad = jnp.floor((cnt + (SEG_ALIGN - 1)) * (1.0 / SEG_ALIGN)) * SEG_ALIGN
    cpad_l = jnp.broadcast_to(cpad, (n_exp, 128))
    li = lax.broadcasted_iota(i32, (n_exp, n_exp), 0)
    lj = lax.broadcasted_iota(i32, (n_exp, n_exp), 1)
    lower = (lj < li).astype(bf16)
    run_start = jnp.dot(lower, cpad_l.astype(bf16), preferred_element_type=f32)
    lane = lax.broadcasted_iota(i32, (n_exp, 128), 1)
    tab = jnp.where(lane == TAB_LEN, cpad_l, jnp.where(lane == TAB_BASE, carry_ref[...], run_start))
    tab_ref[...] = tab.astype(i32)
    carry_ref[...] = carry_ref[...] + cpad_l


def _route(h2, rwt, rb3, l):
    n, d = h2.shape
    n_exp = rwt.shape[1]
    tm = ROUTE_TILE
    assert n % tm == 0
    lst = pl.BlockSpec((n_exp, tm), lambda i: (0, i))
    return pl.pallas_call(
        _route_kernel,
        grid=(n // tm,),
        in_specs=[pl.BlockSpec((tm, d), lambda i: (i, 0)),
                  pl.BlockSpec((None, n_exp, d), lambda i: (l, 0, 0)),
                  pl.BlockSpec((None, n_exp, 1), lambda i: (l, 0, 0))],
        out_specs=[lst, lst, pl.BlockSpec((None, n_exp, 128), lambda i: (i, 0, 0))],
        out_shape=[jax.ShapeDtypeStruct((n_exp, n), f32), jax.ShapeDtypeStruct((n_exp, n), f32),
                   jax.ShapeDtypeStruct((n // tm, n_exp, 128), i32)],
        scratch_shapes=[pltpu.VMEM((n_exp, 128), f32)],
        compiler_params=_cparams(1),
        name="moe_route",
    )(h2, rwt, rb3)


def _pow2_pieces(count, max_rows, fn):
    size = max_rows
    while size >= SEG_ALIGN:
        off = count & (-2 * size)
        pl.when((count & size) != 0)(functools.partial(fn, off, size))
        size //= 2


def _wait_rows(count, max_rows, src_ref, dst_ref, sem):
    def wait(_, size):
        pltpu.make_async_copy(src_ref.at[pl.ds(0, size)], dst_ref.at[pl.ds(0, size)], sem).wait()

    _pow2_pieces(count, max_rows, wait)


def _pad_experts(x):
    return jnp.concatenate([x, jnp.zeros((128 - x.shape[0], x.shape[1]), x.dtype)], axis=0)


def _dispatch_kernel(pd_ref, tot_ref, ts_ref, tl_ref, pos_ref, tab_ref, h_ref, xs_ref, buf_ref, z_ref,
                     sem, *, n_exp, n_tiles, tile_rows, wait_max):
    t = pl.program_id(0)
    slot = t % 2
    tm = h_ref.shape[0]
    pieces = tile_rows // SEG_ALIGN

    @pl.when(t >= 2)
    def _():
        _wait_rows(tot_ref[t - 2], wait_max, buf_ref.at[slot], xs_ref, sem.at[slot])

    tab = _pad_experts(tab_ref[...].astype(f32))
    tab_t = tab.T
    start_row, len_row = tab_t[TAB_START:TAB_START + 1, :], tab_t[TAB_LEN:TAB_LEN + 1, :]
    rid = lax.broadcasted_iota(i32, (tile_rows, 128), 0).astype(f32)
    in_run = ((rid >= start_row) & (rid < start_row + len_row)).astype(bf16)
    pos_rows = jnp.dot(in_run, _pad_experts(pos_ref[...]).astype(bf16), preferred_element_type=f32)
    start16 = (tab[:, TAB_START:TAB_START + 1] * (1.0 / SEG_ALIGN)).astype(bf16)
    run_start = jnp.dot(in_run, jnp.broadcast_to(start16, (128, 128)), preferred_element_type=f32) * SEG_ALIGN
    ordinal = rid - run_start
    ordinal = jnp.concatenate([ordinal] * (tm // 128), axis=1)
    perm = jnp.where(pos_rows == ordinal, 1.0, 0.0).astype(bf16)
    buf_ref[slot] = jnp.dot(perm, h_ref[...].astype(bf16), preferred_element_type=f32).astype(bf16)

    def issue(p, _):
        src = pl.multiple_of(p * SEG_ALIGN, SEG_ALIGN)
        dst = pl.multiple_of(pd_ref[t * pieces + p], SEG_ALIGN)
        pltpu.make_async_copy(buf_ref.at[slot, pl.ds(src, SEG_ALIGN)], xs_ref.at[pl.ds(dst, SEG_ALIGN)],
                              sem.at[slot]).start()
        return 0

    lax.fori_loop(0, tot_ref[t] // SEG_ALIGN, issue, 0)

    @pl.when(t == n_tiles - 1)
    def _():
        z_ref[...] = jnp.zeros_like(z_ref)

        def fill(e, _):
            def copy(off, size):
                dst = pl.multiple_of(ts_ref[e] + off, SEG_ALIGN)
                pltpu.make_async_copy(z_ref.at[pl.ds(0, size)], xs_ref.at[pl.ds(dst, size)], sem.at[2]).start()

            _pow2_pieces(tl_ref[e], MOE_BLOCK // 2, copy)
            return 0

        lax.fori_loop(0, n_exp, fill, 0)

        def drain(e, _):
            _wait_rows(tl_ref[e], MOE_BLOCK // 2, z_ref, xs_ref, sem.at[2])
            return 0

        lax.fori_loop(0, n_exp, drain, 0)
        if n_tiles > 1:
            _wait_rows(tot_ref[t - 1], wait_max, buf_ref.at[1 - slot], xs_ref, sem.at[1 - slot])
        _wait_rows(tot_ref[t], wait_max, buf_ref.at[slot], xs_ref, sem.at[slot])


def _tile_rows(tm, n_exp):
    return -(-(tm * TOP_K + n_exp * (SEG_ALIGN - 1)) // 128) * 128


def _wait_max(tile_rows):
    return 1 << (tile_rows.bit_length() - 1)


def _dispatch(h2, pos, tab, piece_dst, tot, tail_start, tail_len, n_rows, n_exp):
    n, d = h2.shape
    tm = ROUTE_TILE
    n_tiles = n // tm
    tile_rows = _tile_rows(tm, n_exp)
    assert n_exp <= 128 and tm % 128 == 0
    return pl.pallas_call(
        functools.partial(_dispatch_kernel, n_exp=n_exp, n_tiles=n_tiles, tile_rows=tile_rows,
                          wait_max=_wait_max(tile_rows)),
        grid_spec=pltpu.PrefetchScalarGridSpec(
            num_scalar_prefetch=4,
            grid=(n_tiles,),
            in_specs=[pl.BlockSpec((n_exp, tm), lambda i, *_: (0, i)),
                      pl.BlockSpec((None, n_exp, 128), lambda i, *_: (i, 0, 0)),
                      pl.BlockSpec((tm, d), lambda i, *_: (i, 0))],
            out_specs=pl.BlockSpec(memory_space=pl.ANY),
            scratch_shapes=[pltpu.VMEM((2, tile_rows, d), bf16), pltpu.VMEM((MOE_BLOCK // 2, d), bf16),
                            pltpu.SemaphoreType.DMA((3,))]),
        out_shape=jax.ShapeDtypeStruct((n_rows, d), bf16),
        compiler_params=_cparams(1),
        name="moe_dispatch",
    )(piece_dst, tot, tail_start, tail_len, pos, tab, h2)


def _expert_kernel(be_ref, nu_ref, x_ref, wg_ref, wu_ref, wd_ref, o_ref):
    @pl.when(pl.program_id(0) < nu_ref[0])
    def _():
        x = x_ref[...]
        g = jnp.dot(x, wg_ref[...].astype(bf16), preferred_element_type=f32)
        u = jnp.dot(x, wu_ref[...].astype(bf16), preferred_element_type=f32)
        h = (g * jax.nn.sigmoid(g)) * u
        o_ref[...] = jnp.dot(h.astype(bf16), wd_ref[...].astype(bf16),
                             preferred_element_type=f32).astype(bf16)


def _experts(xs, blk_e, n_used, we_g, we_u, we_d, l):
    n_rows, d = xs.shape
    f = we_g.shape[-1]
    nblk = n_rows // MOE_BLOCK
    blk = lambda b, be, nu: (jnp.minimum(b, nu[0] - 1), 0)
    wsel = lambda b, be, nu: (l, be[jnp.minimum(b, nu[0] - 1)], 0, 0)
    return pl.pallas_call(
        _expert_kernel,
        grid_spec=pltpu.PrefetchScalarGridSpec(
            num_scalar_prefetch=2,
            grid=(nblk,),
            in_specs=[pl.BlockSpec((MOE_BLOCK, d), blk),
                      pl.BlockSpec((None, None, d, f), wsel),
                      pl.BlockSpec((None, None, d, f), wsel),
                      pl.BlockSpec((None, None, f, d), wsel)],
            out_specs=pl.BlockSpec((MOE_BLOCK, d), blk)),
        out_shape=jax.ShapeDtypeStruct((n_rows, d), bf16),
        compiler_params=_cparams(1),
        name="moe_experts",
    )(blk_e, n_used, xs, we_g, we_u, we_d)


def _combine_kernel(pd_ref, tot_ref, pos_ref, gate_ref, tab_ref, ys_ref, x1_ref, h2_ref, gt_ref, g_ref, b_ref,
                    sg_ref, su_ref, sd_ref, o_ref, buf_ref, sem, *, alpha, n_exp, n_tiles, tile_rows, wait_max):
    t = pl.program_id(0)
    slot = t % 2
    tm = x1_ref.shape[0]
    pieces = tile_rows // SEG_ALIGN

    def fetch(tile, dst_slot):
        def issue(p, _):
            src = pl.multiple_of(pd_ref[tile * pieces + p], SEG_ALIGN)
            dst = pl.multiple_of(p * SEG_ALIGN, SEG_ALIGN)
            pltpu.make_async_copy(ys_ref.at[pl.ds(src, SEG_ALIGN)], buf_ref.at[dst_slot, pl.ds(dst, SEG_ALIGN)],
                                  sem.at[dst_slot]).start()
            return 0

        lax.fori_loop(0, tot_ref[tile] // SEG_ALIGN, issue, 0)

    @pl.when(t == 0)
    def _():
        buf_ref[...] = jnp.zeros_like(buf_ref)
        fetch(0, 0)

    @pl.when(t + 1 < n_tiles)
    def _():
        fetch(t + 1, 1 - slot)

    hb = h2_ref[...].astype(bf16)
    g = jnp.dot(hb, sg_ref[...], preferred_element_type=f32)
    u = jnp.dot(hb, su_ref[...], preferred_element_type=f32)
    shared = jnp.dot(((g * jax.nn.sigmoid(g)) * u).astype(bf16), sd_ref[...], preferred_element_type=f32)
    tab = _pad_experts(tab_ref[...].astype(f32))
    start_col, len_col = tab[:, TAB_START:TAB_START + 1], tab[:, TAB_LEN:TAB_LEN + 1]
    rid = lax.broadcasted_iota(i32, (128, tile_rows), 1).astype(f32)
    in_run = (rid >= start_col) & (rid < start_col + len_col)
    ordinal = rid[0:1, :] - jnp.sum(jnp.where(in_run, start_col, 0.0), axis=0, keepdims=True)
    in_run = in_run.astype(bf16)
    pos_t = _pad_experts(pos_ref[...]).T.astype(bf16)
    gate_t = _pad_experts(gate_ref[...]).T.astype(bf16)
    pos_rows = jnp.dot(pos_t, in_run, preferred_element_type=f32)
    gate_rows = jnp.dot(gate_t, in_run, preferred_element_type=f32)
    wgt = jnp.where(pos_rows == ordinal, gate_rows, 0.0).astype(bf16)
    _wait_rows(tot_ref[t], wait_max, ys_ref, buf_ref.at[slot], sem.at[slot])
    routed = jnp.dot(wgt, buf_ref[slot], preferred_element_type=f32)
    o_ref[...] = _layer_norm(alpha * x1_ref[...] + gt_ref[0] * (shared + routed), g_ref[0], b_ref[0])


def _combine(ys, pos, gate, tab, piece_dst, tot, x1, h2, mod4, ln_g3, ln_b3, sg, su, sd, l, n_ctx, dec_seq, alpha,
             n_exp):
    n, d = x1.shape
    f = sg.shape[-1]
    tm = ROUTE_TILE
    assert n_ctx % tm == 0 and dec_seq % tm == 0
    n_tiles = n // tm
    tile_rows = _tile_rows(tm, n_exp)
    n_ctx_tiles = n_ctx // tm
    tiles_per_seq = dec_seq // tm

    def seg(i):
        return jnp.where(i < n_ctx_tiles, 0, 1 + (i - n_ctx_tiles) // tiles_per_seq)

    base = l * MOD_ROWS * 6
    row = lambda i, *_: (i, 0)
    lst = lambda i, *_: (0, i)
    return pl.pallas_call(
        functools.partial(_combine_kernel, alpha=alpha, n_exp=n_exp, n_tiles=n_tiles, tile_rows=tile_rows,
                          wait_max=_wait_max(tile_rows)),
        grid_spec=pltpu.PrefetchScalarGridSpec(
            num_scalar_prefetch=2,
            grid=(n_tiles,),
            in_specs=[pl.BlockSpec((n_exp, tm), lst), pl.BlockSpec((n_exp, tm), lst),
                      pl.BlockSpec((None, n_exp, 128), lambda i, *_: (i, 0, 0)),
                      pl.BlockSpec(memory_space=pl.ANY),
                      pl.BlockSpec((tm, d), row), pl.BlockSpec((tm, d), row),
                      pl.BlockSpec((1, 1, d), lambda i, *_: (base + seg(i) * 6 + 5, 0, 0)),
                      pl.BlockSpec((1, 1, d), lambda i, *_: (2 * l + 1, 0, 0)),
                      pl.BlockSpec((1, 1, d), lambda i, *_: (2 * l + 1, 0, 0)),
                      pl.BlockSpec((None, d, f), lambda i, *_: (l, 0, 0)),
                      pl.BlockSpec((None, d, f), lambda i, *_: (l, 0, 0)),
                      pl.BlockSpec((None, f, d), lambda i, *_: (l, 0, 0))],
            out_specs=pl.BlockSpec((tm, d), row),
            scratch_shapes=[pltpu.VMEM((2, tile_rows, d), bf16), pltpu.SemaphoreType.DMA((2,))]),
        out_shape=jax.ShapeDtypeStruct((n, d), f32),
        compiler_params=_cparams(1),
        name="moe_combine_ln",
    )(piece_dst, tot, pos, gate, tab, ys, x1, h2, mod4, ln_g3, ln_b3, sg, su, sd)


def _rope_tables(seq, width):
    rows = seq // GRID_W
    row = jnp.repeat(jnp.arange(rows, dtype=f32), GRID_W)
    col = jnp.tile(jnp.arange(GRID_W, dtype=f32), rows)
    half = HEAD_DIM // 2
    inv = ROPE_BASE ** (-jnp.arange(0, half, 2, dtype=f32) / half)
    lane = jnp.arange(width)
    d = lane % HEAD_DIM
    pos = jnp.where((d // half)[None, :] == 0, row[:, None], col[:, None])
    ang = pos * inv[d % (HEAD_DIM // 4)][None, :]
    sign = jnp.where((d % half) < HEAD_DIM // 4, -1.0, 1.0)[None, :]
    return jnp.cos(ang), jnp.sin(ang) * sign


def _block_diag_gates(gate_w_r, gate_w_i, cb):
    depth, _, nb, bw, _ = gate_w_r.shape
    per = cb // bw
    w = jnp.stack([gate_w_r[:, 0], gate_w_i[:, 0], gate_w_r[:, 1], gate_w_i[:, 1]], axis=1)
    w = w.reshape(depth, 4, nb // per, per, bw, bw)
    eye = jnp.eye(per, dtype=w.dtype)
    bd = jnp.einsum("lgcaio,ab->lcgaibo", w, eye)
    return bd.reshape(depth, nb // per, 4, cb, cb).astype(bf16)


def kernel(x_prompt, x_sample, c, cache_k, cache_v, state_lru, c_ctx, w_mod, b_mod, w_in, lam_param, subln_w,
           conv_w, conv_b, gate_w_r, gate_b_r, gate_w_i, gate_b_i, lru_lambda, w_br_attn, w_br_lru, w_out,
           ln_g, ln_b, router_w, router_b, exp_w_gate, exp_w_up, exp_w_down, sh_w_gate, sh_w_up, sh_w_down):
    batch, seq, d = x_prompt.shape
    dec_batch, dec_seq, _ = x_sample.shape
    depth = w_mod.shape[0]
    past = cache_k.shape[2]
    d_lru = conv_w.shape[-1]
    n_exp = router_w.shape[-1]
    qk = N_HEADS * 2 * HEAD_DIM
    attn_w = N_HEADS * V_DIM
    n_ctx, n_lat = batch * seq, dec_batch * dec_seq
    n = n_ctx + n_lat
    alpha = (2 * depth) ** 0.25
    assert 1 + dec_batch <= MOD_ROWS and n_ctx % dec_seq == 0

    cond = jnp.concatenate([c_ctx[None], c, jnp.zeros((MOD_ROWS - 1 - dec_batch, d), f32)], 0)
    w_in_b = w_in.astype(bf16)
    wa, wl, wo = w_br_attn.astype(bf16), w_br_lru.astype(bf16), w_out.astype(bf16)
    sg, su, sd = sh_w_gate.astype(bf16), sh_w_up.astype(bf16), sh_w_down.astype(bf16)
    rwt = jnp.swapaxes(router_w, 1, 2)
    rb3 = router_b.reshape(depth, n_exp, 1)
    wg = _block_diag_gates(gate_w_r, gate_w_i, 256)
    bg = jnp.stack([gate_b_r[:, 0], gate_b_i[:, 0], gate_b_r[:, 1], gate_b_i[:, 1]], axis=1)
    conv_b3 = conv_b.reshape(depth, 1, d_lru)
    subln_w3 = subln_w.reshape(depth, 1, V_DIM)
    ln_g3 = ln_g.reshape(depth * 2, 1, d)
    ln_b3 = ln_b.reshape(depth * 2, 1, d)
    ck = cache_k.reshape(dec_batch, depth, past, N_HEADS * 2 * HEAD_DIM)
    cv = cache_v.reshape(dec_batch, depth, past, attn_w)
    cos_t, sin_t = _rope_tables(dec_seq, V_DIM)
    h0_ctx = jnp.zeros((1, 2, d_lru), f32)

    mod4 = _mod_table(cond, w_mod, b_mod).reshape(depth * MOD_ROWS * 6, 1, d)
    x = jnp.concatenate([x_prompt.reshape(n_ctx, d), x_sample.reshape(n_lat, d)], 0)
    n_rows = n * TOP_K + (n // ROUTE_TILE) * n_exp * (SEG_ALIGN - 1) + n_exp * (MOE_BLOCK - 1)
    n_rows = -(-n_rows // MOE_BLOCK) * MOE_BLOCK
    xr_col, yr_col = 2 * qk + attn_w, 2 * qk + attn_w + d_lru
    ga_col, gl_col = yr_col + d_lru, yr_col + d_lru + d

    new_k, new_v, new_s = [], [], []
    for l in range(depth):
        lam_init = 0.8 - 0.6 * math.exp(-0.3 * l)
        y = _inproj(x, mod4, w_in_b, cos_t, sin_t, l, n_ctx, dec_seq, 2 * qk)
        new_k.append(y[:n_ctx, qk:2 * qk].reshape(batch, seq, N_HEADS, 2 * HEAD_DIM))
        new_v.append(y[:n_ctx, 2 * qk:2 * qk + attn_w].reshape(batch, seq, N_HEADS, V_DIM))
        attn_c = _attention(y, lam_param, subln_w3, l, lam_init, 0, batch, seq)
        attn_l = _attention(y, lam_param, subln_w3, l, lam_init, n_ctx, dec_batch, dec_seq, ck, cv)
        rec_c, hfin = _lru(y, conv_w, conv_b3, wg, bg, lru_lambda, h0_ctx, l, 0, batch, seq, xr_col, yr_col)
        rec_l, _ = _lru(y, conv_w, conv_b3, wg, bg, lru_lambda, state_lru[:, l], l, n_ctx, dec_batch, dec_seq,
                        xr_col, yr_col)
        new_s.append(hfin)
        x1, h2 = _merge(attn_c, attn_l, rec_c, rec_l, y, x, mod4, ln_g3, ln_b3, wa, wl, wo, l, n_ctx, dec_seq, ga_col, gl_col, alpha)
        pos, gate, tab = _route(h2, rwt, rb3, l)
        cpad, grel, start = tab[:, :, TAB_LEN], tab[:, :, TAB_BASE], tab[:, :, TAB_START]
        used = grel[-1] + cpad[-1]
        padded = (used + MOE_BLOCK - 1) // MOE_BLOCK * MOE_BLOCK
        pad_end = jnp.cumsum(padded)
        pad_start = pad_end - padded
        tot = jnp.sum(cpad, axis=1).astype(i32)
        blk_row0 = jnp.arange(n_rows // MOE_BLOCK, dtype=i32) * MOE_BLOCK
        blk_e = jnp.minimum(jnp.sum(pad_end[None, :] <= blk_row0[:, None], axis=1), n_exp - 1).astype(i32)
        n_used = (pad_end[-1:] // MOE_BLOCK).astype(i32)
        piece0 = jnp.arange(_tile_rows(ROUTE_TILE, n_exp) // SEG_ALIGN, dtype=i32)[None, :, None] * SEG_ALIGN
        in_run = (piece0 >= start[:, None, :]) & (piece0 < (start + cpad)[:, None, :])
        shift = (pad_start[None, :] + grel - start)[:, None, :]
        piece_dst = (piece0[:, :, 0] + jnp.sum(jnp.where(in_run, shift, 0), axis=-1)).reshape(-1).astype(i32)
        xs = _dispatch(h2, pos, tab, piece_dst, tot, (pad_start + used).astype(i32),
                       (padded - used).astype(i32), n_rows, n_exp)
        ys = _experts(xs, blk_e, n_used, exp_w_gate, exp_w_up, exp_w_down, l)
        x = _combine(ys, pos, gate, tab, piece_dst, tot, x1, h2, mod4, ln_g3, ln_b3, sg, su, sd, l, n_ctx, dec_seq,
                     alpha, n_exp)
    return (x[:n_ctx].reshape(batch, seq, d), x[n_ctx:].reshape(dec_batch, dec_seq, d),
            jnp.stack(new_k, 1), jnp.stack(new_v, 1), jnp.stack(new_s, 1))
```

```python
import functools
import math

import jax
import jax.numpy as jnp
from jax import lax
from jax.experimental import pallas as pl
from jax.experimental.pallas import tpu as pltpu

f32, bf16, i32 = jnp.float32, jnp.bfloat16, jnp.int32

GRID_W = 64
N_HEADS = 8
HEAD_DIM = 64
V_DIM = 2 * HEAD_DIM
ROPE_BASE = 10000.0
LRU_BLOCKS = 16
CONV_WIDTH = 4
LRU_C = 8.0
TOP_K = 8
N_GROUPS = 8
TOPK_GROUPS = 4
ROUTE_SCALE = 2.5
LN_EPS = 1e-5
MOD_ROWS = 8
MOE_BLOCK = 1024
ROUTE_TILE = 256
SEG_ALIGN = 16
PIECE_UNROLL = 4
ROPE_CHUNK = 256
TAB_LEN, TAB_BASE, TAB_START = 0, 1, 2
SCAN_UNROLL = 4
VMEM_LIMIT = 56 * 1024 * 1024
HIGHEST = lax.Precision.HIGHEST


def _cparams(n_axes):
    return pltpu.CompilerParams(dimension_semantics=("arbitrary",) * n_axes,
                                vmem_limit_bytes=VMEM_LIMIT)


def _tile(n, pref):
    t = min(n, pref)
    while n % t:
        t -= 8
    assert t > 0 and n % t == 0
    return t


def _sigmoid(x):
    return 0.5 * jnp.tanh(0.5 * x) + 0.5


def _layer_norm(z, g, b):
    mu = jnp.mean(z, -1, keepdims=True)
    zc = z - mu
    var = jnp.mean(zc * zc, -1, keepdims=True)
    return zc * lax.rsqrt(var + LN_EPS) * g + b


def _mod_kernel(c_ref, w_ref, b_ref, o_ref):
    c = c_ref[...]
    a = c * jax.nn.sigmoid(c)
    o_ref[...] = jnp.dot(a, w_ref[...], precision=HIGHEST, preferred_element_type=f32) + b_ref[...]


def _mod_table(cond, w_mod, b_mod):
    depth, d, d6 = w_mod.shape
    tn = _tile(d6, 1024)
    return pl.pallas_call(
        _mod_kernel,
        grid=(depth, d6 // tn),
        in_specs=[pl.BlockSpec((MOD_ROWS, d), lambda l, j: (0, 0)),
                  pl.BlockSpec((None, d, tn), lambda l, j: (l, 0, j)),
                  pl.BlockSpec((None, 1, tn), lambda l, j: (l, 0, j))],
        out_specs=pl.BlockSpec((None, MOD_ROWS, tn), lambda l, j: (l, 0, j)),
        out_shape=jax.ShapeDtypeStruct((depth, MOD_ROWS, d6), f32),
        compiler_params=_cparams(2),
        name="mod_table",
    )(cond, w_mod, b_mod.reshape(depth, 1, d6))


def _inproj_kernel(x_ref, sh_ref, sc_ref, w_ref, cos_ref, sin_ref, o_ref, hb_ref, *,
                   n_ctx_tiles, n_rope_cols, tn):
    i = pl.program_id(0)
    j = pl.program_id(1)

    @pl.when(j == 0)
    def _():
        hb_ref[...] = (x_ref[...] * (1.0 + sc_ref[0]) + sh_ref[0]).astype(bf16)

    rope = jnp.logical_and(i >= n_ctx_tiles, j < n_rope_cols)

    @pl.when(rope)
    def _():
        cos, sin = cos_ref[...], sin_ref[...]
        lane = lax.broadcasted_iota(i32, cos.shape, 1)
        first = (lane % (HEAD_DIM // 2)) < (HEAD_DIM // 4)
        for c in range(tn // ROPE_CHUNK):
            acc = jnp.dot(hb_ref[...], w_ref[:, c * ROPE_CHUNK:(c + 1) * ROPE_CHUNK], preferred_element_type=f32)
            for h in range(ROPE_CHUNK // V_DIM):
                a = acc[:, h * V_DIM:(h + 1) * V_DIM]
                partner = jnp.where(first, pltpu.roll(a, V_DIM - HEAD_DIM // 4, 1), pltpu.roll(a, HEAD_DIM // 4, 1))
                col = c * ROPE_CHUNK + h * V_DIM
                o_ref[:, col:col + V_DIM] = a * cos + partner * sin

    @pl.when(jnp.logical_not(rope))
    def _():
        o_ref[...] = jnp.dot(hb_ref[...], w_ref[...], preferred_element_type=f32)


def _inproj(x, mod4, w_in_b, cos_t, sin_t, l, n_ctx, dec_seq, qk_cols):
    n, d = x.shape
    d_in = w_in_b.shape[-1]
    tm = _tile(math.gcd(n_ctx, dec_seq), 1024)
    tn = _tile(math.gcd(d_in, qk_cols), 1024)
    assert tn % V_DIM == 0
    n_ctx_tiles = n_ctx // tm
    tiles_per_seq = dec_seq // tm

    def seg(i):
        return jnp.where(i < n_ctx_tiles, 0, 1 + (i - n_ctx_tiles) // tiles_per_seq)

    def pos(i):
        return jnp.where(i < n_ctx_tiles, 0, (i - n_ctx_tiles) % tiles_per_seq)

    base = l * MOD_ROWS * 6
    return pl.pallas_call(
        functools.partial(_inproj_kernel, n_ctx_tiles=n_ctx_tiles, n_rope_cols=qk_cols // tn, tn=tn),
        grid=(n // tm, d_in // tn),
        in_specs=[pl.BlockSpec((tm, d), lambda i, j: (i, 0)),
                  pl.BlockSpec((1, 1, d), lambda i, j: (base + seg(i) * 6 + 0, 0, 0)),
                  pl.BlockSpec((1, 1, d), lambda i, j: (base + seg(i) * 6 + 1, 0, 0)),
                  pl.BlockSpec((None, d, tn), lambda i, j: (l, 0, j)),
                  pl.BlockSpec((tm, V_DIM), lambda i, j: (pos(i), 0)),
                  pl.BlockSpec((tm, V_DIM), lambda i, j: (pos(i), 0))],
        out_specs=pl.BlockSpec((tm, tn), lambda i, j: (i, j)),
        out_shape=jax.ShapeDtypeStruct((n, d_in), f32),
        scratch_shapes=[pltpu.VMEM((tm, d), bf16)],
        compiler_params=_cparams(2),
        name="in_proj",
    )(x, mod4, mod4, w_in_b, cos_t, sin_t)


def _key_chunk(lk):
    for ck in (512, 384, 256, 128):
        if lk % ck == 0:
            return ck
    return lk


def _attn_kernel(*refs, has_cache, lam_init, past, heads):
    if has_cache:
        q_ref, k_ref, v_ref, ck_ref, cv_ref, lp_ref, sw_ref, o_ref, kt_ref, vb_ref = refs
    else:
        q_ref, k_ref, v_ref, lp_ref, sw_ref, o_ref, kt_ref, vb_ref = refs

    @pl.when(pl.program_id(2) == 0)
    def _():
        for h in range(heads):
            cols = slice(h * V_DIM, (h + 1) * V_DIM)
            if has_cache:
                kt_ref[h, :, :past] = ck_ref[:, cols].T.astype(bf16)
                vb_ref[h, :past, :V_DIM] = cv_ref[:, cols].astype(bf16)
            kt_ref[h, :, past:] = k_ref[:, cols].T.astype(bf16)
            vb_ref[h, past:, :V_DIM] = v_ref[:, cols].astype(bf16)
            vb_ref[h, :, V_DIM:] = jnp.ones((vb_ref.shape[1], V_DIM), bf16)

    lp = lp_ref[...]
    lam = (jnp.exp(jnp.sum(lp[0:1] * lp[1:2], keepdims=True))
           - jnp.exp(jnp.sum(lp[2:3] * lp[3:4], keepdims=True)) + lam_init)
    for h in range(heads):
        cols = slice(h * V_DIM, (h + 1) * V_DIM)
        o = _attn_head(q_ref[:, cols], kt_ref.at[h], vb_ref.at[h], lam)
        o = o * lax.rsqrt(jnp.mean(o * o, -1, keepdims=True) + LN_EPS) * sw_ref[...]
        o_ref[:, cols] = o * (1.0 - lam_init)


def _attn_head(q, kt_ref, vb_ref, lam):
    tq = q.shape[0]
    q = q * (HEAD_DIM ** -0.5 * math.log2(math.e))
    lane = lax.broadcasted_iota(i32, q.shape, 1)
    q2 = jnp.concatenate([jnp.where(lane < HEAD_DIM, q, 0.0).astype(bf16),
                          jnp.where(lane >= HEAD_DIM, q, 0.0).astype(bf16)], axis=0)
    lk = kt_ref.shape[1]
    ck = _key_chunk(lk)
    acc = jnp.zeros((2 * tq, 2 * V_DIM), f32)
    m_run = jnp.full((2 * tq, 1), -jnp.inf, f32)
    for c in range(lk // ck):
        s_c = jnp.dot(q2, kt_ref[:, c * ck:(c + 1) * ck], preferred_element_type=f32)
        m_new = jnp.maximum(m_run, jnp.max(s_c, axis=-1, keepdims=True))
        e_c = jnp.exp2(s_c - m_new).astype(bf16)
        acc = acc * jnp.exp2(m_run - m_new) + jnp.dot(e_c, vb_ref[c * ck:(c + 1) * ck, :],
                                                        preferred_element_type=f32)
        m_run = m_new
    o2 = acc[:, :V_DIM] / acc[:, V_DIM:]
    return o2[:tq] - lam * o2[tq:]


def _attention(y, lam_param, subln_w3, l, lam_init, row0, nseq, seq, cache_k=None, cache_v=None):
    has_cache = cache_k is not None
    past = cache_k.shape[2] if has_cache else 0
    lk = past + seq
    tq = _tile(seq, 512)
    nq = seq // tq
    heads = N_HEADS if lk <= 512 else 1
    hg = N_HEADS // heads
    w = heads * V_DIM
    assert row0 % seq == 0
    qb0, kb0 = row0 // tq, row0 // seq
    in_specs = [pl.BlockSpec((tq, w), lambda b, h, i: (qb0 + b * nq + i, h)),
                pl.BlockSpec((seq, w), lambda b, h, i: (kb0 + b, hg + h)),
                pl.BlockSpec((seq, w), lambda b, h, i: (kb0 + b, 2 * hg + h))]
    args = [y, y, y]
    if has_cache:
        in_specs += [pl.BlockSpec((None, None, past, w), lambda b, h, i: (b, l, 0, h)),
                     pl.BlockSpec((None, None, past, w), lambda b, h, i: (b, l, 0, h))]
        args += [cache_k, cache_v]
    in_specs += [pl.BlockSpec((None, 4, HEAD_DIM), lambda b, h, i: (l, 0, 0)),
                 pl.BlockSpec((None, 1, V_DIM), lambda b, h, i: (l, 0, 0))]
    args += [lam_param, subln_w3]
    return pl.pallas_call(
        functools.partial(_attn_kernel, has_cache=has_cache, lam_init=lam_init, past=past, heads=heads),
        grid=(nseq, hg, nq),
        in_specs=in_specs,
        out_specs=pl.BlockSpec((tq, w), lambda b, h, i: (b * nq + i, h)),
        out_shape=jax.ShapeDtypeStruct((nseq * seq, N_HEADS * V_DIM), f32),
        scratch_shapes=[pltpu.VMEM((heads, V_DIM, lk), bf16), pltpu.VMEM((heads, lk, 2 * V_DIM), bf16)],
        compiler_params=_cparams(3),
        name="diff_attn_lat" if has_cache else "diff_attn_ctx",
    )(*args)


def _lru_kernel(xr_ref, yr_ref, cw_ref, cb_ref, wg_ref, bg_ref, lam_ref, h0_ref,
                rec_ref, hfin_ref, xp_ref, a_ref, b_ref, *, chunk):
    seq, c = xr_ref.shape
    xp_ref[0:8, :] = jnp.zeros((8, c), f32)
    xp_ref[8:8 + seq, :] = xr_ref[...]
    xp_ref[8 + seq:16 + seq, :] = jnp.zeros((8, c), f32)
    lam = lam_ref[...]
    nlam = -lam
    softplus = jnp.maximum(nlam, 0.0) + jnp.log(1.0 + jnp.exp(-jnp.abs(nlam)))
    for ci in range(seq // chunk):
        t0 = ci * chunk
        xc = cb_ref[...] + sum(cw_ref[j:j + 1, :] * xp_ref[8 + t0 + j - 1:8 + t0 + j - 1 + chunk, :]
                               for j in range(CONV_WIDTH))
        xcb = xc.astype(bf16)
        for d in range(2):
            r = _sigmoid(jnp.dot(xcb, wg_ref[2 * d], preferred_element_type=f32) + bg_ref[2 * d:2 * d + 1, :])
            g = _sigmoid(jnp.dot(xcb, wg_ref[2 * d + 1], preferred_element_type=f32)
                         + bg_ref[2 * d + 1:2 * d + 2, :])
            log_a = -LRU_C * r * softplus[d:d + 1, :]
            a = jnp.exp(log_a)
            a_ref[d, t0:t0 + chunk, :] = a
            u = -jnp.tanh(log_a) * (a * a + 1.0)
            b_ref[d, t0:t0 + chunk, :] = jnp.where(u > 0.0, u * lax.rsqrt(u), 0.0) * (g * xc)

    rows = lax.broadcasted_iota(i32, (8, c), 0)
    nch = seq // 8

    def scan_body(i, carry):
        cf, cb = carry
        tf = pl.multiple_of(8 * i, 8)
        af = a_ref[0, pl.ds(tf, 8), :]
        bf = b_ref[0, pl.ds(tf, 8), :]
        for d in (1, 2, 4):
            keep = rows >= d
            bf = af * jnp.where(keep, pltpu.roll(bf, d, 0), 0.0) + bf
            af = af * jnp.where(keep, pltpu.roll(af, d, 0), 1.0)
        hf = af * cf + bf
        rec_ref[pl.ds(tf, 8), :] = hf
        tb = pl.multiple_of(8 * (nch - 1 - i), 8)
        ab = a_ref[1, pl.ds(tb, 8), :]
        bb = b_ref[1, pl.ds(tb, 8), :]
        for d in (1, 2, 4):
            keep = rows < 8 - d
            bb = ab * jnp.where(keep, pltpu.roll(bb, 8 - d, 0), 0.0) + bb
            ab = ab * jnp.where(keep, pltpu.roll(ab, 8 - d, 0), 1.0)
        hb = ab * cb + bb
        xp_ref[pl.ds(tb, 8), :] = hb
        return hf[7:8, :], hb[0:1, :]

    h0 = h0_ref[...]
    cf, cb = lax.fori_loop(0, nch, scan_body, (h0[0:1, :], h0[1:2, :]), unroll=SCAN_UNROLL)
    hfin_ref[0:1, :] = cf
    hfin_ref[1:2, :] = cb
    for ci in range(seq // chunk):
        t0 = ci * chunk
        rec_ref[t0:t0 + chunk, :] = ((rec_ref[t0:t0 + chunk, :] + xp_ref[t0:t0 + chunk, :])
                                     * jax.nn.gelu(yr_ref[t0:t0 + chunk, :]))


def _lru(y, conv_w, conv_b3, wg, bg, lru_lambda, h0, l, row0, nseq, seq, xr_col, yr_col):
    d_lru = conv_w.shape[-1]
    cb = 256
    ncb = d_lru // cb
    assert row0 % seq == 0
    rb0 = row0 // seq
    h0_batched = h0.shape[0] > 1
    in_specs = [pl.BlockSpec((seq, cb), lambda b, c: (rb0 + b, xr_col // cb + c)),
                pl.BlockSpec((seq, cb), lambda b, c: (rb0 + b, yr_col // cb + c)),
                pl.BlockSpec((None, CONV_WIDTH, cb), lambda b, c: (l, 0, c)),
                pl.BlockSpec((None, 1, cb), lambda b, c: (l, 0, c)),
                pl.BlockSpec((None, None, 4, cb, cb), lambda b, c: (l, c, 0, 0, 0)),
                pl.BlockSpec((None, 4, cb), lambda b, c: (l, 0, c)),
                pl.BlockSpec((None, 2, cb), lambda b, c: (l, 0, c)),
                pl.BlockSpec((None, 2, cb), (lambda b, c: (b, 0, c)) if h0_batched else (lambda b, c: (0, 0, c)))]
    return pl.pallas_call(
        functools.partial(_lru_kernel, chunk=_tile(seq, 512)),
        grid=(nseq, ncb),
        in_specs=in_specs,
        out_specs=[pl.BlockSpec((seq, cb), lambda b, c: (b, c)),
                   pl.BlockSpec((None, 2, cb), lambda b, c: (b, 0, c))],
        out_shape=[jax.ShapeDtypeStruct((nseq * seq, d_lru), f32),
                   jax.ShapeDtypeStruct((nseq, 2, d_lru), f32)],
        scratch_shapes=[pltpu.VMEM((seq + 16, cb), f32), pltpu.VMEM((2, seq, cb), f32),
                        pltpu.VMEM((2, seq, cb), f32)],
        compiler_params=_cparams(2),
        name="rglru",
    )(y, y, conv_w, conv_b3, wg, bg, lru_lambda, h0)


def _merge_kernel(atc_ref, atl_ref, rcc_ref, rcl_ref, ga_ref, gl_ref, x_ref, gt_ref, shf_ref, scf_ref,
                  g_ref, b_ref, wa_ref, wl_ref, wo_ref, x1_ref, h2_ref, ab_ref, rb_ref, *, alpha, n_ctx_tiles):
    @pl.when(pl.program_id(0) < n_ctx_tiles)
    def _():
        ab_ref[...] = atc_ref[...].astype(bf16)
        rb_ref[...] = rcc_ref[...].astype(bf16)

    @pl.when(pl.program_id(0) >= n_ctx_tiles)
    def _():
        ab_ref[...] = atl_ref[...].astype(bf16)
        rb_ref[...] = rcl_ref[...].astype(bf16)

    pa = jnp.dot(ab_ref[...], wa_ref[...], preferred_element_type=f32)
    pr = jnp.dot(rb_ref[...], wl_ref[...], preferred_element_type=f32)
    merged = jax.nn.sigmoid(ga_ref[...]) * pa + jax.nn.sigmoid(gl_ref[...]) * pr
    out = jnp.dot(merged.astype(bf16), wo_ref[...], preferred_element_type=f32)
    x1 = _layer_norm(alpha * x_ref[...] + gt_ref[0] * out, g_ref[0], b_ref[0])
    x1_ref[...] = x1
    h2_ref[...] = x1 * (1.0 + scf_ref[0]) + shf_ref[0]


def _merge(attn_c, attn_l, rec_c, rec_l, y, x, mod4, ln_g3, ln_b3, wa, wl, wo, l, n_ctx, dec_seq, ga_col, gl_col,
           alpha):
    n, d = x.shape
    tm = _tile(math.gcd(n_ctx, dec_seq), 512)
    n_ctx_tiles = n_ctx // tm
    tiles_per_seq = dec_seq // tm

    def seg(i):
        return jnp.where(i < n_ctx_tiles, 0, 1 + (i - n_ctx_tiles) // tiles_per_seq)

    base = l * MOD_ROWS * 6
    row = lambda i: (i, 0)
    ctx_row = lambda i: (jnp.minimum(i, n_ctx_tiles - 1), 0)
    lat_row = lambda i: (jnp.maximum(i - n_ctx_tiles, 0), 0)
    mod = lambda k: pl.BlockSpec((1, 1, d), lambda i: (base + seg(i) * 6 + k, 0, 0))
    wspec = pl.BlockSpec((None, d, d), lambda i: (l, 0, 0))
    return pl.pallas_call(
        functools.partial(_merge_kernel, alpha=alpha, n_ctx_tiles=n_ctx_tiles),
        grid=(n // tm,),
        in_specs=[pl.BlockSpec((tm, d), ctx_row), pl.BlockSpec((tm, d), lat_row),
                  pl.BlockSpec((tm, d), ctx_row), pl.BlockSpec((tm, d), lat_row),
                  pl.BlockSpec((tm, d), lambda i: (i, ga_col // d)),
                  pl.BlockSpec((tm, d), lambda i: (i, gl_col // d)),
                  pl.BlockSpec((tm, d), row), mod(2), mod(3), mod(4),
                  pl.BlockSpec((1, 1, d), lambda i: (2 * l, 0, 0)),
                  pl.BlockSpec((1, 1, d), lambda i: (2 * l, 0, 0)),
                  wspec, wspec, wspec],
        out_specs=[pl.BlockSpec((tm, d), row), pl.BlockSpec((tm, d), row)],
        out_shape=[jax.ShapeDtypeStruct((n, d), f32), jax.ShapeDtypeStruct((n, d), f32)],
        scratch_shapes=[pltpu.VMEM((tm, d), bf16), pltpu.VMEM((tm, d), bf16)],
        compiler_params=_cparams(1),
        name="merge_outproj_ln",
    )(attn_c, attn_l, rec_c, rec_l, y, y, x, mod4, mod4, mod4, ln_g3, ln_b3, wa, wl, wo)


def _route_kernel(h_ref, rw_ref, rb_ref, pos_ref, gate_ref, tab_ref, carry_ref):
    n_exp = rw_ref.shape[0]
    tm = h_ref.shape[0]
    gsz = n_exp // N_GROUPS

    @pl.when(pl.program_id(0) == 0)
    def _():
        carry_ref[...] = jnp.zeros_like(carry_ref)

    logits = lax.dot_general(rw_ref[...], h_ref[...], (((1,), (1,)), ((), ())),
                             precision=HIGHEST, preferred_element_type=f32)
    s = jax.nn.sigmoid(logits)
    sb = s + rb_ref[...]
    ninf = -jnp.inf
    gs_rows = []
    for g in range(N_GROUPS):
        blk = sb[g * gsz:(g + 1) * gsz]
        m1 = jnp.max(blk, axis=0, keepdims=True)
        cnt = jnp.sum((blk == m1).astype(f32), axis=0, keepdims=True)
        m2 = jnp.max(jnp.where(blk < m1, blk, ninf), axis=0, keepdims=True)
        gs_rows.append(m1 + jnp.where(cnt >= 2.0, m1, m2))
    gs = jnp.concatenate(gs_rows, axis=0)
    gi = lax.broadcasted_iota(i32, gs.shape, 0)
    grank = jnp.zeros(gs.shape, f32)
    for g2 in range(N_GROUPS):
        row = gs[g2:g2 + 1]
        grank += jnp.where((row > gs) | ((row == gs) & (g2 < gi)), 1.0, 0.0)
    gsel = grank < float(TOPK_GROUPS)
    sbm = jnp.concatenate([jnp.where(gsel[g:g + 1], sb[g * gsz:(g + 1) * gsz], ninf)
                           for g in range(N_GROUPS)], axis=0)
    ei = lax.broadcasted_iota(i32, sbm.shape, 0)
    erank = jnp.zeros(sbm.shape, f32)
    for e2 in range(n_exp):
        row = sbm[e2:e2 + 1]
        erank += jnp.where((row > sbm) | ((row == sbm) & (e2 < ei)), 1.0, 0.0)
    sel = erank < float(TOP_K)
    ssel = jnp.where(sel, s, 0.0)
    gate = ssel / jnp.sum(ssel, axis=0, keepdims=True) * ROUTE_SCALE
    selb = sel.astype(bf16)
    ti = lax.broadcasted_iota(i32, (tm, tm), 0)
    tj = lax.broadcasted_iota(i32, (tm, tm), 1)
    upper = (ti < tj).astype(bf16)
    pos_local = jnp.dot(selb, upper, preferred_element_type=f32)
    pos_ref[...] = jnp.where(sel, pos_local, -1.0)
    gate_ref[...] = gate
    cnt = jnp.sum(sel.astype(f32), axis=1, keepdims=True)
    cpad = jnp.floor((cnt + (SEG_ALIGN - 1)) * (1.0 / SEG_ALIGN)) * SEG_ALIGN
    cpad_l = jnp.broadcast_to(cpad, (n_exp, 128))
    li = lax.broadcasted_iota(i32, (n_exp, n_exp), 0)
    lj = lax.broadcasted_iota(i32, (n_exp, n_exp), 1)
    lower = (lj < li).astype(bf16)
    run_start = jnp.dot(lower, cpad_l.astype(bf16), preferred_element_type=f32)
    lane = lax.broadcasted_iota(i32, (n_exp, 128), 1)
    tab = jnp.where(lane == TAB_LEN, cpad_l, jnp.where(lane == TAB_BASE, carry_ref[...], run_start))
    tab_ref[...] = tab.astype(i32)
    carry_ref[...] = carry_ref[...] + cpad_l


def _route(h2, rwt, rb3, l):
    n, d = h2.shape
    n_exp = rwt.shape[1]
    tm = ROUTE_TILE
    assert n % tm == 0
    lst = pl.BlockSpec((n_exp, tm), lambda i: (0, i))
    return pl.pallas_call(
        _route_kernel,
        grid=(n // tm,),
        in_specs=[pl.BlockSpec((tm, d), lambda i: (i, 0)),
                  pl.BlockSpec((None, n_exp, d), lambda i: (l, 0, 0)),
                  pl.BlockSpec((None, n_exp, 1), lambda i: (l, 0, 0))],
        out_specs=[lst, lst, pl.BlockSpec((None, n_exp, 128), lambda i: (i, 0, 0))],
        out_shape=[jax.ShapeDtypeStruct((n_exp, n), f32), jax.ShapeDtypeStruct((n_exp, n), f32),
                   jax.ShapeDtypeStruct((n // tm, n_exp, 128), i32)],
        scratch_shapes=[pltpu.VMEM((n_exp, 128), f32)],
        compiler_params=_cparams(1),
        name="moe_route",
    )(h2, rwt, rb3)


def _pow2_pieces(count, max_rows, fn):
    size = max_rows
    while size >= SEG_ALIGN:
        off = count & (-2 * size)
        pl.when((count & size) != 0)(functools.partial(fn, off, size))
        size //= 2


def _wait_rows(count, max_rows, src_ref, dst_ref, sem):
    def wait(_, size):
        pltpu.make_async_copy(src_ref.at[pl.ds(0, size)], dst_ref.at[pl.ds(0, size)], sem).wait()

    _pow2_pieces(count, max_rows, wait)


def _pad_experts(x):
    return jnp.concatenate([x, jnp.zeros((128 - x.shape[0], x.shape[1]), x.dtype)], axis=0)


def _dispatch_kernel(pd_ref, tot_ref, ts_ref, tl_ref, pos_ref, tab_ref, h_ref, xs_ref, buf_ref, z_ref,
                     sem, *, n_exp, n_tiles, tile_rows, wait_max):
    t = pl.program_id(0)
    slot = t % 2
    tm = h_ref.shape[0]
    pieces = tile_rows // SEG_ALIGN

    @pl.when(t >= 2)
    def _():
        _wait_rows(tot_ref[t - 2], wait_max, buf_ref.at[slot], xs_ref, sem.at[slot])

    tab = _pad_experts(tab_ref[...].astype(f32))
    tab_t = tab.T
    start_row, len_row = tab_t[TAB_START:TAB_START + 1, :], tab_t[TAB_LEN:TAB_LEN + 1, :]
    rid = lax.broadcasted_iota(i32, (tile_rows, 128), 0).astype(f32)
    in_run = ((rid >= start_row) & (rid < start_row + len_row)).astype(bf16)
    pos_rows = jnp.dot(in_run, _pad_experts(pos_ref[...]).astype(bf16), preferred_element_type=f32)
    start16 = (tab[:, TAB_START:TAB_START + 1] * (1.0 / SEG_ALIGN)).astype(bf16)
    run_start = jnp.dot(in_run, jnp.broadcast_to(start16, (128, 128)), preferred_element_type=f32) * SEG_ALIGN
    ordinal = rid - run_start
    ordinal = jnp.concatenate([ordinal] * (tm // 128), axis=1)
    perm = jnp.where(pos_rows == ordinal, 1.0, 0.0).astype(bf16)
    buf_ref[slot] = jnp.dot(perm, h_ref[...].astype(bf16), preferred_element_type=f32).astype(bf16)

    def issue(i, _):
        for u in range(PIECE_UNROLL):
            p = i * PIECE_UNROLL + u
            src = pl.multiple_of(p * SEG_ALIGN, SEG_ALIGN)
            dst = pl.multiple_of(pd_ref[t * pieces + p], SEG_ALIGN)
            pltpu.make_async_copy(buf_ref.at[slot, pl.ds(src, SEG_ALIGN)], xs_ref.at[pl.ds(dst, SEG_ALIGN)],
                                  sem.at[slot]).start()
        return 0

    lax.fori_loop(0, tot_ref[t] // (SEG_ALIGN * PIECE_UNROLL), issue, 0)

    @pl.when(t == n_tiles - 1)
    def _():
        z_ref[...] = jnp.zeros_like(z_ref)

        def fill(e, _):
            def copy(off, size):
                dst = pl.multiple_of(ts_ref[e] + off, SEG_ALIGN)
                pltpu.make_async_copy(z_ref.at[pl.ds(0, size)], xs_ref.at[pl.ds(dst, size)], sem.at[2]).start()

            _pow2_pieces(tl_ref[e], MOE_BLOCK // 2, copy)
            return 0

        lax.fori_loop(0, n_exp, fill, 0)

        def drain(e, _):
            _wait_rows(tl_ref[e], MOE_BLOCK // 2, z_ref, xs_ref, sem.at[2])
            return 0

        lax.fori_loop(0, n_exp, drain, 0)
        if n_tiles > 1:
            _wait_rows(tot_ref[t - 1], wait_max, buf_ref.at[1 - slot], xs_ref, sem.at[1 - slot])
        _wait_rows(tot_ref[t], wait_max, buf_ref.at[slot], xs_ref, sem.at[slot])


def _tile_rows(tm, n_exp):
    return -(-(tm * TOP_K + n_exp * (SEG_ALIGN - 1)) // 128) * 128


def _wait_max(tile_rows):
    return 1 << (tile_rows.bit_length() - 1)


def _dispatch(h2, pos, tab, piece_dst, tot, tail_start, tail_len, n_rows, n_exp):
    n, d = h2.shape
    tm = ROUTE_TILE
    n_tiles = n // tm
    tile_rows = _tile_rows(tm, n_exp)
    assert n_exp <= 128 and tm % 128 == 0
    return pl.pallas_call(
        functools.partial(_dispatch_kernel, n_exp=n_exp, n_tiles=n_tiles, tile_rows=tile_rows,
                          wait_max=_wait_max(tile_rows)),
        grid_spec=pltpu.PrefetchScalarGridSpec(
            num_scalar_prefetch=4,
            grid=(n_tiles,),
            in_specs=[pl.BlockSpec((n_exp, tm), lambda i, *_: (0, i)),
                      pl.BlockSpec((None, n_exp, 128), lambda i, *_: (i, 0, 0)),
                      pl.BlockSpec((tm, d), lambda i, *_: (i, 0))],
            out_specs=pl.BlockSpec(memory_space=pl.ANY),
            scratch_shapes=[pltpu.VMEM((2, tile_rows, d), bf16), pltpu.VMEM((MOE_BLOCK // 2, d), bf16),
                            pltpu.SemaphoreType.DMA((3,))]),
        out_shape=jax.ShapeDtypeStruct((n_rows, d), bf16),
        compiler_params=_cparams(1),
        name="moe_dispatch",
    )(piece_dst, tot, tail_start, tail_len, pos, tab, h2)


def _expert_kernel(be_ref, nu_ref, x_ref, wg_ref, wu_ref, wd_ref, o_ref):
    @pl.when(pl.program_id(0) < nu_ref[0])
    def _():
        x = x_ref[...]
        g = jnp.dot(x, wg_ref[...].astype(bf16), preferred_element_type=f32)
        u = jnp.dot(x, wu_ref[...].astype(bf16), preferred_element_type=f32)
        h = (g * jax.nn.sigmoid(g)) * u
        o_ref[...] = jnp.dot(h.astype(bf16), wd_ref[...].astype(bf16),
                             preferred_element_type=f32).astype(bf16)


def _experts(xs, blk_e, n_used, we_g, we_u, we_d, l):
    n_rows, d = xs.shape
    f = we_g.shape[-1]
    nblk = n_rows // MOE_BLOCK
    blk = lambda b, be, nu: (jnp.minimum(b, nu[0] - 1), 0)
    wsel = lambda b, be, nu: (l, be[jnp.minimum(b, nu[0] - 1)], 0, 0)
    return pl.pallas_call(
        _expert_kernel,
        grid_spec=pltpu.PrefetchScalarGridSpec(
            num_scalar_prefetch=2,
            grid=(nblk,),
            in_specs=[pl.BlockSpec((MOE_BLOCK, d), blk),
                      pl.BlockSpec((None, None, d, f), wsel),
                      pl.BlockSpec((None, None, d, f), wsel),
                      pl.BlockSpec((None, None, f, d), wsel)],
            out_specs=pl.BlockSpec((MOE_BLOCK, d), blk)),
        out_shape=jax.ShapeDtypeStruct((n_rows, d), bf16),
        compiler_params=_cparams(1),
        name="moe_experts",
    )(blk_e, n_used, xs, we_g, we_u, we_d)


def _combine_kernel(pd_ref, tot_ref, pos_ref, gate_ref, tab_ref, ys_ref, x1_ref, h2_ref, gt_ref, g_ref, b_ref,
                    sg_ref, su_ref, sd_ref, o_ref, buf_ref, sem, *, alpha, n_exp, n_tiles, tile_rows, wait_max):
    t = pl.program_id(0)
    slot = t % 2
    tm = x1_ref.shape[0]
    pieces = tile_rows // SEG_ALIGN

    def fetch(tile, dst_slot):
        def issue(i, _):
            for u in range(PIECE_UNROLL):
                p = i * PIECE_UNROLL + u
                src = pl.multiple_of(pd_ref[tile * pieces + p], SEG_ALIGN)
                dst = pl.multiple_of(p * SEG_ALIGN, SEG_ALIGN)
                pltpu.make_async_copy(ys_ref.at[pl.ds(src, SEG_ALIGN)], buf_ref.at[dst_slot, pl.ds(dst, SEG_ALIGN)],
                                      sem.at[dst_slot]).start()
            return 0

        lax.fori_loop(0, tot_ref[tile] // (SEG_ALIGN * PIECE_UNROLL), issue, 0)

    @pl.when(t == 0)
    def _():
        buf_ref[...] = jnp.zeros_like(buf_ref)
        fetch(0, 0)

    @pl.when(t + 1 < n_tiles)
    def _():
        fetch(t + 1, 1 - slot)

    hb = h2_ref[...].astype(bf16)
    g = jnp.dot(hb, sg_ref[...], preferred_element_type=f32)
    u = jnp.dot(hb, su_ref[...], preferred_element_type=f32)
    shared = jnp.dot(((g * jax.nn.sigmoid(g)) * u).astype(bf16), sd_ref[...], preferred_element_type=f32)
    tab = _pad_experts(tab_ref[...].astype(f32))
    start_col, len_col = tab[:, TAB_START:TAB_START + 1], tab[:, TAB_LEN:TAB_LEN + 1]
    rid = lax.broadcasted_iota(i32, (128, tile_rows), 1).astype(f32)
    in_run = (rid >= start_col) & (rid < start_col + len_col)
    ordinal = rid[0:1, :] - jnp.sum(jnp.where(in_run, start_col, 0.0), axis=0, keepdims=True)
    in_run = in_run.astype(bf16)
    pos_t = _pad_experts(pos_ref[...]).T.astype(bf16)
    gate_t = _pad_experts(gate_ref[...]).T.astype(bf16)
    pos_rows = jnp.dot(pos_t, in_run, preferred_element_type=f32)
    gate_rows = jnp.dot(gate_t, in_run, preferred_element_type=f32)
    wgt = jnp.where(pos_rows == ordinal, gate_rows, 0.0).astype(bf16)
    _wait_rows(tot_ref[t], wait_max, ys_ref, buf_ref.at[slot], sem.at[slot])
    routed = jnp.dot(wgt, buf_ref[slot], preferred_element_type=f32)
    o_ref[...] = _layer_norm(alpha * x1_ref[...] + gt_ref[0] * (shared + routed), g_ref[0], b_ref[0])


def _combine(ys, pos, gate, tab, piece_dst, tot, x1, h2, mod4, ln_g3, ln_b3, sg, su, sd, l, n_ctx, dec_seq, alpha,
             n_exp):
    n, d = x1.shape
    f = sg.shape[-1]
    tm = ROUTE_TILE
    assert n_ctx % tm == 0 and dec_seq % tm == 0
    n_tiles = n // tm
    tile_rows = _tile_rows(tm, n_exp)
    n_ctx_tiles = n_ctx // tm
    tiles_per_seq = dec_seq // tm

    def seg(i):
        return jnp.where(i < n_ctx_tiles, 0, 1 + (i - n_ctx_tiles) // tiles_per_seq)

    base = l * MOD_ROWS * 6
    row = lambda i, *_: (i, 0)
    lst = lambda i, *_: (0, i)
    return pl.pallas_call(
        functools.partial(_combine_kernel, alpha=alpha, n_exp=n_exp, n_tiles=n_tiles, tile_rows=tile_rows,
                          wait_max=_wait_max(tile_rows)),
        grid_spec=pltpu.PrefetchScalarGridSpec(
            num_scalar_prefetch=2,
            grid=(n_tiles,),
            in_specs=[pl.BlockSpec((n_exp, tm), lst), pl.BlockSpec((n_exp, tm), lst),
                      pl.BlockSpec((None, n_exp, 128), lambda i, *_: (i, 0, 0)),
                      pl.BlockSpec(memory_space=pl.ANY),
                      pl.BlockSpec((tm, d), row), pl.BlockSpec((tm, d), row),
                      pl.BlockSpec((1, 1, d), lambda i, *_: (base + seg(i) * 6 + 5, 0, 0)),
                      pl.BlockSpec((1, 1, d), lambda i, *_: (2 * l + 1, 0, 0)),
                      pl.BlockSpec((1, 1, d), lambda i, *_: (2 * l + 1, 0, 0)),
                      pl.BlockSpec((None, d, f), lambda i, *_: (l, 0, 0)),
                      pl.BlockSpec((None, d, f), lambda i, *_: (l, 0, 0)),
                      pl.BlockSpec((None, f, d), lambda i, *_: (l, 0, 0))],
            out_specs=pl.BlockSpec((tm, d), row),
            scratch_shapes=[pltpu.VMEM((2, tile_rows, d), bf16), pltpu.SemaphoreType.DMA((2,))]),
        out_shape=jax.ShapeDtypeStruct((n, d), f32),
        compiler_params=_cparams(1),
        name="moe_combine_ln",
    )(piece_dst, tot, pos, gate, tab, ys, x1, h2, mod4, ln_g3, ln_b3, sg, su, sd)


def _rope_tables(seq, width):
    rows = seq // GRID_W
    row = jnp.repeat(jnp.arange(rows, dtype=f32), GRID_W)
    col = jnp.tile(jnp.arange(GRID_W, dtype=f32), rows)
    half = HEAD_DIM // 2
    inv = ROPE_BASE ** (-jnp.arange(0, half, 2, dtype=f32) / half)
    lane = jnp.arange(width)
    d = lane % HEAD_DIM
    pos = jnp.where((d // half)[None, :] == 0, row[:, None], col[:, None])
    ang = pos * inv[d % (HEAD_DIM // 4)][None, :]
    sign = jnp.where((d % half) < HEAD_DIM // 4, -1.0, 1.0)[None, :]
    return jnp.cos(ang), jnp.sin(ang) * sign


def _block_diag_gates(gate_w_r, gate_w_i, cb):
    depth, _, nb, bw, _ = gate_w_r.shape
    per = cb // bw
    w = jnp.stack([gate_w_r[:, 0], gate_w_i[:, 0], gate_w_r[:, 1], gate_w_i[:, 1]], axis=1)
    w = w.reshape(depth, 4, nb // per, per, bw, bw)
    eye = jnp.eye(per, dtype=w.dtype)
    bd = jnp.einsum("lgcaio,ab->lcgaibo", w, eye)
    return bd.reshape(depth, nb // per, 4, cb, cb).astype(bf16)


def kernel(x_prompt, x_sample, c, cache_k, cache_v, state_lru, c_ctx, w_mod, b_mod, w_in, lam_param, subln_w,
           conv_w, conv_b, gate_w_r, gate_b_r, gate_w_i, gate_b_i, lru_lambda, w_br_attn, w_br_lru, w_out,
           ln_g, ln_b, router_w, router_b, exp_w_gate, exp_w_up, exp_w_down, sh_w_gate, sh_w_up, sh_w_down):
    batch, seq, d = x_prompt.shape
    dec_batch, dec_seq, _ = x_sample.shape
    depth = w_mod.shape[0]
    past = cache_k.shape[2]
    d_lru = conv_w.shape[-1]
    n_exp = router_w.shape[-1]
    qk = N_HEADS * 2 * HEAD_DIM
    attn_w = N_HEADS * V_DIM
    n_ctx, n_lat = batch * seq, dec_batch * dec_seq
    n = n_ctx + n_lat
    alpha = (2 * depth) ** 0.25
    assert 1 + dec_batch <= MOD_ROWS and n_ctx % dec_seq == 0

    cond = jnp.concatenate([c_ctx[None], c, jnp.zeros((MOD_ROWS - 1 - dec_batch, d), f32)], 0)
    w_in_b = w_in.astype(bf16)
    wa, wl, wo = w_br_attn.astype(bf16), w_br_lru.astype(bf16), w_out.astype(bf16)
    sg, su, sd = sh_w_gate.astype(bf16), sh_w_up.astype(bf16), sh_w_down.astype(bf16)
    rwt = jnp.swapaxes(router_w, 1, 2)
    rb3 = router_b.reshape(depth, n_exp, 1)
    wg = _block_diag_gates(gate_w_r, gate_w_i, 256)
    bg = jnp.stack([gate_b_r[:, 0], gate_b_i[:, 0], gate_b_r[:, 1], gate_b_i[:, 1]], axis=1)
    conv_b3 = conv_b.reshape(depth, 1, d_lru)
    subln_w3 = subln_w.reshape(depth, 1, V_DIM)
    ln_g3 = ln_g.reshape(depth * 2, 1, d)
    ln_b3 = ln_b.reshape(depth * 2, 1, d)
    ck = cache_k.reshape(dec_batch, depth, past, N_HEADS * 2 * HEAD_DIM)
    cv = cache_v.reshape(dec_batch, depth, past, attn_w)
    cos_t, sin_t = _rope_tables(dec_seq, V_DIM)
    h0_ctx = jnp.zeros((1, 2, d_lru), f32)

    mod4 = _mod_table(cond, w_mod, b_mod).reshape(depth * MOD_ROWS * 6, 1, d)
    x = jnp.concatenate([x_prompt.reshape(n_ctx, d), x_sample.reshape(n_lat, d)], 0)
    n_tiles = n // ROUTE_TILE
    spare0 = n * TOP_K + n_tiles * n_exp * (SEG_ALIGN - 1) + n_exp * (MOE_BLOCK - 1)
    spare0 = -(-spare0 // SEG_ALIGN) * SEG_ALIGN
    n_rows = -(-(spare0 + n_tiles * (PIECE_UNROLL - 1) * SEG_ALIGN) // MOE_BLOCK) * MOE_BLOCK
    xr_col, yr_col = 2 * qk + attn_w, 2 * qk + attn_w + d_lru
    ga_col, gl_col = yr_col + d_lru, yr_col + d_lru + d

    new_k, new_v, new_s = [], [], []
    for l in range(depth):
        lam_init = 0.8 - 0.6 * math.exp(-0.3 * l)
        y = _inproj(x, mod4, w_in_b, cos_t, sin_t, l, n_ctx, dec_seq, 2 * qk)
        new_k.append(y[:n_ctx, qk:2 * qk].reshape(batch, seq, N_HEADS, 2 * HEAD_DIM))
        new_v.append(y[:n_ctx, 2 * qk:2 * qk + attn_w].reshape(batch, seq, N_HEADS, V_DIM))
        attn_c = _attention(y, lam_param, subln_w3, l, lam_init, 0, batch, seq)
        attn_l = _attention(y, lam_param, subln_w3, l, lam_init, n_ctx, dec_batch, dec_seq, ck, cv)
        rec_c, hfin = _lru(y, conv_w, conv_b3, wg, bg, lru_lambda, h0_ctx, l, 0, batch, seq, xr_col, yr_col)
        rec_l, _ = _lru(y, conv_w, conv_b3, wg, bg, lru_lambda, state_lru[:, l], l, n_ctx, dec_batch, dec_seq,
                        xr_col, yr_col)
        new_s.append(hfin)
        x1, h2 = _merge(attn_c, attn_l, rec_c, rec_l, y, x, mod4, ln_g3, ln_b3, wa, wl, wo, l, n_ctx, dec_seq, ga_col, gl_col, alpha)
        pos, gate, tab = _route(h2, rwt, rb3, l)
        cpad, grel, start = tab[:, :, TAB_LEN], tab[:, :, TAB_BASE], tab[:, :, TAB_START]
        used = grel[-1] + cpad[-1]
        padded = (used + MOE_BLOCK - 1) // MOE_BLOCK * MOE_BLOCK
        pad_end = jnp.cumsum(padded)
        pad_start = pad_end - padded
        tot = jnp.sum(cpad, axis=1).astype(i32)
        blk_row0 = jnp.arange(n_rows // MOE_BLOCK, dtype=i32) * MOE_BLOCK
        blk_e = jnp.minimum(jnp.sum(pad_end[None, :] <= blk_row0[:, None], axis=1), n_exp - 1).astype(i32)
        n_used = (pad_end[-1:] // MOE_BLOCK).astype(i32)
        piece0 = jnp.arange(_tile_rows(ROUTE_TILE, n_exp) // SEG_ALIGN, dtype=i32)[None, :, None] * SEG_ALIGN
        in_run = (piece0 >= start[:, None, :]) & (piece0 < (start + cpad)[:, None, :])
        shift = (pad_start[None, :] + grel - start)[:, None, :]
        piece0 = piece0[:, :, 0]
        piece_row = piece0 + jnp.sum(jnp.where(in_run, shift, 0), axis=-1)
        real = piece0 < tot[:, None]
        spare = (spare0 + jnp.arange(n_tiles, dtype=i32)[:, None] * ((PIECE_UNROLL - 1) * SEG_ALIGN)
                 + piece0 - tot[:, None])
        piece_dst = jnp.where(real, piece_row, spare).reshape(-1).astype(i32)
        piece_src = jnp.where(real, piece_row, 0).reshape(-1).astype(i32)
        step = SEG_ALIGN * PIECE_UNROLL
        tot = (tot + step - 1) // step * step
        xs = _dispatch(h2, pos, tab, piece_dst, tot, (pad_start + used).astype(i32),
                       (padded - used).astype(i32), n_rows, n_exp)
        ys = _experts(xs, blk_e, n_used, exp_w_gate, exp_w_up, exp_w_down, l)
        x = _combine(ys, pos, gate, tab, piece_src, tot, x1, h2, mod4, ln_g3, ln_b3, sg, su, sd, l, n_ctx, dec_seq,
                     alpha, n_exp)
    return (x[:n_ctx].reshape(batch, seq, d), x[n_ctx:].reshape(dec_batch, dec_seq, d),
            jnp.stack(new_k, 1), jnp.stack(new_v, 1), jnp.stack(new_s, 1))
```

```python
import functools
import math

import jax
import jax.numpy as jnp
from jax import lax
from jax.experimental import pallas as pl
from jax.experimental.pallas import tpu as pltpu

f32, bf16, i32 = jnp.float32, jnp.bfloat16, jnp.int32

GRID_W = 64
N_HEADS = 8
HEAD_DIM = 64
V_DIM = 2 * HEAD_DIM
ROPE_BASE = 10000.0
LRU_BLOCKS = 16
CONV_WIDTH = 4
LRU_C = 8.0
TOP_K = 8
N_GROUPS = 8
TOPK_GROUPS = 4
ROUTE_SCALE = 2.5
LN_EPS = 1e-5
MOD_ROWS = 8
MOE_BLOCK = 1024
ROUTE_TILE = 256
SEG_ALIGN = 16
PIECE_UNROLL = 4
ROPE_CHUNK = 256
TAB_LEN, TAB_BASE, TAB_START = 0, 1, 2
SCAN_UNROLL = 4
VMEM_LIMIT = 56 * 1024 * 1024
HIGHEST = lax.Precision.HIGHEST


def _cparams(n_axes):
    return pltpu.CompilerParams(dimension_semantics=("arbitrary",) * n_axes,
                                vmem_limit_bytes=VMEM_LIMIT)


def _tile(n, pref):
    t = min(n, pref)
    while n % t:
        t -= 8
    assert t > 0 and n % t == 0
    return t


def _layer_norm(z, g, b):
    mu = jnp.mean(z, -1, keepdims=True)
    zc = z - mu
    var = jnp.mean(zc * zc, -1, keepdims=True)
    return zc * lax.rsqrt(var + LN_EPS) * g + b


def _mod_kernel(c_ref, w_ref, b_ref, o_ref):
    c = c_ref[...]
    a = c * jax.nn.sigmoid(c)
    o_ref[...] = jnp.dot(a, w_ref[...], precision=HIGHEST, preferred_element_type=f32) + b_ref[...]


def _mod_table(cond, w_mod, b_mod):
    depth, d, d6 = w_mod.shape
    tn = _tile(d6, 1024)
    return pl.pallas_call(
        _mod_kernel,
        grid=(depth, d6 // tn),
        in_specs=[pl.BlockSpec((MOD_ROWS, d), lambda l, j: (0, 0)),
                  pl.BlockSpec((None, d, tn), lambda l, j: (l, 0, j)),
                  pl.BlockSpec((None, 1, tn), lambda l, j: (l, 0, j))],
        out_specs=pl.BlockSpec((None, MOD_ROWS, tn), lambda l, j: (l, 0, j)),
        out_shape=jax.ShapeDtypeStruct((depth, MOD_ROWS, d6), f32),
        compiler_params=_cparams(2),
        name="mod_table",
    )(cond, w_mod, b_mod.reshape(depth, 1, d6))


def _inproj_kernel(x_ref, sh_ref, sc_ref, w_ref, cos_ref, sin_ref, o_ref, hb_ref, *,
                   n_ctx_tiles, n_rope_cols, tn):
    i = pl.program_id(0)
    j = pl.program_id(1)

    @pl.when(j == 0)
    def _():
        hb_ref[...] = (x_ref[...] * (1.0 + sc_ref[0]) + sh_ref[0]).astype(bf16)

    rope = jnp.logical_and(i >= n_ctx_tiles, j < n_rope_cols)

    @pl.when(rope)
    def _():
        cos, sin = cos_ref[...], sin_ref[...]
        lane = lax.broadcasted_iota(i32, cos.shape, 1)
        first = (lane % (HEAD_DIM // 2)) < (HEAD_DIM // 4)
        for c in range(tn // ROPE_CHUNK):
            acc = jnp.dot(hb_ref[...], w_ref[:, c * ROPE_CHUNK:(c + 1) * ROPE_CHUNK], preferred_element_type=f32)
            for h in range(ROPE_CHUNK // V_DIM):
                a = acc[:, h * V_DIM:(h + 1) * V_DIM]
                partner = jnp.where(first, pltpu.roll(a, V_DIM - HEAD_DIM // 4, 1), pltpu.roll(a, HEAD_DIM // 4, 1))
                col = c * ROPE_CHUNK + h * V_DIM
                o_ref[:, col:col + V_DIM] = a * cos + partner * sin

    @pl.when(jnp.logical_not(rope))
    def _():
        o_ref[...] = jnp.dot(hb_ref[...], w_ref[...], preferred_element_type=f32)


def _inproj(x, mod4, w_in_b, cos_t, sin_t, l, n_ctx, dec_seq, qk_cols):
    n, d = x.shape
    d_in = w_in_b.shape[-1]
    tm = _tile(math.gcd(n_ctx, dec_seq), 1024)
    tn = _tile(math.gcd(d_in, qk_cols), 1024)
    assert tn % V_DIM == 0
    n_ctx_tiles = n_ctx // tm
    tiles_per_seq = dec_seq // tm

    def seg(i):
        return jnp.where(i < n_ctx_tiles, 0, 1 + (i - n_ctx_tiles) // tiles_per_seq)

    def pos(i):
        return jnp.where(i < n_ctx_tiles, 0, (i - n_ctx_tiles) % tiles_per_seq)

    base = l * MOD_ROWS * 6
    return pl.pallas_call(
        functools.partial(_inproj_kernel, n_ctx_tiles=n_ctx_tiles, n_rope_cols=qk_cols // tn, tn=tn),
        grid=(n // tm, d_in // tn),
        in_specs=[pl.BlockSpec((tm, d), lambda i, j: (i, 0)),
                  pl.BlockSpec((1, 1, d), lambda i, j: (base + seg(i) * 6 + 0, 0, 0)),
                  pl.BlockSpec((1, 1, d), lambda i, j: (base + seg(i) * 6 + 1, 0, 0)),
                  pl.BlockSpec((None, d, tn), lambda i, j: (l, 0, j)),
                  pl.BlockSpec((tm, V_DIM), lambda i, j: (pos(i), 0)),
                  pl.BlockSpec((tm, V_DIM), lambda i, j: (pos(i), 0))],
        out_specs=pl.BlockSpec((tm, tn), lambda i, j: (i, j)),
        out_shape=jax.ShapeDtypeStruct((n, d_in), f32),
        scratch_shapes=[pltpu.VMEM((tm, d), bf16)],
        compiler_params=_cparams(2),
        name="in_proj",
    )(x, mod4, mod4, w_in_b, cos_t, sin_t)


def _key_chunk(lk):
    for ck in (512, 384, 256, 128):
        if lk % ck == 0:
            return ck
    return lk


def _attn_kernel(*refs, has_cache, lam_init, past, heads):
    if has_cache:
        q_ref, k_ref, v_ref, ck_ref, cv_ref, lp_ref, sw_ref, o_ref, kt_ref, vb_ref = refs
    else:
        q_ref, k_ref, v_ref, lp_ref, sw_ref, o_ref, kt_ref, vb_ref = refs

    @pl.when(pl.program_id(2) == 0)
    def _():
        for h in range(heads):
            cols = slice(h * V_DIM, (h + 1) * V_DIM)
            if has_cache:
                kt_ref[h, :, :past] = ck_ref[:, cols].T.astype(bf16)
                vb_ref[h, :past, :V_DIM] = cv_ref[:, cols].astype(bf16)
            kt_ref[h, :, past:] = k_ref[:, cols].T.astype(bf16)
            vb_ref[h, past:, :V_DIM] = v_ref[:, cols].astype(bf16)
            vb_ref[h, :, V_DIM:] = jnp.ones((vb_ref.shape[1], V_DIM), bf16)

    lp = lp_ref[...]
    lam = (jnp.exp(jnp.sum(lp[0:1] * lp[1:2], keepdims=True))
           - jnp.exp(jnp.sum(lp[2:3] * lp[3:4], keepdims=True)) + lam_init)
    for h in range(heads):
        cols = slice(h * V_DIM, (h + 1) * V_DIM)
        o = _attn_head(q_ref[:, cols], kt_ref.at[h], vb_ref.at[h], lam)
        o = o * lax.rsqrt(jnp.mean(o * o, -1, keepdims=True) + LN_EPS) * sw_ref[...]
        o_ref[:, cols] = o * (1.0 - lam_init)


def _attn_head(q, kt_ref, vb_ref, lam):
    tq = q.shape[0]
    q = q * (HEAD_DIM ** -0.5 * math.log2(math.e))
    lane = lax.broadcasted_iota(i32, q.shape, 1)
    q2 = jnp.concatenate([jnp.where(lane < HEAD_DIM, q, 0.0).astype(bf16),
                          jnp.where(lane >= HEAD_DIM, q, 0.0).astype(bf16)], axis=0)
    lk = kt_ref.shape[1]
    ck = _key_chunk(lk)
    acc = jnp.zeros((2 * tq, 2 * V_DIM), f32)
    m_run = jnp.full((2 * tq, 1), -jnp.inf, f32)
    for c in range(lk // ck):
        s_c = jnp.dot(q2, kt_ref[:, c * ck:(c + 1) * ck], preferred_element_type=f32)
        m_new = jnp.maximum(m_run, jnp.max(s_c, axis=-1, keepdims=True))
        e_c = jnp.exp2(s_c - m_new).astype(bf16)
        acc = acc * jnp.exp2(m_run - m_new) + jnp.dot(e_c, vb_ref[c * ck:(c + 1) * ck, :],
                                                        preferred_element_type=f32)
        m_run = m_new
    o2 = acc[:, :V_DIM] / acc[:, V_DIM:]
    return o2[:tq] - lam * o2[tq:]


def _attention(y, lam_param, subln_w3, l, lam_init, row0, nseq, seq, cache_k=None, cache_v=None):
    has_cache = cache_k is not None
    past = cache_k.shape[2] if has_cache else 0
    lk = past + seq
    tq = _tile(seq, 1024)
    nq = seq // tq
    heads = N_HEADS if lk <= 512 else 1
    hg = N_HEADS // heads
    w = heads * V_DIM
    assert row0 % seq == 0
    qb0, kb0 = row0 // tq, row0 // seq
    in_specs = [pl.BlockSpec((tq, w), lambda b, h, i: (qb0 + b * nq + i, h)),
                pl.BlockSpec((seq, w), lambda b, h, i: (kb0 + b, hg + h)),
                pl.BlockSpec((seq, w), lambda b, h, i: (kb0 + b, 2 * hg + h))]
    args = [y, y, y]
    if has_cache:
        in_specs += [pl.BlockSpec((None, None, past, w), lambda b, h, i: (b, l, 0, h)),
                     pl.BlockSpec((None, None, past, w), lambda b, h, i: (b, l, 0, h))]
        args += [cache_k, cache_v]
    in_specs += [pl.BlockSpec((None, 4, HEAD_DIM), lambda b, h, i: (l, 0, 0)),
                 pl.BlockSpec((None, 1, V_DIM), lambda b, h, i: (l, 0, 0))]
    args += [lam_param, subln_w3]
    return pl.pallas_call(
        functools.partial(_attn_kernel, has_cache=has_cache, lam_init=lam_init, past=past, heads=heads),
        grid=(nseq, hg, nq),
        in_specs=in_specs,
        out_specs=pl.BlockSpec((tq, w), lambda b, h, i: (b * nq + i, h)),
        out_shape=jax.ShapeDtypeStruct((nseq * seq, N_HEADS * V_DIM), f32),
        scratch_shapes=[pltpu.VMEM((heads, V_DIM, lk), bf16), pltpu.VMEM((heads, lk, 2 * V_DIM), bf16)],
        compiler_params=_cparams(3),
        name="diff_attn_lat" if has_cache else "diff_attn_ctx",
    )(*args)


def _lru_kernel(xr_ref, yr_ref, cw_ref, cb_ref, wg_ref, bg_ref, lam_ref, h0_ref,
                rec_ref, hfin_ref, xp_ref, a_ref, b_ref, *, chunk):
    seq, c = xr_ref.shape
    xp_ref[0:8, :] = jnp.zeros((8, c), f32)
    xp_ref[8:8 + seq, :] = xr_ref[...]
    xp_ref[8 + seq:16 + seq, :] = jnp.zeros((8, c), f32)
    lam = lam_ref[...]
    nlam = -lam
    softplus = jnp.maximum(nlam, 0.0) + jnp.log(1.0 + jnp.exp(-jnp.abs(nlam)))
    kh = (-0.5 * LRU_C) * softplus
    for ci in range(seq // chunk):
        t0 = ci * chunk
        xc = cb_ref[...] + sum(cw_ref[j:j + 1, :] * xp_ref[8 + t0 + j - 1:8 + t0 + j - 1 + chunk, :]
                               for j in range(CONV_WIDTH))
        xcb = xc.astype(bf16)
        hx = 0.5 * xc
        for d in range(2):
            tr = jnp.tanh(jnp.dot(xcb, wg_ref[2 * d], preferred_element_type=f32) + bg_ref[2 * d:2 * d + 1, :])
            ti = jnp.tanh(jnp.dot(xcb, wg_ref[2 * d + 1], preferred_element_type=f32)
                          + bg_ref[2 * d + 1:2 * d + 2, :])
            log_a = tr * kh[d:d + 1, :] + kh[d:d + 1, :]
            a = jnp.exp(log_a)
            a_ref[d, t0:t0 + chunk, :] = a
            u = -jnp.tanh(log_a) * (a * a + 1.0)
            b_ref[d, t0:t0 + chunk, :] = jnp.where(u > 0.0, u * lax.rsqrt(u), 0.0) * (ti * hx + hx)

    rows = lax.broadcasted_iota(i32, (8, c), 0)
    nch = seq // 8

    def scan_body(i, carry):
        cf, cb = carry
        tf = pl.multiple_of(8 * i, 8)
        af = a_ref[0, pl.ds(tf, 8), :]
        bf = b_ref[0, pl.ds(tf, 8), :]
        for d in (1, 2, 4):
            keep = rows >= d
            bf = af * jnp.where(keep, pltpu.roll(bf, d, 0), 0.0) + bf
            af = af * jnp.where(keep, pltpu.roll(af, d, 0), 1.0)
        hf = af * cf + bf
        rec_ref[pl.ds(tf, 8), :] = hf
        tb = pl.multiple_of(8 * (nch - 1 - i), 8)
        ab = a_ref[1, pl.ds(tb, 8), :]
        bb = b_ref[1, pl.ds(tb, 8), :]
        for d in (1, 2, 4):
            keep = rows < 8 - d
            bb = ab * jnp.where(keep, pltpu.roll(bb, 8 - d, 0), 0.0) + bb
            ab = ab * jnp.where(keep, pltpu.roll(ab, 8 - d, 0), 1.0)
        hb = ab * cb + bb
        xp_ref[pl.ds(tb, 8), :] = hb
        return hf[7:8, :], hb[0:1, :]

    h0 = h0_ref[...]
    cf, cb = lax.fori_loop(0, nch, scan_body, (h0[0:1, :], h0[1:2, :]), unroll=SCAN_UNROLL)
    hfin_ref[0:1, :] = cf
    hfin_ref[1:2, :] = cb
    for ci in range(seq // chunk):
        t0 = ci * chunk
        rec_ref[t0:t0 + chunk, :] = ((rec_ref[t0:t0 + chunk, :] + xp_ref[t0:t0 + chunk, :])
                                     * jax.nn.gelu(yr_ref[t0:t0 + chunk, :]))


def _lru(y, conv_w, conv_b3, wg, bg, lru_lambda, h0, l, row0, nseq, seq, xr_col, yr_col):
    d_lru = conv_w.shape[-1]
    cb = 256
    ncb = d_lru // cb
    assert row0 % seq == 0
    rb0 = row0 // seq
    h0_batched = h0.shape[0] > 1
    in_specs = [pl.BlockSpec((seq, cb), lambda b, c: (rb0 + b, xr_col // cb + c)),
                pl.BlockSpec((seq, cb), lambda b, c: (rb0 + b, yr_col // cb + c)),
                pl.BlockSpec((None, CONV_WIDTH, cb), lambda b, c: (l, 0, c)),
                pl.BlockSpec((None, 1, cb), lambda b, c: (l, 0, c)),
                pl.BlockSpec((None, None, 4, cb, cb), lambda b, c: (l, c, 0, 0, 0)),
                pl.BlockSpec((None, 4, cb), lambda b, c: (l, 0, c)),
                pl.BlockSpec((None, 2, cb), lambda b, c: (l, 0, c)),
                pl.BlockSpec((None, 2, cb), (lambda b, c: (b, 0, c)) if h0_batched else (lambda b, c: (0, 0, c)))]
    return pl.pallas_call(
        functools.partial(_lru_kernel, chunk=_tile(seq, 512)),
        grid=(nseq, ncb),
        in_specs=in_specs,
        out_specs=[pl.BlockSpec((seq, cb), lambda b, c: (b, c)),
                   pl.BlockSpec((None, 2, cb), lambda b, c: (b, 0, c))],
        out_shape=[jax.ShapeDtypeStruct((nseq * seq, d_lru), f32),
                   jax.ShapeDtypeStruct((nseq, 2, d_lru), f32)],
        scratch_shapes=[pltpu.VMEM((seq + 16, cb), f32), pltpu.VMEM((2, seq, cb), f32),
                        pltpu.VMEM((2, seq, cb), f32)],
        compiler_params=_cparams(2),
        name="rglru",
    )(y, y, conv_w, conv_b3, wg, bg, lru_lambda, h0)


def _merge_kernel(atc_ref, atl_ref, rcc_ref, rcl_ref, ga_ref, gl_ref, x_ref, gt_ref, shf_ref, scf_ref,
                  g_ref, b_ref, wa_ref, wl_ref, wo_ref, x1_ref, h2_ref, ab_ref, rb_ref, *, alpha, n_ctx_tiles):
    @pl.when(pl.program_id(0) < n_ctx_tiles)
    def _():
        ab_ref[...] = atc_ref[...].astype(bf16)
        rb_ref[...] = rcc_ref[...].astype(bf16)

    @pl.when(pl.program_id(0) >= n_ctx_tiles)
    def _():
        ab_ref[...] = atl_ref[...].astype(bf16)
        rb_ref[...] = rcl_ref[...].astype(bf16)

    pa = jnp.dot(ab_ref[...], wa_ref[...], preferred_element_type=f32)
    pr = jnp.dot(rb_ref[...], wl_ref[...], preferred_element_type=f32)
    merged = jax.nn.sigmoid(ga_ref[...]) * pa + jax.nn.sigmoid(gl_ref[...]) * pr
    out = jnp.dot(merged.astype(bf16), wo_ref[...], preferred_element_type=f32)
    x1 = _layer_norm(alpha * x_ref[...] + gt_ref[0] * out, g_ref[0], b_ref[0])
    x1_ref[...] = x1
    h2_ref[...] = x1 * (1.0 + scf_ref[0]) + shf_ref[0]


def _merge(attn_c, attn_l, rec_c, rec_l, y, x, mod4, ln_g3, ln_b3, wa, wl, wo, l, n_ctx, dec_seq, ga_col, gl_col,
           alpha):
    n, d = x.shape
    tm = _tile(math.gcd(n_ctx, dec_seq), 512)
    n_ctx_tiles = n_ctx // tm
    tiles_per_seq = dec_seq // tm

    def seg(i):
        return jnp.where(i < n_ctx_tiles, 0, 1 + (i - n_ctx_tiles) // tiles_per_seq)

    base = l * MOD_ROWS * 6
    row = lambda i: (i, 0)
    ctx_row = lambda i: (jnp.minimum(i, n_ctx_tiles - 1), 0)
    lat_row = lambda i: (jnp.maximum(i - n_ctx_tiles, 0), 0)
    mod = lambda k: pl.BlockSpec((1, 1, d), lambda i: (base + seg(i) * 6 + k, 0, 0))
    wspec = pl.BlockSpec((None, d, d), lambda i: (l, 0, 0))
    return pl.pallas_call(
        functools.partial(_merge_kernel, alpha=alpha, n_ctx_tiles=n_ctx_tiles),
        grid=(n // tm,),
        in_specs=[pl.BlockSpec((tm, d), ctx_row), pl.BlockSpec((tm, d), lat_row),
                  pl.BlockSpec((tm, d), ctx_row), pl.BlockSpec((tm, d), lat_row),
                  pl.BlockSpec((tm, d), lambda i: (i, ga_col // d)),
                  pl.BlockSpec((tm, d), lambda i: (i, gl_col // d)),
                  pl.BlockSpec((tm, d), row), mod(2), mod(3), mod(4),
                  pl.BlockSpec((1, 1, d), lambda i: (2 * l, 0, 0)),
                  pl.BlockSpec((1, 1, d), lambda i: (2 * l, 0, 0)),
                  wspec, wspec, wspec],
        out_specs=[pl.BlockSpec((tm, d), row), pl.BlockSpec((tm, d), row)],
        out_shape=[jax.ShapeDtypeStruct((n, d), f32), jax.ShapeDtypeStruct((n, d), f32)],
        scratch_shapes=[pltpu.VMEM((tm, d), bf16), pltpu.VMEM((tm, d), bf16)],
        compiler_params=_cparams(1),
        name="merge_outproj_ln",
    )(attn_c, attn_l, rec_c, rec_l, y, y, x, mod4, mod4, mod4, ln_g3, ln_b3, wa, wl, wo)


def _route_kernel(h_ref, rw_ref, rb_ref, pos_ref, gate_ref, tab_ref, carry_ref):
    n_exp = rw_ref.shape[0]
    tm = h_ref.shape[0]
    gsz = n_exp // N_GROUPS

    @pl.when(pl.program_id(0) == 0)
    def _():
        carry_ref[...] = jnp.zeros_like(carry_ref)

    logits = lax.dot_general(rw_ref[...], h_ref[...], (((1,), (1,)), ((), ())),
                             precision=HIGHEST, preferred_element_type=f32)
    s = jax.nn.sigmoid(logits)
    sb = s + rb_ref[...]
    ninf = -jnp.inf
    gs_rows = []
    for g in range(N_GROUPS):
        blk = sb[g * gsz:(g + 1) * gsz]
        m1 = jnp.max(blk, axis=0, keepdims=True)
        cnt = jnp.sum((blk == m1).astype(f32), axis=0, keepdims=True)
        m2 = jnp.max(jnp.where(blk < m1, blk, ninf), axis=0, keepdims=True)
        gs_rows.append(m1 + jnp.where(cnt >= 2.0, m1, m2))
    gs = jnp.concatenate(gs_rows, axis=0)
    gi = lax.broadcasted_iota(i32, gs.shape, 0)
    grank = jnp.zeros(gs.shape, f32)
    for g2 in range(N_GROUPS):
        row = gs[g2:g2 + 1]
        grank += jnp.where((row > gs) | ((row == gs) & (g2 < gi)), 1.0, 0.0)
    gsel = grank < float(TOPK_GROUPS)
    sbm = jnp.concatenate([jnp.where(gsel[g:g + 1], sb[g * gsz:(g + 1) * gsz], ninf)
                           for g in range(N_GROUPS)], axis=0)
    ei = lax.broadcasted_iota(i32, sbm.shape, 0)
    erank = jnp.zeros(sbm.shape, f32)
    for e2 in range(n_exp):
        row = sbm[e2:e2 + 1]
        erank += jnp.where((row > sbm) | ((row == sbm) & (e2 < ei)), 1.0, 0.0)
    sel = erank < float(TOP_K)
    ssel = jnp.where(sel, s, 0.0)
    gate = ssel / jnp.sum(ssel, axis=0, keepdims=True) * ROUTE_SCALE
    selb = sel.astype(bf16)
    ti = lax.broadcasted_iota(i32, (tm, tm), 0)
    tj = lax.broadcasted_iota(i32, (tm, tm), 1)
    upper = (ti < tj).astype(bf16)
    pos_local = jnp.dot(selb, upper, preferred_element_type=f32)
    pos_ref[...] = jnp.where(sel, pos_local, -1.0)
    gate_ref[...] = gate
    cnt = jnp.sum(sel.astype(f32), axis=1, keepdims=True)
    cpad = jnp.floor((cnt + (SEG_ALIGN - 1)) * (1.0 / SEG_ALIGN)) * SEG_ALIGN
    cpad_l = jnp.broadcast_to(cpad, (n_exp, 128))
    li = lax.broadcasted_iota(i32, (n_exp, n_exp), 0)
    lj = lax.broadcasted_iota(i32, (n_exp, n_exp), 1)
    lower = (lj < li).astype(bf16)
    run_start = jnp.dot(lower, cpad_l.astype(bf16), preferred_element_type=f32)
    lane = lax.broadcasted_iota(i32, (n_exp, 128), 1)
    tab = jnp.where(lane == TAB_LEN, cpad_l, jnp.where(lane == TAB_BASE, carry_ref[...], run_start))
    tab_ref[...] = tab.astype(i32)
    carry_ref[...] = carry_ref[...] + cpad_l


def _route(h2, rwt, rb3, l):
    n, d = h2.shape
    n_exp = rwt.shape[1]
    tm = ROUTE_TILE
    assert n % tm == 0
    lst = pl.BlockSpec((n_exp, tm), lambda i: (0, i))
    return pl.pallas_call(
        _route_kernel,
        grid=(n // tm,),
        in_specs=[pl.BlockSpec((tm, d), lambda i: (i, 0)),
                  pl.BlockSpec((None, n_exp, d), lambda i: (l, 0, 0)),
                  pl.BlockSpec((None, n_exp, 1), lambda i: (l, 0, 0))],
        out_specs=[lst, lst, pl.BlockSpec((None, n_exp, 128), lambda i: (i, 0, 0))],
        out_shape=[jax.ShapeDtypeStruct((n_exp, n), f32), jax.ShapeDtypeStruct((n_exp, n), f32),
                   jax.ShapeDtypeStruct((n // tm, n_exp, 128), i32)],
        scratch_shapes=[pltpu.VMEM((n_exp, 128), f32)],
        compiler_params=_cparams(1),
        name="moe_route",
    )(h2, rwt, rb3)


def _pow2_pieces(count, max_rows, fn):
    size = max_rows
    while size >= SEG_ALIGN:
        off = count & (-2 * size)
        pl.when((count & size) != 0)(functools.partial(fn, off, size))
        size //= 2


def _wait_rows(count, max_rows, src_ref, dst_ref, sem):
    def wait(_, size):
        pltpu.make_async_copy(src_ref.at[pl.ds(0, size)], dst_ref.at[pl.ds(0, size)], sem).wait()

    _pow2_pieces(count, max_rows, wait)


def _pad_experts(x):
    return jnp.concatenate([x, jnp.zeros((128 - x.shape[0], x.shape[1]), x.dtype)], axis=0)


def _dispatch_kernel(pd_ref, tot_ref, ts_ref, tl_ref, pos_ref, tab_ref, h_ref, xs_ref, buf_ref, z_ref,
                     sem, *, n_exp, n_tiles, tile_rows, wait_max):
    t = pl.program_id(0)
    slot = t % 2
    tm = h_ref.shape[0]
    pieces = tile_rows // SEG_ALIGN

    @pl.when(t >= 2)
    def _():
        _wait_rows(tot_ref[t - 2], wait_max, buf_ref.at[slot], xs_ref, sem.at[slot])

    tab = _pad_experts(tab_ref[...].astype(f32))
    tab_t = tab.T
    start_row, len_row = tab_t[TAB_START:TAB_START + 1, :], tab_t[TAB_LEN:TAB_LEN + 1, :]
    rid = lax.broadcasted_iota(i32, (tile_rows, 128), 0).astype(f32)
    in_run = ((rid >= start_row) & (rid < start_row + len_row)).astype(bf16)
    pos_rows = jnp.dot(in_run, _pad_experts(pos_ref[...]).astype(bf16), preferred_element_type=f32)
    start16 = (tab[:, TAB_START:TAB_START + 1] * (1.0 / SEG_ALIGN)).astype(bf16)
    run_start = jnp.dot(in_run, jnp.broadcast_to(start16, (128, 128)), preferred_element_type=f32) * SEG_ALIGN
    ordinal = rid - run_start
    ordinal = jnp.concatenate([ordinal] * (tm // 128), axis=1)
    perm = jnp.where(pos_rows == ordinal, 1.0, 0.0).astype(bf16)
    buf_ref[slot] = jnp.dot(perm, h_ref[...].astype(bf16), preferred_element_type=f32).astype(bf16)

    def issue(i, _):
        for u in range(PIECE_UNROLL):
            p = i * PIECE_UNROLL + u
            src = pl.multiple_of(p * SEG_ALIGN, SEG_ALIGN)
            dst = pl.multiple_of(pd_ref[t * pieces + p], SEG_ALIGN)
            pltpu.make_async_copy(buf_ref.at[slot, pl.ds(src, SEG_ALIGN)], xs_ref.at[pl.ds(dst, SEG_ALIGN)],
                                  sem.at[slot]).start()
        return 0

    lax.fori_loop(0, tot_ref[t] // (SEG_ALIGN * PIECE_UNROLL), issue, 0)

    @pl.when(t == n_tiles - 1)
    def _():
        z_ref[...] = jnp.zeros_like(z_ref)

        def fill(e, _):
            def copy(off, size):
                dst = pl.multiple_of(ts_ref[e] + off, SEG_ALIGN)
                pltpu.make_async_copy(z_ref.at[pl.ds(0, size)], xs_ref.at[pl.ds(dst, size)], sem.at[2]).start()

            _pow2_pieces(tl_ref[e], MOE_BLOCK // 2, copy)
            return 0

        lax.fori_loop(0, n_exp, fill, 0)

        def drain(e, _):
            _wait_rows(tl_ref[e], MOE_BLOCK // 2, z_ref, xs_ref, sem.at[2])
            return 0

        lax.fori_loop(0, n_exp, drain, 0)
        if n_tiles > 1:
            _wait_rows(tot_ref[t - 1], wait_max, buf_ref.at[1 - slot], xs_ref, sem.at[1 - slot])
        _wait_rows(tot_ref[t], wait_max, buf_ref.at[slot], xs_ref, sem.at[slot])


def _tile_rows(tm, n_exp):
    return -(-(tm * TOP_K + n_exp * (SEG_ALIGN - 1)) // 128) * 128


def _wait_max(tile_rows):
    return 1 << (tile_rows.bit_length() - 1)


def _dispatch(h2, pos, tab, piece_dst, tot, tail_start, tail_len, n_rows, n_exp):
    n, d = h2.shape
    tm = ROUTE_TILE
    n_tiles = n // tm
    tile_rows = _tile_rows(tm, n_exp)
    assert n_exp <= 128 and tm % 128 == 0
    return pl.pallas_call(
        functools.partial(_dispatch_kernel, n_exp=n_exp, n_tiles=n_tiles, tile_rows=tile_rows,
                          wait_max=_wait_max(tile_rows)),
        grid_spec=pltpu.PrefetchScalarGridSpec(
            num_scalar_prefetch=4,
            grid=(n_tiles,),
            in_specs=[pl.BlockSpec((n_exp, tm), lambda i, *_: (0, i)),
                      pl.BlockSpec((None, n_exp, 128), lambda i, *_: (i, 0, 0)),
                      pl.BlockSpec((tm, d), lambda i, *_: (i, 0))],
            out_specs=pl.BlockSpec(memory_space=pl.ANY),
            scratch_shapes=[pltpu.VMEM((2, tile_rows, d), bf16), pltpu.VMEM((MOE_BLOCK // 2, d), bf16),
                            pltpu.SemaphoreType.DMA((3,))]),
        out_shape=jax.ShapeDtypeStruct((n_rows, d), bf16),
        compiler_params=_cparams(1),
        name="moe_dispatch",
    )(piece_dst, tot, tail_start, tail_len, pos, tab, h2)


def _expert_kernel(be_ref, nu_ref, x_ref, wg_ref, wu_ref, wd_ref, o_ref):
    @pl.when(pl.program_id(0) < nu_ref[0])
    def _():
        x = x_ref[...]
        g = jnp.dot(x, wg_ref[...].astype(bf16), preferred_element_type=f32)
        u = jnp.dot(x, wu_ref[...].astype(bf16), preferred_element_type=f32)
        h = (g * jax.nn.sigmoid(g)) * u
        o_ref[...] = jnp.dot(h.astype(bf16), wd_ref[...].astype(bf16),
                             preferred_element_type=f32).astype(bf16)


def _experts(xs, blk_e, n_used, we_g, we_u, we_d, l):
    n_rows, d = xs.shape
    f = we_g.shape[-1]
    nblk = n_rows // MOE_BLOCK
    blk = lambda b, be, nu: (jnp.minimum(b, nu[0] - 1), 0)
    wsel = lambda b, be, nu: (l, be[jnp.minimum(b, nu[0] - 1)], 0, 0)
    return pl.pallas_call(
        _expert_kernel,
        grid_spec=pltpu.PrefetchScalarGridSpec(
            num_scalar_prefetch=2,
            grid=(nblk,),
            in_specs=[pl.BlockSpec((MOE_BLOCK, d), blk),
                      pl.BlockSpec((None, None, d, f), wsel),
                      pl.BlockSpec((None, None, d, f), wsel),
                      pl.BlockSpec((None, None, f, d), wsel)],
            out_specs=pl.BlockSpec((MOE_BLOCK, d), blk)),
        out_shape=jax.ShapeDtypeStruct((n_rows, d), bf16),
        compiler_params=_cparams(1),
        name="moe_experts",
    )(blk_e, n_used, xs, we_g, we_u, we_d)


def _combine_kernel(pd_ref, tot_ref, pos_ref, gate_ref, tab_ref, ys_ref, x1_ref, h2_ref, gt_ref, g_ref, b_ref,
                    sg_ref, su_ref, sd_ref, o_ref, buf_ref, sem, *, alpha, n_exp, n_tiles, tile_rows, wait_max):
    t = pl.program_id(0)
    slot = t % 2
    tm = x1_ref.shape[0]
    pieces = tile_rows // SEG_ALIGN

    def fetch(tile, dst_slot):
        def issue(i, _):
            for u in range(PIECE_UNROLL):
                p = i * PIECE_UNROLL + u
                src = pl.multiple_of(pd_ref[tile * pieces + p], SEG_ALIGN)
                dst = pl.multiple_of(p * SEG_ALIGN, SEG_ALIGN)
                pltpu.make_async_copy(ys_ref.at[pl.ds(src, SEG_ALIGN)], buf_ref.at[dst_slot, pl.ds(dst, SEG_ALIGN)],
                                      sem.at[dst_slot]).start()
            return 0

        lax.fori_loop(0, tot_ref[tile] // (SEG_ALIGN * PIECE_UNROLL), issue, 0)

    @pl.when(t == 0)
    def _():
        buf_ref[...] = jnp.zeros_like(buf_ref)
        fetch(0, 0)

    @pl.when(t + 1 < n_tiles)
    def _():
        fetch(t + 1, 1 - slot)

    hb = h2_ref[...].astype(bf16)
    g = jnp.dot(hb, sg_ref[...], preferred_element_type=f32)
    u = jnp.dot(hb, su_ref[...], preferred_element_type=f32)
    shared = jnp.dot(((g * jax.nn.sigmoid(g)) * u).astype(bf16), sd_ref[...], preferred_element_type=f32)
    tab = _pad_experts(tab_ref[...].astype(f32))
    start_col, len_col = tab[:, TAB_START:TAB_START + 1], tab[:, TAB_LEN:TAB_LEN + 1]
    rid = lax.broadcasted_iota(i32, (128, tile_rows), 1).astype(f32)
    in_run = (rid >= start_col) & (rid < start_col + len_col)
    ordinal = rid[0:1, :] - jnp.sum(jnp.where(in_run, start_col, 0.0), axis=0, keepdims=True)
    in_run = in_run.astype(bf16)
    pos_t = _pad_experts(pos_ref[...]).T.astype(bf16)
    gate_t = _pad_experts(gate_ref[...]).T.astype(bf16)
    pos_rows = jnp.dot(pos_t, in_run, preferred_element_type=f32)
    gate_rows = jnp.dot(gate_t, in_run, preferred_element_type=f32)
    wgt = jnp.where(pos_rows == ordinal, gate_rows, 0.0).astype(bf16)
    _wait_rows(tot_ref[t], wait_max, ys_ref, buf_ref.at[slot], sem.at[slot])
    routed = jnp.dot(wgt, buf_ref[slot], preferred_element_type=f32)
    o_ref[...] = _layer_norm(alpha * x1_ref[...] + gt_ref[0] * (shared + routed), g_ref[0], b_ref[0])


def _combine(ys, pos, gate, tab, piece_dst, tot, x1, h2, mod4, ln_g3, ln_b3, sg, su, sd, l, n_ctx, dec_seq, alpha,
             n_exp):
    n, d = x1.shape
    f = sg.shape[-1]
    tm = ROUTE_TILE
    assert n_ctx % tm == 0 and dec_seq % tm == 0
    n_tiles = n // tm
    tile_rows = _tile_rows(tm, n_exp)
    n_ctx_tiles = n_ctx // tm
    tiles_per_seq = dec_seq // tm

    def seg(i):
        return jnp.where(i < n_ctx_tiles, 0, 1 + (i - n_ctx_tiles) // tiles_per_seq)

    base = l * MOD_ROWS * 6
    row = lambda i, *_: (i, 0)
    lst = lambda i, *_: (0, i)
    return pl.pallas_call(
        functools.partial(_combine_kernel, alpha=alpha, n_exp=n_exp, n_tiles=n_tiles, tile_rows=tile_rows,
                          wait_max=_wait_max(tile_rows)),
        grid_spec=pltpu.PrefetchScalarGridSpec(
            num_scalar_prefetch=2,
            grid=(n_tiles,),
            in_specs=[pl.BlockSpec((n_exp, tm), lst), pl.BlockSpec((n_exp, tm), lst),
                      pl.BlockSpec((None, n_exp, 128), lambda i, *_: (i, 0, 0)),
                      pl.BlockSpec(memory_space=pl.ANY),
                      pl.BlockSpec((tm, d), row), pl.BlockSpec((tm, d), row),
                      pl.BlockSpec((1, 1, d), lambda i, *_: (base + seg(i) * 6 + 5, 0, 0)),
                      pl.BlockSpec((1, 1, d), lambda i, *_: (2 * l + 1, 0, 0)),
                      pl.BlockSpec((1, 1, d), lambda i, *_: (2 * l + 1, 0, 0)),
                      pl.BlockSpec((None, d, f), lambda i, *_: (l, 0, 0)),
                      pl.BlockSpec((None, d, f), lambda i, *_: (l, 0, 0)),
                      pl.BlockSpec((None, f, d), lambda i, *_: (l, 0, 0))],
            out_specs=pl.BlockSpec((tm, d), row),
            scratch_shapes=[pltpu.VMEM((2, tile_rows, d), bf16), pltpu.SemaphoreType.DMA((2,))]),
        out_shape=jax.ShapeDtypeStruct((n, d), f32),
        compiler_params=_cparams(1),
        name="moe_combine_ln",
    )(piece_dst, tot, pos, gate, tab, ys, x1, h2, mod4, ln_g3, ln_b3, sg, su, sd)


def _rope_tables(seq, width):
    rows = seq // GRID_W
    row = jnp.repeat(jnp.arange(rows, dtype=f32), GRID_W)
    col = jnp.tile(jnp.arange(GRID_W, dtype=f32), rows)
    half = HEAD_DIM // 2
    inv = ROPE_BASE ** (-jnp.arange(0, half, 2, dtype=f32) / half)
    lane = jnp.arange(width)
    d = lane % HEAD_DIM
    pos = jnp.where((d // half)[None, :] == 0, row[:, None], col[:, None])
    ang = pos * inv[d % (HEAD_DIM // 4)][None, :]
    sign = jnp.where((d % half) < HEAD_DIM // 4, -1.0, 1.0)[None, :]
    return jnp.cos(ang), jnp.sin(ang) * sign


def _block_diag_gates(gate_w_r, gate_w_i, cb):
    depth, _, nb, bw, _ = gate_w_r.shape
    per = cb // bw
    w = jnp.stack([gate_w_r[:, 0], gate_w_i[:, 0], gate_w_r[:, 1], gate_w_i[:, 1]], axis=1)
    w = w.reshape(depth, 4, nb // per, per, bw, bw)
    eye = jnp.eye(per, dtype=w.dtype)
    bd = jnp.einsum("lgcaio,ab->lcgaibo", w, eye)
    return bd.reshape(depth, nb // per, 4, cb, cb).astype(bf16)


def kernel(x_prompt, x_sample, c, cache_k, cache_v, state_lru, c_ctx, w_mod, b_mod, w_in, lam_param, subln_w,
           conv_w, conv_b, gate_w_r, gate_b_r, gate_w_i, gate_b_i, lru_lambda, w_br_attn, w_br_lru, w_out,
           ln_g, ln_b, router_w, router_b, exp_w_gate, exp_w_up, exp_w_down, sh_w_gate, sh_w_up, sh_w_down):
    batch, seq, d = x_prompt.shape
    dec_batch, dec_seq, _ = x_sample.shape
    depth = w_mod.shape[0]
    past = cache_k.shape[2]
    d_lru = conv_w.shape[-1]
    n_exp = router_w.shape[-1]
    qk = N_HEADS * 2 * HEAD_DIM
    attn_w = N_HEADS * V_DIM
    n_ctx, n_lat = batch * seq, dec_batch * dec_seq
    n = n_ctx + n_lat
    alpha = (2 * depth) ** 0.25
    assert 1 + dec_batch <= MOD_ROWS and n_ctx % dec_seq == 0

    cond = jnp.concatenate([c_ctx[None], c, jnp.zeros((MOD_ROWS - 1 - dec_batch, d), f32)], 0)
    w_in_b = w_in.astype(bf16)
    wa, wl, wo = w_br_attn.astype(bf16), w_br_lru.astype(bf16), w_out.astype(bf16)
    sg, su, sd = sh_w_gate.astype(bf16), sh_w_up.astype(bf16), sh_w_down.astype(bf16)
    rwt = jnp.swapaxes(router_w, 1, 2)
    rb3 = router_b.reshape(depth, n_exp, 1)
    wg = _block_diag_gates(0.5 * gate_w_r, 0.5 * gate_w_i, 256)
    bg = 0.5 * jnp.stack([gate_b_r[:, 0], gate_b_i[:, 0], gate_b_r[:, 1], gate_b_i[:, 1]], axis=1)
    conv_b3 = conv_b.reshape(depth, 1, d_lru)
    subln_w3 = subln_w.reshape(depth, 1, V_DIM)
    ln_g3 = ln_g.reshape(depth * 2, 1, d)
    ln_b3 = ln_b.reshape(depth * 2, 1, d)
    ck = cache_k.reshape(dec_batch, depth, past, N_HEADS * 2 * HEAD_DIM)
    cv = cache_v.reshape(dec_batch, depth, past, attn_w)
    cos_t, sin_t = _rope_tables(dec_seq, V_DIM)
    h0_ctx = jnp.zeros((1, 2, d_lru), f32)

    mod4 = _mod_table(cond, w_mod, b_mod).reshape(depth * MOD_ROWS * 6, 1, d)
    x = jnp.concatenate([x_prompt.reshape(n_ctx, d), x_sample.reshape(n_lat, d)], 0)
    n_tiles = n // ROUTE_TILE
    spare0 = n * TOP_K + n_tiles * n_exp * (SEG_ALIGN - 1) + n_exp * (MOE_BLOCK - 1)
    spare0 = -(-spare0 // SEG_ALIGN) * SEG_ALIGN
    n_rows = -(-(spare0 + n_tiles * (PIECE_UNROLL - 1) * SEG_ALIGN) // MOE_BLOCK) * MOE_BLOCK
    xr_col, yr_col = 2 * qk + attn_w, 2 * qk + attn_w + d_lru
    ga_col, gl_col = yr_col + d_lru, yr_col + d_lru + d

    new_k, new_v, new_s = [], [], []
    for l in range(depth):
        lam_init = 0.8 - 0.6 * math.exp(-0.3 * l)
        y = _inproj(x, mod4, w_in_b, cos_t, sin_t, l, n_ctx, dec_seq, 2 * qk)
        new_k.append(y[:n_ctx, qk:2 * qk].reshape(batch, seq, N_HEADS, 2 * HEAD_DIM))
        new_v.append(y[:n_ctx, 2 * qk:2 * qk + attn_w].reshape(batch, seq, N_HEADS, V_DIM))
        attn_c = _attention(y, lam_param, subln_w3, l, lam_init, 0, batch, seq)
        attn_l = _attention(y, lam_param, subln_w3, l, lam_init, n_ctx, dec_batch, dec_seq, ck, cv)
        rec_c, hfin = _lru(y, conv_w, conv_b3, wg, bg, lru_lambda, h0_ctx, l, 0, batch, seq, xr_col, yr_col)
        rec_l, _ = _lru(y, conv_w, conv_b3, wg, bg, lru_lambda, state_lru[:, l], l, n_ctx, dec_batch, dec_seq,
                        xr_col, yr_col)
        new_s.append(hfin)
        x1, h2 = _merge(attn_c, attn_l, rec_c, rec_l, y, x, mod4, ln_g3, ln_b3, wa, wl, wo, l, n_ctx, dec_seq, ga_col, gl_col, alpha)
        pos, gate, tab = _route(h2, rwt, rb3, l)
        cpad, grel, start = tab[:, :, TAB_LEN], tab[:, :, TAB_BASE], tab[:, :, TAB_START]
        used = grel[-1] + cpad[-1]
        padded = (used + MOE_BLOCK - 1) // MOE_BLOCK * MOE_BLOCK
        pad_end = jnp.cumsum(padded)
        pad_start = pad_end - padded
        tot = jnp.sum(cpad, axis=1).astype(i32)
        blk_row0 = jnp.arange(n_rows // MOE_BLOCK, dtype=i32) * MOE_BLOCK
        blk_e = jnp.minimum(jnp.sum(pad_end[None, :] <= blk_row0[:, None], axis=1), n_exp - 1).astype(i32)
        n_used = (pad_end[-1:] // MOE_BLOCK).astype(i32)
        piece0 = jnp.arange(_tile_rows(ROUTE_TILE, n_exp) // SEG_ALIGN, dtype=i32)[None, :, None] * SEG_ALIGN
        in_run = (piece0 >= start[:, None, :]) & (piece0 < (start + cpad)[:, None, :])
        shift = (pad_start[None, :] + grel - start)[:, None, :]
        piece0 = piece0[:, :, 0]
        piece_row = piece0 + jnp.sum(jnp.where(in_run, shift, 0), axis=-1)
        real = piece0 < tot[:, None]
        spare = (spare0 + jnp.arange(n_tiles, dtype=i32)[:, None] * ((PIECE_UNROLL - 1) * SEG_ALIGN)
                 + piece0 - tot[:, None])
        piece_dst = jnp.where(real, piece_row, spare).reshape(-1).astype(i32)
        piece_src = jnp.where(real, piece_row, 0).reshape(-1).astype(i32)
        step = SEG_ALIGN * PIECE_UNROLL
        tot = (tot + step - 1) // step * step
        xs = _dispatch(h2, pos, tab, piece_dst, tot, (pad_start + used).astype(i32),
                       (padded - used).astype(i32), n_rows, n_exp)
        ys = _experts(xs, blk_e, n_used, exp_w_gate, exp_w_up, exp_w_down, l)
        x = _combine(ys, pos, gate, tab, piece_src, tot, x1, h2, mod4, ln_g3, ln_b3, sg, su, sd, l, n_ctx, dec_seq,
                     alpha, n_exp)
    return (x[:n_ctx].reshape(batch, seq, d), x[n_ctx:].reshape(dec_batch, dec_seq, d),
            jnp.stack(new_k, 1), jnp.stack(new_v, 1), jnp.stack(new_s, 1))
```

```python
import functools
import math

import jax
import jax.numpy as jnp
from jax import lax
from jax.experimental import pallas as pl
from jax.experimental.pallas import tpu as pltpu

f32, bf16, i32 = jnp.float32, jnp.bfloat16, jnp.int32

GRID_W = 64
N_HEADS = 8
HEAD_DIM = 64
V_DIM = 2 * HEAD_DIM
ROPE_BASE = 10000.0
LRU_BLOCKS = 16
CONV_WIDTH = 4
LRU_C = 8.0
TOP_K = 8
N_GROUPS = 8
TOPK_GROUPS = 4
ROUTE_SCALE = 2.5
LN_EPS = 1e-5
MOD_ROWS = 8
MOE_BLOCK = 1024
ROUTE_TILE = 256
SEG_ALIGN = 16
PIECE_UNROLL = 4
ROPE_CHUNK = 256
TAB_LEN, TAB_BASE, TAB_START = 0, 1, 2
SCAN_UNROLL = 4
VMEM_LIMIT = 56 * 1024 * 1024
HIGHEST = lax.Precision.HIGHEST


def _cparams(n_axes):
    return pltpu.CompilerParams(dimension_semantics=("arbitrary",) * n_axes,
                                vmem_limit_bytes=VMEM_LIMIT)


def _tile(n, pref):
    t = min(n, pref)
    while n % t:
        t -= 8
    assert t > 0 and n % t == 0
    return t


def _layer_norm(z, g, b):
    mu = jnp.mean(z, -1, keepdims=True)
    zc = z - mu
    var = jnp.mean(zc * zc, -1, keepdims=True)
    return zc * lax.rsqrt(var + LN_EPS) * g + b


def _mod_kernel(c_ref, w_ref, b_ref, o_ref):
    c = c_ref[...]
    a = c * jax.nn.sigmoid(c)
    o_ref[...] = jnp.dot(a, w_ref[...], precision=HIGHEST, preferred_element_type=f32) + b_ref[...]


def _mod_table(cond, w_mod, b_mod):
    depth, d, d6 = w_mod.shape
    tn = _tile(d6, 1024)
    return pl.pallas_call(
        _mod_kernel,
        grid=(depth, d6 // tn),
        in_specs=[pl.BlockSpec((MOD_ROWS, d), lambda l, j: (0, 0)),
                  pl.BlockSpec((None, d, tn), lambda l, j: (l, 0, j)),
                  pl.BlockSpec((None, 1, tn), lambda l, j: (l, 0, j))],
        out_specs=pl.BlockSpec((None, MOD_ROWS, tn), lambda l, j: (l, 0, j)),
        out_shape=jax.ShapeDtypeStruct((depth, MOD_ROWS, d6), f32),
        compiler_params=_cparams(2),
        name="mod_table",
    )(cond, w_mod, b_mod.reshape(depth, 1, d6))


def _inproj_kernel(x_ref, sh_ref, sc_ref, w_ref, cos_ref, sin_ref, o_ref, hb_ref, *,
                   n_ctx_tiles, n_rope_cols, tn):
    i = pl.program_id(0)
    j = pl.program_id(1)

    @pl.when(j == 0)
    def _():
        hb_ref[...] = (x_ref[...] * (1.0 + sc_ref[0]) + sh_ref[0]).astype(bf16)

    rope = jnp.logical_and(i >= n_ctx_tiles, j < n_rope_cols)

    @pl.when(rope)
    def _():
        cos, sin = cos_ref[...], sin_ref[...]
        lane = lax.broadcasted_iota(i32, cos.shape, 1)
        first = (lane % (HEAD_DIM // 2)) < (HEAD_DIM // 4)
        for c in range(tn // ROPE_CHUNK):
            acc = jnp.dot(hb_ref[...], w_ref[:, c * ROPE_CHUNK:(c + 1) * ROPE_CHUNK], preferred_element_type=f32)
            for h in range(ROPE_CHUNK // V_DIM):
                a = acc[:, h * V_DIM:(h + 1) * V_DIM]
                partner = jnp.where(first, pltpu.roll(a, V_DIM - HEAD_DIM // 4, 1), pltpu.roll(a, HEAD_DIM // 4, 1))
                col = c * ROPE_CHUNK + h * V_DIM
                o_ref[:, col:col + V_DIM] = a * cos + partner * sin

    @pl.when(jnp.logical_not(rope))
    def _():
        o_ref[...] = jnp.dot(hb_ref[...], w_ref[...], preferred_element_type=f32)


def _inproj(x, mod4, w_in_b, cos_t, sin_t, l, n_ctx, dec_seq, qk_cols):
    n, d = x.shape
    d_in = w_in_b.shape[-1]
    tm = _tile(math.gcd(n_ctx, dec_seq), 1024)
    tn = _tile(math.gcd(d_in, qk_cols), 1024)
    assert tn % V_DIM == 0
    n_ctx_tiles = n_ctx // tm
    tiles_per_seq = dec_seq // tm

    def seg(i):
        return jnp.where(i < n_ctx_tiles, 0, 1 + (i - n_ctx_tiles) // tiles_per_seq)

    def pos(i):
        return jnp.where(i < n_ctx_tiles, 0, (i - n_ctx_tiles) % tiles_per_seq)

    base = l * MOD_ROWS * 6
    return pl.pallas_call(
        functools.partial(_inproj_kernel, n_ctx_tiles=n_ctx_tiles, n_rope_cols=qk_cols // tn, tn=tn),
        grid=(n // tm, d_in // tn),
        in_specs=[pl.BlockSpec((tm, d), lambda i, j: (i, 0)),
                  pl.BlockSpec((1, 1, d), lambda i, j: (base + seg(i) * 6 + 0, 0, 0)),
                  pl.BlockSpec((1, 1, d), lambda i, j: (base + seg(i) * 6 + 1, 0, 0)),
                  pl.BlockSpec((None, d, tn), lambda i, j: (l, 0, j)),
                  pl.BlockSpec((tm, V_DIM), lambda i, j: (pos(i), 0)),
                  pl.BlockSpec((tm, V_DIM), lambda i, j: (pos(i), 0))],
        out_specs=pl.BlockSpec((tm, tn), lambda i, j: (i, j)),
        out_shape=jax.ShapeDtypeStruct((n, d_in), f32),
        scratch_shapes=[pltpu.VMEM((tm, d), bf16)],
        compiler_params=_cparams(2),
        name="in_proj",
    )(x, mod4, mod4, w_in_b, cos_t, sin_t)


def _key_chunk(lk):
    for ck in (512, 384, 256, 128):
        if lk % ck == 0:
            return ck
    return lk


def _attn_kernel(*refs, has_cache, lam_init, past, heads):
    if has_cache:
        q_ref, k_ref, v_ref, ck_ref, cv_ref, lp_ref, sw_ref, o_ref, kt_ref, vb_ref = refs
    else:
        q_ref, k_ref, v_ref, lp_ref, sw_ref, o_ref, kt_ref, vb_ref = refs

    @pl.when(pl.program_id(2) == 0)
    def _():
        for h in range(heads):
            cols = slice(h * V_DIM, (h + 1) * V_DIM)
            if has_cache:
                kt_ref[h, :, :past] = ck_ref[:, cols].T.astype(bf16)
                vb_ref[h, :past, :V_DIM] = cv_ref[:, cols].astype(bf16)
            kt_ref[h, :, past:] = k_ref[:, cols].T.astype(bf16)
            vb_ref[h, past:, :V_DIM] = v_ref[:, cols].astype(bf16)
            vb_ref[h, :, V_DIM:] = jnp.ones((vb_ref.shape[1], V_DIM), bf16)

    lp = lp_ref[...]
    lam = (jnp.exp(jnp.sum(lp[0:1] * lp[1:2], keepdims=True))
           - jnp.exp(jnp.sum(lp[2:3] * lp[3:4], keepdims=True)) + lam_init)
    for h in range(heads):
        cols = slice(h * V_DIM, (h + 1) * V_DIM)
        o = _attn_head(q_ref[:, cols], kt_ref.at[h], vb_ref.at[h], lam)
        o = o * lax.rsqrt(jnp.mean(o * o, -1, keepdims=True) + LN_EPS) * sw_ref[...]
        o_ref[:, cols] = (o * (1.0 - lam_init)).astype(o_ref.dtype)


def _attn_head(q, kt_ref, vb_ref, lam):
    tq = q.shape[0]
    q = q * (HEAD_DIM ** -0.5 * math.log2(math.e))
    lane = lax.broadcasted_iota(i32, q.shape, 1)
    q2 = jnp.concatenate([jnp.where(lane < HEAD_DIM, q, 0.0).astype(bf16),
                          jnp.where(lane >= HEAD_DIM, q, 0.0).astype(bf16)], axis=0)
    lk = kt_ref.shape[1]
    ck = _key_chunk(lk)
    acc = jnp.zeros((2 * tq, 2 * V_DIM), f32)
    m_run = jnp.full((2 * tq, 1), -jnp.inf, f32)
    for c in range(lk // ck):
        s_c = jnp.dot(q2, kt_ref[:, c * ck:(c + 1) * ck], preferred_element_type=f32)
        m_new = jnp.maximum(m_run, jnp.max(s_c, axis=-1, keepdims=True))
        e_c = jnp.exp2(s_c - m_new).astype(bf16)
        acc = acc * jnp.exp2(m_run - m_new) + jnp.dot(e_c, vb_ref[c * ck:(c + 1) * ck, :],
                                                        preferred_element_type=f32)
        m_run = m_new
    o2 = acc[:, :V_DIM] / acc[:, V_DIM:]
    return o2[:tq] - lam * o2[tq:]


def _attention(y, lam_param, subln_w3, l, lam_init, row0, nseq, seq, cache_k=None, cache_v=None):
    has_cache = cache_k is not None
    past = cache_k.shape[2] if has_cache else 0
    lk = past + seq
    tq = _tile(seq, 1024)
    nq = seq // tq
    heads = N_HEADS if lk <= 512 else 1
    hg = N_HEADS // heads
    w = heads * V_DIM
    assert row0 % seq == 0
    qb0, kb0 = row0 // tq, row0 // seq
    in_specs = [pl.BlockSpec((tq, w), lambda b, h, i: (qb0 + b * nq + i, h)),
                pl.BlockSpec((seq, w), lambda b, h, i: (kb0 + b, hg + h)),
                pl.BlockSpec((seq, w), lambda b, h, i: (kb0 + b, 2 * hg + h))]
    args = [y, y, y]
    if has_cache:
        in_specs += [pl.BlockSpec((None, None, past, w), lambda b, h, i: (b, l, 0, h)),
                     pl.BlockSpec((None, None, past, w), lambda b, h, i: (b, l, 0, h))]
        args += [cache_k, cache_v]
    in_specs += [pl.BlockSpec((None, 4, HEAD_DIM), lambda b, h, i: (l, 0, 0)),
                 pl.BlockSpec((None, 1, V_DIM), lambda b, h, i: (l, 0, 0))]
    args += [lam_param, subln_w3]
    return pl.pallas_call(
        functools.partial(_attn_kernel, has_cache=has_cache, lam_init=lam_init, past=past, heads=heads),
        grid=(nseq, hg, nq),
        in_specs=in_specs,
        out_specs=pl.BlockSpec((tq, w), lambda b, h, i: (b * nq + i, h)),
        out_shape=jax.ShapeDtypeStruct((nseq * seq, N_HEADS * V_DIM), bf16),
        scratch_shapes=[pltpu.VMEM((heads, V_DIM, lk), bf16), pltpu.VMEM((heads, lk, 2 * V_DIM), bf16)],
        compiler_params=_cparams(3),
        name="diff_attn_lat" if has_cache else "diff_attn_ctx",
    )(*args)


def _lru_kernel(xr_ref, yr_ref, cw_ref, cb_ref, wg_ref, bg_ref, lam_ref, h0_ref,
                rec_ref, hfin_ref, xp_ref, a_ref, b_ref, hf_ref, *, chunk):
    seq, c = xr_ref.shape
    xp_ref[0:8, :] = jnp.zeros((8, c), f32)
    xp_ref[8:8 + seq, :] = xr_ref[...]
    xp_ref[8 + seq:16 + seq, :] = jnp.zeros((8, c), f32)
    lam = lam_ref[...]
    nlam = -lam
    softplus = jnp.maximum(nlam, 0.0) + jnp.log(1.0 + jnp.exp(-jnp.abs(nlam)))
    kh = (-0.5 * LRU_C) * softplus
    for ci in range(seq // chunk):
        t0 = ci * chunk
        xc = cb_ref[...] + sum(cw_ref[j:j + 1, :] * xp_ref[8 + t0 + j - 1:8 + t0 + j - 1 + chunk, :]
                               for j in range(CONV_WIDTH))
        xcb = xc.astype(bf16)
        hx = 0.5 * xc
        for d in range(2):
            tr = jnp.tanh(jnp.dot(xcb, wg_ref[2 * d], preferred_element_type=f32) + bg_ref[2 * d:2 * d + 1, :])
            ti = jnp.tanh(jnp.dot(xcb, wg_ref[2 * d + 1], preferred_element_type=f32)
                          + bg_ref[2 * d + 1:2 * d + 2, :])
            log_a = tr * kh[d:d + 1, :] + kh[d:d + 1, :]
            a = jnp.exp(log_a)
            a_ref[d, t0:t0 + chunk, :] = a
            u = -jnp.tanh(log_a) * (a * a + 1.0)
            b_ref[d, t0:t0 + chunk, :] = jnp.where(u > 0.0, u * lax.rsqrt(u), 0.0) * (ti * hx + hx)

    rows = lax.broadcasted_iota(i32, (8, c), 0)
    nch = seq // 8

    def scan_body(i, carry):
        cf, cb = carry
        tf = pl.multiple_of(8 * i, 8)
        af = a_ref[0, pl.ds(tf, 8), :]
        bf = b_ref[0, pl.ds(tf, 8), :]
        for d in (1, 2, 4):
            keep = rows >= d
            bf = af * jnp.where(keep, pltpu.roll(bf, d, 0), 0.0) + bf
            af = af * jnp.where(keep, pltpu.roll(af, d, 0), 1.0)
        hf = af * cf + bf
        hf_ref[pl.ds(tf, 8), :] = hf
        tb = pl.multiple_of(8 * (nch - 1 - i), 8)
        ab = a_ref[1, pl.ds(tb, 8), :]
        bb = b_ref[1, pl.ds(tb, 8), :]
        for d in (1, 2, 4):
            keep = rows < 8 - d
            bb = ab * jnp.where(keep, pltpu.roll(bb, 8 - d, 0), 0.0) + bb
            ab = ab * jnp.where(keep, pltpu.roll(ab, 8 - d, 0), 1.0)
        hb = ab * cb + bb
        xp_ref[pl.ds(tb, 8), :] = hb
        return hf[7:8, :], hb[0:1, :]

    h0 = h0_ref[...]
    cf, cb = lax.fori_loop(0, nch, scan_body, (h0[0:1, :], h0[1:2, :]), unroll=SCAN_UNROLL)
    hfin_ref[0:1, :] = cf
    hfin_ref[1:2, :] = cb
    for ci in range(seq // chunk):
        t0 = ci * chunk
        rec_ref[t0:t0 + chunk, :] = ((hf_ref[t0:t0 + chunk, :] + xp_ref[t0:t0 + chunk, :])
                                     * jax.nn.gelu(yr_ref[t0:t0 + chunk, :])).astype(rec_ref.dtype)


def _lru(y, conv_w, conv_b3, wg, bg, lru_lambda, h0, l, row0, nseq, seq, xr_col, yr_col):
    d_lru = conv_w.shape[-1]
    cb = 256
    ncb = d_lru // cb
    assert row0 % seq == 0
    rb0 = row0 // seq
    h0_batched = h0.shape[0] > 1
    in_specs = [pl.BlockSpec((seq, cb), lambda b, c: (rb0 + b, xr_col // cb + c)),
                pl.BlockSpec((seq, cb), lambda b, c: (rb0 + b, yr_col // cb + c)),
                pl.BlockSpec((None, CONV_WIDTH, cb), lambda b, c: (l, 0, c)),
                pl.BlockSpec((None, 1, cb), lambda b, c: (l, 0, c)),
                pl.BlockSpec((None, None, 4, cb, cb), lambda b, c: (l, c, 0, 0, 0)),
                pl.BlockSpec((None, 4, cb), lambda b, c: (l, 0, c)),
                pl.BlockSpec((None, 2, cb), lambda b, c: (l, 0, c)),
                pl.BlockSpec((None, 2, cb), (lambda b, c: (b, 0, c)) if h0_batched else (lambda b, c: (0, 0, c)))]
    return pl.pallas_call(
        functools.partial(_lru_kernel, chunk=_tile(seq, 512)),
        grid=(nseq, ncb),
        in_specs=in_specs,
        out_specs=[pl.BlockSpec((seq, cb), lambda b, c: (b, c)),
                   pl.BlockSpec((None, 2, cb), lambda b, c: (b, 0, c))],
        out_shape=[jax.ShapeDtypeStruct((nseq * seq, d_lru), bf16),
                   jax.ShapeDtypeStruct((nseq, 2, d_lru), f32)],
        scratch_shapes=[pltpu.VMEM((seq + 16, cb), f32), pltpu.VMEM((2, seq, cb), f32),
                        pltpu.VMEM((2, seq, cb), f32), pltpu.VMEM((seq, cb), f32)],
        compiler_params=_cparams(2),
        name="rglru",
    )(y, y, conv_w, conv_b3, wg, bg, lru_lambda, h0)


def _merge_kernel(atc_ref, atl_ref, rcc_ref, rcl_ref, ga_ref, gl_ref, x_ref, gt_ref, shf_ref, scf_ref,
                  g_ref, b_ref, wa_ref, wl_ref, wo_ref, x1_ref, h2_ref, ab_ref, rb_ref, *, alpha, n_ctx_tiles):
    @pl.when(pl.program_id(0) < n_ctx_tiles)
    def _():
        ab_ref[...] = atc_ref[...]
        rb_ref[...] = rcc_ref[...]

    @pl.when(pl.program_id(0) >= n_ctx_tiles)
    def _():
        ab_ref[...] = atl_ref[...]
        rb_ref[...] = rcl_ref[...]

    pa = jnp.dot(ab_ref[...], wa_ref[...], preferred_element_type=f32)
    pr = jnp.dot(rb_ref[...], wl_ref[...], preferred_element_type=f32)
    merged = jax.nn.sigmoid(ga_ref[...]) * pa + jax.nn.sigmoid(gl_ref[...]) * pr
    out = jnp.dot(merged.astype(bf16), wo_ref[...], preferred_element_type=f32)
    x1 = _layer_norm(alpha * x_ref[...] + gt_ref[0] * out, g_ref[0], b_ref[0])
    x1_ref[...] = x1
    h2_ref[...] = x1 * (1.0 + scf_ref[0]) + shf_ref[0]


def _merge(attn_c, attn_l, rec_c, rec_l, y, x, mod4, ln_g3, ln_b3, wa, wl, wo, l, n_ctx, dec_seq, ga_col, gl_col,
           alpha):
    n, d = x.shape
    tm = _tile(math.gcd(n_ctx, dec_seq), 512)
    n_ctx_tiles = n_ctx // tm
    tiles_per_seq = dec_seq // tm

    def seg(i):
        return jnp.where(i < n_ctx_tiles, 0, 1 + (i - n_ctx_tiles) // tiles_per_seq)

    base = l * MOD_ROWS * 6
    row = lambda i: (i, 0)
    ctx_row = lambda i: (jnp.minimum(i, n_ctx_tiles - 1), 0)
    lat_row = lambda i: (jnp.maximum(i - n_ctx_tiles, 0), 0)
    mod = lambda k: pl.BlockSpec((1, 1, d), lambda i: (base + seg(i) * 6 + k, 0, 0))
    wspec = pl.BlockSpec((None, d, d), lambda i: (l, 0, 0))
    return pl.pallas_call(
        functools.partial(_merge_kernel, alpha=alpha, n_ctx_tiles=n_ctx_tiles),
        grid=(n // tm,),
        in_specs=[pl.BlockSpec((tm, d), ctx_row), pl.BlockSpec((tm, d), lat_row),
                  pl.BlockSpec((tm, d), ctx_row), pl.BlockSpec((tm, d), lat_row),
                  pl.BlockSpec((tm, d), lambda i: (i, ga_col // d)),
                  pl.BlockSpec((tm, d), lambda i: (i, gl_col // d)),
                  pl.BlockSpec((tm, d), row), mod(2), mod(3), mod(4),
                  pl.BlockSpec((1, 1, d), lambda i: (2 * l, 0, 0)),
                  pl.BlockSpec((1, 1, d), lambda i: (2 * l, 0, 0)),
                  wspec, wspec, wspec],
        out_specs=[pl.BlockSpec((tm, d), row), pl.BlockSpec((tm, d), row)],
        out_shape=[jax.ShapeDtypeStruct((n, d), f32), jax.ShapeDtypeStruct((n, d), f32)],
        scratch_shapes=[pltpu.VMEM((tm, d), bf16), pltpu.VMEM((tm, d), bf16)],
        compiler_params=_cparams(1),
        name="merge_outproj_ln",
    )(attn_c, attn_l, rec_c, rec_l, y, y, x, mod4, mod4, mod4, ln_g3, ln_b3, wa, wl, wo)


def _route_kernel(h_ref, rw_ref, rb_ref, pos_ref, gate_ref, tab_ref, carry_ref):
    n_exp = rw_ref.shape[0]
    tm = h_ref.shape[0]
    gsz = n_exp // N_GROUPS

    @pl.when(pl.program_id(0) == 0)
    def _():
        carry_ref[...] = jnp.zeros_like(carry_ref)

    logits = lax.dot_general(rw_ref[...], h_ref[...], (((1,), (1,)), ((), ())),
                             precision=HIGHEST, preferred_element_type=f32)
    s = jax.nn.sigmoid(logits)
    sb = s + rb_ref[...]
    ninf = -jnp.inf
    gs_rows = []
    for g in range(N_GROUPS):
        blk = sb[g * gsz:(g + 1) * gsz]
        m1 = jnp.max(blk, axis=0, keepdims=True)
        cnt = jnp.sum((blk == m1).astype(f32), axis=0, keepdims=True)
        m2 = jnp.max(jnp.where(blk < m1, blk, ninf), axis=0, keepdims=True)
        gs_rows.append(m1 + jnp.where(cnt >= 2.0, m1, m2))
    gs = jnp.concatenate(gs_rows, axis=0)
    gi = lax.broadcasted_iota(i32, gs.shape, 0)
    grank = jnp.zeros(gs.shape, f32)
    for g2 in range(N_GROUPS):
        row = gs[g2:g2 + 1]
        grank += jnp.where((row > gs) | ((row == gs) & (g2 < gi)), 1.0, 0.0)
    gsel = grank < float(TOPK_GROUPS)
    sbm = jnp.concatenate([jnp.where(gsel[g:g + 1], sb[g * gsz:(g + 1) * gsz], ninf)
                           for g in range(N_GROUPS)], axis=0)
    ei = lax.broadcasted_iota(i32, sbm.shape, 0)
    erank = jnp.zeros(sbm.shape, f32)
    for e2 in range(n_exp):
        row = sbm[e2:e2 + 1]
        erank += jnp.where((row > sbm) | ((row == sbm) & (e2 < ei)), 1.0, 0.0)
    sel = erank < float(TOP_K)
    ssel = jnp.where(sel, s, 0.0)
    gate = ssel / jnp.sum(ssel, axis=0, keepdims=True) * ROUTE_SCALE
    selb = sel.astype(bf16)
    ti = lax.broadcasted_iota(i32, (tm, tm), 0)
    tj = lax.broadcasted_iota(i32, (tm, tm), 1)
    upper = (ti < tj).astype(bf16)
    pos_local = jnp.dot(selb, upper, preferred_element_type=f32)
    pos_ref[...] = jnp.where(sel, pos_local, -1.0)
    gate_ref[...] = gate
    cnt = jnp.sum(sel.astype(f32), axis=1, keepdims=True)
    cpad = jnp.floor((cnt + (SEG_ALIGN - 1)) * (1.0 / SEG_ALIGN)) * SEG_ALIGN
    cpad_l = jnp.broadcast_to(cpad, (n_exp, 128))
    li = lax.broadcasted_iota(i32, (n_exp, n_exp), 0)
    lj = lax.broadcasted_iota(i32, (n_exp, n_exp), 1)
    lower = (lj < li).astype(bf16)
    run_start = jnp.dot(lower, cpad_l.astype(bf16), preferred_element_type=f32)
    lane = lax.broadcasted_iota(i32, (n_exp, 128), 1)
    tab = jnp.where(lane == TAB_LEN, cpad_l, jnp.where(lane == TAB_BASE, carry_ref[...], run_start))
    tab_ref[...] = tab.astype(i32)
    carry_ref[...] = carry_ref[...] + cpad_l


def _route(h2, rwt, rb3, l):
    n, d = h2.shape
    n_exp = rwt.shape[1]
    tm = ROUTE_TILE
    assert n % tm == 0
    lst = pl.BlockSpec((n_exp, tm), lambda i: (0, i))
    return pl.pallas_call(
        _route_kernel,
        grid=(n // tm,),
        in_specs=[pl.BlockSpec((tm, d), lambda i: (i, 0)),
                  pl.BlockSpec((None, n_exp, d), lambda i: (l, 0, 0)),
                  pl.BlockSpec((None, n_exp, 1), lambda i: (l, 0, 0))],
        out_specs=[lst, lst, pl.BlockSpec((None, n_exp, 128), lambda i: (i, 0, 0))],
        out_shape=[jax.ShapeDtypeStruct((n_exp, n), f32), jax.ShapeDtypeStruct((n_exp, n), f32),
                   jax.ShapeDtypeStruct((n // tm, n_exp, 128), i32)],
        scratch_shapes=[pltpu.VMEM((n_exp, 128), f32)],
        compiler_params=_cparams(1),
        name="moe_route",
    )(h2, rwt, rb3)


def _pow2_pieces(count, max_rows, fn):
    size = max_rows
    while size >= SEG_ALIGN:
        off = count & (-2 * size)
        pl.when((count & size) != 0)(functools.partial(fn, off, size))
        size //= 2


def _wait_rows(count, max_rows, src_ref, dst_ref, sem):
    def wait(_, size):
        pltpu.make_async_copy(src_ref.at[pl.ds(0, size)], dst_ref.at[pl.ds(0, size)], sem).wait()

    _pow2_pieces(count, max_rows, wait)


def _pad_experts(x):
    return jnp.concatenate([x, jnp.zeros((128 - x.shape[0], x.shape[1]), x.dtype)], axis=0)


def _dispatch_kernel(pd_ref, tot_ref, ts_ref, tl_ref, pos_ref, tab_ref, h_ref, xs_ref, buf_ref, z_ref,
                     sem, *, n_exp, n_tiles, tile_rows, wait_max):
    t = pl.program_id(0)
    slot = t % 2
    tm = h_ref.shape[0]
    pieces = tile_rows // SEG_ALIGN

    @pl.when(t >= 2)
    def _():
        _wait_rows(tot_ref[t - 2], wait_max, buf_ref.at[slot], xs_ref, sem.at[slot])

    tab = _pad_experts(tab_ref[...].astype(f32))
    tab_t = tab.T
    start_row, len_row = tab_t[TAB_START:TAB_START + 1, :], tab_t[TAB_LEN:TAB_LEN + 1, :]
    rid = lax.broadcasted_iota(i32, (tile_rows, 128), 0).astype(f32)
    in_run = ((rid >= start_row) & (rid < start_row + len_row)).astype(bf16)
    pos_rows = jnp.dot(in_run, _pad_experts(pos_ref[...]).astype(bf16), preferred_element_type=f32)
    start16 = (tab[:, TAB_START:TAB_START + 1] * (1.0 / SEG_ALIGN)).astype(bf16)
    run_start = jnp.dot(in_run, jnp.broadcast_to(start16, (128, 128)), preferred_element_type=f32) * SEG_ALIGN
    ordinal = rid - run_start
    ordinal = jnp.concatenate([ordinal] * (tm // 128), axis=1)
    perm = jnp.where(pos_rows == ordinal, 1.0, 0.0).astype(bf16)
    buf_ref[slot] = jnp.dot(perm, h_ref[...].astype(bf16), preferred_element_type=f32).astype(bf16)

    def issue(i, _):
        for u in range(PIECE_UNROLL):
            p = i * PIECE_UNROLL + u
            src = pl.multiple_of(p * SEG_ALIGN, SEG_ALIGN)
            dst = pl.multiple_of(pd_ref[t * pieces + p], SEG_ALIGN)
            pltpu.make_async_copy(buf_ref.at[slot, pl.ds(src, SEG_ALIGN)], xs_ref.at[pl.ds(dst, SEG_ALIGN)],
                                  sem.at[slot]).start()
        return 0

    lax.fori_loop(0, tot_ref[t] // (SEG_ALIGN * PIECE_UNROLL), issue, 0)

    @pl.when(t == n_tiles - 1)
    def _():
        z_ref[...] = jnp.zeros_like(z_ref)

        def fill(e, _):
            def copy(off, size):
                dst = pl.multiple_of(ts_ref[e] + off, SEG_ALIGN)
                pltpu.make_async_copy(z_ref.at[pl.ds(0, size)], xs_ref.at[pl.ds(dst, size)], sem.at[2]).start()

            _pow2_pieces(tl_ref[e], MOE_BLOCK // 2, copy)
            return 0

        lax.fori_loop(0, n_exp, fill, 0)

        def drain(e, _):
            _wait_rows(tl_ref[e], MOE_BLOCK // 2, z_ref, xs_ref, sem.at[2])
            return 0

        lax.fori_loop(0, n_exp, drain, 0)
        if n_tiles > 1:
            _wait_rows(tot_ref[t - 1], wait_max, buf_ref.at[1 - slot], xs_ref, sem.at[1 - slot])
        _wait_rows(tot_ref[t], wait_max, buf_ref.at[slot], xs_ref, sem.at[slot])


def _tile_rows(tm, n_exp):
    return -(-(tm * TOP_K + n_exp * (SEG_ALIGN - 1)) // 128) * 128


def _wait_max(tile_rows):
    return 1 << (tile_rows.bit_length() - 1)


def _dispatch(h2, pos, tab, piece_dst, tot, tail_start, tail_len, n_rows, n_exp):
    n, d = h2.shape
    tm = ROUTE_TILE
    n_tiles = n // tm
    tile_rows = _tile_rows(tm, n_exp)
    assert n_exp <= 128 and tm % 128 == 0
    return pl.pallas_call(
        functools.partial(_dispatch_kernel, n_exp=n_exp, n_tiles=n_tiles, tile_rows=tile_rows,
                          wait_max=_wait_max(tile_rows)),
        grid_spec=pltpu.PrefetchScalarGridSpec(
            num_scalar_prefetch=4,
            grid=(n_tiles,),
            in_specs=[pl.BlockSpec((n_exp, tm), lambda i, *_: (0, i)),
                      pl.BlockSpec((None, n_exp, 128), lambda i, *_: (i, 0, 0)),
                      pl.BlockSpec((tm, d), lambda i, *_: (i, 0))],
            out_specs=pl.BlockSpec(memory_space=pl.ANY),
            scratch_shapes=[pltpu.VMEM((2, tile_rows, d), bf16), pltpu.VMEM((MOE_BLOCK // 2, d), bf16),
                            pltpu.SemaphoreType.DMA((3,))]),
        out_shape=jax.ShapeDtypeStruct((n_rows, d), bf16),
        compiler_params=_cparams(1),
        name="moe_dispatch",
    )(piece_dst, tot, tail_start, tail_len, pos, tab, h2)


def _expert_kernel(be_ref, nu_ref, x_ref, wg_ref, wu_ref, wd_ref, o_ref):
    @pl.when(pl.program_id(0) < nu_ref[0])
    def _():
        x = x_ref[...]
        g = jnp.dot(x, wg_ref[...].astype(bf16), preferred_element_type=f32)
        u = jnp.dot(x, wu_ref[...].astype(bf16), preferred_element_type=f32)
        h = (g * jax.nn.sigmoid(g)) * u
        o_ref[...] = jnp.dot(h.astype(bf16), wd_ref[...].astype(bf16),
                             preferred_element_type=f32).astype(bf16)


def _experts(xs, blk_e, n_used, we_g, we_u, we_d, l):
    n_rows, d = xs.shape
    f = we_g.shape[-1]
    nblk = n_rows // MOE_BLOCK
    blk = lambda b, be, nu: (jnp.minimum(b, nu[0] - 1), 0)
    wsel = lambda b, be, nu: (l, be[jnp.minimum(b, nu[0] - 1)], 0, 0)
    return pl.pallas_call(
        _expert_kernel,
        grid_spec=pltpu.PrefetchScalarGridSpec(
            num_scalar_prefetch=2,
            grid=(nblk,),
            in_specs=[pl.BlockSpec((MOE_BLOCK, d), blk),
                      pl.BlockSpec((None, None, d, f), wsel),
                      pl.BlockSpec((None, None, d, f), wsel),
                      pl.BlockSpec((None, None, f, d), wsel)],
            out_specs=pl.BlockSpec((MOE_BLOCK, d), blk)),
        out_shape=jax.ShapeDtypeStruct((n_rows, d), bf16),
        compiler_params=_cparams(1),
        name="moe_experts",
    )(blk_e, n_used, xs, we_g, we_u, we_d)


def _combine_kernel(pd_ref, tot_ref, pos_ref, gate_ref, tab_ref, ys_ref, x1_ref, h2_ref, gt_ref, g_ref, b_ref,
                    sg_ref, su_ref, sd_ref, o_ref, buf_ref, sem, *, alpha, n_exp, n_tiles, tile_rows, wait_max):
    t = pl.program_id(0)
    slot = t % 2
    tm = x1_ref.shape[0]
    pieces = tile_rows // SEG_ALIGN

    def fetch(tile, dst_slot):
        def issue(i, _):
            for u in range(PIECE_UNROLL):
                p = i * PIECE_UNROLL + u
                src = pl.multiple_of(pd_ref[tile * pieces + p], SEG_ALIGN)
                dst = pl.multiple_of(p * SEG_ALIGN, SEG_ALIGN)
                pltpu.make_async_copy(ys_ref.at[pl.ds(src, SEG_ALIGN)], buf_ref.at[dst_slot, pl.ds(dst, SEG_ALIGN)],
                                      sem.at[dst_slot]).start()
            return 0

        lax.fori_loop(0, tot_ref[tile] // (SEG_ALIGN * PIECE_UNROLL), issue, 0)

    @pl.when(t == 0)
    def _():
        buf_ref[...] = jnp.zeros_like(buf_ref)
        fetch(0, 0)

    @pl.when(t + 1 < n_tiles)
    def _():
        fetch(t + 1, 1 - slot)

    hb = h2_ref[...].astype(bf16)
    g = jnp.dot(hb, sg_ref[...], preferred_element_type=f32)
    u = jnp.dot(hb, su_ref[...], preferred_element_type=f32)
    shared = jnp.dot(((g * jax.nn.sigmoid(g)) * u).astype(bf16), sd_ref[...], preferred_element_type=f32)
    tab = _pad_experts(tab_ref[...].astype(f32))
    start_col, len_col = tab[:, TAB_START:TAB_START + 1], tab[:, TAB_LEN:TAB_LEN + 1]
    rid = lax.broadcasted_iota(i32, (128, tile_rows), 1).astype(f32)
    in_run = (rid >= start_col) & (rid < start_col + len_col)
    ordinal = rid[0:1, :] - jnp.sum(jnp.where(in_run, start_col, 0.0), axis=0, keepdims=True)
    in_run = in_run.astype(bf16)
    pos_t = _pad_experts(pos_ref[...]).T.astype(bf16)
    gate_t = _pad_experts(gate_ref[...]).T.astype(bf16)
    pos_rows = jnp.dot(pos_t, in_run, preferred_element_type=f32)
    gate_rows = jnp.dot(gate_t, in_run, preferred_element_type=f32)
    wgt = jnp.where(pos_rows == ordinal, gate_rows, 0.0).astype(bf16)
    _wait_rows(tot_ref[t], wait_max, ys_ref, buf_ref.at[slot], sem.at[slot])
    routed = jnp.dot(wgt, buf_ref[slot], preferred_element_type=f32)
    o_ref[...] = _layer_norm(alpha * x1_ref[...] + gt_ref[0] * (shared + routed), g_ref[0], b_ref[0])


def _combine(ys, pos, gate, tab, piece_dst, tot, x1, h2, mod4, ln_g3, ln_b3, sg, su, sd, l, n_ctx, dec_seq, alpha,
             n_exp):
    n, d = x1.shape
    f = sg.shape[-1]
    tm = ROUTE_TILE
    assert n_ctx % tm == 0 and dec_seq % tm == 0
    n_tiles = n // tm
    tile_rows = _tile_rows(tm, n_exp)
    n_ctx_tiles = n_ctx // tm
    tiles_per_seq = dec_seq // tm

    def seg(i):
        return jnp.where(i < n_ctx_tiles, 0, 1 + (i - n_ctx_tiles) // tiles_per_seq)

    base = l * MOD_ROWS * 6
    row = lambda i, *_: (i, 0)
    lst = lambda i, *_: (0, i)
    return pl.pallas_call(
        functools.partial(_combine_kernel, alpha=alpha, n_exp=n_exp, n_tiles=n_tiles, tile_rows=tile_rows,
                          wait_max=_wait_max(tile_rows)),
        grid_spec=pltpu.PrefetchScalarGridSpec(
            num_scalar_prefetch=2,
            grid=(n_tiles,),
            in_specs=[pl.BlockSpec((n_exp, tm), lst), pl.BlockSpec((n_exp, tm), lst),
                      pl.BlockSpec((None, n_exp, 128), lambda i, *_: (i, 0, 0)),
                      pl.BlockSpec(memory_space=pl.ANY),
                      pl.BlockSpec((tm, d), row), pl.BlockSpec((tm, d), row),
                      pl.BlockSpec((1, 1, d), lambda i, *_: (base + seg(i) * 6 + 5, 0, 0)),
                      pl.BlockSpec((1, 1, d), lambda i, *_: (2 * l + 1, 0, 0)),
                      pl.BlockSpec((1, 1, d), lambda i, *_: (2 * l + 1, 0, 0)),
                      pl.BlockSpec((None, d, f), lambda i, *_: (l, 0, 0)),
                      pl.BlockSpec((None, d, f), lambda i, *_: (l, 0, 0)),
                      pl.BlockSpec((None, f, d), lambda i, *_: (l, 0, 0))],
            out_specs=pl.BlockSpec((tm, d), row),
            scratch_shapes=[pltpu.VMEM((2, tile_rows, d), bf16), pltpu.SemaphoreType.DMA((2,))]),
        out_shape=jax.ShapeDtypeStruct((n, d), f32),
        compiler_params=_cparams(1),
        name="moe_combine_ln",
    )(piece_dst, tot, pos, gate, tab, ys, x1, h2, mod4, ln_g3, ln_b3, sg, su, sd)


def _rope_tables(seq, width):
    rows = seq // GRID_W
    row = jnp.repeat(jnp.arange(rows, dtype=f32), GRID_W)
    col = jnp.tile(jnp.arange(GRID_W, dtype=f32), rows)
    half = HEAD_DIM // 2
    inv = ROPE_BASE ** (-jnp.arange(0, half, 2, dtype=f32) / half)
    lane = jnp.arange(width)
    d = lane % HEAD_DIM
    pos = jnp.where((d // half)[None, :] == 0, row[:, None], col[:, None])
    ang = pos * inv[d % (HEAD_DIM // 4)][None, :]
    sign = jnp.where((d % half) < HEAD_DIM // 4, -1.0, 1.0)[None, :]
    return jnp.cos(ang), jnp.sin(ang) * sign


def _block_diag_gates(gate_w_r, gate_w_i, cb):
    depth, _, nb, bw, _ = gate_w_r.shape
    per = cb // bw
    w = jnp.stack([gate_w_r[:, 0], gate_w_i[:, 0], gate_w_r[:, 1], gate_w_i[:, 1]], axis=1)
    w = w.reshape(depth, 4, nb // per, per, bw, bw)
    eye = jnp.eye(per, dtype=w.dtype)
    bd = jnp.einsum("lgcaio,ab->lcgaibo", w, eye)
    return bd.reshape(depth, nb // per, 4, cb, cb).astype(bf16)


def kernel(x_prompt, x_sample, c, cache_k, cache_v, state_lru, c_ctx, w_mod, b_mod, w_in, lam_param, subln_w,
           conv_w, conv_b, gate_w_r, gate_b_r, gate_w_i, gate_b_i, lru_lambda, w_br_attn, w_br_lru, w_out,
           ln_g, ln_b, router_w, router_b, exp_w_gate, exp_w_up, exp_w_down, sh_w_gate, sh_w_up, sh_w_down):
    batch, seq, d = x_prompt.shape
    dec_batch, dec_seq, _ = x_sample.shape
    depth = w_mod.shape[0]
    past = cache_k.shape[2]
    d_lru = conv_w.shape[-1]
    n_exp = router_w.shape[-1]
    qk = N_HEADS * 2 * HEAD_DIM
    attn_w = N_HEADS * V_DIM
    n_ctx, n_lat = batch * seq, dec_batch * dec_seq
    n = n_ctx + n_lat
    alpha = (2 * depth) ** 0.25
    assert 1 + dec_batch <= MOD_ROWS and n_ctx % dec_seq == 0

    cond = jnp.concatenate([c_ctx[None], c, jnp.zeros((MOD_ROWS - 1 - dec_batch, d), f32)], 0)
    w_in_b = w_in.astype(bf16)
    wa, wl, wo = w_br_attn.astype(bf16), w_br_lru.astype(bf16), w_out.astype(bf16)
    sg, su, sd = sh_w_gate.astype(bf16), sh_w_up.astype(bf16), sh_w_down.astype(bf16)
    rwt = jnp.swapaxes(router_w, 1, 2)
    rb3 = router_b.reshape(depth, n_exp, 1)
    wg = _block_diag_gates(0.5 * gate_w_r, 0.5 * gate_w_i, 256)
    bg = 0.5 * jnp.stack([gate_b_r[:, 0], gate_b_i[:, 0], gate_b_r[:, 1], gate_b_i[:, 1]], axis=1)
    conv_b3 = conv_b.reshape(depth, 1, d_lru)
    subln_w3 = subln_w.reshape(depth, 1, V_DIM)
    ln_g3 = ln_g.reshape(depth * 2, 1, d)
    ln_b3 = ln_b.reshape(depth * 2, 1, d)
    ck = cache_k.reshape(dec_batch, depth, past, N_HEADS * 2 * HEAD_DIM)
    cv = cache_v.reshape(dec_batch, depth, past, attn_w)
    cos_t, sin_t = _rope_tables(dec_seq, V_DIM)
    h0_ctx = jnp.zeros((1, 2, d_lru), f32)

    mod4 = _mod_table(cond, w_mod, b_mod).reshape(depth * MOD_ROWS * 6, 1, d)
    x = jnp.concatenate([x_prompt.reshape(n_ctx, d), x_sample.reshape(n_lat, d)], 0)
    n_tiles = n // ROUTE_TILE
    spare0 = n * TOP_K + n_tiles * n_exp * (SEG_ALIGN - 1) + n_exp * (MOE_BLOCK - 1)
    spare0 = -(-spare0 // SEG_ALIGN) * SEG_ALIGN
    n_rows = -(-(spare0 + n_tiles * (PIECE_UNROLL - 1) * SEG_ALIGN) // MOE_BLOCK) * MOE_BLOCK
    xr_col, yr_col = 2 * qk + attn_w, 2 * qk + attn_w + d_lru
    ga_col, gl_col = yr_col + d_lru, yr_col + d_lru + d

    new_k, new_v, new_s = [], [], []
    for l in range(depth):
        lam_init = 0.8 - 0.6 * math.exp(-0.3 * l)
        y = _inproj(x, mod4, w_in_b, cos_t, sin_t, l, n_ctx, dec_seq, 2 * qk)
        new_k.append(y[:n_ctx, qk:2 * qk].reshape(batch, seq, N_HEADS, 2 * HEAD_DIM))
        new_v.append(y[:n_ctx, 2 * qk:2 * qk + attn_w].reshape(batch, seq, N_HEADS, V_DIM))
        attn_c = _attention(y, lam_param, subln_w3, l, lam_init, 0, batch, seq)
        attn_l = _attention(y, lam_param, subln_w3, l, lam_init, n_ctx, dec_batch, dec_seq, ck, cv)
        rec_c, hfin = _lru(y, conv_w, conv_b3, wg, bg, lru_lambda, h0_ctx, l, 0, batch, seq, xr_col, yr_col)
        rec_l, _ = _lru(y, conv_w, conv_b3, wg, bg, lru_lambda, state_lru[:, l], l, n_ctx, dec_batch, dec_seq,
                        xr_col, yr_col)
        new_s.append(hfin)
        x1, h2 = _merge(attn_c, attn_l, rec_c, rec_l, y, x, mod4, ln_g3, ln_b3, wa, wl, wo, l, n_ctx, dec_seq, ga_col, gl_col, alpha)
        pos, gate, tab = _route(h2, rwt, rb3, l)
        cpad, grel, start = tab[:, :, TAB_LEN], tab[:, :, TAB_BASE], tab[:, :, TAB_START]
        used = grel[-1] + cpad[-1]
        padded = (used + MOE_BLOCK - 1) // MOE_BLOCK * MOE_BLOCK
        pad_end = jnp.cumsum(padded)
        pad_start = pad_end - padded
        tot = jnp.sum(cpad, axis=1).astype(i32)
        blk_row0 = jnp.arange(n_rows // MOE_BLOCK, dtype=i32) * MOE_BLOCK
        blk_e = jnp.minimum(jnp.sum(pad_end[None, :] <= blk_row0[:, None], axis=1), n_exp - 1).astype(i32)
        n_used = (pad_end[-1:] // MOE_BLOCK).astype(i32)
        piece0 = jnp.arange(_tile_rows(ROUTE_TILE, n_exp) // SEG_ALIGN, dtype=i32)[None, :, None] * SEG_ALIGN
        in_run = (piece0 >= start[:, None, :]) & (piece0 < (start + cpad)[:, None, :])
        shift = (pad_start[None, :] + grel - start)[:, None, :]
        piece0 = piece0[:, :, 0]
        piece_row = piece0 + jnp.sum(jnp.where(in_run, shift, 0), axis=-1)
        real = piece0 < tot[:, None]
        spare = (spare0 + jnp.arange(n_tiles, dtype=i32)[:, None] * ((PIECE_UNROLL - 1) * SEG_ALIGN)
                 + piece0 - tot[:, None])
        piece_dst = jnp.where(real, piece_row, spare).reshape(-1).astype(i32)
        piece_src = jnp.where(real, piece_row, 0).reshape(-1).astype(i32)
        step = SEG_ALIGN * PIECE_UNROLL
        tot = (tot + step - 1) // step * step
        xs = _dispatch(h2, pos, tab, piece_dst, tot, (pad_start + used).astype(i32),
                       (padded - used).astype(i32), n_rows, n_exp)
        ys = _experts(xs, blk_e, n_used, exp_w_gate, exp_w_up, exp_w_down, l)
        x = _combine(ys, pos, gate, tab, piece_src, tot, x1, h2, mod4, ln_g3, ln_b3, sg, su, sd, l, n_ctx, dec_seq,
                     alpha, n_exp)
    return (x[:n_ctx].reshape(batch, seq, d), x[n_ctx:].reshape(dec_batch, dec_seq, d),
            jnp.stack(new_k, 1), jnp.stack(new_v, 1), jnp.stack(new_s, 1))
```

```python
import functools
import math

import jax
import jax.numpy as jnp
from jax import lax
from jax.experimental import pallas as pl
from jax.experimental.pallas import tpu as pltpu

f32, bf16, i32 = jnp.float32, jnp.bfloat16, jnp.int32

GRID_W = 64
N_HEADS = 8
HEAD_DIM = 64
V_DIM = 2 * HEAD_DIM
ROPE_BASE = 10000.0
LRU_BLOCKS = 16
CONV_WIDTH = 4
LRU_C = 8.0
TOP_K = 8
N_GROUPS = 8
TOPK_GROUPS = 4
ROUTE_SCALE = 2.5
LN_EPS = 1e-5
MOD_ROWS = 8
MOE_BLOCK = 1024
ROUTE_TILE = 256
SEG_ALIGN = 16
PIECE_UNROLL = 4
ROPE_CHUNK = 256
TAB_LEN, TAB_BASE, TAB_START = 0, 1, 2
SCAN_UNROLL = 4
VMEM_LIMIT = 56 * 1024 * 1024
HIGHEST = lax.Precision.HIGHEST


def _cparams(n_axes):
    return pltpu.CompilerParams(dimension_semantics=("arbitrary",) * n_axes,
                                vmem_limit_bytes=VMEM_LIMIT)


def _tile(n, pref):
    t = min(n, pref)
    while n % t:
        t -= 8
    assert t > 0 and n % t == 0
    return t


def _layer_norm(z, g, b):
    mu = jnp.mean(z, -1, keepdims=True)
    zc = z - mu
    var = jnp.mean(zc * zc, -1, keepdims=True)
    return zc * lax.rsqrt(var + LN_EPS) * g + b


def _mod_kernel(c_ref, w_ref, b_ref, o_ref):
    c = c_ref[...]
    a = c * jax.nn.sigmoid(c)
    o_ref[...] = jnp.dot(a, w_ref[...], precision=HIGHEST, preferred_element_type=f32) + b_ref[...]


def _mod_table(cond, w_mod, b_mod):
    depth, d, d6 = w_mod.shape
    tn = _tile(d6, 1024)
    return pl.pallas_call(
        _mod_kernel,
        grid=(depth, d6 // tn),
        in_specs=[pl.BlockSpec((MOD_ROWS, d), lambda l, j: (0, 0)),
                  pl.BlockSpec((None, d, tn), lambda l, j: (l, 0, j)),
                  pl.BlockSpec((None, 1, tn), lambda l, j: (l, 0, j))],
        out_specs=pl.BlockSpec((None, MOD_ROWS, tn), lambda l, j: (l, 0, j)),
        out_shape=jax.ShapeDtypeStruct((depth, MOD_ROWS, d6), f32),
        compiler_params=_cparams(2),
        name="mod_table",
    )(cond, w_mod, b_mod.reshape(depth, 1, d6))


def _inproj_kernel(x_ref, sh_ref, sc_ref, w_ref, cos_ref, sin_ref, o_ref, hb_ref, *,
                   n_ctx_tiles, n_rope_cols, tn):
    i = pl.program_id(0)
    j = pl.program_id(1)

    @pl.when(j == 0)
    def _():
        hb_ref[...] = (x_ref[...] * (1.0 + sc_ref[0]) + sh_ref[0]).astype(bf16)

    rope = jnp.logical_and(i >= n_ctx_tiles, j < n_rope_cols)

    @pl.when(rope)
    def _():
        cos, sin = cos_ref[...], sin_ref[...]
        lane = lax.broadcasted_iota(i32, cos.shape, 1)
        first = (lane % (HEAD_DIM // 2)) < (HEAD_DIM // 4)
        for c in range(tn // ROPE_CHUNK):
            acc = jnp.dot(hb_ref[...], w_ref[:, c * ROPE_CHUNK:(c + 1) * ROPE_CHUNK], preferred_element_type=f32)
            for h in range(ROPE_CHUNK // V_DIM):
                a = acc[:, h * V_DIM:(h + 1) * V_DIM]
                partner = jnp.where(first, pltpu.roll(a, V_DIM - HEAD_DIM // 4, 1), pltpu.roll(a, HEAD_DIM // 4, 1))
                col = c * ROPE_CHUNK + h * V_DIM
                o_ref[:, col:col + V_DIM] = a * cos + partner * sin

    @pl.when(jnp.logical_not(rope))
    def _():
        o_ref[...] = jnp.dot(hb_ref[...], w_ref[...], preferred_element_type=f32)


def _inproj(x, mod4, w_in_b, cos_t, sin_t, l, n_ctx, dec_seq, qk_cols):
    n, d = x.shape
    d_in = w_in_b.shape[-1]
    tm = _tile(math.gcd(n_ctx, dec_seq), 1024)
    tn = _tile(math.gcd(d_in, qk_cols), 1024)
    assert tn % V_DIM == 0
    n_ctx_tiles = n_ctx // tm
    tiles_per_seq = dec_seq // tm

    def seg(i):
        return jnp.where(i < n_ctx_tiles, 0, 1 + (i - n_ctx_tiles) // tiles_per_seq)

    def pos(i):
        return jnp.where(i < n_ctx_tiles, 0, (i - n_ctx_tiles) % tiles_per_seq)

    base = l * MOD_ROWS * 6
    return pl.pallas_call(
        functools.partial(_inproj_kernel, n_ctx_tiles=n_ctx_tiles, n_rope_cols=qk_cols // tn, tn=tn),
        grid=(n // tm, d_in // tn),
        in_specs=[pl.BlockSpec((tm, d), lambda i, j: (i, 0)),
                  pl.BlockSpec((1, 1, d), lambda i, j: (base + seg(i) * 6 + 0, 0, 0)),
                  pl.BlockSpec((1, 1, d), lambda i, j: (base + seg(i) * 6 + 1, 0, 0)),
                  pl.BlockSpec((None, d, tn), lambda i, j: (l, 0, j)),
                  pl.BlockSpec((tm, V_DIM), lambda i, j: (pos(i), 0)),
                  pl.BlockSpec((tm, V_DIM), lambda i, j: (pos(i), 0))],
        out_specs=pl.BlockSpec((tm, tn), lambda i, j: (i, j)),
        out_shape=jax.ShapeDtypeStruct((n, d_in), f32),
        scratch_shapes=[pltpu.VMEM((tm, d), bf16)],
        compiler_params=_cparams(2),
        name="in_proj",
    )(x, mod4, mod4, w_in_b, cos_t, sin_t)


def _key_chunk(lk):
    for ck in (512, 384, 256, 128):
        if lk % ck == 0:
            return ck
    return lk


def _attn_kernel(*refs, has_cache, lam_init, past, heads):
    if has_cache:
        q_ref, k_ref, v_ref, ck_ref, cv_ref, lp_ref, sw_ref, o_ref, kt_ref, vb_ref = refs
    else:
        q_ref, k_ref, v_ref, lp_ref, sw_ref, o_ref, kt_ref, vb_ref = refs

    @pl.when(pl.program_id(2) == 0)
    def _():
        for h in range(heads):
            cols = slice(h * V_DIM, (h + 1) * V_DIM)
            if has_cache:
                kt_ref[h, :, :past] = ck_ref[:, cols].T.astype(bf16)
                vb_ref[h, :past, :V_DIM] = cv_ref[:, cols].astype(bf16)
            kt_ref[h, :, past:] = k_ref[:, cols].T.astype(bf16)
            vb_ref[h, past:, :V_DIM] = v_ref[:, cols].astype(bf16)
            vb_ref[h, :, V_DIM:] = jnp.ones((vb_ref.shape[1], V_DIM), bf16)

    lp = lp_ref[...]
    lam = (jnp.exp(jnp.sum(lp[0:1] * lp[1:2], keepdims=True))
           - jnp.exp(jnp.sum(lp[2:3] * lp[3:4], keepdims=True)) + lam_init)
    for h in range(heads):
        cols = slice(h * V_DIM, (h + 1) * V_DIM)
        o = _attn_head(q_ref[:, cols], kt_ref.at[h], vb_ref.at[h], lam)
        o = o * lax.rsqrt(jnp.mean(o * o, -1, keepdims=True) + LN_EPS) * sw_ref[...]
        o_ref[:, cols] = (o * (1.0 - lam_init)).astype(o_ref.dtype)


def _attn_head(q, kt_ref, vb_ref, lam):
    tq = q.shape[0]
    q = q * (HEAD_DIM ** -0.5 * math.log2(math.e))
    lane = lax.broadcasted_iota(i32, q.shape, 1)
    q2 = jnp.concatenate([jnp.where(lane < HEAD_DIM, q, 0.0).astype(bf16),
                          jnp.where(lane >= HEAD_DIM, q, 0.0).astype(bf16)], axis=0)
    lk = kt_ref.shape[1]
    ck = _key_chunk(lk)
    acc = jnp.zeros((2 * tq, 2 * V_DIM), f32)
    m_run = jnp.full((2 * tq, 1), -jnp.inf, f32)
    for c in range(lk // ck):
        s_c = jnp.dot(q2, kt_ref[:, c * ck:(c + 1) * ck], preferred_element_type=f32)
        m_new = jnp.maximum(m_run, jnp.max(s_c, axis=-1, keepdims=True))
        e_c = jnp.exp2(s_c - m_new).astype(bf16)
        acc = acc * jnp.exp2(m_run - m_new) + jnp.dot(e_c, vb_ref[c * ck:(c + 1) * ck, :],
                                                        preferred_element_type=f32)
        m_run = m_new
    o2 = acc[:, :V_DIM] / acc[:, V_DIM:]
    return o2[:tq] - lam * o2[tq:]


def _attention(y, lam_param, subln_w3, l, lam_init, row0, nseq, seq, cache_k=None, cache_v=None):
    has_cache = cache_k is not None
    past = cache_k.shape[2] if has_cache else 0
    lk = past + seq
    tq = _tile(seq, 1024)
    nq = seq // tq
    heads = N_HEADS if lk <= 512 else 1
    hg = N_HEADS // heads
    w = heads * V_DIM
    assert row0 % seq == 0
    qb0, kb0 = row0 // tq, row0 // seq
    in_specs = [pl.BlockSpec((tq, w), lambda b, h, i: (qb0 + b * nq + i, h)),
                pl.BlockSpec((seq, w), lambda b, h, i: (kb0 + b, hg + h)),
                pl.BlockSpec((seq, w), lambda b, h, i: (kb0 + b, 2 * hg + h))]
    args = [y, y, y]
    if has_cache:
        in_specs += [pl.BlockSpec((None, None, past, w), lambda b, h, i: (b, l, 0, h)),
                     pl.BlockSpec((None, None, past, w), lambda b, h, i: (b, l, 0, h))]
        args += [cache_k, cache_v]
    in_specs += [pl.BlockSpec((None, 4, HEAD_DIM), lambda b, h, i: (l, 0, 0)),
                 pl.BlockSpec((None, 1, V_DIM), lambda b, h, i: (l, 0, 0))]
    args += [lam_param, subln_w3]
    return pl.pallas_call(
        functools.partial(_attn_kernel, has_cache=has_cache, lam_init=lam_init, past=past, heads=heads),
        grid=(nseq, hg, nq),
        in_specs=in_specs,
        out_specs=pl.BlockSpec((tq, w), lambda b, h, i: (b * nq + i, h)),
        out_shape=jax.ShapeDtypeStruct((nseq * seq, N_HEADS * V_DIM), bf16),
        scratch_shapes=[pltpu.VMEM((heads, V_DIM, lk), bf16), pltpu.VMEM((heads, lk, 2 * V_DIM), bf16)],
        compiler_params=_cparams(3),
        name="diff_attn_lat" if has_cache else "diff_attn_ctx",
    )(*args)


def _lru_kernel(xr_ref, yr_ref, cw_ref, cb_ref, wg_ref, bg_ref, lam_ref, h0_ref,
                rec_ref, hfin_ref, xp_ref, a_ref, b_ref, hf_ref, *, chunk):
    seq, c = xr_ref.shape
    xp_ref[0:8, :] = jnp.zeros((8, c), f32)
    xp_ref[8:8 + seq, :] = xr_ref[...]
    xp_ref[8 + seq:16 + seq, :] = jnp.zeros((8, c), f32)
    lam = lam_ref[...]
    nlam = -lam
    softplus = jnp.maximum(nlam, 0.0) + jnp.log(1.0 + jnp.exp(-jnp.abs(nlam)))
    kh = (-0.5 * LRU_C) * softplus
    for ci in range(seq // chunk):
        t0 = ci * chunk
        xc = cb_ref[...] + sum(cw_ref[j:j + 1, :] * xp_ref[8 + t0 + j - 1:8 + t0 + j - 1 + chunk, :]
                               for j in range(CONV_WIDTH))
        xcb = xc.astype(bf16)
        hx = 0.5 * xc
        for d in range(2):
            tr = jnp.tanh(jnp.dot(xcb, wg_ref[2 * d], preferred_element_type=f32) + bg_ref[2 * d:2 * d + 1, :])
            ti = jnp.tanh(jnp.dot(xcb, wg_ref[2 * d + 1], preferred_element_type=f32)
                          + bg_ref[2 * d + 1:2 * d + 2, :])
            log_a = tr * kh[d:d + 1, :] + kh[d:d + 1, :]
            a = jnp.exp(log_a)
            a_ref[d, t0:t0 + chunk, :] = a
            u = -jnp.tanh(log_a) * (a * a + 1.0)
            b_ref[d, t0:t0 + chunk, :] = jnp.where(u > 0.0, u * lax.rsqrt(u), 0.0) * (ti * hx + hx)

    rows = lax.broadcasted_iota(i32, (8, c), 0)
    nch = seq // 8

    def scan_body(i, carry):
        cf, cb = carry
        tf = pl.multiple_of(8 * i, 8)
        af = a_ref[0, pl.ds(tf, 8), :]
        bf = b_ref[0, pl.ds(tf, 8), :]
        for d in (1, 2, 4):
            keep = rows >= d
            bf = af * jnp.where(keep, pltpu.roll(bf, d, 0), 0.0) + bf
            af = af * jnp.where(keep, pltpu.roll(af, d, 0), 1.0)
        hf = af * cf + bf
        hf_ref[pl.ds(tf, 8), :] = hf
        tb = pl.multiple_of(8 * (nch - 1 - i), 8)
        ab = a_ref[1, pl.ds(tb, 8), :]
        bb = b_ref[1, pl.ds(tb, 8), :]
        for d in (1, 2, 4):
            keep = rows < 8 - d
            bb = ab * jnp.where(keep, pltpu.roll(bb, 8 - d, 0), 0.0) + bb
            ab = ab * jnp.where(keep, pltpu.roll(ab, 8 - d, 0), 1.0)
        hb = ab * cb + bb
        xp_ref[pl.ds(tb, 8), :] = hb
        return hf[7:8, :], hb[0:1, :]

    h0 = h0_ref[...]
    cf, cb = lax.fori_loop(0, nch, scan_body, (h0[0:1, :], h0[1:2, :]), unroll=SCAN_UNROLL)
    hfin_ref[0:1, :] = cf
    hfin_ref[1:2, :] = cb
    for ci in range(seq // chunk):
        t0 = ci * chunk
        rec_ref[t0:t0 + chunk, :] = ((hf_ref[t0:t0 + chunk, :] + xp_ref[t0:t0 + chunk, :])
                                     * jax.nn.gelu(yr_ref[t0:t0 + chunk, :])).astype(rec_ref.dtype)


def _lru(y, conv_w, conv_b3, wg, bg, lru_lambda, h0, l, row0, nseq, seq, xr_col, yr_col):
    d_lru = conv_w.shape[-1]
    cb = 256
    ncb = d_lru // cb
    assert row0 % seq == 0
    rb0 = row0 // seq
    h0_batched = h0.shape[0] > 1
    in_specs = [pl.BlockSpec((seq, cb), lambda b, c: (rb0 + b, xr_col // cb + c)),
                pl.BlockSpec((seq, cb), lambda b, c: (rb0 + b, yr_col // cb + c)),
                pl.BlockSpec((None, CONV_WIDTH, cb), lambda b, c: (l, 0, c)),
                pl.BlockSpec((None, 1, cb), lambda b, c: (l, 0, c)),
                pl.BlockSpec((None, None, 4, cb, cb), lambda b, c: (l, c, 0, 0, 0)),
                pl.BlockSpec((None, 4, cb), lambda b, c: (l, 0, c)),
                pl.BlockSpec((None, 2, cb), lambda b, c: (l, 0, c)),
                pl.BlockSpec((None, 2, cb), (lambda b, c: (b, 0, c)) if h0_batched else (lambda b, c: (0, 0, c)))]
    return pl.pallas_call(
        functools.partial(_lru_kernel, chunk=_tile(seq, 512)),
        grid=(nseq, ncb),
        in_specs=in_specs,
        out_specs=[pl.BlockSpec((seq, cb), lambda b, c: (b, c)),
                   pl.BlockSpec((None, 2, cb), lambda b, c: (b, 0, c))],
        out_shape=[jax.ShapeDtypeStruct((nseq * seq, d_lru), bf16),
                   jax.ShapeDtypeStruct((nseq, 2, d_lru), f32)],
        scratch_shapes=[pltpu.VMEM((seq + 16, cb), f32), pltpu.VMEM((2, seq, cb), f32),
                        pltpu.VMEM((2, seq, cb), f32), pltpu.VMEM((seq, cb), f32)],
        compiler_params=_cparams(2),
        name="rglru",
    )(y, y, conv_w, conv_b3, wg, bg, lru_lambda, h0)


def _merge_kernel(atc_ref, atl_ref, rcc_ref, rcl_ref, ga_ref, gl_ref, x_ref, gt_ref, shf_ref, scf_ref,
                  g_ref, b_ref, wa_ref, wl_ref, wo_ref, x1_ref, h2_ref, ab_ref, rb_ref, *, alpha, n_ctx_tiles):
    @pl.when(pl.program_id(0) < n_ctx_tiles)
    def _():
        ab_ref[...] = atc_ref[...]
        rb_ref[...] = rcc_ref[...]

    @pl.when(pl.program_id(0) >= n_ctx_tiles)
    def _():
        ab_ref[...] = atl_ref[...]
        rb_ref[...] = rcl_ref[...]

    pa = jnp.dot(ab_ref[...], wa_ref[...], preferred_element_type=f32)
    pr = jnp.dot(rb_ref[...], wl_ref[...], preferred_element_type=f32)
    merged = jax.nn.sigmoid(ga_ref[...]) * pa + jax.nn.sigmoid(gl_ref[...]) * pr
    out = jnp.dot(merged.astype(bf16), wo_ref[...], preferred_element_type=f32)
    x1 = _layer_norm(alpha * x_ref[...] + gt_ref[0] * out, g_ref[0], b_ref[0])
    x1_ref[...] = x1
    h2_ref[...] = x1 * (1.0 + scf_ref[0]) + shf_ref[0]


def _merge(attn_c, attn_l, rec_c, rec_l, y, x, mod4, ln_g3, ln_b3, wa, wl, wo, l, n_ctx, dec_seq, ga_col, gl_col,
           alpha):
    n, d = x.shape
    tm = _tile(math.gcd(n_ctx, dec_seq), 512)
    n_ctx_tiles = n_ctx // tm
    tiles_per_seq = dec_seq // tm

    def seg(i):
        return jnp.where(i < n_ctx_tiles, 0, 1 + (i - n_ctx_tiles) // tiles_per_seq)

    base = l * MOD_ROWS * 6
    row = lambda i: (i, 0)
    ctx_row = lambda i: (jnp.minimum(i, n_ctx_tiles - 1), 0)
    lat_row = lambda i: (jnp.maximum(i - n_ctx_tiles, 0), 0)
    mod = lambda k: pl.BlockSpec((1, 1, d), lambda i: (base + seg(i) * 6 + k, 0, 0))
    wspec = pl.BlockSpec((None, d, d), lambda i: (l, 0, 0))
    return pl.pallas_call(
        functools.partial(_merge_kernel, alpha=alpha, n_ctx_tiles=n_ctx_tiles),
        grid=(n // tm,),
        in_specs=[pl.BlockSpec((tm, d), ctx_row), pl.BlockSpec((tm, d), lat_row),
                  pl.BlockSpec((tm, d), ctx_row), pl.BlockSpec((tm, d), lat_row),
                  pl.BlockSpec((tm, d), lambda i: (i, ga_col // d)),
                  pl.BlockSpec((tm, d), lambda i: (i, gl_col // d)),
                  pl.BlockSpec((tm, d), row), mod(2), mod(3), mod(4),
                  pl.BlockSpec((1, 1, d), lambda i: (2 * l, 0, 0)),
                  pl.BlockSpec((1, 1, d), lambda i: (2 * l, 0, 0)),
                  wspec, wspec, wspec],
        out_specs=[pl.BlockSpec((tm, d), row), pl.BlockSpec((tm, d), row)],
        out_shape=[jax.ShapeDtypeStruct((n, d), f32), jax.ShapeDtypeStruct((n, d), f32)],
        scratch_shapes=[pltpu.VMEM((tm, d), bf16), pltpu.VMEM((tm, d), bf16)],
        compiler_params=_cparams(1),
        name="merge_outproj_ln",
    )(attn_c, attn_l, rec_c, rec_l, y, y, x, mod4, mod4, mod4, ln_g3, ln_b3, wa, wl, wo)


def _route_kernel(h_ref, rw_ref, rb_ref, pos_ref, gate_ref, tab_ref, carry_ref):
    n_exp = rw_ref.shape[0]
    tm = h_ref.shape[0]
    gsz = n_exp // N_GROUPS

    @pl.when(pl.program_id(0) == 0)
    def _():
        carry_ref[...] = jnp.zeros_like(carry_ref)

    logits = lax.dot_general(rw_ref[...], h_ref[...], (((1,), (1,)), ((), ())),
                             precision=HIGHEST, preferred_element_type=f32)
    s = jax.nn.sigmoid(logits)
    sb = s + rb_ref[...]
    ninf = -jnp.inf
    gs_rows = []
    for g in range(N_GROUPS):
        blk = sb[g * gsz:(g + 1) * gsz]
        m1 = jnp.max(blk, axis=0, keepdims=True)
        cnt = jnp.sum((blk == m1).astype(f32), axis=0, keepdims=True)
        m2 = jnp.max(jnp.where(blk < m1, blk, ninf), axis=0, keepdims=True)
        gs_rows.append(m1 + jnp.where(cnt >= 2.0, m1, m2))
    gs = jnp.concatenate(gs_rows, axis=0)
    gi = lax.broadcasted_iota(i32, gs.shape, 0)
    grank = jnp.zeros(gs.shape, f32)
    for g2 in range(N_GROUPS):
        row = gs[g2:g2 + 1]
        grank += jnp.where((row > gs) | ((row == gs) & (g2 < gi)), 1.0, 0.0)
    gsel = grank < float(TOPK_GROUPS)
    sbm = jnp.concatenate([jnp.where(gsel[g:g + 1], sb[g * gsz:(g + 1) * gsz], ninf)
                           for g in range(N_GROUPS)], axis=0)
    ei = lax.broadcasted_iota(i32, sbm.shape, 0)
    erank = jnp.zeros(sbm.shape, f32)
    for e2 in range(n_exp):
        row = sbm[e2:e2 + 1]
        erank += jnp.where((row > sbm) | ((row == sbm) & (e2 < ei)), 1.0, 0.0)
    sel = erank < float(TOP_K)
    ssel = jnp.where(sel, s, 0.0)
    gate = ssel / jnp.sum(ssel, axis=0, keepdims=True) * ROUTE_SCALE
    selb = sel.astype(bf16)
    ti = lax.broadcasted_iota(i32, (tm, tm), 0)
    tj = lax.broadcasted_iota(i32, (tm, tm), 1)
    upper = (ti < tj).astype(bf16)
    pos_local = jnp.dot(selb, upper, preferred_element_type=f32)
    pos_ref[...] = jnp.where(sel, pos_local, -1.0)
    gate_ref[...] = gate
    cnt = jnp.sum(sel.astype(f32), axis=1, keepdims=True)
    cpad = jnp.floor((cnt + (SEG_ALIGN - 1)) * (1.0 / SEG_ALIGN)) * SEG_ALIGN
    cpad_l = jnp.broadcast_to(cpad, (n_exp, 128))
    li = lax.broadcasted_iota(i32, (n_exp, n_exp), 0)
    lj = lax.broadcasted_iota(i32, (n_exp, n_exp), 1)
    lower = (lj < li).astype(bf16)
    run_start = jnp.dot(lower, cpad_l.astype(bf16), preferred_element_type=f32)
    lane = lax.broadcasted_iota(i32, (n_exp, 128), 1)
    tab = jnp.where(lane == TAB_LEN, cpad_l, jnp.where(lane == TAB_BASE, carry_ref[...], run_start))
    tab_ref[...] = tab.astype(i32)
    carry_ref[...] = carry_ref[...] + cpad_l


def _route(h2, rwt, rb3, l):
    n, d = h2.shape
    n_exp = rwt.shape[1]
    tm = ROUTE_TILE
    assert n % tm == 0
    lst = pl.BlockSpec((n_exp, tm), lambda i: (0, i))
    return pl.pallas_call(
        _route_kernel,
        grid=(n // tm,),
        in_specs=[pl.BlockSpec((tm, d), lambda i: (i, 0)),
                  pl.BlockSpec((None, n_exp, d), lambda i: (l, 0, 0)),
                  pl.BlockSpec((None, n_exp, 1), lambda i: (l, 0, 0))],
        out_specs=[lst, lst, pl.BlockSpec((None, n_exp, 128), lambda i: (i, 0, 0))],
        out_shape=[jax.ShapeDtypeStruct((n_exp, n), f32), jax.ShapeDtypeStruct((n_exp, n), f32),
                   jax.ShapeDtypeStruct((n // tm, n_exp, 128), i32)],
        scratch_shapes=[pltpu.VMEM((n_exp, 128), f32)],
        compiler_params=_cparams(1),
        name="moe_route",
    )(h2, rwt, rb3)


def _pow2_pieces(count, max_rows, fn):
    size = max_rows
    while size >= SEG_ALIGN:
        off = count & (-2 * size)
        pl.when((count & size) != 0)(functools.partial(fn, off, size))
        size //= 2


def _wait_rows(count, max_rows, src_ref, dst_ref, sem):
    def wait(_, size):
        pltpu.make_async_copy(src_ref.at[pl.ds(0, size)], dst_ref.at[pl.ds(0, size)], sem).wait()

    _pow2_pieces(count, max_rows, wait)


def _pad_experts(x):
    return jnp.concatenate([x, jnp.zeros((128 - x.shape[0], x.shape[1]), x.dtype)], axis=0)


def _dispatch_kernel(pd_ref, tot_ref, ts_ref, tl_ref, pos_ref, tab_ref, h_ref, xs_ref, buf_ref, z_ref,
                     sem, *, n_exp, n_tiles, tile_rows, wait_max):
    t = pl.program_id(0)
    slot = t % 2
    tm = h_ref.shape[0]
    pieces = tile_rows // SEG_ALIGN

    @pl.when(t >= 2)
    def _():
        _wait_rows(tot_ref[t - 2], wait_max, buf_ref.at[slot], xs_ref, sem.at[slot])

    tab = _pad_experts(tab_ref[...].astype(f32))
    tab_t = tab.T
    start_row, len_row = tab_t[TAB_START:TAB_START + 1, :], tab_t[TAB_LEN:TAB_LEN + 1, :]
    rid = lax.broadcasted_iota(i32, (tile_rows, 128), 0).astype(f32)
    in_run = ((rid >= start_row) & (rid < start_row + len_row)).astype(bf16)
    pos_rows = jnp.dot(in_run, _pad_experts(pos_ref[...]).astype(bf16), preferred_element_type=f32)
    start16 = (tab[:, TAB_START:TAB_START + 1] * (1.0 / SEG_ALIGN)).astype(bf16)
    run_start = jnp.dot(in_run, jnp.broadcast_to(start16, (128, 128)), preferred_element_type=f32) * SEG_ALIGN
    ordinal = rid - run_start
    ordinal = jnp.concatenate([ordinal] * (tm // 128), axis=1)
    perm = jnp.where(pos_rows == ordinal, 1.0, 0.0).astype(bf16)
    buf_ref[slot] = jnp.dot(perm, h_ref[...].astype(bf16), preferred_element_type=f32).astype(bf16)

    def issue(i, _):
        for u in range(PIECE_UNROLL):
            p = i * PIECE_UNROLL + u
            src = pl.multiple_of(p * SEG_ALIGN, SEG_ALIGN)
            dst = pl.multiple_of(pd_ref[t * pieces + p], SEG_ALIGN)
            pltpu.make_async_copy(buf_ref.at[slot, pl.ds(src, SEG_ALIGN)], xs_ref.at[pl.ds(dst, SEG_ALIGN)],
                                  sem.at[slot]).start()
        return 0

    lax.fori_loop(0, tot_ref[t] // (SEG_ALIGN * PIECE_UNROLL), issue, 0)

    @pl.when(t == n_tiles - 1)
    def _():
        z_ref[...] = jnp.zeros_like(z_ref)

        def fill(e, _):
            def copy(off, size):
                dst = pl.multiple_of(ts_ref[e] + off, SEG_ALIGN)
                pltpu.make_async_copy(z_ref.at[pl.ds(0, size)], xs_ref.at[pl.ds(dst, size)], sem.at[2]).start()

            _pow2_pieces(tl_ref[e], MOE_BLOCK // 2, copy)
            return 0

        lax.fori_loop(0, n_exp, fill, 0)

        def drain(e, _):
            _wait_rows(tl_ref[e], MOE_BLOCK // 2, z_ref, xs_ref, sem.at[2])
            return 0

        lax.fori_loop(0, n_exp, drain, 0)
        if n_tiles > 1:
            _wait_rows(tot_ref[t - 1], wait_max, buf_ref.at[1 - slot], xs_ref, sem.at[1 - slot])
        _wait_rows(tot_ref[t], wait_max, buf_ref.at[slot], xs_ref, sem.at[slot])


def _tile_rows(tm, n_exp):
    return -(-(tm * TOP_K + n_exp * (SEG_ALIGN - 1)) // 128) * 128


def _wait_max(tile_rows):
    return 1 << (tile_rows.bit_length() - 1)


def _dispatch(h2, pos, tab, piece_dst, tot, tail_start, tail_len, n_rows, n_exp):
    n, d = h2.shape
    tm = ROUTE_TILE
    n_tiles = n // tm
    tile_rows = _tile_rows(tm, n_exp)
    assert n_exp <= 128 and tm % 128 == 0
    return pl.pallas_call(
        functools.partial(_dispatch_kernel, n_exp=n_exp, n_tiles=n_tiles, tile_rows=tile_rows,
                          wait_max=_wait_max(tile_rows)),
        grid_spec=pltpu.PrefetchScalarGridSpec(
            num_scalar_prefetch=4,
            grid=(n_tiles,),
            in_specs=[pl.BlockSpec((n_exp, tm), lambda i, *_: (0, i)),
                      pl.BlockSpec((None, n_exp, 128), lambda i, *_: (i, 0, 0)),
                      pl.BlockSpec((tm, d), lambda i, *_: (i, 0))],
            out_specs=pl.BlockSpec(memory_space=pl.ANY),
            scratch_shapes=[pltpu.VMEM((2, tile_rows, d), bf16), pltpu.VMEM((MOE_BLOCK // 2, d), bf16),
                            pltpu.SemaphoreType.DMA((3,))]),
        out_shape=jax.ShapeDtypeStruct((n_rows, d), bf16),
        compiler_params=_cparams(1),
        name="moe_dispatch",
    )(piece_dst, tot, tail_start, tail_len, pos, tab, h2)


def _expert_kernel(be_ref, nu_ref, x_ref, wg_ref, wu_ref, wd_ref, o_ref):
    @pl.when(pl.program_id(0) < nu_ref[0])
    def _():
        x = x_ref[...]
        g = jnp.dot(x, wg_ref[...].astype(bf16), preferred_element_type=f32)
        u = jnp.dot(x, wu_ref[...].astype(bf16), preferred_element_type=f32)
        h = (g * jax.nn.sigmoid(g)) * u
        o_ref[...] = jnp.dot(h.astype(bf16), wd_ref[...].astype(bf16),
                             preferred_element_type=f32).astype(bf16)


def _experts(xs, blk_e, n_used, we_g, we_u, we_d, l):
    n_rows, d = xs.shape
    f = we_g.shape[-1]
    nblk = n_rows // MOE_BLOCK
    blk = lambda b, be, nu: (jnp.minimum(b, nu[0] - 1), 0)
    wsel = lambda b, be, nu: (l, be[jnp.minimum(b, nu[0] - 1)], 0, 0)
    return pl.pallas_call(
        _expert_kernel,
        grid_spec=pltpu.PrefetchScalarGridSpec(
            num_scalar_prefetch=2,
            grid=(nblk,),
            in_specs=[pl.BlockSpec((MOE_BLOCK, d), blk),
                      pl.BlockSpec((None, None, d, f), wsel),
                      pl.BlockSpec((None, None, d, f), wsel),
                      pl.BlockSpec((None, None, f, d), wsel)],
            out_specs=pl.BlockSpec((MOE_BLOCK, d), blk)),
        out_shape=jax.ShapeDtypeStruct((n_rows, d), bf16),
        compiler_params=_cparams(1),
        name="moe_experts",
    )(blk_e, n_used, xs, we_g, we_u, we_d)


def _combine_kernel(pd_ref, tot_ref, pos_ref, gate_ref, tab_ref, ys_ref, x1_ref, shf_ref, scf_ref, gt_ref, g_ref, b_ref,
                    sg_ref, su_ref, sd_ref, o_ref, buf_ref, sem, *, alpha, n_exp, n_tiles, tile_rows, wait_max):
    t = pl.program_id(0)
    slot = t % 2
    tm = x1_ref.shape[0]
    pieces = tile_rows // SEG_ALIGN

    def fetch(tile, dst_slot):
        def issue(i, _):
            for u in range(PIECE_UNROLL):
                p = i * PIECE_UNROLL + u
                src = pl.multiple_of(pd_ref[tile * pieces + p], SEG_ALIGN)
                dst = pl.multiple_of(p * SEG_ALIGN, SEG_ALIGN)
                pltpu.make_async_copy(ys_ref.at[pl.ds(src, SEG_ALIGN)], buf_ref.at[dst_slot, pl.ds(dst, SEG_ALIGN)],
                                      sem.at[dst_slot]).start()
            return 0

        lax.fori_loop(0, tot_ref[tile] // (SEG_ALIGN * PIECE_UNROLL), issue, 0)

    @pl.when(t == 0)
    def _():
        buf_ref[...] = jnp.zeros_like(buf_ref)
        fetch(0, 0)

    @pl.when(t + 1 < n_tiles)
    def _():
        fetch(t + 1, 1 - slot)

    hb = (x1_ref[...] * (1.0 + scf_ref[0]) + shf_ref[0]).astype(bf16)
    g = jnp.dot(hb, sg_ref[...], preferred_element_type=f32)
    u = jnp.dot(hb, su_ref[...], preferred_element_type=f32)
    shared = jnp.dot(((g * jax.nn.sigmoid(g)) * u).astype(bf16), sd_ref[...], preferred_element_type=f32)
    tab = _pad_experts(tab_ref[...].astype(f32))
    start_col, len_col = tab[:, TAB_START:TAB_START + 1], tab[:, TAB_LEN:TAB_LEN + 1]
    rid = lax.broadcasted_iota(i32, (128, tile_rows), 1).astype(f32)
    in_run = (rid >= start_col) & (rid < start_col + len_col)
    ordinal = rid[0:1, :] - jnp.sum(jnp.where(in_run, start_col, 0.0), axis=0, keepdims=True)
    in_run = in_run.astype(bf16)
    pos_t = _pad_experts(pos_ref[...]).T.astype(bf16)
    gate_t = _pad_experts(gate_ref[...]).T.astype(bf16)
    pos_rows = jnp.dot(pos_t, in_run, preferred_element_type=f32)
    gate_rows = jnp.dot(gate_t, in_run, preferred_element_type=f32)
    wgt = jnp.where(pos_rows == ordinal, gate_rows, 0.0).astype(bf16)
    _wait_rows(tot_ref[t], wait_max, ys_ref, buf_ref.at[slot], sem.at[slot])
    routed = jnp.dot(wgt, buf_ref[slot], preferred_element_type=f32)
    o_ref[...] = _layer_norm(alpha * x1_ref[...] + gt_ref[0] * (shared + routed), g_ref[0], b_ref[0])


def _combine(ys, pos, gate, tab, piece_dst, tot, x1, mod4, ln_g3, ln_b3, sg, su, sd, l, n_ctx, dec_seq, alpha,
             n_exp):
    n, d = x1.shape
    f = sg.shape[-1]
    tm = ROUTE_TILE
    assert n_ctx % tm == 0 and dec_seq % tm == 0
    n_tiles = n // tm
    tile_rows = _tile_rows(tm, n_exp)
    n_ctx_tiles = n_ctx // tm
    tiles_per_seq = dec_seq // tm

    def seg(i):
        return jnp.where(i < n_ctx_tiles, 0, 1 + (i - n_ctx_tiles) // tiles_per_seq)

    base = l * MOD_ROWS * 6
    row = lambda i, *_: (i, 0)
    lst = lambda i, *_: (0, i)
    return pl.pallas_call(
        functools.partial(_combine_kernel, alpha=alpha, n_exp=n_exp, n_tiles=n_tiles, tile_rows=tile_rows,
                          wait_max=_wait_max(tile_rows)),
        grid_spec=pltpu.PrefetchScalarGridSpec(
            num_scalar_prefetch=2,
            grid=(n_tiles,),
            in_specs=[pl.BlockSpec((n_exp, tm), lst), pl.BlockSpec((n_exp, tm), lst),
                      pl.BlockSpec((None, n_exp, 128), lambda i, *_: (i, 0, 0)),
                      pl.BlockSpec(memory_space=pl.ANY),
                      pl.BlockSpec((tm, d), row),
                      pl.BlockSpec((1, 1, d), lambda i, *_: (base + seg(i) * 6 + 3, 0, 0)),
                      pl.BlockSpec((1, 1, d), lambda i, *_: (base + seg(i) * 6 + 4, 0, 0)),
                      pl.BlockSpec((1, 1, d), lambda i, *_: (base + seg(i) * 6 + 5, 0, 0)),
                      pl.BlockSpec((1, 1, d), lambda i, *_: (2 * l + 1, 0, 0)),
                      pl.BlockSpec((1, 1, d), lambda i, *_: (2 * l + 1, 0, 0)),
                      pl.BlockSpec((None, d, f), lambda i, *_: (l, 0, 0)),
                      pl.BlockSpec((None, d, f), lambda i, *_: (l, 0, 0)),
                      pl.BlockSpec((None, f, d), lambda i, *_: (l, 0, 0))],
            out_specs=pl.BlockSpec((tm, d), row),
            scratch_shapes=[pltpu.VMEM((2, tile_rows, d), bf16), pltpu.SemaphoreType.DMA((2,))]),
        out_shape=jax.ShapeDtypeStruct((n, d), f32),
        compiler_params=_cparams(1),
        name="moe_combine_ln",
    )(piece_dst, tot, pos, gate, tab, ys, x1, mod4, mod4, mod4, ln_g3, ln_b3, sg, su, sd)


def _rope_tables(seq, width):
    rows = seq // GRID_W
    row = jnp.repeat(jnp.arange(rows, dtype=f32), GRID_W)
    col = jnp.tile(jnp.arange(GRID_W, dtype=f32), rows)
    half = HEAD_DIM // 2
    inv = ROPE_BASE ** (-jnp.arange(0, half, 2, dtype=f32) / half)
    lane = jnp.arange(width)
    d = lane % HEAD_DIM
    pos = jnp.where((d // half)[None, :] == 0, row[:, None], col[:, None])
    ang = pos * inv[d % (HEAD_DIM // 4)][None, :]
    sign = jnp.where((d % half) < HEAD_DIM // 4, -1.0, 1.0)[None, :]
    return jnp.cos(ang), jnp.sin(ang) * sign


def _block_diag_gates(gate_w_r, gate_w_i, cb):
    depth, _, nb, bw, _ = gate_w_r.shape
    per = cb // bw
    w = jnp.stack([gate_w_r[:, 0], gate_w_i[:, 0], gate_w_r[:, 1], gate_w_i[:, 1]], axis=1)
    w = w.reshape(depth, 4, nb // per, per, bw, bw)
    eye = jnp.eye(per, dtype=w.dtype)
    bd = jnp.einsum("lgcaio,ab->lcgaibo", w, eye)
    return bd.reshape(depth, nb // per, 4, cb, cb).astype(bf16)


def kernel(x_prompt, x_sample, c, cache_k, cache_v, state_lru, c_ctx, w_mod, b_mod, w_in, lam_param, subln_w,
           conv_w, conv_b, gate_w_r, gate_b_r, gate_w_i, gate_b_i, lru_lambda, w_br_attn, w_br_lru, w_out,
           ln_g, ln_b, router_w, router_b, exp_w_gate, exp_w_up, exp_w_down, sh_w_gate, sh_w_up, sh_w_down):
    batch, seq, d = x_prompt.shape
    dec_batch, dec_seq, _ = x_sample.shape
    depth = w_mod.shape[0]
    past = cache_k.shape[2]
    d_lru = conv_w.shape[-1]
    n_exp = router_w.shape[-1]
    qk = N_HEADS * 2 * HEAD_DIM
    attn_w = N_HEADS * V_DIM
    n_ctx, n_lat = batch * seq, dec_batch * dec_seq
    n = n_ctx + n_lat
    alpha = (2 * depth) ** 0.25
    assert 1 + dec_batch <= MOD_ROWS and n_ctx % dec_seq == 0

    cond = jnp.concatenate([c_ctx[None], c, jnp.zeros((MOD_ROWS - 1 - dec_batch, d), f32)], 0)
    w_in_b = w_in.astype(bf16)
    wa, wl, wo = w_br_attn.astype(bf16), w_br_lru.astype(bf16), w_out.astype(bf16)
    sg, su, sd = sh_w_gate.astype(bf16), sh_w_up.astype(bf16), sh_w_down.astype(bf16)
    rwt = jnp.swapaxes(router_w, 1, 2)
    rb3 = router_b.reshape(depth, n_exp, 1)
    wg = _block_diag_gates(0.5 * gate_w_r, 0.5 * gate_w_i, 256)
    bg = 0.5 * jnp.stack([gate_b_r[:, 0], gate_b_i[:, 0], gate_b_r[:, 1], gate_b_i[:, 1]], axis=1)
    conv_b3 = conv_b.reshape(depth, 1, d_lru)
    subln_w3 = subln_w.reshape(depth, 1, V_DIM)
    ln_g3 = ln_g.reshape(depth * 2, 1, d)
    ln_b3 = ln_b.reshape(depth * 2, 1, d)
    ck = cache_k.reshape(dec_batch, depth, past, N_HEADS * 2 * HEAD_DIM)
    cv = cache_v.reshape(dec_batch, depth, past, attn_w)
    cos_t, sin_t = _rope_tables(dec_seq, V_DIM)
    h0_ctx = jnp.zeros((1, 2, d_lru), f32)

    mod4 = _mod_table(cond, w_mod, b_mod).reshape(depth * MOD_ROWS * 6, 1, d)
    x = jnp.concatenate([x_prompt.reshape(n_ctx, d), x_sample.reshape(n_lat, d)], 0)
    n_tiles = n // ROUTE_TILE
    spare0 = n * TOP_K + n_tiles * n_exp * (SEG_ALIGN - 1) + n_exp * (MOE_BLOCK - 1)
    spare0 = -(-spare0 // SEG_ALIGN) * SEG_ALIGN
    n_rows = -(-(spare0 + n_tiles * (PIECE_UNROLL - 1) * SEG_ALIGN) // MOE_BLOCK) * MOE_BLOCK
    xr_col, yr_col = 2 * qk + attn_w, 2 * qk + attn_w + d_lru
    ga_col, gl_col = yr_col + d_lru, yr_col + d_lru + d

    new_k, new_v, new_s = [], [], []
    for l in range(depth):
        lam_init = 0.8 - 0.6 * math.exp(-0.3 * l)
        y = _inproj(x, mod4, w_in_b, cos_t, sin_t, l, n_ctx, dec_seq, 2 * qk)
        new_k.append(y[:n_ctx, qk:2 * qk].reshape(batch, seq, N_HEADS, 2 * HEAD_DIM))
        new_v.append(y[:n_ctx, 2 * qk:2 * qk + attn_w].reshape(batch, seq, N_HEADS, V_DIM))
        attn_c = _attention(y, lam_param, subln_w3, l, lam_init, 0, batch, seq)
        attn_l = _attention(y, lam_param, subln_w3, l, lam_init, n_ctx, dec_batch, dec_seq, ck, cv)
        rec_c, hfin = _lru(y, conv_w, conv_b3, wg, bg, lru_lambda, h0_ctx, l, 0, batch, seq, xr_col, yr_col)
        rec_l, _ = _lru(y, conv_w, conv_b3, wg, bg, lru_lambda, state_lru[:, l], l, n_ctx, dec_batch, dec_seq,
                        xr_col, yr_col)
        new_s.append(hfin)
        x1, h2 = _merge(attn_c, attn_l, rec_c, rec_l, y, x, mod4, ln_g3, ln_b3, wa, wl, wo, l, n_ctx, dec_seq, ga_col, gl_col, alpha)
        pos, gate, tab = _route(h2, rwt, rb3, l)
        cpad, grel, start = tab[:, :, TAB_LEN], tab[:, :, TAB_BASE], tab[:, :, TAB_START]
        used = grel[-1] + cpad[-1]
        padded = (used + MOE_BLOCK - 1) // MOE_BLOCK * MOE_BLOCK
        pad_end = jnp.cumsum(padded)
        pad_start = pad_end - padded
        tot = jnp.sum(cpad, axis=1).astype(i32)
        blk_row0 = jnp.arange(n_rows // MOE_BLOCK, dtype=i32) * MOE_BLOCK
        blk_e = jnp.minimum(jnp.sum(pad_end[None, :] <= blk_row0[:, None], axis=1), n_exp - 1).astype(i32)
        n_used = (pad_end[-1:] // MOE_BLOCK).astype(i32)
        piece0 = jnp.arange(_tile_rows(ROUTE_TILE, n_exp) // SEG_ALIGN, dtype=i32)[None, :, None] * SEG_ALIGN
        in_run = (piece0 >= start[:, None, :]) & (piece0 < (start + cpad)[:, None, :])
        shift = (pad_start[None, :] + grel - start)[:, None, :]
        piece0 = piece0[:, :, 0]
        piece_row = piece0 + jnp.sum(jnp.where(in_run, shift, 0), axis=-1)
        real = piece0 < tot[:, None]
        spare = (spare0 + jnp.arange(n_tiles, dtype=i32)[:, None] * ((PIECE_UNROLL - 1) * SEG_ALIGN)
                 + piece0 - tot[:, None])
        piece_dst = jnp.where(real, piece_row, spare).reshape(-1).astype(i32)
        piece_src = jnp.where(real, piece_row, 0).reshape(-1).astype(i32)
        step = SEG_ALIGN * PIECE_UNROLL
        tot = (tot + step - 1) // step * step
        xs = _dispatch(h2, pos, tab, piece_dst, tot, (pad_start + used).astype(i32),
                       (padded - used).astype(i32), n_rows, n_exp)
        ys = _experts(xs, blk_e, n_used, exp_w_gate, exp_w_up, exp_w_down, l)
        x = _combine(ys, pos, gate, tab, piece_src, tot, x1, mod4, ln_g3, ln_b3, sg, su, sd, l, n_ctx, dec_seq,
                     alpha, n_exp)
    return (x[:n_ctx].reshape(batch, seq, d), x[n_ctx:].reshape(dec_batch, dec_seq, d),
            jnp.stack(new_k, 1), jnp.stack(new_v, 1), jnp.stack(new_s, 1))
```

```python
import functools
import math

import jax
import jax.numpy as jnp
from jax import lax
from jax.experimental import pallas as pl
from jax.experimental.pallas import tpu as pltpu

f32, bf16, i32 = jnp.float32, jnp.bfloat16, jnp.int32

GRID_W = 64
N_HEADS = 8
HEAD_DIM = 64
V_DIM = 2 * HEAD_DIM
ROPE_BASE = 10000.0
LRU_BLOCKS = 16
CONV_WIDTH = 4
LRU_C = 8.0
TOP_K = 8
N_GROUPS = 8
TOPK_GROUPS = 4
ROUTE_SCALE = 2.5
LN_EPS = 1e-5
MOD_ROWS = 8
MOE_BLOCK = 1024
ROUTE_TILE = 256
SEG_ALIGN = 16
PIECE_UNROLL = 4
ROPE_CHUNK = 256
TAB_LEN, TAB_BASE, TAB_START = 0, 1, 2
SCAN_UNROLL = 4
VMEM_LIMIT = 56 * 1024 * 1024
HIGHEST = lax.Precision.HIGHEST


def _cparams(n_axes):
    return pltpu.CompilerParams(dimension_semantics=("arbitrary",) * n_axes,
                                vmem_limit_bytes=VMEM_LIMIT)


def _tile(n, pref):
    t = min(n, pref)
    while n % t:
        t -= 8
    assert t > 0 and n % t == 0
    return t


def _layer_norm(z, g, b):
    mu = jnp.mean(z, -1, keepdims=True)
    zc = z - mu
    var = jnp.mean(zc * zc, -1, keepdims=True)
    return zc * lax.rsqrt(var + LN_EPS) * g + b


def _mod_kernel(c_ref, w_ref, b_ref, o_ref):
    c = c_ref[...]
    a = c * jax.nn.sigmoid(c)
    o_ref[...] = jnp.dot(a, w_ref[...], precision=HIGHEST, preferred_element_type=f32) + b_ref[...]


def _mod_table(cond, w_mod, b_mod):
    depth, d, d6 = w_mod.shape
    tn = _tile(d6, 1024)
    return pl.pallas_call(
        _mod_kernel,
        grid=(depth, d6 // tn),
        in_specs=[pl.BlockSpec((MOD_ROWS, d), lambda l, j: (0, 0)),
                  pl.BlockSpec((None, d, tn), lambda l, j: (l, 0, j)),
                  pl.BlockSpec((None, 1, tn), lambda l, j: (l, 0, j))],
        out_specs=pl.BlockSpec((None, MOD_ROWS, tn), lambda l, j: (l, 0, j)),
        out_shape=jax.ShapeDtypeStruct((depth, MOD_ROWS, d6), f32),
        compiler_params=_cparams(2),
        name="mod_table",
    )(cond, w_mod, b_mod.reshape(depth, 1, d6))


def _inproj_kernel(x_ref, sh_ref, sc_ref, w_ref, cos_ref, sin_ref, o_ref, hb_ref, *,
                   n_ctx_tiles, n_rope_cols, tn):
    i = pl.program_id(0)
    j = pl.program_id(1)

    @pl.when(j == 0)
    def _():
        hb_ref[...] = (x_ref[...] * (1.0 + sc_ref[0]) + sh_ref[0]).astype(bf16)

    rope = jnp.logical_and(i >= n_ctx_tiles, j < n_rope_cols)

    @pl.when(rope)
    def _():
        cos, sin = cos_ref[...], sin_ref[...]
        lane = lax.broadcasted_iota(i32, cos.shape, 1)
        first = (lane % (HEAD_DIM // 2)) < (HEAD_DIM // 4)
        for c in range(tn // ROPE_CHUNK):
            acc = jnp.dot(hb_ref[...], w_ref[:, c * ROPE_CHUNK:(c + 1) * ROPE_CHUNK], preferred_element_type=f32)
            for h in range(ROPE_CHUNK // V_DIM):
                a = acc[:, h * V_DIM:(h + 1) * V_DIM]
                partner = jnp.where(first, pltpu.roll(a, V_DIM - HEAD_DIM // 4, 1), pltpu.roll(a, HEAD_DIM // 4, 1))
                col = c * ROPE_CHUNK + h * V_DIM
                o_ref[:, col:col + V_DIM] = a * cos + partner * sin

    @pl.when(jnp.logical_not(rope))
    def _():
        o_ref[...] = jnp.dot(hb_ref[...], w_ref[...], preferred_element_type=f32)


def _inproj(x, mod4, w_in_b, cos_t, sin_t, l, n_ctx, dec_seq, qk_cols):
    n, d = x.shape
    d_in = w_in_b.shape[-1]
    tm = _tile(math.gcd(n_ctx, dec_seq), 2048)
    tn = _tile(math.gcd(d_in, qk_cols), 1024)
    assert tn % V_DIM == 0
    n_ctx_tiles = n_ctx // tm
    tiles_per_seq = dec_seq // tm

    def seg(i):
        return jnp.where(i < n_ctx_tiles, 0, 1 + (i - n_ctx_tiles) // tiles_per_seq)

    def pos(i):
        return jnp.where(i < n_ctx_tiles, 0, (i - n_ctx_tiles) % tiles_per_seq)

    base = l * MOD_ROWS * 6
    return pl.pallas_call(
        functools.partial(_inproj_kernel, n_ctx_tiles=n_ctx_tiles, n_rope_cols=qk_cols // tn, tn=tn),
        grid=(n // tm, d_in // tn),
        in_specs=[pl.BlockSpec((tm, d), lambda i, j: (i, 0)),
                  pl.BlockSpec((1, 1, d), lambda i, j: (base + seg(i) * 6 + 0, 0, 0)),
                  pl.BlockSpec((1, 1, d), lambda i, j: (base + seg(i) * 6 + 1, 0, 0)),
                  pl.BlockSpec((None, d, tn), lambda i, j: (l, 0, j)),
                  pl.BlockSpec((tm, V_DIM), lambda i, j: (pos(i), 0)),
                  pl.BlockSpec((tm, V_DIM), lambda i, j: (pos(i), 0))],
        out_specs=pl.BlockSpec((tm, tn), lambda i, j: (i, j)),
        out_shape=jax.ShapeDtypeStruct((n, d_in), f32),
        scratch_shapes=[pltpu.VMEM((tm, d), bf16)],
        compiler_params=_cparams(2),
        name="in_proj",
    )(x, mod4, mod4, w_in_b, cos_t, sin_t)


def _key_chunk(lk):
    for ck in (1152, 512, 384, 256, 128):
        if lk % ck == 0:
            return ck
    return lk


def _attn_kernel(*refs, has_cache, lam_init, past, heads):
    if has_cache:
        q_ref, k_ref, v_ref, ck_ref, cv_ref, lp_ref, sw_ref, o_ref, kt_ref, vb_ref = refs
    else:
        q_ref, k_ref, v_ref, lp_ref, sw_ref, o_ref, kt_ref, vb_ref = refs

    @pl.when(pl.program_id(2) == 0)
    def _():
        for h in range(heads):
            cols = slice(h * V_DIM, (h + 1) * V_DIM)
            if has_cache:
                kt_ref[h, :, :past] = ck_ref[:, cols].T.astype(bf16)
                vb_ref[h, :past, :V_DIM] = cv_ref[:, cols].astype(bf16)
            kt_ref[h, :, past:] = k_ref[:, cols].T.astype(bf16)
            vb_ref[h, past:, :V_DIM] = v_ref[:, cols].astype(bf16)
            vb_ref[h, :, V_DIM:] = jnp.ones((vb_ref.shape[1], V_DIM), bf16)

    lp = lp_ref[...]
    lam = (jnp.exp(jnp.sum(lp[0:1] * lp[1:2], keepdims=True))
           - jnp.exp(jnp.sum(lp[2:3] * lp[3:4], keepdims=True)) + lam_init)
    for h in range(heads):
        cols = slice(h * V_DIM, (h + 1) * V_DIM)
        o = _attn_head(q_ref[:, cols], kt_ref.at[h], vb_ref.at[h], lam)
        o = o * lax.rsqrt(jnp.mean(o * o, -1, keepdims=True) + LN_EPS) * sw_ref[...]
        o_ref[:, cols] = (o * (1.0 - lam_init)).astype(o_ref.dtype)


def _attn_head(q, kt_ref, vb_ref, lam):
    tq = q.shape[0]
    q = q * (HEAD_DIM ** -0.5 * math.log2(math.e))
    lane = lax.broadcasted_iota(i32, q.shape, 1)
    q2 = jnp.concatenate([jnp.where(lane < HEAD_DIM, q, 0.0).astype(bf16),
                          jnp.where(lane >= HEAD_DIM, q, 0.0).astype(bf16)], axis=0)
    lk = kt_ref.shape[1]
    ck = _key_chunk(lk)
    acc = jnp.zeros((2 * tq, 2 * V_DIM), f32)
    m_run = jnp.full((2 * tq, 1), -jnp.inf, f32)
    for c in range(lk // ck):
        s_c = jnp.dot(q2, kt_ref[:, c * ck:(c + 1) * ck], preferred_element_type=f32)
        m_new = jnp.maximum(m_run, jnp.max(s_c, axis=-1, keepdims=True))
        e_c = jnp.exp2(s_c - m_new).astype(bf16)
        acc = acc * jnp.exp2(m_run - m_new) + jnp.dot(e_c, vb_ref[c * ck:(c + 1) * ck, :],
                                                        preferred_element_type=f32)
        m_run = m_new
    o2 = acc[:, :V_DIM] / acc[:, V_DIM:]
    return o2[:tq] - lam * o2[tq:]


def _attention(y, lam_param, subln_w3, l, lam_init, row0, nseq, seq, cache_k=None, cache_v=None):
    has_cache = cache_k is not None
    past = cache_k.shape[2] if has_cache else 0
    lk = past + seq
    tq = _tile(seq, 1024)
    nq = seq // tq
    heads = N_HEADS if lk <= 512 else 1
    hg = N_HEADS // heads
    w = heads * V_DIM
    assert row0 % seq == 0
    qb0, kb0 = row0 // tq, row0 // seq
    in_specs = [pl.BlockSpec((tq, w), lambda b, h, i: (qb0 + b * nq + i, h)),
                pl.BlockSpec((seq, w), lambda b, h, i: (kb0 + b, hg + h)),
                pl.BlockSpec((seq, w), lambda b, h, i: (kb0 + b, 2 * hg + h))]
    args = [y, y, y]
    if has_cache:
        in_specs += [pl.BlockSpec((None, None, past, w), lambda b, h, i: (b, l, 0, h)),
                     pl.BlockSpec((None, None, past, w), lambda b, h, i: (b, l, 0, h))]
        args += [cache_k, cache_v]
    in_specs += [pl.BlockSpec((None, 4, HEAD_DIM), lambda b, h, i: (l, 0, 0)),
                 pl.BlockSpec((None, 1, V_DIM), lambda b, h, i: (l, 0, 0))]
    args += [lam_param, subln_w3]
    return pl.pallas_call(
        functools.partial(_attn_kernel, has_cache=has_cache, lam_init=lam_init, past=past, heads=heads),
        grid=(nseq, hg, nq),
        in_specs=in_specs,
        out_specs=pl.BlockSpec((tq, w), lambda b, h, i: (b * nq + i, h)),
        out_shape=jax.ShapeDtypeStruct((nseq * seq, N_HEADS * V_DIM), bf16),
        scratch_shapes=[pltpu.VMEM((heads, V_DIM, lk), bf16), pltpu.VMEM((heads, lk, 2 * V_DIM), bf16)],
        compiler_params=_cparams(3),
        name="diff_attn_lat" if has_cache else "diff_attn_ctx",
    )(*args)


def _lru_kernel(xr_ref, yr_ref, cw_ref, cb_ref, wg_ref, bg_ref, lam_ref, h0_ref,
                rec_ref, hfin_ref, xp_ref, a_ref, b_ref, hf_ref, *, chunk):
    seq, c = xr_ref.shape
    xp_ref[0:8, :] = jnp.zeros((8, c), f32)
    xp_ref[8:8 + seq, :] = xr_ref[...]
    xp_ref[8 + seq:16 + seq, :] = jnp.zeros((8, c), f32)
    lam = lam_ref[...]
    nlam = -lam
    softplus = jnp.maximum(nlam, 0.0) + jnp.log(1.0 + jnp.exp(-jnp.abs(nlam)))
    kh = (-0.5 * LRU_C) * softplus
    for ci in range(seq // chunk):
        t0 = ci * chunk
        xc = cb_ref[...] + sum(cw_ref[j:j + 1, :] * xp_ref[8 + t0 + j - 1:8 + t0 + j - 1 + chunk, :]
                               for j in range(CONV_WIDTH))
        xcb = xc.astype(bf16)
        hx = 0.5 * xc
        for d in range(2):
            tr = jnp.tanh(jnp.dot(xcb, wg_ref[2 * d], preferred_element_type=f32) + bg_ref[2 * d:2 * d + 1, :])
            ti = jnp.tanh(jnp.dot(xcb, wg_ref[2 * d + 1], preferred_element_type=f32)
                          + bg_ref[2 * d + 1:2 * d + 2, :])
            log_a = tr * kh[d:d + 1, :] + kh[d:d + 1, :]
            a = jnp.exp(log_a)
            a_ref[d, t0:t0 + chunk, :] = a
            u = -jnp.tanh(log_a) * (a * a + 1.0)
            b_ref[d, t0:t0 + chunk, :] = jnp.where(u > 0.0, u * lax.rsqrt(u), 0.0) * (ti * hx + hx)

    rows = lax.broadcasted_iota(i32, (8, c), 0)
    nch = seq // 8

    def scan_body(i, carry):
        cf, cb = carry
        tf = pl.multiple_of(8 * i, 8)
        af = a_ref[0, pl.ds(tf, 8), :]
        bf = b_ref[0, pl.ds(tf, 8), :]
        for d in (1, 2, 4):
            keep = rows >= d
            bf = af * jnp.where(keep, pltpu.roll(bf, d, 0), 0.0) + bf
            af = af * jnp.where(keep, pltpu.roll(af, d, 0), 1.0)
        hf = af * cf + bf
        hf_ref[pl.ds(tf, 8), :] = hf
        tb = pl.multiple_of(8 * (nch - 1 - i), 8)
        ab = a_ref[1, pl.ds(tb, 8), :]
        bb = b_ref[1, pl.ds(tb, 8), :]
        for d in (1, 2, 4):
            keep = rows < 8 - d
            bb = ab * jnp.where(keep, pltpu.roll(bb, 8 - d, 0), 0.0) + bb
            ab = ab * jnp.where(keep, pltpu.roll(ab, 8 - d, 0), 1.0)
        hb = ab * cb + bb
        xp_ref[pl.ds(tb, 8), :] = hb
        return hf[7:8, :], hb[0:1, :]

    h0 = h0_ref[...]
    cf, cb = lax.fori_loop(0, nch, scan_body, (h0[0:1, :], h0[1:2, :]), unroll=SCAN_UNROLL)
    hfin_ref[0:1, :] = cf
    hfin_ref[1:2, :] = cb
    for ci in range(seq // chunk):
        t0 = ci * chunk
        rec_ref[t0:t0 + chunk, :] = ((hf_ref[t0:t0 + chunk, :] + xp_ref[t0:t0 + chunk, :])
                                     * jax.nn.gelu(yr_ref[t0:t0 + chunk, :])).astype(rec_ref.dtype)


def _lru(y, conv_w, conv_b3, wg, bg, lru_lambda, h0, l, row0, nseq, seq, xr_col, yr_col):
    d_lru = conv_w.shape[-1]
    cb = 256
    ncb = d_lru // cb
    assert row0 % seq == 0
    rb0 = row0 // seq
    h0_batched = h0.shape[0] > 1
    in_specs = [pl.BlockSpec((seq, cb), lambda b, c: (rb0 + b, xr_col // cb + c)),
                pl.BlockSpec((seq, cb), lambda b, c: (rb0 + b, yr_col // cb + c)),
                pl.BlockSpec((None, CONV_WIDTH, cb), lambda b, c: (l, 0, c)),
                pl.BlockSpec((None, 1, cb), lambda b, c: (l, 0, c)),
                pl.BlockSpec((None, None, 4, cb, cb), lambda b, c: (l, c, 0, 0, 0)),
                pl.BlockSpec((None, 4, cb), lambda b, c: (l, 0, c)),
                pl.BlockSpec((None, 2, cb), lambda b, c: (l, 0, c)),
                pl.BlockSpec((None, 2, cb), (lambda b, c: (b, 0, c)) if h0_batched else (lambda b, c: (0, 0, c)))]
    return pl.pallas_call(
        functools.partial(_lru_kernel, chunk=_tile(seq, 512)),
        grid=(nseq, ncb),
        in_specs=in_specs,
        out_specs=[pl.BlockSpec((seq, cb), lambda b, c: (b, c)),
                   pl.BlockSpec((None, 2, cb), lambda b, c: (b, 0, c))],
        out_shape=[jax.ShapeDtypeStruct((nseq * seq, d_lru), bf16),
                   jax.ShapeDtypeStruct((nseq, 2, d_lru), f32)],
        scratch_shapes=[pltpu.VMEM((seq + 16, cb), f32), pltpu.VMEM((2, seq, cb), f32),
                        pltpu.VMEM((2, seq, cb), f32), pltpu.VMEM((seq, cb), f32)],
        compiler_params=_cparams(2),
        name="rglru",
    )(y, y, conv_w, conv_b3, wg, bg, lru_lambda, h0)


def _merge_kernel(atc_ref, atl_ref, rcc_ref, rcl_ref, ga_ref, gl_ref, x_ref, gt_ref, shf_ref, scf_ref,
                  g_ref, b_ref, wa_ref, wl_ref, wo_ref, x1_ref, h2_ref, ab_ref, rb_ref, *, alpha, n_ctx_tiles):
    @pl.when(pl.program_id(0) < n_ctx_tiles)
    def _():
        ab_ref[...] = atc_ref[...]
        rb_ref[...] = rcc_ref[...]

    @pl.when(pl.program_id(0) >= n_ctx_tiles)
    def _():
        ab_ref[...] = atl_ref[...]
        rb_ref[...] = rcl_ref[...]

    pa = jnp.dot(ab_ref[...], wa_ref[...], preferred_element_type=f32)
    pr = jnp.dot(rb_ref[...], wl_ref[...], preferred_element_type=f32)
    merged = jax.nn.sigmoid(ga_ref[...]) * pa + jax.nn.sigmoid(gl_ref[...]) * pr
    out = jnp.dot(merged.astype(bf16), wo_ref[...], preferred_element_type=f32)
    x1 = _layer_norm(alpha * x_ref[...] + gt_ref[0] * out, g_ref[0], b_ref[0])
    x1_ref[...] = x1
    h2_ref[...] = x1 * (1.0 + scf_ref[0]) + shf_ref[0]


def _merge(attn_c, attn_l, rec_c, rec_l, y, x, mod4, ln_g3, ln_b3, wa, wl, wo, l, n_ctx, dec_seq, ga_col, gl_col,
           alpha):
    n, d = x.shape
    tm = _tile(math.gcd(n_ctx, dec_seq), 512)
    n_ctx_tiles = n_ctx // tm
    tiles_per_seq = dec_seq // tm

    def seg(i):
        return jnp.where(i < n_ctx_tiles, 0, 1 + (i - n_ctx_tiles) // tiles_per_seq)

    base = l * MOD_ROWS * 6
    row = lambda i: (i, 0)
    ctx_row = lambda i: (jnp.minimum(i, n_ctx_tiles - 1), 0)
    lat_row = lambda i: (jnp.maximum(i - n_ctx_tiles, 0), 0)
    mod = lambda k: pl.BlockSpec((1, 1, d), lambda i: (base + seg(i) * 6 + k, 0, 0))
    wspec = pl.BlockSpec((None, d, d), lambda i: (l, 0, 0))
    return pl.pallas_call(
        functools.partial(_merge_kernel, alpha=alpha, n_ctx_tiles=n_ctx_tiles),
        grid=(n // tm,),
        in_specs=[pl.BlockSpec((tm, d), ctx_row), pl.BlockSpec((tm, d), lat_row),
                  pl.BlockSpec((tm, d), ctx_row), pl.BlockSpec((tm, d), lat_row),
                  pl.BlockSpec((tm, d), lambda i: (i, ga_col // d)),
                  pl.BlockSpec((tm, d), lambda i: (i, gl_col // d)),
                  pl.BlockSpec((tm, d), row), mod(2), mod(3), mod(4),
                  pl.BlockSpec((1, 1, d), lambda i: (2 * l, 0, 0)),
                  pl.BlockSpec((1, 1, d), lambda i: (2 * l, 0, 0)),
                  wspec, wspec, wspec],
        out_specs=[pl.BlockSpec((tm, d), row), pl.BlockSpec((tm, d), row)],
        out_shape=[jax.ShapeDtypeStruct((n, d), f32), jax.ShapeDtypeStruct((n, d), f32)],
        scratch_shapes=[pltpu.VMEM((tm, d), bf16), pltpu.VMEM((tm, d), bf16)],
        compiler_params=_cparams(1),
        name="merge_outproj_ln",
    )(attn_c, attn_l, rec_c, rec_l, y, y, x, mod4, mod4, mod4, ln_g3, ln_b3, wa, wl, wo)


def _route_kernel(h_ref, rw_ref, rb_ref, pos_ref, gate_ref, tab_ref, carry_ref):
    n_exp = rw_ref.shape[0]
    tm = h_ref.shape[0]
    gsz = n_exp // N_GROUPS

    @pl.when(pl.program_id(0) == 0)
    def _():
        carry_ref[...] = jnp.zeros_like(carry_ref)

    logits = lax.dot_general(rw_ref[...], h_ref[...], (((1,), (1,)), ((), ())),
                             precision=HIGHEST, preferred_element_type=f32)
    s = jax.nn.sigmoid(logits)
    sb = s + rb_ref[...]
    ninf = -jnp.inf
    gs_rows = []
    for g in range(N_GROUPS):
        blk = sb[g * gsz:(g + 1) * gsz]
        m1 = jnp.max(blk, axis=0, keepdims=True)
        cnt = jnp.sum((blk == m1).astype(f32), axis=0, keepdims=True)
        m2 = jnp.max(jnp.where(blk < m1, blk, ninf), axis=0, keepdims=True)
        gs_rows.append(m1 + jnp.where(cnt >= 2.0, m1, m2))
    gs = jnp.concatenate(gs_rows, axis=0)
    gi = lax.broadcasted_iota(i32, gs.shape, 0)
    grank = jnp.zeros(gs.shape, f32)
    for g2 in range(N_GROUPS):
        row = gs[g2:g2 + 1]
        grank += jnp.where((row > gs) | ((row == gs) & (g2 < gi)), 1.0, 0.0)
    gsel = grank < float(TOPK_GROUPS)
    sbm = jnp.concatenate([jnp.where(gsel[g:g + 1], sb[g * gsz:(g + 1) * gsz], ninf)
                           for g in range(N_GROUPS)], axis=0)
    ei = lax.broadcasted_iota(i32, sbm.shape, 0)
    erank = jnp.zeros(sbm.shape, f32)
    for e2 in range(n_exp):
        row = sbm[e2:e2 + 1]
        erank += jnp.where((row > sbm) | ((row == sbm) & (e2 < ei)), 1.0, 0.0)
    sel = erank < float(TOP_K)
    ssel = jnp.where(sel, s, 0.0)
    gate = ssel / jnp.sum(ssel, axis=0, keepdims=True) * ROUTE_SCALE
    selb = sel.astype(bf16)
    ti = lax.broadcasted_iota(i32, (tm, tm), 0)
    tj = lax.broadcasted_iota(i32, (tm, tm), 1)
    upper = (ti < tj).astype(bf16)
    pos_local = jnp.dot(selb, upper, preferred_element_type=f32)
    pos_ref[...] = jnp.where(sel, pos_local, -1.0)
    gate_ref[...] = gate
    cnt = jnp.sum(sel.astype(f32), axis=1, keepdims=True)
    cpad = jnp.floor((cnt + (SEG_ALIGN - 1)) * (1.0 / SEG_ALIGN)) * SEG_ALIGN
    cpad_l = jnp.broadcast_to(cpad, (n_exp, 128))
    li = lax.broadcasted_iota(i32, (n_exp, n_exp), 0)
    lj = lax.broadcasted_iota(i32, (n_exp, n_exp), 1)
    lower = (lj < li).astype(bf16)
    run_start = jnp.dot(lower, cpad_l.astype(bf16), preferred_element_type=f32)
    lane = lax.broadcasted_iota(i32, (n_exp, 128), 1)
    tab = jnp.where(lane == TAB_LEN, cpad_l, jnp.where(lane == TAB_BASE, carry_ref[...], run_start))
    tab_ref[...] = tab.astype(i32)
    carry_ref[...] = carry_ref[...] + cpad_l


def _route(h2, rwt, rb3, l):
    n, d = h2.shape
    n_exp = rwt.shape[1]
    tm = ROUTE_TILE
    assert n % tm == 0
    lst = pl.BlockSpec((n_exp, tm), lambda i: (0, i))
    return pl.pallas_call(
        _route_kernel,
        grid=(n // tm,),
        in_specs=[pl.BlockSpec((tm, d), lambda i: (i, 0)),
                  pl.BlockSpec((None, n_exp, d), lambda i: (l, 0, 0)),
                  pl.BlockSpec((None, n_exp, 1), lambda i: (l, 0, 0))],
        out_specs=[lst, lst, pl.BlockSpec((None, n_exp, 128), lambda i: (i, 0, 0))],
        out_shape=[jax.ShapeDtypeStruct((n_exp, n), f32), jax.ShapeDtypeStruct((n_exp, n), f32),
                   jax.ShapeDtypeStruct((n // tm, n_exp, 128), i32)],
        scratch_shapes=[pltpu.VMEM((n_exp, 128), f32)],
        compiler_params=_cparams(1),
        name="moe_route",
    )(h2, rwt, rb3)


def _pow2_pieces(count, max_rows, fn):
    size = max_rows
    while size >= SEG_ALIGN:
        off = count & (-2 * size)
        pl.when((count & size) != 0)(functools.partial(fn, off, size))
        size //= 2


def _wait_rows(count, max_rows, src_ref, dst_ref, sem):
    def wait(_, size):
        pltpu.make_async_copy(src_ref.at[pl.ds(0, size)], dst_ref.at[pl.ds(0, size)], sem).wait()

    _pow2_pieces(count, max_rows, wait)


def _pad_experts(x):
    return jnp.concatenate([x, jnp.zeros((128 - x.shape[0], x.shape[1]), x.dtype)], axis=0)


def _dispatch_kernel(pd_ref, tot_ref, ts_ref, tl_ref, pos_ref, tab_ref, h_ref, xs_ref, buf_ref, z_ref,
                     sem, *, n_exp, n_tiles, tile_rows, wait_max):
    t = pl.program_id(0)
    slot = t % 2
    tm = h_ref.shape[0]
    pieces = tile_rows // SEG_ALIGN

    @pl.when(t >= 2)
    def _():
        _wait_rows(tot_ref[t - 2], wait_max, buf_ref.at[slot], xs_ref, sem.at[slot])

    tab = _pad_experts(tab_ref[...].astype(f32))
    tab_t = tab.T
    start_row, len_row = tab_t[TAB_START:TAB_START + 1, :], tab_t[TAB_LEN:TAB_LEN + 1, :]
    rid = lax.broadcasted_iota(i32, (tile_rows, 128), 0).astype(f32)
    in_run = ((rid >= start_row) & (rid < start_row + len_row)).astype(bf16)
    pos_rows = jnp.dot(in_run, _pad_experts(pos_ref[...]).astype(bf16), preferred_element_type=f32)
    start16 = (tab[:, TAB_START:TAB_START + 1] * (1.0 / SEG_ALIGN)).astype(bf16)
    run_start = jnp.dot(in_run, jnp.broadcast_to(start16, (128, 128)), preferred_element_type=f32) * SEG_ALIGN
    ordinal = rid - run_start
    ordinal = jnp.concatenate([ordinal] * (tm // 128), axis=1)
    perm = jnp.where(pos_rows == ordinal, 1.0, 0.0).astype(bf16)
    buf_ref[slot] = jnp.dot(perm, h_ref[...].astype(bf16), preferred_element_type=f32).astype(bf16)

    def issue(i, _):
        for u in range(PIECE_UNROLL):
            p = i * PIECE_UNROLL + u
            src = pl.multiple_of(p * SEG_ALIGN, SEG_ALIGN)
            dst = pl.multiple_of(pd_ref[t * pieces + p], SEG_ALIGN)
            pltpu.make_async_copy(buf_ref.at[slot, pl.ds(src, SEG_ALIGN)], xs_ref.at[pl.ds(dst, SEG_ALIGN)],
                                  sem.at[slot]).start()
        return 0

    lax.fori_loop(0, tot_ref[t] // (SEG_ALIGN * PIECE_UNROLL), issue, 0)

    @pl.when(t == n_tiles - 1)
    def _():
        z_ref[...] = jnp.zeros_like(z_ref)

        def fill(e, _):
            def copy(off, size):
                dst = pl.multiple_of(ts_ref[e] + off, SEG_ALIGN)
                pltpu.make_async_copy(z_ref.at[pl.ds(0, size)], xs_ref.at[pl.ds(dst, size)], sem.at[2]).start()

            _pow2_pieces(tl_ref[e], MOE_BLOCK // 2, copy)
            return 0

        lax.fori_loop(0, n_exp, fill, 0)

        def drain(e, _):
            _wait_rows(tl_ref[e], MOE_BLOCK // 2, z_ref, xs_ref, sem.at[2])
            return 0

        lax.fori_loop(0, n_exp, drain, 0)
        if n_tiles > 1:
            _wait_rows(tot_ref[t - 1], wait_max, buf_ref.at[1 - slot], xs_ref, sem.at[1 - slot])
        _wait_rows(tot_ref[t], wait_max, buf_ref.at[slot], xs_ref, sem.at[slot])


def _tile_rows(tm, n_exp):
    return -(-(tm * TOP_K + n_exp * (SEG_ALIGN - 1)) // 128) * 128


def _wait_max(tile_rows):
    return 1 << (tile_rows.bit_length() - 1)


def _dispatch(h2, pos, tab, piece_dst, tot, tail_start, tail_len, n_rows, n_exp):
    n, d = h2.shape
    tm = ROUTE_TILE
    n_tiles = n // tm
    tile_rows = _tile_rows(tm, n_exp)
    assert n_exp <= 128 and tm % 128 == 0
    return pl.pallas_call(
        functools.partial(_dispatch_kernel, n_exp=n_exp, n_tiles=n_tiles, tile_rows=tile_rows,
                          wait_max=_wait_max(tile_rows)),
        grid_spec=pltpu.PrefetchScalarGridSpec(
            num_scalar_prefetch=4,
            grid=(n_tiles,),
            in_specs=[pl.BlockSpec((n_exp, tm), lambda i, *_: (0, i)),
                      pl.BlockSpec((None, n_exp, 128), lambda i, *_: (i, 0, 0)),
                      pl.BlockSpec((tm, d), lambda i, *_: (i, 0))],
            out_specs=pl.BlockSpec(memory_space=pl.ANY),
            scratch_shapes=[pltpu.VMEM((2, tile_rows, d), bf16), pltpu.VMEM((MOE_BLOCK // 2, d), bf16),
                            pltpu.SemaphoreType.DMA((3,))]),
        out_shape=jax.ShapeDtypeStruct((n_rows, d), bf16),
        compiler_params=_cparams(1),
        name="moe_dispatch",
    )(piece_dst, tot, tail_start, tail_len, pos, tab, h2)


def _expert_kernel(be_ref, nu_ref, x_ref, wg_ref, wu_ref, wd_ref, o_ref):
    @pl.when(pl.program_id(0) < nu_ref[0])
    def _():
        x = x_ref[...]
        g = jnp.dot(x, wg_ref[...].astype(bf16), preferred_element_type=f32)
        u = jnp.dot(x, wu_ref[...].astype(bf16), preferred_element_type=f32)
        h = (g * jax.nn.sigmoid(g)) * u
        o_ref[...] = jnp.dot(h.astype(bf16), wd_ref[...].astype(bf16),
                             preferred_element_type=f32).astype(bf16)


def _experts(xs, blk_e, n_used, we_g, we_u, we_d, l):
    n_rows, d = xs.shape
    f = we_g.shape[-1]
    nblk = n_rows // MOE_BLOCK
    blk = lambda b, be, nu: (jnp.minimum(b, nu[0] - 1), 0)
    wsel = lambda b, be, nu: (l, be[jnp.minimum(b, nu[0] - 1)], 0, 0)
    return pl.pallas_call(
        _expert_kernel,
        grid_spec=pltpu.PrefetchScalarGridSpec(
            num_scalar_prefetch=2,
            grid=(nblk,),
            in_specs=[pl.BlockSpec((MOE_BLOCK, d), blk),
                      pl.BlockSpec((None, None, d, f), wsel),
                      pl.BlockSpec((None, None, d, f), wsel),
                      pl.BlockSpec((None, None, f, d), wsel)],
            out_specs=pl.BlockSpec((MOE_BLOCK, d), blk)),
        out_shape=jax.ShapeDtypeStruct((n_rows, d), bf16),
        compiler_params=_cparams(1),
        name="moe_experts",
    )(blk_e, n_used, xs, we_g, we_u, we_d)


def _combine_kernel(pd_ref, tot_ref, pos_ref, gate_ref, tab_ref, ys_ref, x1_ref, h2_ref, gt_ref, g_ref, b_ref,
                    sg_ref, su_ref, sd_ref, o_ref, buf_ref, sem, *, alpha, n_exp, n_tiles, tile_rows, wait_max):
    t = pl.program_id(0)
    slot = t % 2
    tm = x1_ref.shape[0]
    pieces = tile_rows // SEG_ALIGN

    def fetch(tile, dst_slot):
        def issue(i, _):
            for u in range(PIECE_UNROLL):
                p = i * PIECE_UNROLL + u
                src = pl.multiple_of(pd_ref[tile * pieces + p], SEG_ALIGN)
                dst = pl.multiple_of(p * SEG_ALIGN, SEG_ALIGN)
                pltpu.make_async_copy(ys_ref.at[pl.ds(src, SEG_ALIGN)], buf_ref.at[dst_slot, pl.ds(dst, SEG_ALIGN)],
                                      sem.at[dst_slot]).start()
            return 0

        lax.fori_loop(0, tot_ref[tile] // (SEG_ALIGN * PIECE_UNROLL), issue, 0)

    @pl.when(t == 0)
    def _():
        buf_ref[...] = jnp.zeros_like(buf_ref)
        fetch(0, 0)

    @pl.when(t + 1 < n_tiles)
    def _():
        fetch(t + 1, 1 - slot)

    hb = h2_ref[...].astype(bf16)
    g = jnp.dot(hb, sg_ref[...], preferred_element_type=f32)
    u = jnp.dot(hb, su_ref[...], preferred_element_type=f32)
    shared = jnp.dot(((g * jax.nn.sigmoid(g)) * u).astype(bf16), sd_ref[...], preferred_element_type=f32)
    tab = _pad_experts(tab_ref[...].astype(f32))
    start_col, len_col = tab[:, TAB_START:TAB_START + 1], tab[:, TAB_LEN:TAB_LEN + 1]
    rid = lax.broadcasted_iota(i32, (128, tile_rows), 1).astype(f32)
    in_run = (rid >= start_col) & (rid < start_col + len_col)
    ordinal = rid[0:1, :] - jnp.sum(jnp.where(in_run, start_col, 0.0), axis=0, keepdims=True)
    in_run = in_run.astype(bf16)
    pos_t = _pad_experts(pos_ref[...]).T.astype(bf16)
    gate_t = _pad_experts(gate_ref[...]).T.astype(bf16)
    pos_rows = jnp.dot(pos_t, in_run, preferred_element_type=f32)
    gate_rows = jnp.dot(gate_t, in_run, preferred_element_type=f32)
    wgt = jnp.where(pos_rows == ordinal, gate_rows, 0.0).astype(bf16)
    _wait_rows(tot_ref[t], wait_max, ys_ref, buf_ref.at[slot], sem.at[slot])
    routed = jnp.dot(wgt, buf_ref[slot], preferred_element_type=f32)
    o_ref[...] = _layer_norm(alpha * x1_ref[...] + gt_ref[0] * (shared + routed), g_ref[0], b_ref[0])


def _combine(ys, pos, gate, tab, piece_dst, tot, x1, h2, mod4, ln_g3, ln_b3, sg, su, sd, l, n_ctx, dec_seq, alpha,
             n_exp):
    n, d = x1.shape
    f = sg.shape[-1]
    tm = ROUTE_TILE
    assert n_ctx % tm == 0 and dec_seq % tm == 0
    n_tiles = n // tm
    tile_rows = _tile_rows(tm, n_exp)
    n_ctx_tiles = n_ctx // tm
    tiles_per_seq = dec_seq // tm

    def seg(i):
        return jnp.where(i < n_ctx_tiles, 0, 1 + (i - n_ctx_tiles) // tiles_per_seq)

    base = l * MOD_ROWS * 6
    row = lambda i, *_: (i, 0)
    lst = lambda i, *_: (0, i)
    return pl.pallas_call(
        functools.partial(_combine_kernel, alpha=alpha, n_exp=n_exp, n_tiles=n_tiles, tile_rows=tile_rows,
                          wait_max=_wait_max(tile_rows)),
        grid_spec=pltpu.PrefetchScalarGridSpec(
            num_scalar_prefetch=2,
            grid=(n_tiles,),
            in_specs=[pl.BlockSpec((n_exp, tm), lst), pl.BlockSpec((n_exp, tm), lst),
                      pl.BlockSpec((None, n_exp, 128), lambda i, *_: (i, 0, 0)),
                      pl.BlockSpec(memory_space=pl.ANY),
                      pl.BlockSpec((tm, d), row), pl.BlockSpec((tm, d), row),
                      pl.BlockSpec((1, 1, d), lambda i, *_: (base + seg(i) * 6 + 5, 0, 0)),
                      pl.BlockSpec((1, 1, d), lambda i, *_: (2 * l + 1, 0, 0)),
                      pl.BlockSpec((1, 1, d), lambda i, *_: (2 * l + 1, 0, 0)),
                      pl.BlockSpec((None, d, f), lambda i, *_: (l, 0, 0)),
                      pl.BlockSpec((None, d, f), lambda i, *_: (l, 0, 0)),
                      pl.BlockSpec((None, f, d), lambda i, *_: (l, 0, 0))],
            out_specs=pl.BlockSpec((tm, d), row),
            scratch_shapes=[pltpu.VMEM((2, tile_rows, d), bf16), pltpu.SemaphoreType.DMA((2,))]),
        out_shape=jax.ShapeDtypeStruct((n, d), f32),
        compiler_params=_cparams(1),
        name="moe_combine_ln",
    )(piece_dst, tot, pos, gate, tab, ys, x1, h2, mod4, ln_g3, ln_b3, sg, su, sd)


def _rope_tables(seq, width):
    rows = seq // GRID_W
    row = jnp.repeat(jnp.arange(rows, dtype=f32), GRID_W)
    col = jnp.tile(jnp.arange(GRID_W, dtype=f32), rows)
    half = HEAD_DIM // 2
    inv = ROPE_BASE ** (-jnp.arange(0, half, 2, dtype=f32) / half)
    lane = jnp.arange(width)
    d = lane % HEAD_DIM
    pos = jnp.where((d // half)[None, :] == 0, row[:, None], col[:, None])
    ang = pos * inv[d % (HEAD_DIM // 4)][None, :]
    sign = jnp.where((d % half) < HEAD_DIM // 4, -1.0, 1.0)[None, :]
    return jnp.cos(ang), jnp.sin(ang) * sign


def _block_diag_gates(gate_w_r, gate_w_i, cb):
    depth, _, nb, bw, _ = gate_w_r.shape
    per = cb // bw
    w = jnp.stack([gate_w_r[:, 0], gate_w_i[:, 0], gate_w_r[:, 1], gate_w_i[:, 1]], axis=1)
    w = w.reshape(depth, 4, nb // per, per, bw, bw)
    eye = jnp.eye(per, dtype=w.dtype)
    bd = jnp.einsum("lgcaio,ab->lcgaibo", w, eye)
    return bd.reshape(depth, nb // per, 4, cb, cb).astype(bf16)


def kernel(x_prompt, x_sample, c, cache_k, cache_v, state_lru, c_ctx, w_mod, b_mod, w_in, lam_param, subln_w,
           conv_w, conv_b, gate_w_r, gate_b_r, gate_w_i, gate_b_i, lru_lambda, w_br_attn, w_br_lru, w_out,
           ln_g, ln_b, router_w, router_b, exp_w_gate, exp_w_up, exp_w_down, sh_w_gate, sh_w_up, sh_w_down):
    batch, seq, d = x_prompt.shape
    dec_batch, dec_seq, _ = x_sample.shape
    depth = w_mod.shape[0]
    past = cache_k.shape[2]
    d_lru = conv_w.shape[-1]
    n_exp = router_w.shape[-1]
    qk = N_HEADS * 2 * HEAD_DIM
    attn_w = N_HEADS * V_DIM
    n_ctx, n_lat = batch * seq, dec_batch * dec_seq
    n = n_ctx + n_lat
    alpha = (2 * depth) ** 0.25
    assert 1 + dec_batch <= MOD_ROWS and n_ctx % dec_seq == 0

    cond = jnp.concatenate([c_ctx[None], c, jnp.zeros((MOD_ROWS - 1 - dec_batch, d), f32)], 0)
    w_in_b = w_in.astype(bf16)
    wa, wl, wo = w_br_attn.astype(bf16), w_br_lru.astype(bf16), w_out.astype(bf16)
    sg, su, sd = sh_w_gate.astype(bf16), sh_w_up.astype(bf16), sh_w_down.astype(bf16)
    rwt = jnp.swapaxes(router_w, 1, 2)
    rb3 = router_b.reshape(depth, n_exp, 1)
    wg = _block_diag_gates(0.5 * gate_w_r, 0.5 * gate_w_i, 256)
    bg = 0.5 * jnp.stack([gate_b_r[:, 0], gate_b_i[:, 0], gate_b_r[:, 1], gate_b_i[:, 1]], axis=1)
    conv_b3 = conv_b.reshape(depth, 1, d_lru)
    subln_w3 = subln_w.reshape(depth, 1, V_DIM)
    ln_g3 = ln_g.reshape(depth * 2, 1, d)
    ln_b3 = ln_b.reshape(depth * 2, 1, d)
    ck = cache_k.reshape(dec_batch, depth, past, N_HEADS * 2 * HEAD_DIM)
    cv = cache_v.reshape(dec_batch, depth, past, attn_w)
    cos_t, sin_t = _rope_tables(dec_seq, V_DIM)
    h0_ctx = jnp.zeros((1, 2, d_lru), f32)

    mod4 = _mod_table(cond, w_mod, b_mod).reshape(depth * MOD_ROWS * 6, 1, d)
    x = jnp.concatenate([x_prompt.reshape(n_ctx, d), x_sample.reshape(n_lat, d)], 0)
    n_tiles = n // ROUTE_TILE
    spare0 = n * TOP_K + n_tiles * n_exp * (SEG_ALIGN - 1) + n_exp * (MOE_BLOCK - 1)
    spare0 = -(-spare0 // SEG_ALIGN) * SEG_ALIGN
    n_rows = -(-(spare0 + n_tiles * (PIECE_UNROLL - 1) * SEG_ALIGN) // MOE_BLOCK) * MOE_BLOCK
    xr_col, yr_col = 2 * qk + attn_w, 2 * qk + attn_w + d_lru
    ga_col, gl_col = yr_col + d_lru, yr_col + d_lru + d

    new_k, new_v, new_s = [], [], []
    for l in range(depth):
        lam_init = 0.8 - 0.6 * math.exp(-0.3 * l)
        y = _inproj(x, mod4, w_in_b, cos_t, sin_t, l, n_ctx, dec_seq, 2 * qk)
        new_k.append(y[:n_ctx, qk:2 * qk].reshape(batch, seq, N_HEADS, 2 * HEAD_DIM))
        new_v.append(y[:n_ctx, 2 * qk:2 * qk + attn_w].reshape(batch, seq, N_HEADS, V_DIM))
        attn_c = _attention(y, lam_param, subln_w3, l, lam_init, 0, batch, seq)
        attn_l = _attention(y, lam_param, subln_w3, l, lam_init, n_ctx, dec_batch, dec_seq, ck, cv)
        rec_c, hfin = _lru(y, conv_w, conv_b3, wg, bg, lru_lambda, h0_ctx, l, 0, batch, seq, xr_col, yr_col)
        rec_l, _ = _lru(y, conv_w, conv_b3, wg, bg, lru_lambda, state_lru[:, l], l, n_ctx, dec_batch, dec_seq,
                        xr_col, yr_col)
        new_s.append(hfin)
        x1, h2 = _merge(attn_c, attn_l, rec_c, rec_l, y, x, mod4, ln_g3, ln_b3, wa, wl, wo, l, n_ctx, dec_seq, ga_col, gl_col, alpha)
        pos, gate, tab = _route(h2, rwt, rb3, l)
        cpad, grel, start = tab[:, :, TAB_LEN], tab[:, :, TAB_BASE], tab[:, :, TAB_START]
        used = grel[-1] + cpad[-1]
        padded = (used + MOE_BLOCK - 1) // MOE_BLOCK * MOE_BLOCK
        pad_end = jnp.cumsum(padded)
        pad_start = pad_end - padded
        tot = jnp.sum(cpad, axis=1).astype(i32)
        blk_row0 = jnp.arange(n_rows // MOE_BLOCK, dtype=i32) * MOE_BLOCK
        blk_e = jnp.minimum(jnp.sum(pad_end[None, :] <= blk_row0[:, None], axis=1), n_exp - 1).astype(i32)
        n_used = (pad_end[-1:] // MOE_BLOCK).astype(i32)
        piece0 = jnp.arange(_tile_rows(ROUTE_TILE, n_exp) // SEG_ALIGN, dtype=i32)[None, :, None] * SEG_ALIGN
        in_run = (piece0 >= start[:, None, :]) & (piece0 < (start + cpad)[:, None, :])
        shift = (pad_start[None, :] + grel - start)[:, None, :]
        piece0 = piece0[:, :, 0]
        piece_row = piece0 + jnp.sum(jnp.where(in_run, shift, 0), axis=-1)
        real = piece0 < tot[:, None]
        spare = (spare0 + jnp.arange(n_tiles, dtype=i32)[:, None] * ((PIECE_UNROLL - 1) * SEG_ALIGN)
                 + piece0 - tot[:, None])
        piece_dst = jnp.where(real, piece_row, spare).reshape(-1).astype(i32)
        piece_src = jnp.where(real, piece_row, 0).reshape(-1).astype(i32)
        step = SEG_ALIGN * PIECE_UNROLL
        tot = (tot + step - 1) // step * step
        xs = _dispatch(h2, pos, tab, piece_dst, tot, (pad_start + used).astype(i32),
                       (padded - used).astype(i32), n_rows, n_exp)
        ys = _experts(xs, blk_e, n_used, exp_w_gate, exp_w_up, exp_w_down, l)
        x = _combine(ys, pos, gate, tab, piece_src, tot, x1, h2, mod4, ln_g3, ln_b3, sg, su, sd, l, n_ctx, dec_seq,
                     alpha, n_exp)
    return (x[:n_ctx].reshape(batch, seq, d), x[n_ctx:].reshape(dec_batch, dec_seq, d),
            jnp.stack(new_k, 1), jnp.stack(new_v, 1), jnp.stack(new_s, 1))
```

```python
import functools
import math

import jax
import jax.numpy as jnp
from jax import lax
from jax.experimental import pallas as pl
from jax.experimental.pallas import tpu as pltpu

f32, bf16, i32 = jnp.float32, jnp.bfloat16, jnp.int32

GRID_W = 64
N_HEADS = 8
HEAD_DIM = 64
V_DIM = 2 * HEAD_DIM
ROPE_BASE = 10000.0
LRU_BLOCKS = 16
CONV_WIDTH = 4
LRU_C = 8.0
TOP_K = 8
N_GROUPS = 8
TOPK_GROUPS = 4
ROUTE_SCALE = 2.5
LN_EPS = 1e-5
MOD_ROWS = 8
MOE_BLOCK = 1024
ROUTE_TILE = 256
SEG_ALIGN = 16
PIECE_UNROLL = 4
ROPE_CHUNK = 256
TAB_LEN, TAB_BASE, TAB_START = 0, 1, 2
SCAN_UNROLL = 4
VMEM_LIMIT = 56 * 1024 * 1024
HIGHEST = lax.Precision.HIGHEST


def _cparams(n_axes):
    return pltpu.CompilerParams(dimension_semantics=("arbitrary",) * n_axes,
                                vmem_limit_bytes=VMEM_LIMIT)


def _tile(n, pref):
    t = min(n, pref)
    while n % t:
        t -= 8
    assert t > 0 and n % t == 0
    return t


def _layer_norm(z, g, b):
    mu = jnp.mean(z, -1, keepdims=True)
    zc = z - mu
    var = jnp.mean(zc * zc, -1, keepdims=True)
    return zc * lax.rsqrt(var + LN_EPS) * g + b


def _mod_kernel(c_ref, w_ref, b_ref, o_ref):
    c = c_ref[...]
    a = c * jax.nn.sigmoid(c)
    o_ref[...] = jnp.dot(a, w_ref[...], precision=HIGHEST, preferred_element_type=f32) + b_ref[...]


def _mod_table(cond, w_mod, b_mod):
    depth, d, d6 = w_mod.shape
    tn = _tile(d6, 1024)
    return pl.pallas_call(
        _mod_kernel,
        grid=(depth, d6 // tn),
        in_specs=[pl.BlockSpec((MOD_ROWS, d), lambda l, j: (0, 0)),
                  pl.BlockSpec((None, d, tn), lambda l, j: (l, 0, j)),
                  pl.BlockSpec((None, 1, tn), lambda l, j: (l, 0, j))],
        out_specs=pl.BlockSpec((None, MOD_ROWS, tn), lambda l, j: (l, 0, j)),
        out_shape=jax.ShapeDtypeStruct((depth, MOD_ROWS, d6), f32),
        compiler_params=_cparams(2),
        name="mod_table",
    )(cond, w_mod, b_mod.reshape(depth, 1, d6))


def _inproj_kernel(x_ref, sh_ref, sc_ref, w_ref, cos_ref, sin_ref, o_ref, hb_ref, *,
                   n_ctx_tiles, n_rope_cols, tn):
    i = pl.program_id(0)
    j = pl.program_id(1)

    @pl.when(j == 0)
    def _():
        hb_ref[...] = (x_ref[...] * (1.0 + sc_ref[0]) + sh_ref[0]).astype(bf16)

    rope = jnp.logical_and(i >= n_ctx_tiles, j < n_rope_cols)

    @pl.when(rope)
    def _():
        cos, sin = cos_ref[...], sin_ref[...]
        lane = lax.broadcasted_iota(i32, cos.shape, 1)
        first = (lane % (HEAD_DIM // 2)) < (HEAD_DIM // 4)
        for c in range(tn // ROPE_CHUNK):
            acc = jnp.dot(hb_ref[...], w_ref[:, c * ROPE_CHUNK:(c + 1) * ROPE_CHUNK], preferred_element_type=f32)
            for h in range(ROPE_CHUNK // V_DIM):
                a = acc[:, h * V_DIM:(h + 1) * V_DIM]
                partner = jnp.where(first, pltpu.roll(a, V_DIM - HEAD_DIM // 4, 1), pltpu.roll(a, HEAD_DIM // 4, 1))
                col = c * ROPE_CHUNK + h * V_DIM
                o_ref[:, col:col + V_DIM] = a * cos + partner * sin

    @pl.when(jnp.logical_not(rope))
    def _():
        o_ref[...] = jnp.dot(hb_ref[...], w_ref[...], preferred_element_type=f32)


def _inproj(x, mod4, w_in_b, cos_t, sin_t, l, n_ctx, dec_seq, qk_cols):
    n, d = x.shape
    d_in = w_in_b.shape[-1]
    tm = _tile(math.gcd(n_ctx, dec_seq), 2048)
    tn = _tile(math.gcd(d_in, qk_cols), 1024)
    assert tn % V_DIM == 0
    n_ctx_tiles = n_ctx // tm
    tiles_per_seq = dec_seq // tm

    def seg(i):
        return jnp.where(i < n_ctx_tiles, 0, 1 + (i - n_ctx_tiles) // tiles_per_seq)

    def pos(i):
        return jnp.where(i < n_ctx_tiles, 0, (i - n_ctx_tiles) % tiles_per_seq)

    base = l * MOD_ROWS * 6
    return pl.pallas_call(
        functools.partial(_inproj_kernel, n_ctx_tiles=n_ctx_tiles, n_rope_cols=qk_cols // tn, tn=tn),
        grid=(n // tm, d_in // tn),
        in_specs=[pl.BlockSpec((tm, d), lambda i, j: (i, 0)),
                  pl.BlockSpec((1, 1, d), lambda i, j: (base + seg(i) * 6 + 0, 0, 0)),
                  pl.BlockSpec((1, 1, d), lambda i, j: (base + seg(i) * 6 + 1, 0, 0)),
                  pl.BlockSpec((None, d, tn), lambda i, j: (l, 0, j)),
                  pl.BlockSpec((tm, V_DIM), lambda i, j: (pos(i), 0)),
                  pl.BlockSpec((tm, V_DIM), lambda i, j: (pos(i), 0))],
        out_specs=pl.BlockSpec((tm, tn), lambda i, j: (i, j)),
        out_shape=jax.ShapeDtypeStruct((n, d_in), f32),
        scratch_shapes=[pltpu.VMEM((tm, d), bf16)],
        compiler_params=_cparams(2),
        name="in_proj",
    )(x, mod4, mod4, w_in_b, cos_t, sin_t)


def _key_chunk(lk):
    for ck in (512, 384, 256, 128):
        if lk % ck == 0:
            return ck
    return lk


def _attn_kernel(*refs, has_cache, lam_init, past, heads):
    if has_cache:
        q_ref, k_ref, v_ref, ck_ref, cv_ref, lp_ref, sw_ref, o_ref, kt_ref, vb_ref = refs
    else:
        q_ref, k_ref, v_ref, lp_ref, sw_ref, o_ref, kt_ref, vb_ref = refs

    @pl.when(pl.program_id(2) == 0)
    def _():
        for h in range(heads):
            cols = slice(h * V_DIM, (h + 1) * V_DIM)
            if has_cache:
                kt_ref[h, :, :past] = ck_ref[:, cols].T.astype(bf16)
                vb_ref[h, :past, :V_DIM] = cv_ref[:, cols].astype(bf16)
            kt_ref[h, :, past:] = k_ref[:, cols].T.astype(bf16)
            vb_ref[h, past:, :V_DIM] = v_ref[:, cols].astype(bf16)
            vb_ref[h, :, V_DIM:] = jnp.ones((vb_ref.shape[1], V_DIM), bf16)

    lp = lp_ref[...]
    lam = (jnp.exp(jnp.sum(lp[0:1] * lp[1:2], keepdims=True))
           - jnp.exp(jnp.sum(lp[2:3] * lp[3:4], keepdims=True)) + lam_init)
    for h in range(heads):
        cols = slice(h * V_DIM, (h + 1) * V_DIM)
        o = _attn_head(q_ref[:, cols], kt_ref.at[h], vb_ref.at[h], lam)
        o = o * lax.rsqrt(jnp.mean(o * o, -1, keepdims=True) + LN_EPS) * sw_ref[...]
        o_ref[:, cols] = (o * (1.0 - lam_init)).astype(o_ref.dtype)


def _attn_head(q, kt_ref, vb_ref, lam):
    tq = q.shape[0]
    q = q * (HEAD_DIM ** -0.5 * math.log2(math.e))
    lane = lax.broadcasted_iota(i32, q.shape, 1)
    q2 = jnp.concatenate([jnp.where(lane < HEAD_DIM, q, 0.0).astype(bf16),
                          jnp.where(lane >= HEAD_DIM, q, 0.0).astype(bf16)], axis=0)
    lk = kt_ref.shape[1]
    ck = _key_chunk(lk)
    acc = jnp.zeros((2 * tq, 2 * V_DIM), f32)
    m_run = jnp.full((2 * tq, 1), -jnp.inf, f32)
    for c in range(lk // ck):
        s_c = jnp.dot(q2, kt_ref[:, c * ck:(c + 1) * ck], preferred_element_type=f32)
        m_new = jnp.maximum(m_run, jnp.max(s_c, axis=-1, keepdims=True))
        e_c = jnp.exp2(s_c - m_new).astype(bf16)
        acc = acc * jnp.exp2(m_run - m_new) + jnp.dot(e_c, vb_ref[c * ck:(c + 1) * ck, :],
                                                        preferred_element_type=f32)
        m_run = m_new
    o2 = acc[:, :V_DIM] / acc[:, V_DIM:]
    return o2[:tq] - lam * o2[tq:]


def _attention(y, lam_param, subln_w3, l, lam_init, row0, nseq, seq, cache_k=None, cache_v=None):
    has_cache = cache_k is not None
    past = cache_k.shape[2] if has_cache else 0
    lk = past + seq
    tq = _tile(seq, 1024)
    nq = seq // tq
    heads = N_HEADS if lk <= 512 else 1
    hg = N_HEADS // heads
    w = heads * V_DIM
    assert row0 % seq == 0
    qb0, kb0 = row0 // tq, row0 // seq
    in_specs = [pl.BlockSpec((tq, w), lambda b, h, i: (qb0 + b * nq + i, h)),
                pl.BlockSpec((seq, w), lambda b, h, i: (kb0 + b, hg + h)),
                pl.BlockSpec((seq, w), lambda b, h, i: (kb0 + b, 2 * hg + h))]
    args = [y, y, y]
    if has_cache:
        in_specs += [pl.BlockSpec((None, None, past, w), lambda b, h, i: (b, l, 0, h)),
                     pl.BlockSpec((None, None, past, w), lambda b, h, i: (b, l, 0, h))]
        args += [cache_k, cache_v]
    in_specs += [pl.BlockSpec((None, 4, HEAD_DIM), lambda b, h, i: (l, 0, 0)),
                 pl.BlockSpec((None, 1, V_DIM), lambda b, h, i: (l, 0, 0))]
    args += [lam_param, subln_w3]
    return pl.pallas_call(
        functools.partial(_attn_kernel, has_cache=has_cache, lam_init=lam_init, past=past, heads=heads),
        grid=(nseq, hg, nq),
        in_specs=in_specs,
        out_specs=pl.BlockSpec((tq, w), lambda b, h, i: (b * nq + i, h)),
        out_shape=jax.ShapeDtypeStruct((nseq * seq, N_HEADS * V_DIM), bf16),
        scratch_shapes=[pltpu.VMEM((heads, V_DIM, lk), bf16), pltpu.VMEM((heads, lk, 2 * V_DIM), bf16)],
        compiler_params=_cparams(3),
        name="diff_attn_lat" if has_cache else "diff_attn_ctx",
    )(*args)


def _lru_kernel(xr_ref, yr_ref, cw_ref, cb_ref, wg_ref, bg_ref, lam_ref, h0_ref,
                rec_ref, hfin_ref, xp_ref, a_ref, b_ref, hf_ref, *, chunk):
    seq, c = xr_ref.shape
    xp_ref[0:8, :] = jnp.zeros((8, c), f32)
    xp_ref[8:8 + seq, :] = xr_ref[...]
    xp_ref[8 + seq:16 + seq, :] = jnp.zeros((8, c), f32)
    lam = lam_ref[...]
    nlam = -lam
    softplus = jnp.maximum(nlam, 0.0) + jnp.log(1.0 + jnp.exp(-jnp.abs(nlam)))
    kh = (-0.5 * LRU_C) * softplus
    for ci in range(seq // chunk):
        t0 = ci * chunk
        xc = cb_ref[...] + sum(cw_ref[j:j + 1, :] * xp_ref[8 + t0 + j - 1:8 + t0 + j - 1 + chunk, :]
                               for j in range(CONV_WIDTH))
        xcb = xc.astype(bf16)
        hx = 0.5 * xc
        for d in range(2):
            tr = jnp.tanh(jnp.dot(xcb, wg_ref[2 * d], preferred_element_type=f32) + bg_ref[2 * d:2 * d + 1, :])
            ti = jnp.tanh(jnp.dot(xcb, wg_ref[2 * d + 1], preferred_element_type=f32)
                          + bg_ref[2 * d + 1:2 * d + 2, :])
            log_a = tr * kh[d:d + 1, :] + kh[d:d + 1, :]
            a = jnp.exp(log_a)
            a_ref[d, t0:t0 + chunk, :] = a
            u = -jnp.tanh(log_a) * (a * a + 1.0)
            b_ref[d, t0:t0 + chunk, :] = jnp.where(u > 0.0, u * lax.rsqrt(u), 0.0) * (ti * hx + hx)

    rows = lax.broadcasted_iota(i32, (8, c), 0)
    nch = seq // 8

    def scan_body(i, carry):
        cf, cb = carry
        tf = pl.multiple_of(8 * i, 8)
        af = a_ref[0, pl.ds(tf, 8), :]
        bf = b_ref[0, pl.ds(tf, 8), :]
        for d in (1, 2, 4):
            keep = rows >= d
            bf = af * jnp.where(keep, pltpu.roll(bf, d, 0), 0.0) + bf
            af = af * jnp.where(keep, pltpu.roll(af, d, 0), 1.0)
        hf = af * cf + bf
        hf_ref[pl.ds(tf, 8), :] = hf
        tb = pl.multiple_of(8 * (nch - 1 - i), 8)
        ab = a_ref[1, pl.ds(tb, 8), :]
        bb = b_ref[1, pl.ds(tb, 8), :]
        for d in (1, 2, 4):
            keep = rows < 8 - d
            bb = ab * jnp.where(keep, pltpu.roll(bb, 8 - d, 0), 0.0) + bb
            ab = ab * jnp.where(keep, pltpu.roll(ab, 8 - d, 0), 1.0)
        hb = ab * cb + bb
        xp_ref[pl.ds(tb, 8), :] = hb
        return hf[7:8, :], hb[0:1, :]

    h0 = h0_ref[...]
    cf, cb = lax.fori_loop(0, nch, scan_body, (h0[0:1, :], h0[1:2, :]), unroll=SCAN_UNROLL)
    hfin_ref[0:1, :] = cf
    hfin_ref[1:2, :] = cb
    for ci in range(seq // chunk):
        t0 = ci * chunk
        rec_ref[t0:t0 + chunk, :] = ((hf_ref[t0:t0 + chunk, :] + xp_ref[t0:t0 + chunk, :])
                                     * jax.nn.gelu(yr_ref[t0:t0 + chunk, :])).astype(rec_ref.dtype)


def _lru(y, conv_w, conv_b3, wg, bg, lru_lambda, h0, l, row0, nseq, seq, xr_col, yr_col):
    d_lru = conv_w.shape[-1]
    cb = 256
    ncb = d_lru // cb
    assert row0 % seq == 0
    rb0 = row0 // seq
    h0_batched = h0.shape[0] > 1
    in_specs = [pl.BlockSpec((seq, cb), lambda b, c: (rb0 + b, xr_col // cb + c)),
                pl.BlockSpec((seq, cb), lambda b, c: (rb0 + b, yr_col // cb + c)),
                pl.BlockSpec((None, CONV_WIDTH, cb), lambda b, c: (l, 0, c)),
                pl.BlockSpec((None, 1, cb), lambda b, c: (l, 0, c)),
                pl.BlockSpec((None, None, 4, cb, cb), lambda b, c: (l, c, 0, 0, 0)),
                pl.BlockSpec((None, 4, cb), lambda b, c: (l, 0, c)),
                pl.BlockSpec((None, 2, cb), lambda b, c: (l, 0, c)),
                pl.BlockSpec((None, 2, cb), (lambda b, c: (b, 0, c)) if h0_batched else (lambda b, c: (0, 0, c)))]
    return pl.pallas_call(
        functools.partial(_lru_kernel, chunk=_tile(seq, 512)),
        grid=(nseq, ncb),
        in_specs=in_specs,
        out_specs=[pl.BlockSpec((seq, cb), lambda b, c: (b, c)),
                   pl.BlockSpec((None, 2, cb), lambda b, c: (b, 0, c))],
        out_shape=[jax.ShapeDtypeStruct((nseq * seq, d_lru), bf16),
                   jax.ShapeDtypeStruct((nseq, 2, d_lru), f32)],
        scratch_shapes=[pltpu.VMEM((seq + 16, cb), f32), pltpu.VMEM((2, seq, cb), f32),
                        pltpu.VMEM((2, seq, cb), f32), pltpu.VMEM((seq, cb), f32)],
        compiler_params=_cparams(2),
        name="rglru",
    )(y, y, conv_w, conv_b3, wg, bg, lru_lambda, h0)


def _merge_kernel(atc_ref, atl_ref, rcc_ref, rcl_ref, ga_ref, gl_ref, x_ref, gt_ref, shf_ref, scf_ref,
                  g_ref, b_ref, wa_ref, wl_ref, wo_ref, x1_ref, h2_ref, ab_ref, rb_ref, *, alpha, n_ctx_tiles):
    @pl.when(pl.program_id(0) < n_ctx_tiles)
    def _():
        ab_ref[...] = atc_ref[...]
        rb_ref[...] = rcc_ref[...]

    @pl.when(pl.program_id(0) >= n_ctx_tiles)
    def _():
        ab_ref[...] = atl_ref[...]
        rb_ref[...] = rcl_ref[...]

    pa = jnp.dot(ab_ref[...], wa_ref[...], preferred_element_type=f32)
    pr = jnp.dot(rb_ref[...], wl_ref[...], preferred_element_type=f32)
    merged = jax.nn.sigmoid(ga_ref[...]) * pa + jax.nn.sigmoid(gl_ref[...]) * pr
    out = jnp.dot(merged.astype(bf16), wo_ref[...], preferred_element_type=f32)
    x1 = _layer_norm(alpha * x_ref[...] + gt_ref[0] * out, g_ref[0], b_ref[0])
    x1_ref[...] = x1
    h2_ref[...] = x1 * (1.0 + scf_ref[0]) + shf_ref[0]


def _merge(attn_c, attn_l, rec_c, rec_l, y, x, mod4, ln_g3, ln_b3, wa, wl, wo, l, n_ctx, dec_seq, ga_col, gl_col,
           alpha):
    n, d = x.shape
    tm = _tile(math.gcd(n_ctx, dec_seq), 512)
    n_ctx_tiles = n_ctx // tm
    tiles_per_seq = dec_seq // tm

    def seg(i):
        return jnp.where(i < n_ctx_tiles, 0, 1 + (i - n_ctx_tiles) // tiles_per_seq)

    base = l * MOD_ROWS * 6
    row = lambda i: (i, 0)
    ctx_row = lambda i: (jnp.minimum(i, n_ctx_tiles - 1), 0)
    lat_row = lambda i: (jnp.maximum(i - n_ctx_tiles, 0), 0)
    mod = lambda k: pl.BlockSpec((1, 1, d), lambda i: (base + seg(i) * 6 + k, 0, 0))
    wspec = pl.BlockSpec((None, d, d), lambda i: (l, 0, 0))
    return pl.pallas_call(
        functools.partial(_merge_kernel, alpha=alpha, n_ctx_tiles=n_ctx_tiles),
        grid=(n // tm,),
        in_specs=[pl.BlockSpec((tm, d), ctx_row), pl.BlockSpec((tm, d), lat_row),
                  pl.BlockSpec((tm, d), ctx_row), pl.BlockSpec((tm, d), lat_row),
                  pl.BlockSpec((tm, d), lambda i: (i, ga_col // d)),
                  pl.BlockSpec((tm, d), lambda i: (i, gl_col // d)),
                  pl.BlockSpec((tm, d), row), mod(2), mod(3), mod(4),
                  pl.BlockSpec((1, 1, d), lambda i: (2 * l, 0, 0)),
                  pl.BlockSpec((1, 1, d), lambda i: (2 * l, 0, 0)),
                  wspec, wspec, wspec],
        out_specs=[pl.BlockSpec((tm, d), row), pl.BlockSpec((tm, d), row)],
        out_shape=[jax.ShapeDtypeStruct((n, d), f32), jax.ShapeDtypeStruct((n, d), f32)],
        scratch_shapes=[pltpu.VMEM((tm, d), bf16), pltpu.VMEM((tm, d), bf16)],
        compiler_params=_cparams(1),
        name="merge_outproj_ln",
    )(attn_c, attn_l, rec_c, rec_l, y, y, x, mod4, mod4, mod4, ln_g3, ln_b3, wa, wl, wo)


def _route_kernel(h_ref, rw_ref, rb_ref, pos_ref, gate_ref, tab_ref, carry_ref):
    n_exp = rw_ref.shape[0]
    tm = h_ref.shape[0]
    gsz = n_exp // N_GROUPS

    @pl.when(pl.program_id(0) == 0)
    def _():
        carry_ref[...] = jnp.zeros_like(carry_ref)

    logits = lax.dot_general(rw_ref[...], h_ref[...], (((1,), (1,)), ((), ())),
                             precision=HIGHEST, preferred_element_type=f32)
    s = jax.nn.sigmoid(logits)
    sb = s + rb_ref[...]
    ninf = -jnp.inf
    gs_rows = []
    for g in range(N_GROUPS):
        blk = sb[g * gsz:(g + 1) * gsz]
        m1 = jnp.max(blk, axis=0, keepdims=True)
        cnt = jnp.sum((blk == m1).astype(f32), axis=0, keepdims=True)
        m2 = jnp.max(jnp.where(blk < m1, blk, ninf), axis=0, keepdims=True)
        gs_rows.append(m1 + jnp.where(cnt >= 2.0, m1, m2))
    gs = jnp.concatenate(gs_rows, axis=0)
    gi = lax.broadcasted_iota(i32, gs.shape, 0)
    grank = jnp.zeros(gs.shape, f32)
    for g2 in range(N_GROUPS):
        row = gs[g2:g2 + 1]
        grank += jnp.where((row > gs) | ((row == gs) & (g2 < gi)), 1.0, 0.0)
    gsel = grank < float(TOPK_GROUPS)
    sbm = jnp.concatenate([jnp.where(gsel[g:g + 1], sb[g * gsz:(g + 1) * gsz], ninf)
                           for g in range(N_GROUPS)], axis=0)
    ei = lax.broadcasted_iota(i32, sbm.shape, 0)
    erank = jnp.zeros(sbm.shape, f32)
    for e2 in range(n_exp):
        row = sbm[e2:e2 + 1]
        erank += jnp.where((row > sbm) | ((row == sbm) & (e2 < ei)), 1.0, 0.0)
    sel = erank < float(TOP_K)
    ssel = jnp.where(sel, s, 0.0)
    gate = ssel / jnp.sum(ssel, axis=0, keepdims=True) * ROUTE_SCALE
    selb = sel.astype(bf16)
    ti = lax.broadcasted_iota(i32, (tm, tm), 0)
    tj = lax.broadcasted_iota(i32, (tm, tm), 1)
    upper = (ti < tj).astype(bf16)
    pos_local = jnp.dot(selb, upper, preferred_element_type=f32)
    pos_ref[...] = jnp.where(sel, pos_local, -1.0)
    gate_ref[...] = gate
    cnt = jnp.sum(sel.astype(f32), axis=1, keepdims=True)
    cpad = jnp.floor((cnt + (SEG_ALIGN - 1)) * (1.0 / SEG_ALIGN)) * SEG_ALIGN
    cpad_l = jnp.broadcast_to(cpad, (n_exp, 128))
    li = lax.broadcasted_iota(i32, (n_exp, n_exp), 0)
    lj = lax.broadcasted_iota(i32, (n_exp, n_exp), 1)
    lower = (lj < li).astype(bf16)
    run_start = jnp.dot(lower, cpad_l.astype(bf16), preferred_element_type=f32)
    lane = lax.broadcasted_iota(i32, (n_exp, 128), 1)
    tab = jnp.where(lane == TAB_LEN, cpad_l, jnp.where(lane == TAB_BASE, carry_ref[...], run_start))
    tab_ref[...] = tab.astype(i32)
    carry_ref[...] = carry_ref[...] + cpad_l


def _route(h2, rwt, rb3, l):
    n, d = h2.shape
    n_exp = rwt.shape[1]
    tm = ROUTE_TILE
    assert n % tm == 0
    lst = pl.BlockSpec((n_exp, tm), lambda i: (0, i))
    return pl.pallas_call(
        _route_kernel,
        grid=(n // tm,),
        in_specs=[pl.BlockSpec((tm, d), lambda i: (i, 0)),
                  pl.BlockSpec((None, n_exp, d), lambda i: (l, 0, 0)),
                  pl.BlockSpec((None, n_exp, 1), lambda i: (l, 0, 0))],
        out_specs=[lst, lst, pl.BlockSpec((None, n_exp, 128), lambda i: (i, 0, 0))],
        out_shape=[jax.ShapeDtypeStruct((n_exp, n), f32), jax.ShapeDtypeStruct((n_exp, n), f32),
                   jax.ShapeDtypeStruct((n // tm, n_exp, 128), i32)],
        scratch_shapes=[pltpu.VMEM((n_exp, 128), f32)],
        compiler_params=_cparams(1),
        name="moe_route",
    )(h2, rwt, rb3)


def _pow2_pieces(count, max_rows, fn):
    size = max_rows
    while size >= SEG_ALIGN:
        off = count & (-2 * size)
        pl.when((count & size) != 0)(functools.partial(fn, off, size))
        size //= 2


def _wait_rows(count, max_rows, src_ref, dst_ref, sem):
    def wait(_, size):
        pltpu.make_async_copy(src_ref.at[pl.ds(0, size)], dst_ref.at[pl.ds(0, size)], sem).wait()

    _pow2_pieces(count, max_rows, wait)


def _pad_experts(x):
    return jnp.concatenate([x, jnp.zeros((128 - x.shape[0], x.shape[1]), x.dtype)], axis=0)


def _dispatch_kernel(pd_ref, tot_ref, ts_ref, tl_ref, pos_ref, tab_ref, h_ref, xs_ref, buf_ref, z_ref,
                     sem, *, n_exp, n_tiles, tile_rows, wait_max):
    t = pl.program_id(0)
    slot = t % 2
    tm = h_ref.shape[0]
    pieces = tile_rows // SEG_ALIGN

    @pl.when(t >= 2)
    def _():
        _wait_rows(tot_ref[t - 2], wait_max, buf_ref.at[slot], xs_ref, sem.at[slot])

    tab = _pad_experts(tab_ref[...].astype(f32))
    tab_t = tab.T
    start_row, len_row = tab_t[TAB_START:TAB_START + 1, :], tab_t[TAB_LEN:TAB_LEN + 1, :]
    rid = lax.broadcasted_iota(i32, (tile_rows, 128), 0).astype(f32)
    in_run = ((rid >= start_row) & (rid < start_row + len_row)).astype(bf16)
    pos_rows = jnp.dot(in_run, _pad_experts(pos_ref[...]).astype(bf16), preferred_element_type=f32)
    start16 = (tab[:, TAB_START:TAB_START + 1] * (1.0 / SEG_ALIGN)).astype(bf16)
    run_start = jnp.dot(in_run, jnp.broadcast_to(start16, (128, 128)), preferred_element_type=f32) * SEG_ALIGN
    ordinal = rid - run_start
    ordinal = jnp.concatenate([ordinal] * (tm // 128), axis=1)
    perm = jnp.where(pos_rows == ordinal, 1.0, 0.0).astype(bf16)
    buf_ref[slot] = jnp.dot(perm, h_ref[...].astype(bf16), preferred_element_type=f32).astype(bf16)

    def issue(i, _):
        for u in range(PIECE_UNROLL):
            p = i * PIECE_UNROLL + u
            src = pl.multiple_of(p * SEG_ALIGN, SEG_ALIGN)
            dst = pl.multiple_of(pd_ref[t * pieces + p], SEG_ALIGN)
            pltpu.make_async_copy(buf_ref.at[slot, pl.ds(src, SEG_ALIGN)], xs_ref.at[pl.ds(dst, SEG_ALIGN)],
                                  sem.at[slot]).start()
        return 0

    lax.fori_loop(0, tot_ref[t] // (SEG_ALIGN * PIECE_UNROLL), issue, 0)

    @pl.when(t == n_tiles - 1)
    def _():
        z_ref[...] = jnp.zeros_like(z_ref)

        def fill(e, _):
            def copy(off, size):
                dst = pl.multiple_of(ts_ref[e] + off, SEG_ALIGN)
                pltpu.make_async_copy(z_ref.at[pl.ds(0, size)], xs_ref.at[pl.ds(dst, size)], sem.at[2]).start()

            _pow2_pieces(tl_ref[e], MOE_BLOCK // 2, copy)
            return 0

        lax.fori_loop(0, n_exp, fill, 0)

        def drain(e, _):
            _wait_rows(tl_ref[e], MOE_BLOCK // 2, z_ref, xs_ref, sem.at[2])
            return 0

        lax.fori_loop(0, n_exp, drain, 0)
        if n_tiles > 1:
            _wait_rows(tot_ref[t - 1], wait_max, buf_ref.at[1 - slot], xs_ref, sem.at[1 - slot])
        _wait_rows(tot_ref[t], wait_max, buf_ref.at[slot], xs_ref, sem.at[slot])


def _tile_rows(tm, n_exp):
    return -(-(tm * TOP_K + n_exp * (SEG_ALIGN - 1)) // 128) * 128


def _wait_max(tile_rows):
    return 1 << (tile_rows.bit_length() - 1)


def _dispatch(h2, pos, tab, piece_dst, tot, tail_start, tail_len, n_rows, n_exp):
    n, d = h2.shape
    tm = ROUTE_TILE
    n_tiles = n // tm
    tile_rows = _tile_rows(tm, n_exp)
    assert n_exp <= 128 and tm % 128 == 0
    return pl.pallas_call(
        functools.partial(_dispatch_kernel, n_exp=n_exp, n_tiles=n_tiles, tile_rows=tile_rows,
                          wait_max=_wait_max(tile_rows)),
        grid_spec=pltpu.PrefetchScalarGridSpec(
            num_scalar_prefetch=4,
            grid=(n_tiles,),
            in_specs=[pl.BlockSpec((n_exp, tm), lambda i, *_: (0, i)),
                      pl.BlockSpec((None, n_exp, 128), lambda i, *_: (i, 0, 0)),
                      pl.BlockSpec((tm, d), lambda i, *_: (i, 0))],
            out_specs=pl.BlockSpec(memory_space=pl.ANY),
            scratch_shapes=[pltpu.VMEM((2, tile_rows, d), bf16), pltpu.VMEM((MOE_BLOCK // 2, d), bf16),
                            pltpu.SemaphoreType.DMA((3,))]),
        out_shape=jax.ShapeDtypeStruct((n_rows, d), bf16),
        compiler_params=_cparams(1),
        name="moe_dispatch",
    )(piece_dst, tot, tail_start, tail_len, pos, tab, h2)


def _expert_kernel(be_ref, nu_ref, x_ref, wg_ref, wu_ref, wd_ref, o_ref):
    @pl.when(pl.program_id(0) < nu_ref[0])
    def _():
        x = x_ref[...]
        g = jnp.dot(x, wg_ref[...].astype(bf16), preferred_element_type=f32)
        u = jnp.dot(x, wu_ref[...].astype(bf16), preferred_element_type=f32)
        h = (g * jax.nn.sigmoid(g)) * u
        o_ref[...] = jnp.dot(h.astype(bf16), wd_ref[...].astype(bf16),
                             preferred_element_type=f32).astype(bf16)


def _experts(xs, blk_e, n_used, we_g, we_u, we_d, l):
    n_rows, d = xs.shape
    f = we_g.shape[-1]
    nblk = n_rows // MOE_BLOCK
    blk = lambda b, be, nu: (jnp.minimum(b, nu[0] - 1), 0)
    wsel = lambda b, be, nu: (l, be[jnp.minimum(b, nu[0] - 1)], 0, 0)
    return pl.pallas_call(
        _expert_kernel,
        grid_spec=pltpu.PrefetchScalarGridSpec(
            num_scalar_prefetch=2,
            grid=(nblk,),
            in_specs=[pl.BlockSpec((MOE_BLOCK, d), blk),
                      pl.BlockSpec((None, None, d, f), wsel),
                      pl.BlockSpec((None, None, d, f), wsel),
                      pl.BlockSpec((None, None, f, d), wsel)],
            out_specs=pl.BlockSpec((MOE_BLOCK, d), blk)),
        out_shape=jax.ShapeDtypeStruct((n_rows, d), bf16),
        compiler_params=_cparams(1),
        name="moe_experts",
    )(blk_e, n_used, xs, we_g, we_u, we_d)


def _combine_kernel(pd_ref, tot_ref, pos_ref, gate_ref, tab_ref, ys_ref, x1_ref, h2_ref, gt_ref, g_ref, b_ref,
                    sg_ref, su_ref, sd_ref, o_ref, buf_ref, sem, *, alpha, n_exp, n_tiles, tile_rows, wait_max):
    t = pl.program_id(0)
    slot = t % 2
    tm = x1_ref.shape[0]
    pieces = tile_rows // SEG_ALIGN

    def fetch(tile, dst_slot):
        def issue(i, _):
            for u in range(PIECE_UNROLL):
                p = i * PIECE_UNROLL + u
                src = pl.multiple_of(pd_ref[tile * pieces + p], SEG_ALIGN)
                dst = pl.multiple_of(p * SEG_ALIGN, SEG_ALIGN)
                pltpu.make_async_copy(ys_ref.at[pl.ds(src, SEG_ALIGN)], buf_ref.at[dst_slot, pl.ds(dst, SEG_ALIGN)],
                                      sem.at[dst_slot]).start()
            return 0

        lax.fori_loop(0, tot_ref[tile] // (SEG_ALIGN * PIECE_UNROLL), issue, 0)

    @pl.when(t == 0)
    def _():
        buf_ref[...] = jnp.zeros_like(buf_ref)
        fetch(0, 0)

    @pl.when(t + 1 < n_tiles)
    def _():
        fetch(t + 1, 1 - slot)

    hb = h2_ref[...].astype(bf16)
    g = jnp.dot(hb, sg_ref[...], preferred_element_type=f32)
    u = jnp.dot(hb, su_ref[...], preferred_element_type=f32)
    shared = jnp.dot(((g * jax.nn.sigmoid(g)) * u).astype(bf16), sd_ref[...], preferred_element_type=f32)
    tab = _pad_experts(tab_ref[...].astype(f32))
    start_col, len_col = tab[:, TAB_START:TAB_START + 1], tab[:, TAB_LEN:TAB_LEN + 1]
    rid = lax.broadcasted_iota(i32, (128, tile_rows), 1).astype(f32)
    in_run = (rid >= start_col) & (rid < start_col + len_col)
    ordinal = rid[0:1, :] - jnp.sum(jnp.where(in_run, start_col, 0.0), axis=0, keepdims=True)
    in_run = in_run.astype(bf16)
    pos_t = _pad_experts(pos_ref[...]).T.astype(bf16)
    gate_t = _pad_experts(gate_ref[...]).T.astype(bf16)
    pos_rows = jnp.dot(pos_t, in_run, preferred_element_type=f32)
    gate_rows = jnp.dot(gate_t, in_run, preferred_element_type=f32)
    wgt = jnp.where(pos_rows == ordinal, gate_rows, 0.0).astype(bf16)
    _wait_rows(tot_ref[t], wait_max, ys_ref, buf_ref.at[slot], sem.at[slot])
    routed = jnp.dot(wgt, buf_ref[slot], preferred_element_type=f32)
    o_ref[...] = _layer_norm(alpha * x1_ref[...] + gt_ref[0] * (shared + routed), g_ref[0], b_ref[0])


def _combine(ys, pos, gate, tab, piece_dst, tot, x1, h2, mod4, ln_g3, ln_b3, sg, su, sd, l, n_ctx, dec_seq, alpha,
             n_exp):
    n, d = x1.shape
    f = sg.shape[-1]
    tm = ROUTE_TILE
    assert n_ctx % tm == 0 and dec_seq % tm == 0
    n_tiles = n // tm
    tile_rows = _tile_rows(tm, n_exp)
    n_ctx_tiles = n_ctx // tm
    tiles_per_seq = dec_seq // tm

    def seg(i):
        return jnp.where(i < n_ctx_tiles, 0, 1 + (i - n_ctx_tiles) // tiles_per_seq)

    base = l * MOD_ROWS * 6
    row = lambda i, *_: (i, 0)
    lst = lambda i, *_: (0, i)
    return pl.pallas_call(
        functools.partial(_combine_kernel, alpha=alpha, n_exp=n_exp, n_tiles=n_tiles, tile_rows=tile_rows,
                          wait_max=_wait_max(tile_rows)),
        grid_spec=pltpu.PrefetchScalarGridSpec(
            num_scalar_prefetch=2,
            grid=(n_tiles,),
            in_specs=[pl.BlockSpec((n_exp, tm), lst), pl.BlockSpec((n_exp, tm), lst),
                      pl.BlockSpec((None, n_exp, 128), lambda i, *_: (i, 0, 0)),
                      pl.BlockSpec(memory_space=pl.ANY),
                      pl.BlockSpec((tm, d), row), pl.BlockSpec((tm, d), row),
                      pl.BlockSpec((1, 1, d), lambda i, *_: (base + seg(i) * 6 + 5, 0, 0)),
                      pl.BlockSpec((1, 1, d), lambda i, *_: (2 * l + 1, 0, 0)),
                      pl.BlockSpec((1, 1, d), lambda i, *_: (2 * l + 1, 0, 0)),
                      pl.BlockSpec((None, d, f), lambda i, *_: (l, 0, 0)),
                      pl.BlockSpec((None, d, f), lambda i, *_: (l, 0, 0)),
                      pl.BlockSpec((None, f, d), lambda i, *_: (l, 0, 0))],
            out_specs=pl.BlockSpec((tm, d), row),
            scratch_shapes=[pltpu.VMEM((2, tile_rows, d), bf16), pltpu.SemaphoreType.DMA((2,))]),
        out_shape=jax.ShapeDtypeStruct((n, d), f32),
        compiler_params=_cparams(1),
        name="moe_combine_ln",
    )(piece_dst, tot, pos, gate, tab, ys, x1, h2, mod4, ln_g3, ln_b3, sg, su, sd)


def _rope_tables(seq, width):
    rows = seq // GRID_W
    row = jnp.repeat(jnp.arange(rows, dtype=f32), GRID_W)
    col = jnp.tile(jnp.arange(GRID_W, dtype=f32), rows)
    half = HEAD_DIM // 2
    inv = ROPE_BASE ** (-jnp.arange(0, half, 2, dtype=f32) / half)
    lane = jnp.arange(width)
    d = lane % HEAD_DIM
    pos = jnp.where((d // half)[None, :] == 0, row[:, None], col[:, None])
    ang = pos * inv[d % (HEAD_DIM // 4)][None, :]
    sign = jnp.where((d % half) < HEAD_DIM // 4, -1.0, 1.0)[None, :]
    return jnp.cos(ang), jnp.sin(ang) * sign


def _block_diag_gates(gate_w_r, gate_w_i, cb):
    depth, _, nb, bw, _ = gate_w_r.shape
    per = cb // bw
    w = jnp.stack([gate_w_r[:, 0], gate_w_i[:, 0], gate_w_r[:, 1], gate_w_i[:, 1]], axis=1)
    w = w.reshape(depth, 4, nb // per, per, bw, bw)
    eye = jnp.eye(per, dtype=w.dtype)
    bd = jnp.einsum("lgcaio,ab->lcgaibo", w, eye)
    return bd.reshape(depth, nb // per, 4, cb, cb).astype(bf16)


def kernel(x_prompt, x_sample, c, cache_k, cache_v, state_lru, c_ctx, w_mod, b_mod, w_in, lam_param, subln_w,
           conv_w, conv_b, gate_w_r, gate_b_r, gate_w_i, gate_b_i, lru_lambda, w_br_attn, w_br_lru, w_out,
           ln_g, ln_b, router_w, router_b, exp_w_gate, exp_w_up, exp_w_down, sh_w_gate, sh_w_up, sh_w_down):
    batch, seq, d = x_prompt.shape
    dec_batch, dec_seq, _ = x_sample.shape
    depth = w_mod.shape[0]
    past = cache_k.shape[2]
    d_lru = conv_w.shape[-1]
    n_exp = router_w.shape[-1]
    qk = N_HEADS * 2 * HEAD_DIM
    attn_w = N_HEADS * V_DIM
    n_ctx, n_lat = batch * seq, dec_batch * dec_seq
    n = n_ctx + n_lat
    alpha = (2 * depth) ** 0.25
    assert 1 + dec_batch <= MOD_ROWS and n_ctx % dec_seq == 0

    cond = jnp.concatenate([c_ctx[None], c, jnp.zeros((MOD_ROWS - 1 - dec_batch, d), f32)], 0)
    w_in_b = w_in.astype(bf16)
    wa, wl, wo = w_br_attn.astype(bf16), w_br_lru.astype(bf16), w_out.astype(bf16)
    sg, su, sd = sh_w_gate.astype(bf16), sh_w_up.astype(bf16), sh_w_down.astype(bf16)
    rwt = jnp.swapaxes(router_w, 1, 2)
    rb3 = router_b.reshape(depth, n_exp, 1)
    wg = _block_diag_gates(0.5 * gate_w_r, 0.5 * gate_w_i, 256)
    bg = 0.5 * jnp.stack([gate_b_r[:, 0], gate_b_i[:, 0], gate_b_r[:, 1], gate_b_i[:, 1]], axis=1)
    conv_b3 = conv_b.reshape(depth, 1, d_lru)
    subln_w3 = subln_w.reshape(depth, 1, V_DIM)
    ln_g3 = ln_g.reshape(depth * 2, 1, d)
    ln_b3 = ln_b.reshape(depth * 2, 1, d)
    ck = cache_k.reshape(dec_batch, depth, past, N_HEADS * 2 * HEAD_DIM)
    cv = cache_v.reshape(dec_batch, depth, past, attn_w)
    cos_t, sin_t = _rope_tables(dec_seq, V_DIM)
    h0_ctx = jnp.zeros((1, 2, d_lru), f32)

    mod4 = _mod_table(cond, w_mod, b_mod).reshape(depth * MOD_ROWS * 6, 1, d)
    x = jnp.concatenate([x_prompt.reshape(n_ctx, d), x_sample.reshape(n_lat, d)], 0)
    n_tiles = n // ROUTE_TILE
    spare0 = n * TOP_K + n_tiles * n_exp * (SEG_ALIGN - 1) + n_exp * (MOE_BLOCK - 1)
    spare0 = -(-spare0 // SEG_ALIGN) * SEG_ALIGN
    n_rows = -(-(spare0 + n_tiles * (PIECE_UNROLL - 1) * SEG_ALIGN) // MOE_BLOCK) * MOE_BLOCK
    xr_col, yr_col = 2 * qk + attn_w, 2 * qk + attn_w + d_lru
    ga_col, gl_col = yr_col + d_lru, yr_col + d_lru + d

    new_k, new_v, new_s = [], [], []
    for l in range(depth):
        lam_init = 0.8 - 0.6 * math.exp(-0.3 * l)
        y = _inproj(x, mod4, w_in_b, cos_t, sin_t, l, n_ctx, dec_seq, 2 * qk)
        new_k.append(y[:n_ctx, qk:2 * qk].reshape(batch, seq, N_HEADS, 2 * HEAD_DIM))
        new_v.append(y[:n_ctx, 2 * qk:2 * qk + attn_w].reshape(batch, seq, N_HEADS, V_DIM))
        attn_c = _attention(y, lam_param, subln_w3, l, lam_init, 0, batch, seq)
        attn_l = _attention(y, lam_param, subln_w3, l, lam_init, n_ctx, dec_batch, dec_seq, ck, cv)
        rec_c, hfin = _lru(y, conv_w, conv_b3, wg, bg, lru_lambda, h0_ctx, l, 0, batch, seq, xr_col, yr_col)
        rec_l, _ = _lru(y, conv_w, conv_b3, wg, bg, lru_lambda, state_lru[:, l], l, n_ctx, dec_batch, dec_seq,
                        xr_col, yr_col)
        new_s.append(hfin)
        x1, h2 = _merge(attn_c, attn_l, rec_c, rec_l, y, x, mod4, ln_g3, ln_b3, wa, wl, wo, l, n_ctx, dec_seq, ga_col, gl_col, alpha)
        pos, gate, tab = _route(h2, rwt, rb3, l)
        cpad, grel, start = tab[:, :, TAB_LEN], tab[:, :, TAB_BASE], tab[:, :, TAB_START]
        used = grel[-1] + cpad[-1]
        padded = (used + MOE_BLOCK - 1) // MOE_BLOCK * MOE_BLOCK
        pad_end = jnp.cumsum(padded)
        pad_start = pad_end - padded
        tot = jnp.sum(cpad, axis=1).astype(i32)
        blk_row0 = jnp.arange(n_rows // MOE_BLOCK, dtype=i32) * MOE_BLOCK
        blk_e = jnp.minimum(jnp.sum(pad_end[None, :] <= blk_row0[:, None], axis=1), n_exp - 1).astype(i32)
        n_used = (pad_end[-1:] // MOE_BLOCK).astype(i32)
        piece0 = jnp.arange(_tile_rows(ROUTE_TILE, n_exp) // SEG_ALIGN, dtype=i32)[None, :, None] * SEG_ALIGN
        in_run = (piece0 >= start[:, None, :]) & (piece0 < (start + cpad)[:, None, :])
        shift = (pad_start[None, :] + grel - start)[:, None, :]
        piece0 = piece0[:, :, 0]
        piece_row = piece0 + jnp.sum(jnp.where(in_run, shift, 0), axis=-1)
        real = piece0 < tot[:, None]
        spare = (spare0 + jnp.arange(n_tiles, dtype=i32)[:, None] * ((PIECE_UNROLL - 1) * SEG_ALIGN)
                 + piece0 - tot[:, None])
        piece_dst = jnp.where(real, piece_row, spare).reshape(-1).astype(i32)
        piece_src = jnp.where(real, piece_row, 0).reshape(-1).astype(i32)
        step = SEG_ALIGN * PIECE_UNROLL
        tot = (tot + step - 1) // step * step
        xs = _dispatch(h2, pos, tab, piece_dst, tot, (pad_start + used).astype(i32),
                       (padded - used).astype(i32), n_rows, n_exp)
        ys = _experts(xs, blk_e, n_used, exp_w_gate, exp_w_up, exp_w_down, l)
        x = _combine(ys, pos, gate, tab, piece_src, tot, x1, h2, mod4, ln_g3, ln_b3, sg, su, sd, l, n_ctx, dec_seq,
                     alpha, n_exp)
    return (x[:n_ctx].reshape(batch, seq, d), x[n_ctx:].reshape(dec_batch, dec_seq, d),
            jnp.stack(new_k, 1), jnp.stack(new_v, 1), jnp.stack(new_s, 1))
```
